```python
import jax, jax.numpy as jnp
from jax import lax
import numpy as np

D_MODEL = 4096
BATCH = 1
SEQ = 8192
DEPTH = 1
DEC_BATCH = 8
DEC_SEQ = 16
PAST_LEN = 4096

CHUNK = 64
A_HEADS = 16
A_KV_HEADS = 4
A_HEAD_DIM = 128
A_GROUP = A_HEADS // A_KV_HEADS
A_WIDTH = A_HEADS * A_HEAD_DIM
KV_W = A_KV_HEADS * A_HEAD_DIM
IDX_HEADS = 16
IDX_DIM = 64
IQ_W = IDX_HEADS * IDX_DIM
TOPK_MAX = 256
Q_BLOCK = 128
ROPE_THETA = 500000.0
ROPE_FRAC = 4
A_SCALE = A_HEAD_DIM ** -0.5
IDX_SCALE = (IDX_HEADS ** -0.5) * (IDX_DIM ** -0.5)
R_HEAD_DIM = 64
R_WIDTH = D_MODEL - A_WIDTH
R_HEADS = R_WIDTH // R_HEAD_DIM
DECAY_LORA = 96
AAA_LORA = 96
GATE_LORA = 256
RWKV_COLS = 3 * R_WIDTH + DECAY_LORA + AAA_LORA + GATE_LORA
GN_EPS = 6.4e-4
IN_COLS = A_WIDTH + 2 * KV_W + IQ_W + IDX_DIM + IDX_HEADS + RWKV_COLS
D_FF = 11008
CONV_W = 3
RMS_EPS = 1e-6

kernel_name = "hybrid_dsa_rwkv7_convffn_stream_step"

F32 = jnp.float32


def _rmsnorm(x, g):
    xf = x.astype(F32)
    y = xf * lax.rsqrt(jnp.mean(xf * xf, axis=-1, keepdims=True) + RMS_EPS)
    return (y * g.astype(F32)).astype(x.dtype)


def _split(z, sizes):
    out, off = [], 0
    for s in sizes:
        out.append(z[..., off:off + s])
        off += s
    return out


def _rope_partial(x, pos):
    rd = x.shape[-1] // ROPE_FRAC
    half = rd // 2
    inv_freq = ROPE_THETA ** (-(jnp.arange(half, dtype=F32) * 2.0 / rd))
    ang = pos.astype(F32)[:, None] * inv_freq[None, :]
    cos = jnp.cos(ang)[:, None, :]
    sin = jnp.sin(ang)[:, None, :]
    xf = x.astype(F32)
    x1, x2, rest = xf[..., :half], xf[..., half:rd], xf[..., rd:]
    return jnp.concatenate([x1 * cos - x2 * sin, x1 * sin + x2 * cos, rest], axis=-1).astype(x.dtype)


def _dsa_attend(q, iq, iw, k_all, v_all, ik_all, q_pos, k_pos):
    B, T = q.shape[0], q.shape[1]
    L = k_all.shape[1]
    topk = min(TOPK_MAX, L // 4)
    blk = min(Q_BLOCK, T)
    nblk = T // blk
    k_chunk = k_pos // CHUNK
    ik_f = ik_all.astype(F32)

    def to_blocks(a):
        return jnp.moveaxis(a.reshape((B, nblk, blk) + a.shape[2:]), 1, 0)

    def one_block(args):
        qb, iqb, iwb, qpb = args
        q_chunk = qpb // CHUNK
        admiss = k_chunk[None, :] <= q_chunk[:, None]
        dots = jnp.einsum('bthd,bsd->bths', iqb.astype(F32), ik_f)
        score = jnp.einsum('bth,bths->bts', iwb.astype(F32), jax.nn.relu(dots)) * IDX_SCALE
        score = jnp.where(admiss[None], score, -jnp.inf)
        _, sel = lax.top_k(score, topk)
        valid = k_chunk[sel] <= q_chunk[None, :, None]
        kg = jax.vmap(lambda kb, ib: kb[ib])(k_all, sel)
        vg = jax.vmap(lambda vb, ib: vb[ib])(v_all, sel)
        qg = qb.reshape(B, blk, A_KV_HEADS, A_GROUP, A_HEAD_DIM).astype(F32)
        logits = jnp.einsum('btngd,btsnd->btngs', qg, kg.astype(F32)) * A_SCALE
        logits = jnp.where(valid[:, :, None, None, :], logits, -jnp.inf)
        p = jax.nn.softmax(logits, axis=-1)
        o = jnp.einsum('btngs,btsnd->btngd', p, vg.astype(F32))
        return o.reshape(B, blk, A_WIDTH)

    out = lax.map(one_block, (to_blocks(q), to_blocks(iq), to_blocks(iw), q_pos.reshape(nblk, blk)))
    return jnp.moveaxis(out, 0, 1).reshape(B, T, A_WIDTH)


def _rwkv7(z, z_prev, S0, mu, w0, w2, a0, a2, g2, k_k, k_a, r_k, lnx_w, lnx_b):
    B, T, _ = z.shape
    zf = z.astype(F32)
    shifted = jnp.concatenate([z_prev.astype(F32), zf[:, :-1]], axis=1)
    zm = zf + (shifted - zf) * mu.astype(F32)
    r, k, v, wd, ad, gd = _split(zm, [R_WIDTH, R_WIDTH, R_WIDTH, DECAY_LORA, AAA_LORA, GATE_LORA])
    w_log = -jax.nn.softplus(-(w0.astype(F32) + jnp.tanh(wd) @ w2.astype(F32))) - 0.5
    decay = jnp.exp(-jnp.exp(w_log))
    a = jax.nn.sigmoid(a0.astype(F32) + ad @ a2.astype(F32))
    g = jax.nn.sigmoid(gd) @ g2.astype(F32)

    def heads(t):
        return t.reshape(B, T, R_HEADS, R_HEAD_DIM)

    kk = heads(k * k_k.astype(F32))
    kk = kk / jnp.maximum(jnp.sqrt(jnp.sum(kk * kk, axis=-1, keepdims=True)), 1e-12)
    k = k * (1.0 + (a - 1.0) * k_a.astype(F32))
    r_h, k_h, v_h, w_h, a_h = heads(r), heads(k), heads(v), heads(decay), heads(a)
    b_h = kk * a_h

    def step(S, xs):
        r_t, w_t, k_t, v_t, kk_t, b_t = xs
        sa = jnp.einsum('bhij,bhj->bhi', S, -kk_t)
        S = S * w_t[:, :, None, :] + sa[..., None] * b_t[:, :, None, :] + v_t[..., None] * k_t[:, :, None, :]
        y = jnp.einsum('bhij,bhj->bhi', S, r_t)
        return S, y

    xs = tuple(jnp.moveaxis(t, 1, 0) for t in (r_h, w_h, k_h, v_h, kk, b_h))
    S_T, y = lax.scan(step, S0.astype(F32), xs)
    y = jnp.moveaxis(y, 0, 1)
    mean = jnp.mean(y, axis=-1, keepdims=True)
    var = jnp.mean(jnp.square(y - mean), axis=-1, keepdims=True)
    y = ((y - mean) * lax.rsqrt(var + GN_EPS)).reshape(B, T, R_WIDTH)
    y = y * lnx_w.astype(F32) + lnx_b.astype(F32)
    bonus = jnp.sum(r_h * k_h * r_k.astype(F32), axis=-1, keepdims=True) * v_h
    y = (y + bonus.reshape(B, T, R_WIDTH)) * g
    return y, S_T, z[:, -1:]


def _conv_ffn(h, prev, w_in, w_conv, b_conv, w_down):
    T = h.shape[1]
    gate_pre, up = _split(h @ w_in, [D_FF, D_FF])
    full = jnp.concatenate([prev.astype(gate_pre.dtype), gate_pre], axis=1)
    conv = b_conv + full[:, 0:T] * w_conv[0]
    for j in range(1, CONV_W):
        conv = conv + full[:, j:j + T] * w_conv[j]
    act = jax.nn.silu(conv) * up
    return act @ w_down, full[:, -(CONV_W - 1):]


def _trunk(x, past_k, past_v, past_ik, S0, shift0, conv0,
           norm_mix_g, w_in, rwkv_mu, rwkv_w0, rwkv_w2, rwkv_a0, rwkv_a2, rwkv_g2,
           rwkv_k_k, rwkv_k_a, rwkv_r_k, rwkv_lnx_w, rwkv_lnx_b, w_out,
           norm_ffn_g, ffn_w_in, ffn_conv_w, ffn_conv_b, ffn_w_down, norm_final_g):
    B, T = x.shape[0], x.shape[1]
    P = past_k.shape[2]
    q_pos = P + jnp.arange(T, dtype=jnp.int32)
    k_pos = jnp.arange(P + T, dtype=jnp.int32)
    new_k, new_v, new_ik, new_S, new_shift, new_conv = [], [], [], [], [], []
    for l in range(DEPTH):
        h = _rmsnorm(x, norm_mix_g[l])
        z = h @ w_in[l]
        q, k, v, iq, ik, iw, zr = _split(z, [A_WIDTH, KV_W, KV_W, IQ_W, IDX_DIM, IDX_HEADS, RWKV_COLS])
        q = _rope_partial(q.reshape(B, T, A_HEADS, A_HEAD_DIM), q_pos)
        k = _rope_partial(k.reshape(B, T, A_KV_HEADS, A_HEAD_DIM), q_pos)
        v = v.reshape(B, T, A_KV_HEADS, A_HEAD_DIM)
        iq = _rope_partial(iq.reshape(B, T, IDX_HEADS, IDX_DIM), q_pos)
        ik = _rope_partial(ik[:, :, None, :], q_pos)[:, :, 0, :]
        k_all = jnp.concatenate([past_k[l].astype(k.dtype), k], axis=1)
        v_all = jnp.concatenate([past_v[l].astype(v.dtype), v], axis=1)
        ik_all = jnp.concatenate([past_ik[l].astype(ik.dtype), ik], axis=1)
        attn = _dsa_attend(q, iq, iw, k_all, v_all, ik_all, q_pos, k_pos)
        rw, S_T, shift_T = _rwkv7(zr, shift0[l], S0[l], rwkv_mu[l], rwkv_w0[l], rwkv_w2[l], rwkv_a0[l],
                                  rwkv_a2[l], rwkv_g2[l], rwkv_k_k[l], rwkv_k_a[l], rwkv_r_k[l],
                                  rwkv_lnx_w[l], rwkv_lnx_b[l])
        mix = jnp.concatenate([attn, rw], axis=-1).astype(x.dtype) @ w_out[l]
        x = x + mix
        f, conv_T = _conv_ffn(_rmsnorm(x, norm_ffn_g[l]), conv0[l], ffn_w_in[l], ffn_conv_w[l],
                              ffn_conv_b[l], ffn_w_down[l])
        x = x + f
        new_k.append(k); new_v.append(v); new_ik.append(ik)
        new_S.append(S_T); new_shift.append(shift_T); new_conv.append(conv_T)
    y = _rmsnorm(x, norm_final_g)
    return y, (jnp.stack(new_k), jnp.stack(new_v), jnp.stack(new_ik),
               jnp.stack(new_S), jnp.stack(new_shift), jnp.stack(new_conv))


def setup_inputs(seed: int = 0) -> dict:
    key = jax.random.key(seed)
    ks = jax.random.split(key, 32)

    def nrm(k, shape, s=1.0):
        return s * jax.random.normal(k, shape, F32)

    L = DEPTH
    return {
        "x_prompt": nrm(ks[0], (BATCH, SEQ, D_MODEL)),
        "x_sample": nrm(ks[1], (DEC_BATCH, DEC_SEQ, D_MODEL)),
        "cache_k": nrm(ks[2], (L, DEC_BATCH, PAST_LEN, A_KV_HEADS, A_HEAD_DIM)),
        "cache_v": nrm(ks[3], (L, DEC_BATCH, PAST_LEN, A_KV_HEADS, A_HEAD_DIM)),
        "cache_idx_k": nrm(ks[4], (L, DEC_BATCH, PAST_LEN, IDX_DIM)),
        "state_rwkv": nrm(ks[5], (L, DEC_BATCH, R_HEADS, R_HEAD_DIM, R_HEAD_DIM), 0.3),
        "state_rwkv_shift": nrm(ks[6], (L, DEC_BATCH, 1, RWKV_COLS)),
        "state_ffn_conv": nrm(ks[7], (L, DEC_BATCH, CONV_W - 1, D_FF)),
        "norm_mix_g": 1.0 + nrm(ks[8], (L, D_MODEL), 0.02),
        "w_in": nrm(ks[9], (L, D_MODEL, IN_COLS), D_MODEL ** -0.5),
        "rwkv_mu": jax.random.uniform(ks[10], (L, RWKV_COLS), F32, 0.1, 0.9),
        "rwkv_w0": jax.random.uniform(ks[11], (L, R_WIDTH), F32, -6.0, -1.0),
        "rwkv_w2": nrm(ks[12], (L, DECAY_LORA, R_WIDTH), 0.5 * DECAY_LORA ** -0.5),
        "rwkv_a0": nrm(ks[13], (L, R_WIDTH), 0.1),
        "rwkv_a2": nrm(ks[14], (L, AAA_LORA, R_WIDTH), AAA_LORA ** -0.5),
        "rwkv_g2": nrm(ks[15], (L, GATE_LORA, R_WIDTH), GATE_LORA ** -0.5),
        "rwkv_k_k": 0.85 + nrm(ks[16], (L, R_WIDTH), 0.02),
        "rwkv_k_a": 1.0 + nrm(ks[17], (L, R_WIDTH), 0.02),
        "rwkv_r_k": nrm(ks[18], (L, R_HEADS, R_HEAD_DIM), 0.1),
        "rwkv_lnx_w": 1.0 + nrm(ks[19], (L, R_WIDTH), 0.02),
        "rwkv_lnx_b": nrm(ks[20], (L, R_WIDTH), 0.02),
        "w_out": nrm(ks[21], (L, D_MODEL, D_MODEL), D_MODEL ** -0.5),
        "norm_ffn_g": 1.0 + nrm(ks[22], (L, D_MODEL), 0.02),
        "ffn_w_in": nrm(ks[23], (L, D_MODEL, 2 * D_FF), D_MODEL ** -0.5),
        "ffn_conv_w": nrm(ks[24], (L, CONV_W, D_FF), 0.5),
        "ffn_conv_b": nrm(ks[25], (L, D_FF), 0.02),
        "ffn_w_down": nrm(ks[26], (L, D_FF, D_MODEL), D_FF ** -0.5),
        "norm_final_g": 1.0 + nrm(ks[27], (D_MODEL,), 0.02),
    }


def reference(x_prompt, x_sample, cache_k, cache_v, cache_idx_k, state_rwkv, state_rwkv_shift,
              state_ffn_conv, norm_mix_g, w_in, rwkv_mu, rwkv_w0, rwkv_w2, rwkv_a0, rwkv_a2, rwkv_g2,
              rwkv_k_k, rwkv_k_a, rwkv_r_k, rwkv_lnx_w, rwkv_lnx_b, w_out, norm_ffn_g, ffn_w_in,
              ffn_conv_w, ffn_conv_b, ffn_w_down, norm_final_g):
    weights = (norm_mix_g, w_in, rwkv_mu, rwkv_w0, rwkv_w2, rwkv_a0, rwkv_a2, rwkv_g2,
               rwkv_k_k, rwkv_k_a, rwkv_r_k, rwkv_lnx_w, rwkv_lnx_b, w_out,
               norm_ffn_g, ffn_w_in, ffn_conv_w, ffn_conv_b, ffn_w_down, norm_final_g)
    Bp = x_prompt.shape[0]
    dt = x_prompt.dtype
    y_prompt, (k_p, v_p, ik_p, S_p, sh_p, cv_p) = _trunk(
        x_prompt,
        jnp.zeros((DEPTH, Bp, 0, A_KV_HEADS, A_HEAD_DIM), dt),
        jnp.zeros((DEPTH, Bp, 0, A_KV_HEADS, A_HEAD_DIM), dt),
        jnp.zeros((DEPTH, Bp, 0, IDX_DIM), dt),
        jnp.zeros((DEPTH, Bp, R_HEADS, R_HEAD_DIM, R_HEAD_DIM), F32),
        jnp.zeros((DEPTH, Bp, 1, RWKV_COLS), dt),
        jnp.zeros((DEPTH, Bp, CONV_W - 1, D_FF), dt),
        *weights)
    y_sample, (k_s, v_s, ik_s, S_s, sh_s, cv_s) = _trunk(
        x_sample, cache_k, cache_v, cache_idx_k, state_rwkv, state_rwkv_shift, state_ffn_conv, *weights)
    return (y_prompt, y_sample, k_p, v_p, ik_p, S_p, sh_p, cv_p, k_s, v_s, ik_s, S_s, sh_s, cv_s)
```

```python
import functools

import jax
import jax.numpy as jnp
from jax import lax
from jax.experimental import pallas as pl
from jax.experimental.pallas import tpu as pltpu

F32 = jnp.float32
BF16 = jnp.bfloat16
I32 = jnp.int32

CHUNK = 64
A_HEADS = 16
A_KV_HEADS = 4
A_HEAD_DIM = 128
A_GROUP = A_HEADS // A_KV_HEADS
A_WIDTH = A_HEADS * A_HEAD_DIM
KV_W = A_KV_HEADS * A_HEAD_DIM
IDX_HEADS = 16
IDX_DIM = 64
IQ_W = IDX_HEADS * IDX_DIM
TOPK_MAX = 256
ROPE_THETA = 500000.0
ROPE_FRAC = 4
A_SCALE = A_HEAD_DIM ** -0.5
IDX_SCALE = (IDX_HEADS ** -0.5) * (IDX_DIM ** -0.5)
R_HEAD_DIM = 64
R_WIDTH = 2048
R_HEADS = R_WIDTH // R_HEAD_DIM
R_PAIRS = R_HEADS // 2
DECAY_LORA = 96
AAA_LORA = 96
GATE_LORA = 256
RWKV_COLS = 3 * R_WIDTH + DECAY_LORA + AAA_LORA + GATE_LORA
GN_EPS = 6.4e-4
CONV_W = 3
RMS_EPS = 1e-6

LANES = 128
VMEM_LIMIT = 56 * 1024 * 1024

ATT_Q0, ATT_K0, ATT_V0, ATT_IQ0, ATT_IK0 = 0, A_WIDTH, A_WIDTH + KV_W, A_WIDTH + 2 * KV_W, A_WIDTH + 2 * KV_W + IQ_W
ATT_USED = ATT_IK0 + IDX_DIM + IDX_HEADS
ATT_COLS = 4608
LORA_PAD = 128
RW_WD0 = 3 * R_WIDTH
RW_AD0 = RW_WD0 + LORA_PAD
RW_GD0 = RW_AD0 + LORA_PAD
RW_COLS = RW_GD0 + GATE_LORA
INT_MIN = -2 ** 31
NEG_BIG = -1e30
HIGHEST = lax.Precision.HIGHEST


def _cparams(sem):
    return pltpu.CompilerParams(dimension_semantics=sem, vmem_limit_bytes=VMEM_LIMIT)


def _rmsnorm_kernel(x_ref, g_ref, o_ref):
    x = x_ref[...]
    y = x * lax.rsqrt(jnp.mean(x * x, axis=-1, keepdims=True) + RMS_EPS)
    o_ref[...] = (y * g_ref[...]).astype(o_ref.dtype)


def _rmsnorm(x, g, out_dtype):
    m, d = x.shape
    tm = min(m, 256)
    return pl.pallas_call(
        _rmsnorm_kernel,
        out_shape=jax.ShapeDtypeStruct((m, d), out_dtype),
        grid=(m // tm,),
        in_specs=[pl.BlockSpec((tm, d), lambda i: (i, 0)), pl.BlockSpec((1, d), lambda i: (0, 0))],
        out_specs=pl.BlockSpec((tm, d), lambda i: (i, 0)),
        compiler_params=_cparams(("parallel",)),
        name="rmsnorm",
    )(x, g.reshape(1, d).astype(F32))


def _mm_kernel(*refs, n_pairs, has_res):
    o_ref = refs[-1]
    acc = jnp.dot(refs[0][...], refs[n_pairs][...], preferred_element_type=F32)
    for p in range(1, n_pairs):
        acc = acc + jnp.dot(refs[p][...], refs[n_pairs + p][...], preferred_element_type=F32)
    if has_res:
        acc = refs[2 * n_pairs][...] + acc
    o_ref[...] = acc.astype(o_ref.dtype)


def _matmul(a_list, b_list, res=None, tm=512, tn=512, name="matmul"):
    m = a_list[0].shape[0]
    n = b_list[0].shape[1]
    tm = min(tm, m)
    tn = min(tn, n)
    assert m % tm == 0 and n % tn == 0, (m, n, tm, tn)
    in_specs = [pl.BlockSpec((tm, a.shape[1]), lambda i, j: (i, 0)) for a in a_list]
    in_specs += [pl.BlockSpec((b.shape[0], tn), lambda i, j: (0, j)) for b in b_list]
    args = list(a_list) + list(b_list)
    if res is not None:
        in_specs.append(pl.BlockSpec((tm, tn), lambda i, j: (i, j)))
        args.append(res)
    return pl.pallas_call(
        functools.partial(_mm_kernel, n_pairs=len(a_list), has_res=res is not None),
        out_shape=jax.ShapeDtypeStruct((m, n), F32),
        grid=(m // tm, n // tn),
        in_specs=in_specs,
        out_specs=pl.BlockSpec((tm, tn), lambda i, j: (i, j)),
        compiler_params=_cparams(("parallel", "arbitrary")),
        name=name,
    )(*args)


def _rope_tile(x, cos, sin, half, d_in_head):
    lo = d_in_head < half
    hi = (d_in_head >= half) & (d_in_head < 2 * half)
    c = jnp.where(lo | hi, cos, 1.0)
    s_up = jnp.where(lo, -sin, 0.0)
    s_dn = jnp.where(hi, sin, 0.0)
    x_up = pltpu.roll(x, LANES - half, axis=1)
    x_dn = pltpu.roll(x, half, axis=1)
    return x * c + x_up * s_up + x_dn * s_dn


def _rope_kernel(z_ref, invf_ref, q_ref, kf_ref, kb_ref, vf_ref, vb_ref, iq_ref, ikw_ref, ikb_ref, *, tm, t_len, pos0):
    i = pl.program_id(0)
    row = lax.broadcasted_iota(I32, (tm, LANES), 0) + i * tm
    pos = (pos0 + lax.rem(row, t_len)).astype(F32)
    lane = lax.broadcasted_iota(I32, (tm, LANES), 1)
    ang = pos * invf_ref[0:1, :]
    cos_a, sin_a = jnp.cos(ang), jnp.sin(ang)
    half_a = A_HEAD_DIM // ROPE_FRAC // 2
    for h in range(A_HEADS):
        x = z_ref[:, ATT_Q0 + h * LANES:ATT_Q0 + (h + 1) * LANES]
        q_ref[:, h * LANES:(h + 1) * LANES] = (_rope_tile(x, cos_a, sin_a, half_a, lane) * A_SCALE).astype(q_ref.dtype)
    for h in range(A_KV_HEADS):
        x = z_ref[:, ATT_K0 + h * LANES:ATT_K0 + (h + 1) * LANES]
        y = _rope_tile(x, cos_a, sin_a, half_a, lane)
        kf_ref[:, h * LANES:(h + 1) * LANES] = y
        kb_ref[:, h * LANES:(h + 1) * LANES] = y.astype(kb_ref.dtype)
    v = z_ref[:, ATT_V0:ATT_V0 + KV_W]
    vf_ref[...] = v
    vb_ref[...] = v.astype(vb_ref.dtype)
    ang = pos * invf_ref[1:2, :]
    cos_i, sin_i = jnp.cos(ang), jnp.sin(ang)
    half_i = IDX_DIM // ROPE_FRAC // 2
    d_i = lane & (IDX_DIM - 1)
    for h in range(IQ_W // LANES):
        x = z_ref[:, ATT_IQ0 + h * LANES:ATT_IQ0 + (h + 1) * LANES]
        iq_ref[:, h * LANES:(h + 1) * LANES] = _rope_tile(x, cos_i, sin_i, half_i, d_i).astype(iq_ref.dtype)
    x = z_ref[:, ATT_IK0:ATT_IK0 + LANES]
    d_k = jnp.where(lane < IDX_DIM, lane, IDX_DIM)
    y = _rope_tile(x, cos_i, sin_i, half_i, d_k)
    ikw_ref[...] = y
    ikb_ref[...] = y[:, :IDX_DIM].astype(ikb_ref.dtype)


def _rope_split(z_att, t_len, pos0):
    m = z_att.shape[0]
    tm = min(m, 256)
    lane = jnp.arange(LANES)
    rd_a = A_HEAD_DIM // ROPE_FRAC
    rd_i = IDX_DIM // ROPE_FRAC
    invf_a = ROPE_THETA ** (-((lane % (rd_a // 2)).astype(F32) * 2.0 / rd_a))
    invf_i = ROPE_THETA ** (-((lane % (rd_i // 2)).astype(F32) * 2.0 / rd_i))
    invf = jnp.zeros((8, LANES), F32).at[0].set(invf_a).at[1].set(invf_i)
    row_spec = lambda w: pl.BlockSpec((tm, w), lambda i: (i, 0))
    shp = lambda w, dt: jax.ShapeDtypeStruct((m, w), dt)
    return pl.pallas_call(
        functools.partial(_rope_kernel, tm=tm, t_len=t_len, pos0=pos0),
        out_shape=(shp(A_WIDTH, BF16), shp(KV_W, F32), shp(KV_W, BF16), shp(KV_W, F32), shp(KV_W, BF16),
                   shp(IQ_W, BF16), shp(LANES, F32), shp(IDX_DIM, BF16)),
        grid=(m // tm,),
        in_specs=[row_spec(ATT_COLS), pl.BlockSpec((8, LANES), lambda i: (0, 0))],
        out_specs=(row_spec(A_WIDTH), row_spec(KV_W), row_spec(KV_W), row_spec(KV_W), row_spec(KV_W),
                   row_spec(IQ_W), row_spec(LANES), row_spec(IDX_DIM)),
        compiler_params=_cparams(("parallel",)),
        name="rope_split",
    )(z_att, invf)


def _sortable(score):
    u = lax.bitcast_convert_type(score, I32)
    return jnp.where(u < 0, u ^ jnp.int32(0x7FFFFFFF), u)


def _dsa_kernel(q_ref, iq_ref, ikw_ref, k_ref, v_ref, ik_ref, o_ref,
                key_ref, iwb_ref, qs_ref, m_ref, l_ref, acc_ref, *, tq, tk, pos0, n_keys, topk):
    i = pl.program_id(1)
    q0 = pos0 + i * tq
    kmax = jnp.minimum((lax.div(q0 + tq - 1, CHUNK) + 1) * CHUNK, n_keys)
    nkb = lax.div(kmax + tk - 1, tk)
    n_rep = tk // LANES
    nt_dims = (((1,), (1,)), ((), ()))

    for h in range(IDX_HEADS):
        iwb_ref[h] = jnp.broadcast_to(ikw_ref[:, IDX_DIM + h:IDX_DIM + h + 1], (tq, LANES))
    for h in range(A_HEADS):
        qs_ref[h * tq:(h + 1) * tq, :] = q_ref[:, h * LANES:(h + 1) * LANES]

    def score_block(kb, carry):
        koff = pl.multiple_of(kb * tk, tk)
        ikb = ik_ref[pl.ds(koff, tk), :]
        acc = jnp.zeros((tq, tk), F32)
        for h in range(IDX_HEADS):
            d = lax.dot_general(iq_ref[:, h * IDX_DIM:(h + 1) * IDX_DIM], ikb, nt_dims, preferred_element_type=F32)
            acc = acc + jnp.maximum(d, 0.0) * pltpu.repeat(iwb_ref[h], n_rep, axis=1)
        kpos = koff + lax.broadcasted_iota(I32, (tq, tk), 1)
        qpos = q0 + lax.broadcasted_iota(I32, (tq, tk), 0)
        adm = (lax.shift_right_logical(kpos, 6) <= lax.shift_right_logical(qpos, 6)) & (kpos < n_keys)
        key_ref[:, pl.ds(koff, tk)] = jnp.where(adm, _sortable(acc * IDX_SCALE), jnp.int32(INT_MIN))
        return carry

    lax.fori_loop(0, nkb, score_block, 0)

    def count(pred):
        def body(kb, cnt):
            koff = pl.multiple_of(kb * tk, tk)
            keys = key_ref[:, pl.ds(koff, tk)]
            kpos = koff + lax.broadcasted_iota(I32, (tq, tk), 1)
            hit = jnp.where(pred(keys, kpos), 1.0, 0.0)
            for c in range(n_rep):
                cnt = cnt + hit[:, c * LANES:(c + 1) * LANES]
            return cnt
        cnt = lax.fori_loop(0, nkb, body, jnp.zeros((tq, LANES), F32))
        return jnp.broadcast_to(jnp.sum(cnt, axis=-1, keepdims=True), (tq, LANES))

    def wide(x):
        return pltpu.repeat(x, n_rep, axis=1)

    def bit_step(it, tu):
        cand_u = tu | lax.shift_left(jnp.int32(1), 31 - it)
        cand_s = wide(cand_u ^ jnp.int32(INT_MIN))
        cnt = count(lambda keys, kpos: keys >= cand_s)
        return jnp.where(cnt >= topk, cand_u, tu)

    tu = lax.fori_loop(0, 32, bit_step, jnp.zeros((tq, LANES), I32))
    thr = jnp.maximum(tu ^ jnp.int32(INT_MIN), jnp.int32(INT_MIN + 1))
    thr_w = wide(thr)

    n_ge = count(lambda keys, kpos: keys >= thr_w)
    n_gt = count(lambda keys, kpos: keys > thr_w)
    excess = n_ge > topk

    @pl.when(jnp.max(jnp.where(excess, 1.0, 0.0)) > 0.0)
    def _():
        need = topk - n_gt

        idx_bits = int(key_ref.shape[1]).bit_length()

        def idx_step(it, jm):
            cand = wide(jm | lax.shift_left(jnp.int32(1), idx_bits - 1 - it))
            cnt = count(lambda keys, kpos: (keys == thr_w) & (kpos < cand))
            return jnp.where(cnt < need, cand[:, :LANES], jm)

        jm = lax.fori_loop(0, idx_bits, idx_step, jnp.zeros((tq, LANES), I32))
        jm_w = wide(jnp.where(excess, jm, jnp.int32(2 ** 31 - 1)))

        def drop(kb, carry):
            koff = pl.multiple_of(kb * tk, tk)
            keys = key_ref[:, pl.ds(koff, tk)]
            kpos = koff + lax.broadcasted_iota(I32, (tq, tk), 1)
            key_ref[:, pl.ds(koff, tk)] = jnp.where((keys == thr_w) & (kpos > jm_w), jnp.int32(INT_MIN), keys)
            return carry

        lax.fori_loop(0, nkb, drop, 0)

    m_ref[...] = jnp.full(m_ref.shape, NEG_BIG, F32)
    l_ref[...] = jnp.zeros(l_ref.shape, F32)
    acc_ref[...] = jnp.zeros(acc_ref.shape, F32)
    rows = A_GROUP * tq

    def attend(kb, carry):
        koff = pl.multiple_of(kb * tk, tk)
        sel = key_ref[:, pl.ds(koff, tk)] >= thr_w
        masked = lambda x, fill: jnp.concatenate(
            [jnp.where(sel, x[g * tq:(g + 1) * tq], fill) for g in range(A_GROUP)], axis=0)
        for n in range(A_KV_HEADS):
            r0 = n * rows
            kn = k_ref[pl.ds(koff, tk), n * LANES:(n + 1) * LANES]
            vn = v_ref[pl.ds(koff, tk), n * LANES:(n + 1) * LANES]
            s = masked(lax.dot_general(qs_ref[r0:r0 + rows, :], kn, nt_dims, preferred_element_type=F32), NEG_BIG)
            m_prev = m_ref[r0:r0 + rows, :]
            m_new = jnp.maximum(m_prev, jnp.max(s, axis=-1, keepdims=True))
            alpha = jnp.exp(m_prev - m_new)
            p = masked(jnp.exp(s - pltpu.repeat(m_new, n_rep, axis=1)), 0.0)
            l_ref[r0:r0 + rows, :] = alpha * l_ref[r0:r0 + rows, :] + jnp.sum(p, axis=-1, keepdims=True)
            acc_ref[r0:r0 + rows, :] = alpha * acc_ref[r0:r0 + rows, :] + jnp.dot(
                p.astype(vn.dtype), vn, preferred_element_type=F32)
            m_ref[r0:r0 + rows, :] = m_new
        return carry

    lax.fori_loop(0, nkb, attend, 0)
    for h in range(A_HEADS):
        o_ref[:, h * LANES:(h + 1) * LANES] = (
            acc_ref[h * tq:(h + 1) * tq, :] / l_ref[h * tq:(h + 1) * tq, :]).astype(o_ref.dtype)


def _dsa(q_bf, iq_bf, ikw, k_all, v_all, ik_all, *, n_batch, t_len, pos0, n_keys):
    lp = k_all.shape[1]
    tq = min(t_len, 128)
    tk = min(lp, 512)
    assert lp % tk == 0 and t_len % tq == 0
    nq = t_len // tq
    topk = min(TOPK_MAX, n_keys // 4)
    qrow = lambda w: pl.BlockSpec((tq, w), lambda b, i: (b * nq + i, 0))
    kv_spec = lambda w: pl.BlockSpec((None, lp, w), lambda b, i: (b, 0, 0), pipeline_mode=pl.Buffered(1))
    return pl.pallas_call(
        functools.partial(_dsa_kernel, tq=tq, tk=tk, pos0=pos0, n_keys=n_keys, topk=float(topk)),
        out_shape=jax.ShapeDtypeStruct((n_batch * t_len, A_WIDTH), BF16),
        grid=(n_batch, nq),
        in_specs=[qrow(A_WIDTH), qrow(IQ_W), qrow(LANES), kv_spec(KV_W), kv_spec(KV_W), kv_spec(IDX_DIM)],
        out_specs=qrow(A_WIDTH),
        scratch_shapes=[
            pltpu.VMEM((tq, lp), I32),
            pltpu.VMEM((IDX_HEADS, tq, LANES), F32),
            pltpu.VMEM((A_HEADS * tq, LANES), BF16),
            pltpu.VMEM((A_HEADS * tq, LANES), F32),
            pltpu.VMEM((A_HEADS * tq, LANES), F32),
            pltpu.VMEM((A_HEADS * tq, LANES), F32),
        ],
        compiler_params=_cparams(("parallel", "arbitrary")),
        name="dsa",
    )(q_bf, iq_bf, ikw, k_all, v_all, ik_all)


def _head_sums(x, ones_bd):
    n = x.shape[1] // LANES
    tm = x.shape[0]
    stacked = jnp.concatenate([x[:, c * LANES:(c + 1) * LANES] for c in range(n)], axis=0)
    s = jnp.dot(stacked, ones_bd, precision=HIGHEST, preferred_element_type=F32)
    return jnp.concatenate([s[c * tm:(c + 1) * tm, :] for c in range(n)], axis=1)


def _rwkv_pre_kernel(z_ref, zp_ref, z0_ref, mu_ref, w0_ref, a0_ref, kk_ref, ka_ref, w2_ref, a2_ref, g2_ref,
                     ones_ref, r_out, w_out, k_out, v_out, nkk_out, b_out, g_out, *, tm):
    i = pl.program_id(1)
    row = lax.broadcasted_iota(I32, (tm, 1), 0)

    def mixed(c0, width):
        z = z_ref[:, c0:c0 + width]
        first = jnp.where(i == 0, z0_ref[:, c0:c0 + width], zp_ref[7:8, c0:c0 + width])
        shifted = jnp.where(row == 0, first, pltpu.roll(z, 1, axis=0))
        return z + (shifted - z) * mu_ref[:, c0:c0 + width]

    r = mixed(0, R_WIDTH)
    k = mixed(R_WIDTH, R_WIDTH)
    v = mixed(2 * R_WIDTH, R_WIDTH)
    wd = mixed(RW_WD0, LORA_PAD)
    ad = mixed(RW_AD0, LORA_PAD)
    gd = mixed(RW_GD0, GATE_LORA)
    lora = lambda x, w_ref: jnp.dot(x, w_ref[...], precision=HIGHEST, preferred_element_type=F32)
    y = -(w0_ref[...] + lora(jnp.tanh(wd), w2_ref))
    softplus = jnp.maximum(y, 0.0) + jnp.log(1.0 + jnp.exp(-jnp.abs(y)))
    decay = jnp.exp(-jnp.exp(-softplus - 0.5))
    a = jax.nn.sigmoid(a0_ref[...] + lora(ad, a2_ref))
    g = lora(jax.nn.sigmoid(gd), g2_ref)
    kk = k * kk_ref[...]
    kk = kk / jnp.maximum(jnp.sqrt(_head_sums(kk * kk, ones_ref[...])), 1e-12)
    r_out[...] = r
    w_out[...] = decay
    k_out[...] = k * (1.0 + (a - 1.0) * ka_ref[...])
    v_out[...] = v
    nkk_out[...] = -kk
    b_out[...] = kk * a
    g_out[...] = g


def _rwkv_pre(z_rw, z0, mu, w0, a0, k_k, k_a, w2, a2, g2, ones_bd):
    nb, t_len, _ = z_rw.shape
    tm = min(t_len, 128)
    zrow = pl.BlockSpec((None, tm, RW_COLS), lambda b, i: (b, i, 0))
    zprev = pl.BlockSpec((None, 8, RW_COLS), lambda b, i: (b, jnp.maximum(i * (tm // 8) - 1, 0), 0))
    full = lambda a: pl.BlockSpec(a.shape, lambda b, i: (0,) * a.ndim)
    orow = pl.BlockSpec((None, tm, R_WIDTH), lambda b, i: (b, i, 0))
    params = (mu, w0, a0, k_k, k_a, w2, a2, g2, ones_bd)
    return pl.pallas_call(
        functools.partial(_rwkv_pre_kernel, tm=tm),
        out_shape=tuple(jax.ShapeDtypeStruct((nb, t_len, R_WIDTH), F32) for _ in range(7)),
        grid=(nb, t_len // tm),
        in_specs=[zrow, zprev, pl.BlockSpec((None, 1, RW_COLS), lambda b, i: (b, 0, 0))] + [full(p) for p in params],
        out_specs=tuple(orow for _ in range(7)),
        compiler_params=_cparams(("parallel", "arbitrary")),
        name="rwkv_pre",
    )(z_rw, z_rw, z0, *params)


def _split3(x):
    hi = x.astype(BF16)
    r1 = x - hi.astype(F32)
    mid = r1.astype(BF16)
    lo = (r1 - mid.astype(F32)).astype(BF16)
    return hi, mid, lo


def _dot3(x, w):
    hi, mid, lo = _split3(x)
    d = lambda t: jnp.dot(t, w, preferred_element_type=F32)
    return d(hi) + d(mid) + d(lo)


def _rwkv_scan_kernel(r_ref, w_ref, k_ref, nkk_ref, b_ref, vx_ref, s0_ref, ones_ref, sel_ref, selt_ref,
                      y_ref, st_ref, s_ref, *, tb):
    tblk = pl.program_id(1)

    @pl.when(tblk == 0)
    def _():
        s_ref[...] = s0_ref[...]

    y_ref[...] = jnp.zeros(y_ref.shape, F32)
    lane_t = lax.rem(lax.broadcasted_iota(I32, (R_HEAD_DIM, 2 * tb), 1), tb)
    ones_bd = ones_ref[...]
    sel = sel_ref[...]
    selt = selt_ref[...]
    cat = lambda xs: jnp.concatenate(xs, axis=0)
    rowp = lambda ref, t, p: ref[t, p:p + 1, :]

    def step(t, carry):
        hit = lane_t == t
        sa = _dot3(cat([s_ref[p] * rowp(nkk_ref, t, p) for p in range(R_PAIRS)]), ones_bd)
        vb = _dot3(cat([jnp.where(hit, vx_ref[p], 0.0) for p in range(R_PAIRS)]), sel)
        prods = []
        for p in range(R_PAIRS):
            rs = slice(p * R_HEAD_DIM, (p + 1) * R_HEAD_DIM)
            s_new = s_ref[p] * rowp(w_ref, t, p) + sa[rs] * rowp(b_ref, t, p) + vb[rs] * rowp(k_ref, t, p)
            s_ref[p] = s_new
            prods.append(s_new * rowp(r_ref, t, p))
        yb = _dot3(cat(prods), selt)
        for p in range(R_PAIRS):
            rs = slice(p * R_HEAD_DIM, (p + 1) * R_HEAD_DIM)
            y_ref[p] = jnp.where(hit, yb[rs], y_ref[p])
        return carry

    lax.fori_loop(0, tb, step, 0)

    @pl.when(tblk == pl.num_programs(1) - 1)
    def _():
        st_ref[...] = s_ref[...]


def _rwkv_scan(r, w, k, nkk, b, v, s0):
    nb, t_len, _ = r.shape
    tb = min(t_len, 64)
    nblk = t_len // tb
    hd = R_HEAD_DIM
    vx = v.reshape(nb, nblk, tb, R_PAIRS, 2, hd).transpose(0, 1, 3, 5, 4, 2).reshape(nb, nblk, R_PAIRS, hd, 2 * tb)
    s0p = s0.reshape(nb, R_PAIRS, 2, hd, hd).transpose(0, 1, 3, 2, 4).reshape(nb, R_PAIRS, hd, LANES)
    lane_h = jnp.arange(LANES) // hd
    ones_bd = (lane_h[:, None] == lane_h[None, :]).astype(BF16)
    sel = ((jnp.arange(2 * tb) // tb)[:, None] == lane_h[None, :]).astype(BF16)
    selt = sel.T
    trow = pl.BlockSpec((None, tb, R_PAIRS, LANES), lambda bb, i: (bb, i, 0, 0))
    r, w, k, nkk, b = (a.reshape(nb, t_len, R_PAIRS, LANES) for a in (r, w, k, nkk, b))
    col = pl.BlockSpec((None, None, R_PAIRS, hd, 2 * tb), lambda bb, i: (bb, i, 0, 0, 0))
    st = pl.BlockSpec((None, R_PAIRS, hd, LANES), lambda bb, i: (bb, 0, 0, 0))
    full = lambda a: pl.BlockSpec(a.shape, lambda bb, i: (0,) * a.ndim)
    ycol, s_t = pl.pallas_call(
        functools.partial(_rwkv_scan_kernel, tb=tb),
        out_shape=(jax.ShapeDtypeStruct((nb, nblk, R_PAIRS, hd, 2 * tb), F32),
                   jax.ShapeDtypeStruct((nb, R_PAIRS, hd, LANES), F32)),
        grid=(nb, nblk),
        in_specs=[trow, trow, trow, trow, trow, col, st, full(ones_bd), full(sel), full(selt)],
        out_specs=(col, st),
        scratch_shapes=[pltpu.VMEM((R_PAIRS, hd, LANES), F32)],
        compiler_params=_cparams(("parallel", "arbitrary")),
        name="rwkv_scan",
    )(r, w, k, nkk, b, vx, s0p, ones_bd, sel, selt)
    y = ycol.reshape(nb, nblk, R_PAIRS, hd, 2, tb).transpose(0, 1, 5, 2, 4, 3).reshape(nb, t_len, R_WIDTH)
    s_t = s_t.reshape(nb, R_PAIRS, hd, 2, hd).transpose(0, 1, 3, 2, 4).reshape(nb, R_HEADS, hd, hd)
    return y, s_t


def _rwkv_post_kernel(y_ref, r_ref, k_ref, v_ref, g_ref, lw_ref, lb_ref, rk_ref, ones_ref, o_ref):
    ones_bd = ones_ref[...]
    y = y_ref[...]
    mean = _head_sums(y, ones_bd) * (1.0 / R_HEAD_DIM)
    d = y - mean
    var = _head_sums(d * d, ones_bd) * (1.0 / R_HEAD_DIM)
    yn = d * lax.rsqrt(var + GN_EPS) * lw_ref[...] + lb_ref[...]
    bonus = _head_sums(r_ref[...] * k_ref[...] * rk_ref[...], ones_bd) * v_ref[...]
    o_ref[...] = ((yn + bonus) * g_ref[...]).astype(o_ref.dtype)


def _rwkv_post(y, r, k, v, g, lnx_w, lnx_b, r_k, ones_bd):
    m = y.shape[0]
    tm = min(m, 128)
    row = pl.BlockSpec((tm, R_WIDTH), lambda i: (i, 0))
    full = lambda a: pl.BlockSpec(a.shape, lambda i: (0,) * a.ndim)
    params = (lnx_w, lnx_b, r_k, ones_bd)
    return pl.pallas_call(
        _rwkv_post_kernel,
        out_shape=jax.ShapeDtypeStruct((m, R_WIDTH), BF16),
        grid=(m // tm,),
        in_specs=[row] * 5 + [full(p) for p in params],
        out_specs=row,
        compiler_params=_cparams(("parallel",)),
        name="rwkv_post",
    )(y, r, k, v, g, *params)


def _ffn_gate_kernel(g_ref, u_ref, gp_ref, c0_ref, cw_ref, cb_ref, o_ref, *, tm):
    i = pl.program_id(1)
    gate = g_ref[...]
    row = lax.broadcasted_iota(I32, (tm, 1), 0)
    prev1 = jnp.where(i == 0, c0_ref[1:2, :], gp_ref[7:8, :])
    prev2 = jnp.where(i == 0, c0_ref[0:1, :], gp_ref[6:7, :])
    g_m1 = jnp.where(row == 0, prev1, pltpu.roll(gate, 1, axis=0))
    g_m2 = jnp.where(row == 0, prev2, jnp.where(row == 1, prev1, pltpu.roll(gate, 2, axis=0)))
    conv = cb_ref[...] + g_m2 * cw_ref[0:1, :]
    conv = conv + g_m1 * cw_ref[1:2, :]
    conv = conv + gate * cw_ref[2:3, :]
    o_ref[...] = (conv * jax.nn.sigmoid(conv) * u_ref[...]).astype(o_ref.dtype)


def _ffn_gate(gate_up, conv0, conv_w, conv_b, n_batch, t_len):
    d_ff = gate_up.shape[1] // 2
    tm = min(t_len, 64)
    nt = t_len // tm
    gu = gate_up.reshape(n_batch, t_len, 2 * d_ff)
    row = lambda c: pl.BlockSpec((None, tm, d_ff), lambda b, i: (b, i, c))
    prev = pl.BlockSpec((None, 8, d_ff), lambda b, i: (b, jnp.maximum(i * (tm // 8) - 1, 0), 0))
    act = pl.pallas_call(
        functools.partial(_ffn_gate_kernel, tm=tm),
        out_shape=jax.ShapeDtypeStruct((n_batch, t_len, d_ff), BF16),
        grid=(n_batch, nt),
        in_specs=[row(0), row(1), prev, pl.BlockSpec((None, CONV_W - 1, d_ff), lambda b, i: (b, 0, 0)),
                  pl.BlockSpec((CONV_W, d_ff), lambda b, i: (0, 0)), pl.BlockSpec((1, d_ff), lambda b, i: (0, 0))],
        out_specs=pl.BlockSpec((None, tm, d_ff), lambda b, i: (b, i, 0)),
        compiler_params=_cparams(("parallel", "arbitrary")),
        name="ffn_gate",
    )(gu, gu, gu, conv0, conv_w, conv_b.reshape(1, d_ff))
    return act.reshape(n_batch * t_len, d_ff)


def _pad_rw_cols(a):
    z = lambda n: jnp.zeros(a.shape[:-1] + (n,), a.dtype)
    wd0, ad0, gd0 = 3 * R_WIDTH, 3 * R_WIDTH + DECAY_LORA, 3 * R_WIDTH + DECAY_LORA + AAA_LORA
    return jnp.concatenate([a[..., :wd0], a[..., wd0:ad0], z(LORA_PAD - DECAY_LORA), a[..., ad0:gd0],
                            z(LORA_PAD - AAA_LORA), a[..., gd0:]], axis=-1)


def _unpad_rw_cols(a):
    return jnp.concatenate([a[..., :RW_WD0 + DECAY_LORA], a[..., RW_AD0:RW_AD0 + AAA_LORA], a[..., RW_GD0:]], axis=-1)


def _prep_weights(norm_mix_g, w_in, rwkv_mu, rwkv_w0, rwkv_w2, rwkv_a0, rwkv_a2, rwkv_g2, rwkv_k_k, rwkv_k_a,
                  rwkv_r_k, rwkv_lnx_w, rwkv_lnx_b, w_out, norm_ffn_g, ffn_w_in, ffn_conv_w, ffn_conv_b,
                  ffn_w_down, norm_final_g, l):
    d = w_in.shape[1]
    w_att = jnp.concatenate([w_in[l][:, :ATT_USED].astype(BF16), jnp.zeros((d, ATT_COLS - ATT_USED), BF16)], axis=1)
    w_rw = _pad_rw_cols(w_in[l][:, ATT_USED:].astype(BF16))
    row = lambda a: a.reshape(1, -1).astype(F32)
    pad_rows = lambda a, n: jnp.concatenate([a, jnp.zeros((n - a.shape[0], a.shape[1]), a.dtype)], axis=0)
    lane_h = jnp.arange(LANES) // R_HEAD_DIM
    return dict(
        norm_mix_g=norm_mix_g[l], w_att=w_att, w_rw=w_rw,
        mu=_pad_rw_cols(row(rwkv_mu[l])), w0=row(rwkv_w0[l]), a0=row(rwkv_a0[l]),
        k_k=row(rwkv_k_k[l]), k_a=row(rwkv_k_a[l]),
        w2=pad_rows(rwkv_w2[l], LORA_PAD), a2=pad_rows(rwkv_a2[l], LORA_PAD), g2=rwkv_g2[l],
        r_k=row(rwkv_r_k[l]), lnx_w=row(rwkv_lnx_w[l]), lnx_b=row(rwkv_lnx_b[l]),
        ones_bd=(lane_h[:, None] == lane_h[None, :]).astype(F32),
        w_out_a=w_out[l][:A_WIDTH].astype(BF16), w_out_r=w_out[l][A_WIDTH:].astype(BF16),
        norm_ffn_g=norm_ffn_g[l], ffn_w_in=ffn_w_in[l].astype(BF16), conv_w=ffn_conv_w[l], conv_b=ffn_conv_b[l],
        ffn_w_down=ffn_w_down[l].astype(BF16), norm_final_g=norm_final_g,
    )


def _trunk(x, past_k, past_v, past_ik, s0, shift0, conv0, wt):
    nb, t_len, d = x.shape
    m = nb * t_len
    p_len = 0 if past_k is None else past_k.shape[1]
    n_keys = p_len + t_len
    x2 = x.reshape(m, d)

    h = _rmsnorm(x2, wt["norm_mix_g"], BF16)
    z_att = _matmul([h], [wt["w_att"]], name="proj_att")
    z_rw = _matmul([h], [wt["w_rw"]], name="proj_rw")

    q_bf, k_f, k_bf, v_f, v_bf, iq_bf, ikw, ik_bf = _rope_split(z_att, t_len, p_len)
    tk = min(-(-n_keys // LANES) * LANES, 512)
    lp = -(-n_keys // tk) * tk

    def with_past(new, past, width):
        new = new.reshape(nb, t_len, width)
        parts = [new] if past is None else [past.reshape(nb, p_len, width).astype(BF16), new]
        if lp > n_keys:
            parts.append(jnp.zeros((nb, lp - n_keys, width), BF16))
        return parts[0] if len(parts) == 1 else jnp.concatenate(parts, axis=1)

    attn = _dsa(q_bf, iq_bf, ikw, with_past(k_bf, past_k, KV_W), with_past(v_bf, past_v, KV_W),
                with_past(ik_bf, past_ik, IDX_DIM), n_batch=nb, t_len=t_len, pos0=p_len, n_keys=n_keys)

    z_rw3 = z_rw.reshape(nb, t_len, RW_COLS)
    r, w, k2, v2, nkk, b, g = _rwkv_pre(z_rw3, _pad_rw_cols(shift0.astype(F32)), wt["mu"], wt["w0"], wt["a0"],
                                       wt["k_k"], wt["k_a"], wt["w2"], wt["a2"], wt["g2"], wt["ones_bd"])
    y, s_t = _rwkv_scan(r, w, k2, nkk, b, v2, s0.astype(F32))
    flat = lambda a: a.reshape(m, R_WIDTH)
    rw = _rwkv_post(flat(y), flat(r), flat(k2), flat(v2), flat(g), wt["lnx_w"], wt["lnx_b"], wt["r_k"], wt["ones_bd"])

    x1 = _matmul([attn, rw], [wt["w_out_a"], wt["w_out_r"]], res=x2, name="out_proj")
    hf = _rmsnorm(x1, wt["norm_ffn_g"], BF16)
    gate_up = _matmul([hf], [wt["ffn_w_in"]], name="ffn_in")
    d_ff = gate_up.shape[1] // 2
    act = _ffn_gate(gate_up, conv0.astype(F32), wt["conv_w"], wt["conv_b"], nb, t_len)
    x3 = _matmul([act], [wt["ffn_w_down"]], res=x1, name="ffn_down")
    y_out = _rmsnorm(x3, wt["norm_final_g"], F32).reshape(nb, t_len, d)

    gate3 = gate_up.reshape(nb, t_len, 2 * d_ff)[:, :, :d_ff]
    conv_t = jnp.concatenate([conv0.astype(F32), gate3], axis=1)[:, -(CONV_W - 1):]
    shift_t = _unpad_rw_cols(z_rw3[:, -1:])
    caches = (k_f.reshape(nb, t_len, A_KV_HEADS, A_HEAD_DIM)[None], v_f.reshape(nb, t_len, A_KV_HEADS, A_HEAD_DIM)[None],
              ikw[:, :IDX_DIM].reshape(nb, t_len, IDX_DIM)[None], s_t[None], shift_t[None], conv_t[None])
    return y_out, caches


def kernel(x_prompt, x_sample, cache_k, cache_v, cache_idx_k, state_rwkv, state_rwkv_shift, state_ffn_conv, norm_mix_g, w_in, rwkv_mu, rwkv_w0, rwkv_w2, rwkv_a0, rwkv_a2, rwkv_g2, rwkv_k_k, rwkv_k_a, rwkv_r_k, rwkv_lnx_w, rwkv_lnx_b, w_out, norm_ffn_g, ffn_w_in, ffn_conv_w, ffn_conv_b, ffn_w_down, norm_final_g):
    assert w_in.shape[0] == 1, "single-layer trunk"
    wt = _prep_weights(norm_mix_g, w_in, rwkv_mu, rwkv_w0, rwkv_w2, rwkv_a0, rwkv_a2, rwkv_g2, rwkv_k_k, rwkv_k_a,
                       rwkv_r_k, rwkv_lnx_w, rwkv_lnx_b, w_out, norm_ffn_g, ffn_w_in, ffn_conv_w, ffn_conv_b,
                       ffn_w_down, norm_final_g, 0)
    bp = x_prompt.shape[0]
    d_ff = ffn_conv_w.shape[-1]
    y_p, c_p = _trunk(x_prompt, None, None, None,
                      jnp.zeros((bp, R_HEADS, R_HEAD_DIM, R_HEAD_DIM), F32), jnp.zeros((bp, 1, RWKV_COLS), F32),
                      jnp.zeros((bp, CONV_W - 1, d_ff), F32), wt)
    y_s, c_s = _trunk(x_sample, cache_k[0], cache_v[0], cache_idx_k[0], state_rwkv[0], state_rwkv_shift[0],
                      state_ffn_conv[0], wt)
    return (y_p, y_s) + c_p + c_s
```

```python
import functools

import jax
import jax.numpy as jnp
from jax import lax
from jax.experimental import pallas as pl
from jax.experimental.pallas import tpu as pltpu

F32 = jnp.float32
BF16 = jnp.bfloat16
I32 = jnp.int32

CHUNK = 64
A_HEADS = 16
A_KV_HEADS = 4
A_HEAD_DIM = 128
A_GROUP = A_HEADS // A_KV_HEADS
A_WIDTH = A_HEADS * A_HEAD_DIM
KV_W = A_KV_HEADS * A_HEAD_DIM
IDX_HEADS = 16
IDX_DIM = 64
IQ_W = IDX_HEADS * IDX_DIM
TOPK_MAX = 256
ROPE_THETA = 500000.0
ROPE_FRAC = 4
A_SCALE = A_HEAD_DIM ** -0.5
Q_SCALE = A_SCALE * 1.4426950408889634
IDX_SCALE = (IDX_HEADS ** -0.5) * (IDX_DIM ** -0.5)
R_HEAD_DIM = 64
R_WIDTH = 2048
R_HEADS = R_WIDTH // R_HEAD_DIM
R_PAIRS = R_HEADS // 2
DECAY_LORA = 96
AAA_LORA = 96
GATE_LORA = 256
RWKV_COLS = 3 * R_WIDTH + DECAY_LORA + AAA_LORA + GATE_LORA
GN_EPS = 6.4e-4
CONV_W = 3
RMS_EPS = 1e-6

LANES = 128
VMEM_LIMIT = 56 * 1024 * 1024

ATT_Q0, ATT_K0, ATT_V0, ATT_IQ0, ATT_IK0 = 0, A_WIDTH, A_WIDTH + KV_W, A_WIDTH + 2 * KV_W, A_WIDTH + 2 * KV_W + IQ_W
ATT_USED = ATT_IK0 + IDX_DIM + IDX_HEADS
ATT_COLS = 4608
LORA_PAD = 128
RW_WD0 = 3 * R_WIDTH
RW_AD0 = RW_WD0 + LORA_PAD
RW_GD0 = RW_AD0 + LORA_PAD
RW_COLS = RW_GD0 + GATE_LORA
SCAN_SUB = 16
SCAN_GROUP_PAIRS = LANES // (2 * SCAN_SUB)
SCAN_GROUPS = R_PAIRS // SCAN_GROUP_PAIRS
INT_MIN = -2 ** 31
NEG_BIG = -1e30
HIGHEST = lax.Precision.HIGHEST


def _cparams(sem):
    return pltpu.CompilerParams(dimension_semantics=sem, vmem_limit_bytes=VMEM_LIMIT)


def _rmsnorm_kernel(x_ref, g_ref, o_ref):
    x = x_ref[...]
    y = x * lax.rsqrt(jnp.mean(x * x, axis=-1, keepdims=True) + RMS_EPS)
    o_ref[...] = (y * g_ref[...]).astype(o_ref.dtype)


def _rmsnorm(x, g, out_dtype):
    m, d = x.shape
    tm = min(m, 256)
    return pl.pallas_call(
        _rmsnorm_kernel,
        out_shape=jax.ShapeDtypeStruct((m, d), out_dtype),
        grid=(m // tm,),
        in_specs=[pl.BlockSpec((tm, d), lambda i: (i, 0)), pl.BlockSpec((1, d), lambda i: (0, 0))],
        out_specs=pl.BlockSpec((tm, d), lambda i: (i, 0)),
        compiler_params=_cparams(("parallel",)),
        name="rmsnorm",
    )(x, g.reshape(1, d).astype(F32))


def _mm_kernel(*refs, n_pairs, has_res):
    o_ref = refs[-1]
    acc = jnp.dot(refs[0][...], refs[n_pairs][...], preferred_element_type=F32)
    for p in range(1, n_pairs):
        acc = acc + jnp.dot(refs[p][...], refs[n_pairs + p][...], preferred_element_type=F32)
    if has_res:
        acc = refs[2 * n_pairs][...] + acc
    o_ref[...] = acc.astype(o_ref.dtype)


def _matmul(a_list, b_list, res=None, tm=512, tn=512, name="matmul"):
    m = a_list[0].shape[0]
    n = b_list[0].shape[1]
    tm = min(tm, m)
    tn = min(tn, n)
    assert m % tm == 0 and n % tn == 0, (m, n, tm, tn)
    in_specs = [pl.BlockSpec((tm, a.shape[1]), lambda i, j: (i, 0)) for a in a_list]
    in_specs += [pl.BlockSpec((b.shape[0], tn), lambda i, j: (0, j)) for b in b_list]
    args = list(a_list) + list(b_list)
    if res is not None:
        in_specs.append(pl.BlockSpec((tm, tn), lambda i, j: (i, j)))
        args.append(res)
    return pl.pallas_call(
        functools.partial(_mm_kernel, n_pairs=len(a_list), has_res=res is not None),
        out_shape=jax.ShapeDtypeStruct((m, n), F32),
        grid=(m // tm, n // tn),
        in_specs=in_specs,
        out_specs=pl.BlockSpec((tm, tn), lambda i, j: (i, j)),
        compiler_params=_cparams(("parallel", "arbitrary")),
        name=name,
    )(*args)


def _rope_tile(x, cos, sin, half, d_in_head):
    lo = d_in_head < half
    hi = (d_in_head >= half) & (d_in_head < 2 * half)
    c = jnp.where(lo | hi, cos, 1.0)
    s_up = jnp.where(lo, -sin, 0.0)
    s_dn = jnp.where(hi, sin, 0.0)
    x_up = pltpu.roll(x, LANES - half, axis=1)
    x_dn = pltpu.roll(x, half, axis=1)
    return x * c + x_up * s_up + x_dn * s_dn


def _rope_kernel(z_ref, invf_ref, q_ref, kf_ref, kb_ref, vf_ref, vb_ref, iq_ref, ikw_ref, ikb_ref, *, tm, t_len, pos0):
    i = pl.program_id(0)
    row = lax.broadcasted_iota(I32, (tm, LANES), 0) + i * tm
    pos = (pos0 + lax.rem(row, t_len)).astype(F32)
    lane = lax.broadcasted_iota(I32, (tm, LANES), 1)
    ang = pos * invf_ref[0:1, :]
    cos_a, sin_a = jnp.cos(ang), jnp.sin(ang)
    half_a = A_HEAD_DIM // ROPE_FRAC // 2
    for h in range(A_HEADS):
        x = z_ref[:, ATT_Q0 + h * LANES:ATT_Q0 + (h + 1) * LANES]
        q_ref[:, h * LANES:(h + 1) * LANES] = (_rope_tile(x, cos_a, sin_a, half_a, lane) * Q_SCALE).astype(q_ref.dtype)
    for h in range(A_KV_HEADS):
        x = z_ref[:, ATT_K0 + h * LANES:ATT_K0 + (h + 1) * LANES]
        y = _rope_tile(x, cos_a, sin_a, half_a, lane)
        kf_ref[:, h * LANES:(h + 1) * LANES] = y
        kb_ref[:, h * LANES:(h + 1) * LANES] = y.astype(kb_ref.dtype)
    v = z_ref[:, ATT_V0:ATT_V0 + KV_W]
    vf_ref[...] = v
    vb_ref[...] = v.astype(vb_ref.dtype)
    ang = pos * invf_ref[1:2, :]
    cos_i, sin_i = jnp.cos(ang), jnp.sin(ang)
    half_i = IDX_DIM // ROPE_FRAC // 2
    d_i = lane & (IDX_DIM - 1)
    for h in range(IQ_W // LANES):
        x = z_ref[:, ATT_IQ0 + h * LANES:ATT_IQ0 + (h + 1) * LANES]
        iq_ref[:, h * LANES:(h + 1) * LANES] = _rope_tile(x, cos_i, sin_i, half_i, d_i).astype(iq_ref.dtype)
    x = z_ref[:, ATT_IK0:ATT_IK0 + LANES]
    d_k = jnp.where(lane < IDX_DIM, lane, IDX_DIM)
    y = _rope_tile(x, cos_i, sin_i, half_i, d_k)
    ikw_ref[...] = y
    ikb_ref[...] = y[:, :IDX_DIM].astype(ikb_ref.dtype)


def _rope_split(z_att, t_len, pos0):
    m = z_att.shape[0]
    tm = min(m, 256)
    lane = jnp.arange(LANES)
    rd_a = A_HEAD_DIM // ROPE_FRAC
    rd_i = IDX_DIM // ROPE_FRAC
    invf_a = ROPE_THETA ** (-((lane % (rd_a // 2)).astype(F32) * 2.0 / rd_a))
    invf_i = ROPE_THETA ** (-((lane % (rd_i // 2)).astype(F32) * 2.0 / rd_i))
    invf = jnp.zeros((8, LANES), F32).at[0].set(invf_a).at[1].set(invf_i)
    row_spec = lambda w: pl.BlockSpec((tm, w), lambda i: (i, 0))
    shp = lambda w, dt: jax.ShapeDtypeStruct((m, w), dt)
    return pl.pallas_call(
        functools.partial(_rope_kernel, tm=tm, t_len=t_len, pos0=pos0),
        out_shape=(shp(A_WIDTH, BF16), shp(KV_W, F32), shp(KV_W, BF16), shp(KV_W, F32), shp(KV_W, BF16),
                   shp(IQ_W, BF16), shp(LANES, F32), shp(IDX_DIM, BF16)),
        grid=(m // tm,),
        in_specs=[row_spec(ATT_COLS), pl.BlockSpec((8, LANES), lambda i: (0, 0))],
        out_specs=(row_spec(A_WIDTH), row_spec(KV_W), row_spec(KV_W), row_spec(KV_W), row_spec(KV_W),
                   row_spec(IQ_W), row_spec(LANES), row_spec(IDX_DIM)),
        compiler_params=_cparams(("parallel",)),
        name="rope_split",
    )(z_att, invf)


def _tile(x, n, axis):
    return x if n == 1 else jnp.concatenate([x] * n, axis=axis)


def _sortable(score):
    u = lax.bitcast_convert_type(score, I32)
    return jnp.where(u < 0, u ^ jnp.int32(0x7FFFFFFF), u)


def _dsa_kernel(q_ref, iq_ref, ikw_ref, k_ref, v_ref, ik_ref, o_ref,
                key_ref, iwb_ref, qs_ref, m_ref, l_ref, acc_ref, *, tq, tk, pos0, n_keys, topk):
    i = pl.program_id(1)
    q0 = pos0 + i * tq
    kmax = jnp.minimum((lax.div(q0 + tq - 1, CHUNK) + 1) * CHUNK, n_keys)
    nkb = lax.div(kmax + tk - 1, tk)
    n_rep = tk // LANES
    nt_dims = (((1,), (1,)), ((), ()))

    for h in range(IDX_HEADS):
        iwb_ref[h] = jnp.broadcast_to(ikw_ref[:, IDX_DIM + h:IDX_DIM + h + 1], (tq, LANES))
    for h in range(A_HEADS):
        qs_ref[h * tq:(h + 1) * tq, :] = q_ref[:, h * LANES:(h + 1) * LANES]

    def score_block(kb, carry):
        koff = pl.multiple_of(kb * tk, tk)
        ikb = ik_ref[pl.ds(koff, tk), :]
        acc = jnp.zeros((tq, tk), F32)
        for h in range(IDX_HEADS):
            d = lax.dot_general(iq_ref[:, h * IDX_DIM:(h + 1) * IDX_DIM], ikb, nt_dims, preferred_element_type=F32)
            acc = acc + jnp.maximum(d, 0.0) * _tile(iwb_ref[h], n_rep, 1)
        kpos = koff + lax.broadcasted_iota(I32, (tq, tk), 1)
        qpos = q0 + lax.broadcasted_iota(I32, (tq, tk), 0)
        adm = (lax.shift_right_logical(kpos, 6) <= lax.shift_right_logical(qpos, 6)) & (kpos < n_keys)
        key_ref[:, pl.ds(koff, tk)] = jnp.where(adm, _sortable(acc * IDX_SCALE), jnp.int32(INT_MIN))
        return carry

    lax.fori_loop(0, nkb, score_block, 0)

    def count(pred):
        def body(kb, cnt):
            koff = pl.multiple_of(kb * tk, tk)
            keys = key_ref[:, pl.ds(koff, tk)]
            kpos = koff + lax.broadcasted_iota(I32, (tq, tk), 1)
            hit = jnp.where(pred(keys, kpos), 1.0, 0.0)
            for c in range(n_rep):
                cnt = cnt + hit[:, c * LANES:(c + 1) * LANES]
            return cnt
        cnt = lax.fori_loop(0, nkb, body, jnp.zeros((tq, LANES), F32))
        return jnp.broadcast_to(jnp.sum(cnt, axis=-1, keepdims=True), (tq, LANES))

    def wide(x):
        return _tile(x, n_rep, 1)

    def bit_step(it, tu):
        cand_u = tu | lax.shift_left(jnp.int32(1), 31 - it)
        cand_s = wide(cand_u ^ jnp.int32(INT_MIN))
        cnt = count(lambda keys, kpos: keys >= cand_s)
        return jnp.where(cnt >= topk, cand_u, tu)

    tu = lax.fori_loop(0, 32, bit_step, jnp.zeros((tq, LANES), I32))
    thr = jnp.maximum(tu ^ jnp.int32(INT_MIN), jnp.int32(INT_MIN + 1))
    thr_w = wide(thr)

    n_ge = count(lambda keys, kpos: keys >= thr_w)
    n_gt = count(lambda keys, kpos: keys > thr_w)
    excess = n_ge > topk

    @pl.when(jnp.max(jnp.where(excess, 1.0, 0.0)) > 0.0)
    def _():
        need = topk - n_gt

        idx_bits = int(key_ref.shape[1]).bit_length()

        def idx_step(it, jm):
            cand = wide(jm | lax.shift_left(jnp.int32(1), idx_bits - 1 - it))
            cnt = count(lambda keys, kpos: (keys == thr_w) & (kpos < cand))
            return jnp.where(cnt < need, cand[:, :LANES], jm)

        jm = lax.fori_loop(0, idx_bits, idx_step, jnp.zeros((tq, LANES), I32))
        jm_w = wide(jnp.where(excess, jm, jnp.int32(2 ** 31 - 1)))

        def drop(kb, carry):
            koff = pl.multiple_of(kb * tk, tk)
            keys = key_ref[:, pl.ds(koff, tk)]
            kpos = koff + lax.broadcasted_iota(I32, (tq, tk), 1)
            key_ref[:, pl.ds(koff, tk)] = jnp.where((keys == thr_w) & (kpos > jm_w), jnp.int32(INT_MIN), keys)
            return carry

        lax.fori_loop(0, nkb, drop, 0)

    m_ref[...] = jnp.full(m_ref.shape, NEG_BIG, F32)
    l_ref[...] = jnp.zeros(l_ref.shape, F32)
    acc_ref[...] = jnp.zeros(acc_ref.shape, F32)
    rows = A_GROUP * tq

    def attend(kb, carry):
        koff = pl.multiple_of(kb * tk, tk)
        bias = _tile(jnp.where(key_ref[:, pl.ds(koff, tk)] >= thr_w, 0.0, NEG_BIG), A_GROUP, 0)
        for n in range(A_KV_HEADS):
            r0 = n * rows
            kn = k_ref[pl.ds(koff, tk), n * LANES:(n + 1) * LANES]
            vn = v_ref[pl.ds(koff, tk), n * LANES:(n + 1) * LANES]
            s = lax.dot_general(qs_ref[r0:r0 + rows, :], kn, nt_dims, preferred_element_type=F32) + bias
            m_prev = m_ref[r0:r0 + rows, :]
            m_new = jnp.maximum(m_prev, jnp.max(s, axis=-1, keepdims=True))
            alpha = jnp.exp2(m_prev - m_new)
            p = jnp.exp2(s - _tile(m_new, n_rep, 1))
            l_ref[r0:r0 + rows, :] = alpha * l_ref[r0:r0 + rows, :] + jnp.sum(p, axis=-1, keepdims=True)
            acc_ref[r0:r0 + rows, :] = alpha * acc_ref[r0:r0 + rows, :] + jnp.dot(
                p.astype(vn.dtype), vn, preferred_element_type=F32)
            m_ref[r0:r0 + rows, :] = m_new
        return carry

    lax.fori_loop(0, nkb, attend, 0)
    for h in range(A_HEADS):
        o_ref[:, h * LANES:(h + 1) * LANES] = (
            acc_ref[h * tq:(h + 1) * tq, :] / l_ref[h * tq:(h + 1) * tq, :]).astype(o_ref.dtype)


def _dsa(q_bf, iq_bf, ikw, k_all, v_all, ik_all, *, n_batch, t_len, pos0, n_keys):
    lp = k_all.shape[1]
    tq = min(t_len, 128)
    tk = min(lp, 512)
    assert lp % tk == 0 and t_len % tq == 0
    nq = t_len // tq
    topk = min(TOPK_MAX, n_keys // 4)
    qrow = lambda w: pl.BlockSpec((tq, w), lambda b, i: (b * nq + i, 0))
    kv_spec = lambda w: pl.BlockSpec((None, lp, w), lambda b, i: (b, 0, 0), pipeline_mode=pl.Buffered(1))
    return pl.pallas_call(
        functools.partial(_dsa_kernel, tq=tq, tk=tk, pos0=pos0, n_keys=n_keys, topk=float(topk)),
        out_shape=jax.ShapeDtypeStruct((n_batch * t_len, A_WIDTH), BF16),
        grid=(n_batch, nq),
        in_specs=[qrow(A_WIDTH), qrow(IQ_W), qrow(LANES), kv_spec(KV_W), kv_spec(KV_W), kv_spec(IDX_DIM)],
        out_specs=qrow(A_WIDTH),
        scratch_shapes=[
            pltpu.VMEM((tq, lp), I32),
            pltpu.VMEM((IDX_HEADS, tq, LANES), F32),
            pltpu.VMEM((A_HEADS * tq, LANES), BF16),
            pltpu.VMEM((A_HEADS * tq, LANES), F32),
            pltpu.VMEM((A_HEADS * tq, LANES), F32),
            pltpu.VMEM((A_HEADS * tq, LANES), F32),
        ],
        compiler_params=_cparams(("parallel", "arbitrary")),
        name="dsa",
    )(q_bf, iq_bf, ikw, k_all, v_all, ik_all)


def _head_sums(x, ones_bd):
    n = x.shape[1] // LANES
    tm = x.shape[0]
    stacked = jnp.concatenate([x[:, c * LANES:(c + 1) * LANES] for c in range(n)], axis=0)
    s = jnp.dot(stacked, ones_bd, precision=HIGHEST, preferred_element_type=F32)
    return jnp.concatenate([s[c * tm:(c + 1) * tm, :] for c in range(n)], axis=1)


def _rwkv_pre_kernel(z_ref, zp_ref, z0_ref, mu_ref, w0_ref, a0_ref, kk_ref, ka_ref, w2_ref, a2_ref, g2_ref,
                     ones_ref, r_out, w_out, k_out, v_out, nkk_out, b_out, g_out, *, tm):
    i = pl.program_id(1)
    row = lax.broadcasted_iota(I32, (tm, 1), 0)

    def mixed(c0, width):
        z = z_ref[:, c0:c0 + width]
        first = jnp.where(i == 0, z0_ref[:, c0:c0 + width], zp_ref[7:8, c0:c0 + width])
        shifted = jnp.where(row == 0, first, pltpu.roll(z, 1, axis=0))
        return z + (shifted - z) * mu_ref[:, c0:c0 + width]

    r = mixed(0, R_WIDTH)
    k = mixed(R_WIDTH, R_WIDTH)
    v = mixed(2 * R_WIDTH, R_WIDTH)
    wd = mixed(RW_WD0, LORA_PAD)
    ad = mixed(RW_AD0, LORA_PAD)
    gd = mixed(RW_GD0, GATE_LORA)
    lora = lambda x, w_ref: jnp.dot(x, w_ref[...], precision=HIGHEST, preferred_element_type=F32)
    y = -(w0_ref[...] + lora(jnp.tanh(wd), w2_ref))
    softplus = jnp.maximum(y, 0.0) + jnp.log(1.0 + jnp.exp(-jnp.abs(y)))
    decay = jnp.exp(-jnp.exp(-softplus - 0.5))
    a = jax.nn.sigmoid(a0_ref[...] + lora(ad, a2_ref))
    g = lora(jax.nn.sigmoid(gd), g2_ref)
    kk = k * kk_ref[...]
    kk = kk / jnp.maximum(jnp.sqrt(_head_sums(kk * kk, ones_ref[...])), 1e-12)
    r_out[...] = r
    w_out[...] = decay
    k_out[...] = k * (1.0 + (a - 1.0) * ka_ref[...])
    v_out[...] = v
    nkk_out[...] = -kk
    b_out[...] = kk * a
    g_out[...] = g


def _rwkv_pre(z_rw, z0, mu, w0, a0, k_k, k_a, w2, a2, g2, ones_bd):
    nb, t_len, _ = z_rw.shape
    tm = min(t_len, 128)
    zrow = pl.BlockSpec((None, tm, RW_COLS), lambda b, i: (b, i, 0))
    zprev = pl.BlockSpec((None, 8, RW_COLS), lambda b, i: (b, jnp.maximum(i * (tm // 8) - 1, 0), 0))
    full = lambda a: pl.BlockSpec(a.shape, lambda b, i: (0,) * a.ndim)
    orow = pl.BlockSpec((None, tm, R_WIDTH), lambda b, i: (b, i, 0))
    params = (mu, w0, a0, k_k, k_a, w2, a2, g2, ones_bd)
    return pl.pallas_call(
        functools.partial(_rwkv_pre_kernel, tm=tm),
        out_shape=tuple(jax.ShapeDtypeStruct((nb, t_len, R_WIDTH), F32) for _ in range(7)),
        grid=(nb, t_len // tm),
        in_specs=[zrow, zprev, pl.BlockSpec((None, 1, RW_COLS), lambda b, i: (b, 0, 0))] + [full(p) for p in params],
        out_specs=tuple(orow for _ in range(7)),
        compiler_params=_cparams(("parallel", "arbitrary")),
        name="rwkv_pre",
    )(z_rw, z_rw, z0, *params)


def _split2(x):
    hi = x.astype(BF16)
    return hi, (x - hi.astype(F32)).astype(BF16)


def _rwkv_scan_kernel(r_ref, w_ref, k_ref, nkk_ref, b_ref, vx_ref, s0_ref, ones2_ref, pick_ref,
                      y_ref, st_ref, s_ref, xs_ref, *, tb, unroll):
    tblk = pl.program_id(1)

    @pl.when(tblk == 0)
    def _():
        s_ref[...] = s0_ref[...]

    for sub in range(tb // SCAN_SUB):
        for g in range(SCAN_GROUPS):
            hi, mid = _split2(vx_ref[sub, g])
            xs_ref[sub, g * R_HEAD_DIM:(g + 1) * R_HEAD_DIM, :] = jnp.concatenate([hi, mid], axis=1)
    ones2 = ones2_ref[...]
    rows16 = lax.broadcasted_iota(I32, (16, LANES), 0)
    lanes16 = lax.broadcasted_iota(I32, (16, LANES), 1)
    head_row = rows16 == jnp.where(lanes16 < R_HEAD_DIM, 0, 1)
    cat = lambda xs, ax=0: jnp.concatenate(xs, axis=ax)
    rowp = lambda ref, t, p: ref[t, p:p + 1, :]
    nt_dims = (((1,), (1,)), ((), ()))

    groups = [range(g * SCAN_GROUP_PAIRS, (g + 1) * SCAN_GROUP_PAIRS) for g in range(SCAN_GROUPS)]

    def sa_dot(g, t):
        his, mids = zip(*[_split2(s_ref[p] * rowp(nkk_ref, t, p)) for p in groups[g]])
        return jnp.dot(cat([cat(his), cat(mids)], 1), ones2, preferred_element_type=F32)

    def vcol_dot(g, t):
        return jnp.dot(xs_ref[lax.div(t, SCAN_SUB), g * R_HEAD_DIM:(g + 1) * R_HEAD_DIM, :],
                       pick_ref[lax.rem(t, SCAN_SUB)], preferred_element_type=F32)

    def update_group(g, t, sa, vcol):
        for q, p in enumerate(groups[g]):
            sa_p = sa[q * R_HEAD_DIM:(q + 1) * R_HEAD_DIM]
            vc = vcol[:, q * LANES:(q + 1) * LANES]
            s_new = s_ref[p] * rowp(w_ref, t, p) + sa_p * rowp(b_ref, t, p) + vc * rowp(k_ref, t, p)
            s_ref[p] = s_new
            r2 = jnp.where(head_row, rowp(r_ref, t, p), 0.0).astype(BF16)
            o = lax.dot_general(r2, s_new.astype(BF16), nt_dims, preferred_element_type=F32)
            y_ref[t, p:p + 1, :] = cat([o[0:1, :], o[1:2, :]], 1)

    def step(t, carry):
        sas = [sa_dot(g, t) for g in range(SCAN_GROUPS)]
        vcols = [vcol_dot(g, t) for g in range(SCAN_GROUPS)]
        for g in range(SCAN_GROUPS):
            update_group(g, t, sas[g], vcols[g])
        return carry

    lax.fori_loop(0, tb, step, 0, unroll=unroll)

    @pl.when(tblk == pl.num_programs(1) - 1)
    def _():
        st_ref[...] = s_ref[...]


def _rwkv_scan(r, w, k, nkk, b, v, s0):
    nb, t_len, _ = r.shape
    tb = min(t_len, 64)
    nblk = t_len // tb
    hd = R_HEAD_DIM
    nsub, sb, ng, gp = tb // SCAN_SUB, SCAN_SUB, SCAN_GROUPS, SCAN_GROUP_PAIRS
    vx = v.reshape(nb, nblk, nsub, sb, ng, gp, 2, hd).transpose(0, 1, 2, 4, 7, 5, 6, 3).reshape(
        nb, nblk, nsub, ng, hd, LANES)
    s0p = s0.reshape(nb, R_PAIRS, 2, hd, hd).transpose(0, 1, 3, 2, 4).reshape(nb, R_PAIRS, hd, LANES)
    lane_h = jnp.arange(LANES) // hd
    ones_bd = lane_h[:, None] == lane_h[None, :]
    ones2 = jnp.concatenate([ones_bd, ones_bd], axis=0).astype(BF16)
    src = jnp.arange(LANES)
    src_q, src_h, src_t = src // (2 * sb), (src // sb) % 2, src % sb
    dst = jnp.arange(gp * LANES)
    dst_q, dst_h = dst // LANES, (dst % LANES) // hd
    pick1 = ((src_t[None, :, None] == jnp.arange(sb)[:, None, None]) & (src_q[None, :, None] == dst_q[None, None, :])
             & (src_h[None, :, None] == dst_h[None, None, :]))
    pick = jnp.concatenate([pick1, pick1], axis=1).astype(BF16)
    trow = pl.BlockSpec((None, tb, R_PAIRS, LANES), lambda bb, i: (bb, i, 0, 0))
    r, w, k, nkk, b = (a.reshape(nb, t_len, R_PAIRS, LANES) for a in (r, w, k, nkk, b))
    col = pl.BlockSpec((None, None, nsub, ng, hd, LANES), lambda bb, i: (bb, i, 0, 0, 0, 0))
    st = pl.BlockSpec((None, R_PAIRS, hd, LANES), lambda bb, i: (bb, 0, 0, 0))
    full = lambda a: pl.BlockSpec(a.shape, lambda bb, i: (0,) * a.ndim)
    y, s_t = pl.pallas_call(
        functools.partial(_rwkv_scan_kernel, tb=tb, unroll=4),
        out_shape=(jax.ShapeDtypeStruct((nb, t_len, R_PAIRS, LANES), F32),
                   jax.ShapeDtypeStruct((nb, R_PAIRS, hd, LANES), F32)),
        grid=(nb, nblk),
        in_specs=[trow, trow, trow, trow, trow, col, st, full(ones2), full(pick)],
        out_specs=(trow, st),
        scratch_shapes=[pltpu.VMEM((R_PAIRS, hd, LANES), F32), pltpu.VMEM((nsub, ng * hd, 2 * LANES), BF16)],
        compiler_params=_cparams(("parallel", "arbitrary")),
        name="rwkv_scan",
    )(r, w, k, nkk, b, vx, s0p, ones2, pick)
    s_t = s_t.reshape(nb, R_PAIRS, hd, 2, hd).transpose(0, 1, 3, 2, 4).reshape(nb, R_HEADS, hd, hd)
    return y.reshape(nb, t_len, R_WIDTH), s_t


def _rwkv_post_kernel(y_ref, r_ref, k_ref, v_ref, g_ref, lw_ref, lb_ref, rk_ref, ones_ref, o_ref):
    ones_bd = ones_ref[...]
    y = y_ref[...]
    mean = _head_sums(y, ones_bd) * (1.0 / R_HEAD_DIM)
    d = y - mean
    var = _head_sums(d * d, ones_bd) * (1.0 / R_HEAD_DIM)
    yn = d * lax.rsqrt(var + GN_EPS) * lw_ref[...] + lb_ref[...]
    bonus = _head_sums(r_ref[...] * k_ref[...] * rk_ref[...], ones_bd) * v_ref[...]
    o_ref[...] = ((yn + bonus) * g_ref[...]).astype(o_ref.dtype)


def _rwkv_post(y, r, k, v, g, lnx_w, lnx_b, r_k, ones_bd):
    m = y.shape[0]
    tm = min(m, 128)
    row = pl.BlockSpec((tm, R_WIDTH), lambda i: (i, 0))
    full = lambda a: pl.BlockSpec(a.shape, lambda i: (0,) * a.ndim)
    params = (lnx_w, lnx_b, r_k, ones_bd)
    return pl.pallas_call(
        _rwkv_post_kernel,
        out_shape=jax.ShapeDtypeStruct((m, R_WIDTH), BF16),
        grid=(m // tm,),
        in_specs=[row] * 5 + [full(p) for p in params],
        out_specs=row,
        compiler_params=_cparams(("parallel",)),
        name="rwkv_post",
    )(y, r, k, v, g, *params)


def _ffn_gate_kernel(g_ref, u_ref, gp_ref, c0_ref, cw_ref, cb_ref, o_ref, *, tm):
    i = pl.program_id(1)
    gate = g_ref[...]
    row = lax.broadcasted_iota(I32, (tm, 1), 0)
    prev1 = jnp.where(i == 0, c0_ref[1:2, :], gp_ref[7:8, :])
    prev2 = jnp.where(i == 0, c0_ref[0:1, :], gp_ref[6:7, :])
    g_m1 = jnp.where(row == 0, prev1, pltpu.roll(gate, 1, axis=0))
    g_m2 = jnp.where(row == 0, prev2, jnp.where(row == 1, prev1, pltpu.roll(gate, 2, axis=0)))
    conv = cb_ref[...] + g_m2 * cw_ref[0:1, :]
    conv = conv + g_m1 * cw_ref[1:2, :]
    conv = conv + gate * cw_ref[2:3, :]
    o_ref[...] = (conv * jax.nn.sigmoid(conv) * u_ref[...]).astype(o_ref.dtype)


def _ffn_gate(gate_up, conv0, conv_w, conv_b, n_batch, t_len):
    d_ff = gate_up.shape[1] // 2
    tm = min(t_len, 64)
    nt = t_len // tm
    gu = gate_up.reshape(n_batch, t_len, 2 * d_ff)
    row = lambda c: pl.BlockSpec((None, tm, d_ff), lambda b, i: (b, i, c))
    prev = pl.BlockSpec((None, 8, d_ff), lambda b, i: (b, jnp.maximum(i * (tm // 8) - 1, 0), 0))
    act = pl.pallas_call(
        functools.partial(_ffn_gate_kernel, tm=tm),
        out_shape=jax.ShapeDtypeStruct((n_batch, t_len, d_ff), BF16),
        grid=(n_batch, nt),
        in_specs=[row(0), row(1), prev, pl.BlockSpec((None, CONV_W - 1, d_ff), lambda b, i: (b, 0, 0)),
                  pl.BlockSpec((CONV_W, d_ff), lambda b, i: (0, 0)), pl.BlockSpec((1, d_ff), lambda b, i: (0, 0))],
        out_specs=pl.BlockSpec((None, tm, d_ff), lambda b, i: (b, i, 0)),
        compiler_params=_cparams(("parallel", "arbitrary")),
        name="ffn_gate",
    )(gu, gu, gu, conv0, conv_w, conv_b.reshape(1, d_ff))
    return act.reshape(n_batch * t_len, d_ff)


def _pad_rw_cols(a):
    z = lambda n: jnp.zeros(a.shape[:-1] + (n,), a.dtype)
    wd0, ad0, gd0 = 3 * R_WIDTH, 3 * R_WIDTH + DECAY_LORA, 3 * R_WIDTH + DECAY_LORA + AAA_LORA
    return jnp.concatenate([a[..., :wd0], a[..., wd0:ad0], z(LORA_PAD - DECAY_LORA), a[..., ad0:gd0],
                            z(LORA_PAD - AAA_LORA), a[..., gd0:]], axis=-1)


def _unpad_rw_cols(a):
    return jnp.concatenate([a[..., :RW_WD0 + DECAY_LORA], a[..., RW_AD0:RW_AD0 + AAA_LORA], a[..., RW_GD0:]], axis=-1)


def _prep_weights(norm_mix_g, w_in, rwkv_mu, rwkv_w0, rwkv_w2, rwkv_a0, rwkv_a2, rwkv_g2, rwkv_k_k, rwkv_k_a,
                  rwkv_r_k, rwkv_lnx_w, rwkv_lnx_b, w_out, norm_ffn_g, ffn_w_in, ffn_conv_w, ffn_conv_b,
                  ffn_w_down, norm_final_g, l):
    d = w_in.shape[1]
    w_att = jnp.concatenate([w_in[l][:, :ATT_USED].astype(BF16), jnp.zeros((d, ATT_COLS - ATT_USED), BF16)], axis=1)
    w_rw = _pad_rw_cols(w_in[l][:, ATT_USED:].astype(BF16))
    row = lambda a: a.reshape(1, -1).astype(F32)
    pad_rows = lambda a, n: jnp.concatenate([a, jnp.zeros((n - a.shape[0], a.shape[1]), a.dtype)], axis=0)
    lane_h = jnp.arange(LANES) // R_HEAD_DIM
    return dict(
        norm_mix_g=norm_mix_g[l], w_att=w_att, w_rw=w_rw,
        mu=_pad_rw_cols(row(rwkv_mu[l])), w0=row(rwkv_w0[l]), a0=row(rwkv_a0[l]),
        k_k=row(rwkv_k_k[l]), k_a=row(rwkv_k_a[l]),
        w2=pad_rows(rwkv_w2[l], LORA_PAD), a2=pad_rows(rwkv_a2[l], LORA_PAD), g2=rwkv_g2[l],
        r_k=row(rwkv_r_k[l]), lnx_w=row(rwkv_lnx_w[l]), lnx_b=row(rwkv_lnx_b[l]),
        ones_bd=(lane_h[:, None] == lane_h[None, :]).astype(F32),
        w_out_a=w_out[l][:A_WIDTH].astype(BF16), w_out_r=w_out[l][A_WIDTH:].astype(BF16),
        norm_ffn_g=norm_ffn_g[l], ffn_w_in=ffn_w_in[l].astype(BF16), conv_w=ffn_conv_w[l], conv_b=ffn_conv_b[l],
        ffn_w_down=ffn_w_down[l].astype(BF16), norm_final_g=norm_final_g,
    )


def _trunk(x, past_k, past_v, past_ik, s0, shift0, conv0, wt):
    nb, t_len, d = x.shape
    m = nb * t_len
    p_len = 0 if past_k is None else past_k.shape[1]
    n_keys = p_len + t_len
    x2 = x.reshape(m, d)

    h = _rmsnorm(x2, wt["norm_mix_g"], BF16)
    z_att = _matmul([h], [wt["w_att"]], name="proj_att")
    z_rw = _matmul([h], [wt["w_rw"]], name="proj_rw")

    q_bf, k_f, k_bf, v_f, v_bf, iq_bf, ikw, ik_bf = _rope_split(z_att, t_len, p_len)
    tk = min(-(-n_keys // LANES) * LANES, 512)
    lp = -(-n_keys // tk) * tk

    def with_past(new, past, width):
        new = new.reshape(nb, t_len, width)
        parts = [new] if past is None else [past.reshape(nb, p_len, width).astype(BF16), new]
        if lp > n_keys:
            parts.append(jnp.zeros((nb, lp - n_keys, width), BF16))
        return parts[0] if len(parts) == 1 else jnp.concatenate(parts, axis=1)

    attn = _dsa(q_bf, iq_bf, ikw, with_past(k_bf, past_k, KV_W), with_past(v_bf, past_v, KV_W),
                with_past(ik_bf, past_ik, IDX_DIM), n_batch=nb, t_len=t_len, pos0=p_len, n_keys=n_keys)

    z_rw3 = z_rw.reshape(nb, t_len, RW_COLS)
    r, w, k2, v2, nkk, b, g = _rwkv_pre(z_rw3, _pad_rw_cols(shift0.astype(F32)), wt["mu"], wt["w0"], wt["a0"],
                                       wt["k_k"], wt["k_a"], wt["w2"], wt["a2"], wt["g2"], wt["ones_bd"])
    y, s_t = _rwkv_scan(r, w, k2, nkk, b, v2, s0.astype(F32))
    flat = lambda a: a.reshape(m, R_WIDTH)
    rw = _rwkv_post(flat(y), flat(r), flat(k2), flat(v2), flat(g), wt["lnx_w"], wt["lnx_b"], wt["r_k"], wt["ones_bd"])

    x1 = _matmul([attn, rw], [wt["w_out_a"], wt["w_out_r"]], res=x2, name="out_proj")
    hf = _rmsnorm(x1, wt["norm_ffn_g"], BF16)
    gate_up = _matmul([hf], [wt["ffn_w_in"]], name="ffn_in")
    d_ff = gate_up.shape[1] // 2
    act = _ffn_gate(gate_up, conv0.astype(F32), wt["conv_w"], wt["conv_b"], nb, t_len)
    x3 = _matmul([act], [wt["ffn_w_down"]], res=x1, name="ffn_down")
    y_out = _rmsnorm(x3, wt["norm_final_g"], F32).reshape(nb, t_len, d)

    gate3 = gate_up.reshape(nb, t_len, 2 * d_ff)[:, :, :d_ff]
    conv_t = jnp.concatenate([conv0.astype(F32), gate3], axis=1)[:, -(CONV_W - 1):]
    shift_t = _unpad_rw_cols(z_rw3[:, -1:])
    caches = (k_f.reshape(nb, t_len, A_KV_HEADS, A_HEAD_DIM)[None], v_f.reshape(nb, t_len, A_KV_HEADS, A_HEAD_DIM)[None],
              ikw[:, :IDX_DIM].reshape(nb, t_len, IDX_DIM)[None], s_t[None], shift_t[None], conv_t[None])
    return y_out, caches


def kernel(x_prompt, x_sample, cache_k, cache_v, cache_idx_k, state_rwkv, state_rwkv_shift, state_ffn_conv, norm_mix_g, w_in, rwkv_mu, rwkv_w0, rwkv_w2, rwkv_a0, rwkv_a2, rwkv_g2, rwkv_k_k, rwkv_k_a, rwkv_r_k, rwkv_lnx_w, rwkv_lnx_b, w_out, norm_ffn_g, ffn_w_in, ffn_conv_w, ffn_conv_b, ffn_w_down, norm_final_g):
    assert w_in.shape[0] == 1, "single-layer trunk"
    wt = _prep_weights(norm_mix_g, w_in, rwkv_mu, rwkv_w0, rwkv_w2, rwkv_a0, rwkv_a2, rwkv_g2, rwkv_k_k, rwkv_k_a,
                       rwkv_r_k, rwkv_lnx_w, rwkv_lnx_b, w_out, norm_ffn_g, ffn_w_in, ffn_conv_w, ffn_conv_b,
                       ffn_w_down, norm_final_g, 0)
    bp = x_prompt.shape[0]
    d_ff = ffn_conv_w.shape[-1]
    y_p, c_p = _trunk(x_prompt, None, None, None,
                      jnp.zeros((bp, R_HEADS, R_HEAD_DIM, R_HEAD_DIM), F32), jnp.zeros((bp, 1, RWKV_COLS), F32),
                      jnp.zeros((bp, CONV_W - 1, d_ff), F32), wt)
    y_s, c_s = _trunk(x_sample, cache_k[0], cache_v[0], cache_idx_k[0], state_rwkv[0], state_rwkv_shift[0],
                      state_ffn_conv[0], wt)
    return (y_p, y_s) + c_p + c_s
```

```python
import functools

import jax
import jax.numpy as jnp
from jax import lax
from jax.experimental import pallas as pl
from jax.experimental.pallas import tpu as pltpu

F32 = jnp.float32
BF16 = jnp.bfloat16
I32 = jnp.int32

CHUNK = 64
A_HEADS = 16
A_KV_HEADS = 4
A_HEAD_DIM = 128
A_GROUP = A_HEADS // A_KV_HEADS
A_WIDTH = A_HEADS * A_HEAD_DIM
KV_W = A_KV_HEADS * A_HEAD_DIM
IDX_HEADS = 16
IDX_DIM = 64
IQ_W = IDX_HEADS * IDX_DIM
TOPK_MAX = 256
ROPE_THETA = 500000.0
ROPE_FRAC = 4
A_SCALE = A_HEAD_DIM ** -0.5
Q_SCALE = A_SCALE * 1.4426950408889634
IDX_SCALE = (IDX_HEADS ** -0.5) * (IDX_DIM ** -0.5)
R_HEAD_DIM = 64
R_WIDTH = 2048
R_HEADS = R_WIDTH // R_HEAD_DIM
R_PAIRS = R_HEADS // 2
DECAY_LORA = 96
AAA_LORA = 96
GATE_LORA = 256
RWKV_COLS = 3 * R_WIDTH + DECAY_LORA + AAA_LORA + GATE_LORA
GN_EPS = 6.4e-4
CONV_W = 3
RMS_EPS = 1e-6

LANES = 128
VMEM_LIMIT = 56 * 1024 * 1024

ATT_Q0, ATT_K0, ATT_V0, ATT_IQ0, ATT_IK0 = 0, A_WIDTH, A_WIDTH + KV_W, A_WIDTH + 2 * KV_W, A_WIDTH + 2 * KV_W + IQ_W
ATT_USED = ATT_IK0 + IDX_DIM + IDX_HEADS
ATT_COLS = 4608
LORA_PAD = 128
RW_WD0 = 3 * R_WIDTH
RW_AD0 = RW_WD0 + LORA_PAD
RW_GD0 = RW_AD0 + LORA_PAD
RW_COLS = RW_GD0 + GATE_LORA
SCAN_SUB = 16
SCAN_GROUP_PAIRS = LANES // (2 * SCAN_SUB)
SCAN_GROUPS = R_PAIRS // SCAN_GROUP_PAIRS
SCAN_MATMULS = 2
SCAN_BULK = 4
INT_MIN = -2 ** 31
NEG_BIG = -1e30
HIGHEST = lax.Precision.HIGHEST


def _cparams(sem):
    return pltpu.CompilerParams(dimension_semantics=sem, vmem_limit_bytes=VMEM_LIMIT)


def _rmsnorm_kernel(x_ref, g_ref, o_ref):
    x = x_ref[...]
    y = x * lax.rsqrt(jnp.mean(x * x, axis=-1, keepdims=True) + RMS_EPS)
    o_ref[...] = (y * g_ref[...]).astype(o_ref.dtype)


def _rmsnorm(x, g, out_dtype):
    m, d = x.shape
    tm = min(m, 256)
    return pl.pallas_call(
        _rmsnorm_kernel,
        out_shape=jax.ShapeDtypeStruct((m, d), out_dtype),
        grid=(m // tm,),
        in_specs=[pl.BlockSpec((tm, d), lambda i: (i, 0)), pl.BlockSpec((1, d), lambda i: (0, 0))],
        out_specs=pl.BlockSpec((tm, d), lambda i: (i, 0)),
        compiler_params=_cparams(("parallel",)),
        name="rmsnorm",
    )(x, g.reshape(1, d).astype(F32))


def _mm_kernel(*refs, n_pairs, has_res):
    o_ref = refs[-1]
    acc = jnp.dot(refs[0][...], refs[n_pairs][...], preferred_element_type=F32)
    for p in range(1, n_pairs):
        acc = acc + jnp.dot(refs[p][...], refs[n_pairs + p][...], preferred_element_type=F32)
    if has_res:
        acc = refs[2 * n_pairs][...] + acc
    o_ref[...] = acc.astype(o_ref.dtype)


def _matmul(a_list, b_list, res=None, tm=512, tn=512, name="matmul"):
    m = a_list[0].shape[0]
    n = b_list[0].shape[1]
    tm = min(tm, m)
    tn = min(tn, n)
    assert m % tm == 0 and n % tn == 0, (m, n, tm, tn)
    in_specs = [pl.BlockSpec((tm, a.shape[1]), lambda i, j: (i, 0)) for a in a_list]
    in_specs += [pl.BlockSpec((b.shape[0], tn), lambda i, j: (0, j)) for b in b_list]
    args = list(a_list) + list(b_list)
    if res is not None:
        in_specs.append(pl.BlockSpec((tm, tn), lambda i, j: (i, j)))
        args.append(res)
    return pl.pallas_call(
        functools.partial(_mm_kernel, n_pairs=len(a_list), has_res=res is not None),
        out_shape=jax.ShapeDtypeStruct((m, n), F32),
        grid=(m // tm, n // tn),
        in_specs=in_specs,
        out_specs=pl.BlockSpec((tm, tn), lambda i, j: (i, j)),
        compiler_params=_cparams(("parallel", "arbitrary")),
        name=name,
    )(*args)


def _rope_tile(x, cos, sin, half, d_in_head):
    lo = d_in_head < half
    hi = (d_in_head >= half) & (d_in_head < 2 * half)
    c = jnp.where(lo | hi, cos, 1.0)
    s_up = jnp.where(lo, -sin, 0.0)
    s_dn = jnp.where(hi, sin, 0.0)
    x_up = pltpu.roll(x, LANES - half, axis=1)
    x_dn = pltpu.roll(x, half, axis=1)
    return x * c + x_up * s_up + x_dn * s_dn


def _rope_kernel(z_ref, invf_ref, q_ref, kf_ref, kb_ref, vf_ref, vb_ref, iq_ref, ikw_ref, ikb_ref, *, tm, t_len, pos0):
    i = pl.program_id(0)
    row = lax.broadcasted_iota(I32, (tm, LANES), 0) + i * tm
    pos = (pos0 + lax.rem(row, t_len)).astype(F32)
    lane = lax.broadcasted_iota(I32, (tm, LANES), 1)
    ang = pos * invf_ref[0:1, :]
    cos_a, sin_a = jnp.cos(ang), jnp.sin(ang)
    half_a = A_HEAD_DIM // ROPE_FRAC // 2
    for h in range(A_HEADS):
        x = z_ref[:, ATT_Q0 + h * LANES:ATT_Q0 + (h + 1) * LANES]
        q_ref[:, h * LANES:(h + 1) * LANES] = (_rope_tile(x, cos_a, sin_a, half_a, lane) * Q_SCALE).astype(q_ref.dtype)
    for h in range(A_KV_HEADS):
        x = z_ref[:, ATT_K0 + h * LANES:ATT_K0 + (h + 1) * LANES]
        y = _rope_tile(x, cos_a, sin_a, half_a, lane)
        kf_ref[:, h * LANES:(h + 1) * LANES] = y
        kb_ref[:, h * LANES:(h + 1) * LANES] = y.astype(kb_ref.dtype)
    v = z_ref[:, ATT_V0:ATT_V0 + KV_W]
    vf_ref[...] = v
    vb_ref[...] = v.astype(vb_ref.dtype)
    ang = pos * invf_ref[1:2, :]
    cos_i, sin_i = jnp.cos(ang), jnp.sin(ang)
    half_i = IDX_DIM // ROPE_FRAC // 2
    d_i = lane & (IDX_DIM - 1)
    for h in range(IQ_W // LANES):
        x = z_ref[:, ATT_IQ0 + h * LANES:ATT_IQ0 + (h + 1) * LANES]
        iq_ref[:, h * LANES:(h + 1) * LANES] = _rope_tile(x, cos_i, sin_i, half_i, d_i).astype(iq_ref.dtype)
    x = z_ref[:, ATT_IK0:ATT_IK0 + LANES]
    d_k = jnp.where(lane < IDX_DIM, lane, IDX_DIM)
    y = _rope_tile(x, cos_i, sin_i, half_i, d_k)
    ikw_ref[...] = y
    ikb_ref[...] = y[:, :IDX_DIM].astype(ikb_ref.dtype)


def _rope_split(z_att, t_len, pos0):
    m = z_att.shape[0]
    tm = min(m, 256)
    lane = jnp.arange(LANES)
    rd_a = A_HEAD_DIM // ROPE_FRAC
    rd_i = IDX_DIM // ROPE_FRAC
    invf_a = ROPE_THETA ** (-((lane % (rd_a // 2)).astype(F32) * 2.0 / rd_a))
    invf_i = ROPE_THETA ** (-((lane % (rd_i // 2)).astype(F32) * 2.0 / rd_i))
    invf = jnp.zeros((8, LANES), F32).at[0].set(invf_a).at[1].set(invf_i)
    row_spec = lambda w: pl.BlockSpec((tm, w), lambda i: (i, 0))
    shp = lambda w, dt: jax.ShapeDtypeStruct((m, w), dt)
    return pl.pallas_call(
        functools.partial(_rope_kernel, tm=tm, t_len=t_len, pos0=pos0),
        out_shape=(shp(A_WIDTH, BF16), shp(KV_W, F32), shp(KV_W, BF16), shp(KV_W, F32), shp(KV_W, BF16),
                   shp(IQ_W, BF16), shp(LANES, F32), shp(IDX_DIM, BF16)),
        grid=(m // tm,),
        in_specs=[row_spec(ATT_COLS), pl.BlockSpec((8, LANES), lambda i: (0, 0))],
        out_specs=(row_spec(A_WIDTH), row_spec(KV_W), row_spec(KV_W), row_spec(KV_W), row_spec(KV_W),
                   row_spec(IQ_W), row_spec(LANES), row_spec(IDX_DIM)),
        compiler_params=_cparams(("parallel",)),
        name="rope_split",
    )(z_att, invf)


def _tile(x, n, axis):
    return x if n == 1 else jnp.concatenate([x] * n, axis=axis)


def _sortable(score):
    u = lax.bitcast_convert_type(score, I32)
    return jnp.where(u < 0, u ^ jnp.int32(0x7FFFFFFF), u)


def _dsa_kernel(q_ref, iq_ref, ikw_ref, k_ref, v_ref, ik_ref, o_ref,
                key_ref, iwb_ref, qs_ref, m_ref, l_ref, acc_ref, *, tq, tk, pos0, n_keys, topk):
    i = pl.program_id(1)
    q0 = pos0 + i * tq
    kmax = jnp.minimum((lax.div(q0 + tq - 1, CHUNK) + 1) * CHUNK, n_keys)
    nkb = lax.div(kmax + tk - 1, tk)
    n_rep = tk // LANES
    nt_dims = (((1,), (1,)), ((), ()))

    for h in range(IDX_HEADS):
        iwb_ref[h] = jnp.broadcast_to(ikw_ref[:, IDX_DIM + h:IDX_DIM + h + 1], (tq, LANES))
    for h in range(A_HEADS):
        qs_ref[h * tq:(h + 1) * tq, :] = q_ref[:, h * LANES:(h + 1) * LANES]

    def score_block(kb, carry):
        koff = pl.multiple_of(kb * tk, tk)
        ikb = ik_ref[pl.ds(koff, tk), :]
        acc = jnp.zeros((tq, tk), F32)
        for h in range(IDX_HEADS):
            d = lax.dot_general(iq_ref[:, h * IDX_DIM:(h + 1) * IDX_DIM], ikb, nt_dims, preferred_element_type=F32)
            acc = acc + jnp.maximum(d, 0.0) * _tile(iwb_ref[h], n_rep, 1)
        kpos = koff + lax.broadcasted_iota(I32, (tq, tk), 1)
        qpos = q0 + lax.broadcasted_iota(I32, (tq, tk), 0)
        adm = (lax.shift_right_logical(kpos, 6) <= lax.shift_right_logical(qpos, 6)) & (kpos < n_keys)
        key_ref[:, pl.ds(koff, tk)] = jnp.where(adm, _sortable(acc * IDX_SCALE), jnp.int32(INT_MIN))
        return carry

    lax.fori_loop(0, nkb, score_block, 0)

    def count(pred):
        def body(kb, cnt):
            koff = pl.multiple_of(kb * tk, tk)
            keys = key_ref[:, pl.ds(koff, tk)]
            kpos = koff + lax.broadcasted_iota(I32, (tq, tk), 1)
            hit = jnp.where(pred(keys, kpos), 1.0, 0.0)
            for c in range(n_rep):
                cnt = cnt + hit[:, c * LANES:(c + 1) * LANES]
            return cnt
        cnt = lax.fori_loop(0, nkb, body, jnp.zeros((tq, LANES), F32))
        return jnp.broadcast_to(jnp.sum(cnt, axis=-1, keepdims=True), (tq, LANES))

    def wide(x):
        return _tile(x, n_rep, 1)

    def bit_step(it, tu):
        cand_u = tu | lax.shift_left(jnp.int32(1), 31 - it)
        cand_s = wide(cand_u ^ jnp.int32(INT_MIN))
        cnt = count(lambda keys, kpos: keys >= cand_s)
        return jnp.where(cnt >= topk, cand_u, tu)

    tu = lax.fori_loop(0, 32, bit_step, jnp.zeros((tq, LANES), I32))
    thr = jnp.maximum(tu ^ jnp.int32(INT_MIN), jnp.int32(INT_MIN + 1))
    thr_w = wide(thr)

    n_ge = count(lambda keys, kpos: keys >= thr_w)
    n_gt = count(lambda keys, kpos: keys > thr_w)
    excess = n_ge > topk

    @pl.when(jnp.max(jnp.where(excess, 1.0, 0.0)) > 0.0)
    def _():
        need = topk - n_gt

        idx_bits = int(key_ref.shape[1]).bit_length()

        def idx_step(it, jm):
            cand = wide(jm | lax.shift_left(jnp.int32(1), idx_bits - 1 - it))
            cnt = count(lambda keys, kpos: (keys == thr_w) & (kpos < cand))
            return jnp.where(cnt < need, cand[:, :LANES], jm)

        jm = lax.fori_loop(0, idx_bits, idx_step, jnp.zeros((tq, LANES), I32))
        jm_w = wide(jnp.where(excess, jm, jnp.int32(2 ** 31 - 1)))

        def drop(kb, carry):
            koff = pl.multiple_of(kb * tk, tk)
            keys = key_ref[:, pl.ds(koff, tk)]
            kpos = koff + lax.broadcasted_iota(I32, (tq, tk), 1)
            key_ref[:, pl.ds(koff, tk)] = jnp.where((keys == thr_w) & (kpos > jm_w), jnp.int32(INT_MIN), keys)
            return carry

        lax.fori_loop(0, nkb, drop, 0)

    m_ref[...] = jnp.full(m_ref.shape, NEG_BIG, F32)
    l_ref[...] = jnp.zeros(l_ref.shape, F32)
    acc_ref[...] = jnp.zeros(acc_ref.shape, F32)
    rows = A_GROUP * tq

    def attend(kb, carry):
        koff = pl.multiple_of(kb * tk, tk)
        bias = _tile(jnp.where(key_ref[:, pl.ds(koff, tk)] >= thr_w, 0.0, NEG_BIG), A_GROUP, 0)
        for n in range(A_KV_HEADS):
            r0 = n * rows
            kn = k_ref[pl.ds(koff, tk), n * LANES:(n + 1) * LANES]
            vn = v_ref[pl.ds(koff, tk), n * LANES:(n + 1) * LANES]
            s = lax.dot_general(qs_ref[r0:r0 + rows, :], kn, nt_dims, preferred_element_type=F32) + bias
            m_prev = m_ref[r0:r0 + rows, :]
            m_new = jnp.maximum(m_prev, jnp.max(s, axis=-1, keepdims=True))
            alpha = jnp.exp2(m_prev - m_new)
            p = jnp.exp2(s - _tile(m_new, n_rep, 1))
            l_ref[r0:r0 + rows, :] = alpha * l_ref[r0:r0 + rows, :] + jnp.sum(p, axis=-1, keepdims=True)
            acc_ref[r0:r0 + rows, :] = alpha * acc_ref[r0:r0 + rows, :] + jnp.dot(
                p.astype(vn.dtype), vn, preferred_element_type=F32)
            m_ref[r0:r0 + rows, :] = m_new
        return carry

    lax.fori_loop(0, nkb, attend, 0)
    for h in range(A_HEADS):
        o_ref[:, h * LANES:(h + 1) * LANES] = (
            acc_ref[h * tq:(h + 1) * tq, :] / l_ref[h * tq:(h + 1) * tq, :]).astype(o_ref.dtype)


def _dsa(q_bf, iq_bf, ikw, k_all, v_all, ik_all, *, n_batch, t_len, pos0, n_keys):
    lp = k_all.shape[1]
    tq = min(t_len, 128)
    tk = min(lp, 512)
    assert lp % tk == 0 and t_len % tq == 0
    nq = t_len // tq
    topk = min(TOPK_MAX, n_keys // 4)
    qrow = lambda w: pl.BlockSpec((tq, w), lambda b, i: (b * nq + i, 0))
    kv_spec = lambda w: pl.BlockSpec((None, lp, w), lambda b, i: (b, 0, 0), pipeline_mode=pl.Buffered(1))
    return pl.pallas_call(
        functools.partial(_dsa_kernel, tq=tq, tk=tk, pos0=pos0, n_keys=n_keys, topk=float(topk)),
        out_shape=jax.ShapeDtypeStruct((n_batch * t_len, A_WIDTH), BF16),
        grid=(n_batch, nq),
        in_specs=[qrow(A_WIDTH), qrow(IQ_W), qrow(LANES), kv_spec(KV_W), kv_spec(KV_W), kv_spec(IDX_DIM)],
        out_specs=qrow(A_WIDTH),
        scratch_shapes=[
            pltpu.VMEM((tq, lp), I32),
            pltpu.VMEM((IDX_HEADS, tq, LANES), F32),
            pltpu.VMEM((A_HEADS * tq, LANES), BF16),
            pltpu.VMEM((A_HEADS * tq, LANES), F32),
            pltpu.VMEM((A_HEADS * tq, LANES), F32),
            pltpu.VMEM((A_HEADS * tq, LANES), F32),
        ],
        compiler_params=_cparams(("parallel", "arbitrary")),
        name="dsa",
    )(q_bf, iq_bf, ikw, k_all, v_all, ik_all)


def _head_sums(x, ones_bd):
    n = x.shape[1] // LANES
    tm = x.shape[0]
    stacked = jnp.concatenate([x[:, c * LANES:(c + 1) * LANES] for c in range(n)], axis=0)
    s = jnp.dot(stacked, ones_bd, precision=HIGHEST, preferred_element_type=F32)
    return jnp.concatenate([s[c * tm:(c + 1) * tm, :] for c in range(n)], axis=1)


def _rwkv_pre_kernel(z_ref, zp_ref, z0_ref, mu_ref, w0_ref, a0_ref, kk_ref, ka_ref, w2_ref, a2_ref, g2_ref,
                     ones_ref, r_out, w_out, k_out, v_out, nkk_out, b_out, g_out, *, tm):
    i = pl.program_id(1)
    row = lax.broadcasted_iota(I32, (tm, 1), 0)

    def mixed(c0, width):
        z = z_ref[:, c0:c0 + width]
        first = jnp.where(i == 0, z0_ref[:, c0:c0 + width], zp_ref[7:8, c0:c0 + width])
        shifted = jnp.where(row == 0, first, pltpu.roll(z, 1, axis=0))
        return z + (shifted - z) * mu_ref[:, c0:c0 + width]

    r = mixed(0, R_WIDTH)
    k = mixed(R_WIDTH, R_WIDTH)
    v = mixed(2 * R_WIDTH, R_WIDTH)
    wd = mixed(RW_WD0, LORA_PAD)
    ad = mixed(RW_AD0, LORA_PAD)
    gd = mixed(RW_GD0, GATE_LORA)
    lora = lambda x, w_ref: jnp.dot(x, w_ref[...], precision=HIGHEST, preferred_element_type=F32)
    y = -(w0_ref[...] + lora(jnp.tanh(wd), w2_ref))
    softplus = jnp.maximum(y, 0.0) + jnp.log(1.0 + jnp.exp(-jnp.abs(y)))
    decay = jnp.exp(-jnp.exp(-softplus - 0.5))
    a = jax.nn.sigmoid(a0_ref[...] + lora(ad, a2_ref))
    g = lora(jax.nn.sigmoid(gd), g2_ref)
    kk = k * kk_ref[...]
    kk = kk / jnp.maximum(jnp.sqrt(_head_sums(kk * kk, ones_ref[...])), 1e-12)
    r_out[...] = r
    w_out[...] = decay
    k_out[...] = k * (1.0 + (a - 1.0) * ka_ref[...])
    v_out[...] = v
    nkk_out[...] = -kk
    b_out[...] = kk * a
    g_out[...] = g


def _rwkv_pre(z_rw, z0, mu, w0, a0, k_k, k_a, w2, a2, g2, ones_bd):
    nb, t_len, _ = z_rw.shape
    tm = min(t_len, 128)
    zrow = pl.BlockSpec((None, tm, RW_COLS), lambda b, i: (b, i, 0))
    zprev = pl.BlockSpec((None, 8, RW_COLS), lambda b, i: (b, jnp.maximum(i * (tm // 8) - 1, 0), 0))
    full = lambda a: pl.BlockSpec(a.shape, lambda b, i: (0,) * a.ndim)
    orow = pl.BlockSpec((None, tm, R_WIDTH), lambda b, i: (b, i, 0))
    params = (mu, w0, a0, k_k, k_a, w2, a2, g2, ones_bd)
    return pl.pallas_call(
        functools.partial(_rwkv_pre_kernel, tm=tm),
        out_shape=tuple(jax.ShapeDtypeStruct((nb, t_len, R_WIDTH), F32) for _ in range(7)),
        grid=(nb, t_len // tm),
        in_specs=[zrow, zprev, pl.BlockSpec((None, 1, RW_COLS), lambda b, i: (b, 0, 0))] + [full(p) for p in params],
        out_specs=tuple(orow for _ in range(7)),
        compiler_params=_cparams(("parallel", "arbitrary")),
        name="rwkv_pre",
    )(z_rw, z_rw, z0, *params)


def _split2(x):
    hi = x.astype(BF16)
    return hi, (x - hi.astype(F32)).astype(BF16)


def _rwkv_scan_kernel(r_ref, w_ref, k_ref, nkk_ref, b_ref, vx_ref, s0_ref, ones2_ref, hot_ref, spread_ref,
                      y_ref, st_ref, s_ref, xs_ref, vc_ref, *, tb, unroll):
    tblk = pl.program_id(1)

    @pl.when(tblk == 0)
    def _():
        s_ref[...] = s0_ref[...]

    for sub in range(tb // SCAN_SUB):
        for g in range(SCAN_GROUPS):
            hi, mid = _split2(vx_ref[sub, g])
            xs_ref[sub, g * R_HEAD_DIM:(g + 1) * R_HEAD_DIM, :] = jnp.concatenate([hi, mid], axis=1)
    ones2 = ones2_ref[...]
    spread = spread_ref[...]
    y_ref[...] = jnp.zeros(y_ref.shape, F32)
    lane_t = lax.rem(lax.broadcasted_iota(I32, (R_HEAD_DIM, 2 * tb), 1), tb)
    cat = lambda xs, ax=0: jnp.concatenate(xs, axis=ax)
    rowp = lambda ref, t, p: ref[t, p:p + 1, :]
    group_pairs = R_PAIRS // SCAN_MATMULS
    half = group_pairs // 2 * R_HEAD_DIM

    groups = [range(g * group_pairs, (g + 1) * group_pairs) for g in range(SCAN_MATMULS)]

    def packed(xs):
        return cat([cat(xs[2 * j:2 * j + 2], 1) for j in range(group_pairs // 2)])

    def pair_tile(res, q):
        return res[(q // 2) * R_HEAD_DIM:(q // 2 + 1) * R_HEAD_DIM, (q % 2) * LANES:(q % 2 + 1) * LANES]

    def head_sums(g, states, t_sa, t_y=None):
        rows = [packed([(s * rowp(nkk_ref, t_sa, p)).astype(BF16) for s, p in zip(states, groups[g])])]
        if t_y is not None:
            rows.append(packed([(s * rowp(r_ref, t_y, p)).astype(BF16) for s, p in zip(states, groups[g])]))
        return jnp.dot(cat(rows), ones2, preferred_element_type=F32)

    def sub_block(sub, sas):
        def bulk(i, c):
            xs = xs_ref[sub]
            n = xs.shape[0]
            res = jnp.dot(cat([xs * hot_ref[SCAN_BULK * i + j, 0:1, :] for j in range(SCAN_BULK)]), spread,
                          preferred_element_type=F32)
            for j in range(SCAN_BULK):
                vc_ref[SCAN_BULK * i + j] = res[j * n:(j + 1) * n]
            return c

        lax.fori_loop(0, SCAN_SUB // SCAN_BULK, bulk, 0)

        def step(tt, sas):
            t = sub * SCAN_SUB + tt
            t_next = jnp.minimum(t + 1, tb - 1)
            hit = lane_t == t
            new_sas = []
            for g in range(SCAN_MATMULS):
                states = []
                for q, p in enumerate(groups[g]):
                    vg, vq = divmod(p, SCAN_GROUP_PAIRS)
                    vc = vc_ref[tt, vg * R_HEAD_DIM:(vg + 1) * R_HEAD_DIM, vq * LANES:(vq + 1) * LANES]
                    s_new = (s_ref[p] * rowp(w_ref, t, p) + pair_tile(sas[g], q) * rowp(b_ref, t, p)
                             + vc * rowp(k_ref, t, p))
                    s_ref[p] = s_new
                    states.append(s_new)
                res = head_sums(g, states, t_next, t)
                new_sas.append(res[:half])
                for q, p in enumerate(groups[g]):
                    y_cols = pair_tile(res[half:], q)[:, R_HEAD_DIM - tb:R_HEAD_DIM + tb]
                    y_ref[p] = jnp.where(hit, y_cols, y_ref[p])
            return tuple(new_sas)

        return lax.fori_loop(0, SCAN_SUB, step, sas, unroll=unroll)

    first = tuple(head_sums(g, [s_ref[p] for p in groups[g]], 0) for g in range(SCAN_MATMULS))
    lax.fori_loop(0, tb // SCAN_SUB, sub_block, first)

    @pl.when(tblk == pl.num_programs(1) - 1)
    def _():
        st_ref[...] = s_ref[...]


def _rwkv_scan(r, w, k, nkk, b, v, s0):
    nb, t_len, _ = r.shape
    tb = min(t_len, 64)
    nblk = t_len // tb
    hd = R_HEAD_DIM
    nsub, sb, ng, gp = tb // SCAN_SUB, SCAN_SUB, SCAN_GROUPS, SCAN_GROUP_PAIRS
    vx = v.reshape(nb, nblk, nsub, sb, ng, gp, 2, hd).transpose(0, 1, 2, 4, 7, 5, 6, 3).reshape(
        nb, nblk, nsub, ng, hd, LANES)
    s0p = s0.reshape(nb, R_PAIRS, 2, hd, hd).transpose(0, 1, 3, 2, 4).reshape(nb, R_PAIRS, hd, LANES)
    lane_h = jnp.arange(LANES) // hd
    ones_bd = lane_h[:, None] == lane_h[None, :]
    lane2_h = jnp.arange(2 * LANES) // hd
    ones2 = (lane2_h[:, None] == lane2_h[None, :]).astype(BF16)
    src = jnp.arange(2 * LANES) % LANES
    src_q, src_h, src_t = src // (2 * sb), (src // sb) % 2, src % sb
    dst = jnp.arange(gp * LANES)
    dst_q, dst_h = dst // LANES, (dst % LANES) // hd
    spread = ((src_q[:, None] == dst_q[None, :]) & (src_h[:, None] == dst_h[None, :])).astype(BF16)
    hot = jnp.broadcast_to((src_t[None, :] == jnp.arange(sb)[:, None])[:, None, :], (sb, 16, 2 * LANES)).astype(BF16)
    trow = pl.BlockSpec((None, tb, R_PAIRS, LANES), lambda bb, i: (bb, i, 0, 0))
    r, w, k, nkk, b = (a.reshape(nb, t_len, R_PAIRS, LANES) for a in (r, w, k, nkk, b))
    col = pl.BlockSpec((None, None, nsub, ng, hd, LANES), lambda bb, i: (bb, i, 0, 0, 0, 0))
    st = pl.BlockSpec((None, R_PAIRS, hd, LANES), lambda bb, i: (bb, 0, 0, 0))
    full = lambda a: pl.BlockSpec(a.shape, lambda bb, i: (0,) * a.ndim)
    ycol_spec = pl.BlockSpec((None, None, R_PAIRS, hd, 2 * tb), lambda bb, i: (bb, i, 0, 0, 0))
    ycol, s_t = pl.pallas_call(
        functools.partial(_rwkv_scan_kernel, tb=tb, unroll=4),
        out_shape=(jax.ShapeDtypeStruct((nb, nblk, R_PAIRS, hd, 2 * tb), F32),
                   jax.ShapeDtypeStruct((nb, R_PAIRS, hd, LANES), F32)),
        grid=(nb, nblk),
        in_specs=[trow, trow, trow, trow, trow, col, st, full(ones2), full(hot), full(spread)],
        out_specs=(ycol_spec, st),
        scratch_shapes=[pltpu.VMEM((R_PAIRS, hd, LANES), F32), pltpu.VMEM((nsub, ng * hd, 2 * LANES), BF16),
                        pltpu.VMEM((sb, ng * hd, gp * LANES), F32)],
        compiler_params=_cparams(("parallel", "arbitrary")),
        name="rwkv_scan",
    )(r, w, k, nkk, b, vx, s0p, ones2, hot, spread)
    y = ycol.reshape(nb, nblk, R_PAIRS, hd, 2, tb).transpose(0, 1, 5, 2, 4, 3).reshape(nb, t_len, R_WIDTH)
    s_t = s_t.reshape(nb, R_PAIRS, hd, 2, hd).transpose(0, 1, 3, 2, 4).reshape(nb, R_HEADS, hd, hd)
    return y, s_t


def _rwkv_post_kernel(y_ref, r_ref, k_ref, v_ref, g_ref, lw_ref, lb_ref, rk_ref, ones_ref, o_ref):
    ones_bd = ones_ref[...]
    y = y_ref[...]
    mean = _head_sums(y, ones_bd) * (1.0 / R_HEAD_DIM)
    d = y - mean
    var = _head_sums(d * d, ones_bd) * (1.0 / R_HEAD_DIM)
    yn = d * lax.rsqrt(var + GN_EPS) * lw_ref[...] + lb_ref[...]
    bonus = _head_sums(r_ref[...] * k_ref[...] * rk_ref[...], ones_bd) * v_ref[...]
    o_ref[...] = ((yn + bonus) * g_ref[...]).astype(o_ref.dtype)


def _rwkv_post(y, r, k, v, g, lnx_w, lnx_b, r_k, ones_bd):
    m = y.shape[0]
    tm = min(m, 128)
    row = pl.BlockSpec((tm, R_WIDTH), lambda i: (i, 0))
    full = lambda a: pl.BlockSpec(a.shape, lambda i: (0,) * a.ndim)
    params = (lnx_w, lnx_b, r_k, ones_bd)
    return pl.pallas_call(
        _rwkv_post_kernel,
        out_shape=jax.ShapeDtypeStruct((m, R_WIDTH), BF16),
        grid=(m // tm,),
        in_specs=[row] * 5 + [full(p) for p in params],
        out_specs=row,
        compiler_params=_cparams(("parallel",)),
        name="rwkv_post",
    )(y, r, k, v, g, *params)


def _ffn_gate_kernel(g_ref, u_ref, gp_ref, c0_ref, cw_ref, cb_ref, o_ref, *, tm):
    i = pl.program_id(1)
    gate = g_ref[...]
    row = lax.broadcasted_iota(I32, (tm, 1), 0)
    prev1 = jnp.where(i == 0, c0_ref[1:2, :], gp_ref[7:8, :])
    prev2 = jnp.where(i == 0, c0_ref[0:1, :], gp_ref[6:7, :])
    g_m1 = jnp.where(row == 0, prev1, pltpu.roll(gate, 1, axis=0))
    g_m2 = jnp.where(row == 0, prev2, jnp.where(row == 1, prev1, pltpu.roll(gate, 2, axis=0)))
    conv = cb_ref[...] + g_m2 * cw_ref[0:1, :]
    conv = conv + g_m1 * cw_ref[1:2, :]
    conv = conv + gate * cw_ref[2:3, :]
    o_ref[...] = (conv * jax.nn.sigmoid(conv) * u_ref[...]).astype(o_ref.dtype)


def _ffn_gate(gate_up, conv0, conv_w, conv_b, n_batch, t_len):
    d_ff = gate_up.shape[1] // 2
    tm = min(t_len, 64)
    nt = t_len // tm
    gu = gate_up.reshape(n_batch, t_len, 2 * d_ff)
    row = lambda c: pl.BlockSpec((None, tm, d_ff), lambda b, i: (b, i, c))
    prev = pl.BlockSpec((None, 8, d_ff), lambda b, i: (b, jnp.maximum(i * (tm // 8) - 1, 0), 0))
    act = pl.pallas_call(
        functools.partial(_ffn_gate_kernel, tm=tm),
        out_shape=jax.ShapeDtypeStruct((n_batch, t_len, d_ff), BF16),
        grid=(n_batch, nt),
        in_specs=[row(0), row(1), prev, pl.BlockSpec((None, CONV_W - 1, d_ff), lambda b, i: (b, 0, 0)),
                  pl.BlockSpec((CONV_W, d_ff), lambda b, i: (0, 0)), pl.BlockSpec((1, d_ff), lambda b, i: (0, 0))],
        out_specs=pl.BlockSpec((None, tm, d_ff), lambda b, i: (b, i, 0)),
        compiler_params=_cparams(("parallel", "arbitrary")),
        name="ffn_gate",
    )(gu, gu, gu, conv0, conv_w, conv_b.reshape(1, d_ff))
    return act.reshape(n_batch * t_len, d_ff)


def _pad_rw_cols(a):
    z = lambda n: jnp.zeros(a.shape[:-1] + (n,), a.dtype)
    wd0, ad0, gd0 = 3 * R_WIDTH, 3 * R_WIDTH + DECAY_LORA, 3 * R_WIDTH + DECAY_LORA + AAA_LORA
    return jnp.concatenate([a[..., :wd0], a[..., wd0:ad0], z(LORA_PAD - DECAY_LORA), a[..., ad0:gd0],
                            z(LORA_PAD - AAA_LORA), a[..., gd0:]], axis=-1)


def _unpad_rw_cols(a):
    return jnp.concatenate([a[..., :RW_WD0 + DECAY_LORA], a[..., RW_AD0:RW_AD0 + AAA_LORA], a[..., RW_GD0:]], axis=-1)


def _prep_weights(norm_mix_g, w_in, rwkv_mu, rwkv_w0, rwkv_w2, rwkv_a0, rwkv_a2, rwkv_g2, rwkv_k_k, rwkv_k_a,
                  rwkv_r_k, rwkv_lnx_w, rwkv_lnx_b, w_out, norm_ffn_g, ffn_w_in, ffn_conv_w, ffn_conv_b,
                  ffn_w_down, norm_final_g, l):
    d = w_in.shape[1]
    w_att = jnp.concatenate([w_in[l][:, :ATT_USED].astype(BF16), jnp.zeros((d, ATT_COLS - ATT_USED), BF16)], axis=1)
    w_rw = _pad_rw_cols(w_in[l][:, ATT_USED:].astype(BF16))
    row = lambda a: a.reshape(1, -1).astype(F32)
    pad_rows = lambda a, n: jnp.concatenate([a, jnp.zeros((n - a.shape[0], a.shape[1]), a.dtype)], axis=0)
    lane_h = jnp.arange(LANES) // R_HEAD_DIM
    return dict(
        norm_mix_g=norm_mix_g[l], w_att=w_att, w_rw=w_rw,
        mu=_pad_rw_cols(row(rwkv_mu[l])), w0=row(rwkv_w0[l]), a0=row(rwkv_a0[l]),
        k_k=row(rwkv_k_k[l]), k_a=row(rwkv_k_a[l]),
        w2=pad_rows(rwkv_w2[l], LORA_PAD), a2=pad_rows(rwkv_a2[l], LORA_PAD), g2=rwkv_g2[l],
        r_k=row(rwkv_r_k[l]), lnx_w=row(rwkv_lnx_w[l]), lnx_b=row(rwkv_lnx_b[l]),
        ones_bd=(lane_h[:, None] == lane_h[None, :]).astype(F32),
        w_out_a=w_out[l][:A_WIDTH].astype(BF16), w_out_r=w_out[l][A_WIDTH:].astype(BF16),
        norm_ffn_g=norm_ffn_g[l], ffn_w_in=ffn_w_in[l].astype(BF16), conv_w=ffn_conv_w[l], conv_b=ffn_conv_b[l],
        ffn_w_down=ffn_w_down[l].astype(BF16), norm_final_g=norm_final_g,
    )


def _trunk(x, past_k, past_v, past_ik, s0, shift0, conv0, wt):
    nb, t_len, d = x.shape
    m = nb * t_len
    p_len = 0 if past_k is None else past_k.shape[1]
    n_keys = p_len + t_len
    x2 = x.reshape(m, d)

    h = _rmsnorm(x2, wt["norm_mix_g"], BF16)
    z_att = _matmul([h], [wt["w_att"]], name="proj_att")
    z_rw = _matmul([h], [wt["w_rw"]], name="proj_rw")

    q_bf, k_f, k_bf, v_f, v_bf, iq_bf, ikw, ik_bf = _rope_split(z_att, t_len, p_len)
    tk = min(-(-n_keys // LANES) * LANES, 512)
    lp = -(-n_keys // tk) * tk

    def with_past(new, past, width):
        new = new.reshape(nb, t_len, width)
        parts = [new] if past is None else [past.reshape(nb, p_len, width).astype(BF16), new]
        if lp > n_keys:
            parts.append(jnp.zeros((nb, lp - n_keys, width), BF16))
        return parts[0] if len(parts) == 1 else jnp.concatenate(parts, axis=1)

    attn = _dsa(q_bf, iq_bf, ikw, with_past(k_bf, past_k, KV_W), with_past(v_bf, past_v, KV_W),
                with_past(ik_bf, past_ik, IDX_DIM), n_batch=nb, t_len=t_len, pos0=p_len, n_keys=n_keys)

    z_rw3 = z_rw.reshape(nb, t_len, RW_COLS)
    r, w, k2, v2, nkk, b, g = _rwkv_pre(z_rw3, _pad_rw_cols(shift0.astype(F32)), wt["mu"], wt["w0"], wt["a0"],
                                       wt["k_k"], wt["k_a"], wt["w2"], wt["a2"], wt["g2"], wt["ones_bd"])
    y, s_t = _rwkv_scan(r, w, k2, nkk, b, v2, s0.astype(F32))
    flat = lambda a: a.reshape(m, R_WIDTH)
    rw = _rwkv_post(flat(y), flat(r), flat(k2), flat(v2), flat(g), wt["lnx_w"], wt["lnx_b"], wt["r_k"], wt["ones_bd"])

    x1 = _matmul([attn, rw], [wt["w_out_a"], wt["w_out_r"]], res=x2, name="out_proj")
    hf = _rmsnorm(x1, wt["norm_ffn_g"], BF16)
    gate_up = _matmul([hf], [wt["ffn_w_in"]], name="ffn_in")
    d_ff = gate_up.shape[1] // 2
    act = _ffn_gate(gate_up, conv0.astype(F32), wt["conv_w"], wt["conv_b"], nb, t_len)
    x3 = _matmul([act], [wt["ffn_w_down"]], res=x1, name="ffn_down")
    y_out = _rmsnorm(x3, wt["norm_final_g"], F32).reshape(nb, t_len, d)

    gate3 = gate_up.reshape(nb, t_len, 2 * d_ff)[:, :, :d_ff]
    conv_t = jnp.concatenate([conv0.astype(F32), gate3], axis=1)[:, -(CONV_W - 1):]
    shift_t = _unpad_rw_cols(z_rw3[:, -1:])
    caches = (k_f.reshape(nb, t_len, A_KV_HEADS, A_HEAD_DIM)[None], v_f.reshape(nb, t_len, A_KV_HEADS, A_HEAD_DIM)[None],
              ikw[:, :IDX_DIM].reshape(nb, t_len, IDX_DIM)[None], s_t[None], shift_t[None], conv_t[None])
    return y_out, caches


def kernel(x_prompt, x_sample, cache_k, cache_v, cache_idx_k, state_rwkv, state_rwkv_shift, state_ffn_conv, norm_mix_g, w_in, rwkv_mu, rwkv_w0, rwkv_w2, rwkv_a0, rwkv_a2, rwkv_g2, rwkv_k_k, rwkv_k_a, rwkv_r_k, rwkv_lnx_w, rwkv_lnx_b, w_out, norm_ffn_g, ffn_w_in, ffn_conv_w, ffn_conv_b, ffn_w_down, norm_final_g):
    assert w_in.shape[0] == 1, "single-layer trunk"
    wt = _prep_weights(norm_mix_g, w_in, rwkv_mu, rwkv_w0, rwkv_w2, rwkv_a0, rwkv_a2, rwkv_g2, rwkv_k_k, rwkv_k_a,
                       rwkv_r_k, rwkv_lnx_w, rwkv_lnx_b, w_out, norm_ffn_g, ffn_w_in, ffn_conv_w, ffn_conv_b,
                       ffn_w_down, norm_final_g, 0)
    bp = x_prompt.shape[0]
    d_ff = ffn_conv_w.shape[-1]
    y_p, c_p = _trunk(x_prompt, None, None, None,
                      jnp.zeros((bp, R_HEADS, R_HEAD_DIM, R_HEAD_DIM), F32), jnp.zeros((bp, 1, RWKV_COLS), F32),
                      jnp.zeros((bp, CONV_W - 1, d_ff), F32), wt)
    y_s, c_s = _trunk(x_sample, cache_k[0], cache_v[0], cache_idx_k[0], state_rwkv[0], state_rwkv_shift[0],
                      state_ffn_conv[0], wt)
    return (y_p, y_s) + c_p + c_s
```

```python
import functools

import jax
import jax.numpy as jnp
from jax import lax
from jax.experimental import pallas as pl
from jax.experimental.pallas import tpu as pltpu

F32 = jnp.float32
BF16 = jnp.bfloat16
I32 = jnp.int32

CHUNK = 64
A_HEADS = 16
A_KV_HEADS = 4
A_HEAD_DIM = 128
A_GROUP = A_HEADS // A_KV_HEADS
A_WIDTH = A_HEADS * A_HEAD_DIM
KV_W = A_KV_HEADS * A_HEAD_DIM
IDX_HEADS = 16
IDX_DIM = 64
IQ_W = IDX_HEADS * IDX_DIM
TOPK_MAX = 256
ROPE_THETA = 500000.0
ROPE_FRAC = 4
A_SCALE = A_HEAD_DIM ** -0.5
Q_SCALE = A_SCALE * 1.4426950408889634
IDX_SCALE = (IDX_HEADS ** -0.5) * (IDX_DIM ** -0.5)
R_HEAD_DIM = 64
R_WIDTH = 2048
R_HEADS = R_WIDTH // R_HEAD_DIM
R_PAIRS = R_HEADS // 2
DECAY_LORA = 96
AAA_LORA = 96
GATE_LORA = 256
RWKV_COLS = 3 * R_WIDTH + DECAY_LORA + AAA_LORA + GATE_LORA
GN_EPS = 6.4e-4
CONV_W = 3
RMS_EPS = 1e-6

LANES = 128
VMEM_LIMIT = 56 * 1024 * 1024

ATT_Q0, ATT_K0, ATT_V0, ATT_IQ0, ATT_IK0 = 0, A_WIDTH, A_WIDTH + KV_W, A_WIDTH + 2 * KV_W, A_WIDTH + 2 * KV_W + IQ_W
ATT_USED = ATT_IK0 + IDX_DIM + IDX_HEADS
ATT_COLS = 4608
LORA_PAD = 128
RW_WD0 = 3 * R_WIDTH
RW_AD0 = RW_WD0 + LORA_PAD
RW_GD0 = RW_AD0 + LORA_PAD
RW_COLS = RW_GD0 + GATE_LORA
SCAN_SUB = 16
SCAN_GROUP_PAIRS = LANES // (2 * SCAN_SUB)
SCAN_GROUPS = R_PAIRS // SCAN_GROUP_PAIRS
SCAN_MATMULS = 2
SCAN_BULK = 4
INT_MIN = -2 ** 31
NEG_BIG = -1e30
HIGHEST = lax.Precision.HIGHEST


def _cparams(sem):
    return pltpu.CompilerParams(dimension_semantics=sem, vmem_limit_bytes=VMEM_LIMIT)


def _rmsnorm_kernel(x_ref, g_ref, o_ref):
    x = x_ref[...]
    y = x * lax.rsqrt(jnp.mean(x * x, axis=-1, keepdims=True) + RMS_EPS)
    o_ref[...] = (y * g_ref[...]).astype(o_ref.dtype)


def _rmsnorm(x, g, out_dtype):
    m, d = x.shape
    tm = min(m, 256)
    return pl.pallas_call(
        _rmsnorm_kernel,
        out_shape=jax.ShapeDtypeStruct((m, d), out_dtype),
        grid=(m // tm,),
        in_specs=[pl.BlockSpec((tm, d), lambda i: (i, 0)), pl.BlockSpec((1, d), lambda i: (0, 0))],
        out_specs=pl.BlockSpec((tm, d), lambda i: (i, 0)),
        compiler_params=_cparams(("parallel",)),
        name="rmsnorm",
    )(x, g.reshape(1, d).astype(F32))


def _mm_kernel(*refs, n_pairs, has_res):
    o_ref = refs[-1]
    acc = jnp.dot(refs[0][...], refs[n_pairs][...], preferred_element_type=F32)
    for p in range(1, n_pairs):
        acc = acc + jnp.dot(refs[p][...], refs[n_pairs + p][...], preferred_element_type=F32)
    if has_res:
        acc = refs[2 * n_pairs][...] + acc
    o_ref[...] = acc.astype(o_ref.dtype)


def _matmul(a_list, b_list, res=None, tm=512, tn=512, name="matmul"):
    m = a_list[0].shape[0]
    n = b_list[0].shape[1]
    tm = min(tm, m)
    tn = min(tn, n)
    assert m % tm == 0 and n % tn == 0, (m, n, tm, tn)
    in_specs = [pl.BlockSpec((tm, a.shape[1]), lambda i, j: (i, 0)) for a in a_list]
    in_specs += [pl.BlockSpec((b.shape[0], tn), lambda i, j: (0, j)) for b in b_list]
    args = list(a_list) + list(b_list)
    if res is not None:
        in_specs.append(pl.BlockSpec((tm, tn), lambda i, j: (i, j)))
        args.append(res)
    return pl.pallas_call(
        functools.partial(_mm_kernel, n_pairs=len(a_list), has_res=res is not None),
        out_shape=jax.ShapeDtypeStruct((m, n), F32),
        grid=(m // tm, n // tn),
        in_specs=in_specs,
        out_specs=pl.BlockSpec((tm, tn), lambda i, j: (i, j)),
        compiler_params=_cparams(("parallel", "arbitrary")),
        name=name,
    )(*args)


def _rope_tile(x, cos, sin, half, d_in_head):
    lo = d_in_head < half
    hi = (d_in_head >= half) & (d_in_head < 2 * half)
    c = jnp.where(lo | hi, cos, 1.0)
    s_up = jnp.where(lo, -sin, 0.0)
    s_dn = jnp.where(hi, sin, 0.0)
    x_up = pltpu.roll(x, LANES - half, axis=1)
    x_dn = pltpu.roll(x, half, axis=1)
    return x * c + x_up * s_up + x_dn * s_dn


def _rope_kernel(z_ref, invf_ref, q_ref, kf_ref, kb_ref, vf_ref, vb_ref, iq_ref, ikw_ref, ikb_ref, *, tm, t_len, pos0):
    i = pl.program_id(0)
    row = lax.broadcasted_iota(I32, (tm, LANES), 0) + i * tm
    pos = (pos0 + lax.rem(row, t_len)).astype(F32)
    lane = lax.broadcasted_iota(I32, (tm, LANES), 1)
    ang = pos * invf_ref[0:1, :]
    cos_a, sin_a = jnp.cos(ang), jnp.sin(ang)
    half_a = A_HEAD_DIM // ROPE_FRAC // 2
    for h in range(A_HEADS):
        x = z_ref[:, ATT_Q0 + h * LANES:ATT_Q0 + (h + 1) * LANES]
        q_ref[:, h * LANES:(h + 1) * LANES] = (_rope_tile(x, cos_a, sin_a, half_a, lane) * Q_SCALE).astype(q_ref.dtype)
    for h in range(A_KV_HEADS):
        x = z_ref[:, ATT_K0 + h * LANES:ATT_K0 + (h + 1) * LANES]
        y = _rope_tile(x, cos_a, sin_a, half_a, lane)
        kf_ref[:, h * LANES:(h + 1) * LANES] = y
        kb_ref[:, h * LANES:(h + 1) * LANES] = y.astype(kb_ref.dtype)
    v = z_ref[:, ATT_V0:ATT_V0 + KV_W]
    vf_ref[...] = v
    vb_ref[...] = v.astype(vb_ref.dtype)
    ang = pos * invf_ref[1:2, :]
    cos_i, sin_i = jnp.cos(ang), jnp.sin(ang)
    half_i = IDX_DIM // ROPE_FRAC // 2
    d_i = lane & (IDX_DIM - 1)
    for h in range(IQ_W // LANES):
        x = z_ref[:, ATT_IQ0 + h * LANES:ATT_IQ0 + (h + 1) * LANES]
        iq_ref[:, h * LANES:(h + 1) * LANES] = _rope_tile(x, cos_i, sin_i, half_i, d_i).astype(iq_ref.dtype)
    x = z_ref[:, ATT_IK0:ATT_IK0 + LANES]
    d_k = jnp.where(lane < IDX_DIM, lane, IDX_DIM)
    y = _rope_tile(x, cos_i, sin_i, half_i, d_k)
    ikw_ref[...] = y
    ikb_ref[...] = y[:, :IDX_DIM].astype(ikb_ref.dtype)


def _rope_split(z_att, t_len, pos0):
    m = z_att.shape[0]
    tm = min(m, 256)
    lane = jnp.arange(LANES)
    rd_a = A_HEAD_DIM // ROPE_FRAC
    rd_i = IDX_DIM // ROPE_FRAC
    invf_a = ROPE_THETA ** (-((lane % (rd_a // 2)).astype(F32) * 2.0 / rd_a))
    invf_i = ROPE_THETA ** (-((lane % (rd_i // 2)).astype(F32) * 2.0 / rd_i))
    invf = jnp.zeros((8, LANES), F32).at[0].set(invf_a).at[1].set(invf_i)
    row_spec = lambda w: pl.BlockSpec((tm, w), lambda i: (i, 0))
    shp = lambda w, dt: jax.ShapeDtypeStruct((m, w), dt)
    return pl.pallas_call(
        functools.partial(_rope_kernel, tm=tm, t_len=t_len, pos0=pos0),
        out_shape=(shp(A_WIDTH, BF16), shp(KV_W, F32), shp(KV_W, BF16), shp(KV_W, F32), shp(KV_W, BF16),
                   shp(IQ_W, BF16), shp(LANES, F32), shp(IDX_DIM, BF16)),
        grid=(m // tm,),
        in_specs=[row_spec(ATT_COLS), pl.BlockSpec((8, LANES), lambda i: (0, 0))],
        out_specs=(row_spec(A_WIDTH), row_spec(KV_W), row_spec(KV_W), row_spec(KV_W), row_spec(KV_W),
                   row_spec(IQ_W), row_spec(LANES), row_spec(IDX_DIM)),
        compiler_params=_cparams(("parallel",)),
        name="rope_split",
    )(z_att, invf)


def _tile(x, n, axis):
    return x if n == 1 else jnp.concatenate([x] * n, axis=axis)


def _sortable(score):
    u = lax.bitcast_convert_type(score, I32)
    return jnp.where(u < 0, u ^ jnp.int32(0x7FFFFFFF), u)


def _dsa_kernel(q_ref, iq_ref, ikw_ref, k_ref, v_ref, ik_ref, o_ref,
                key_ref, iwb_ref, qs_ref, m_ref, l_ref, acc_ref, *, tq, tk, pos0, n_keys, topk):
    i = pl.program_id(1)
    q0 = pos0 + i * tq
    kmax = jnp.minimum((lax.div(q0 + tq - 1, CHUNK) + 1) * CHUNK, n_keys)
    nkb = lax.div(kmax + tk - 1, tk)
    n_rep = tk // LANES
    nt_dims = (((1,), (1,)), ((), ()))

    for h in range(IDX_HEADS):
        iwb_ref[h] = jnp.broadcast_to(ikw_ref[:, IDX_DIM + h:IDX_DIM + h + 1], (tq, LANES))
    for h in range(A_HEADS):
        qs_ref[h * tq:(h + 1) * tq, :] = q_ref[:, h * LANES:(h + 1) * LANES]

    def score_block(kb, carry):
        koff = pl.multiple_of(kb * tk, tk)
        ikb = ik_ref[pl.ds(koff, tk), :]
        acc = jnp.zeros((tq, tk), F32)
        for h in range(IDX_HEADS):
            d = lax.dot_general(iq_ref[:, h * IDX_DIM:(h + 1) * IDX_DIM], ikb, nt_dims, preferred_element_type=F32)
            acc = acc + jnp.maximum(d, 0.0) * _tile(iwb_ref[h], n_rep, 1)
        kpos = koff + lax.broadcasted_iota(I32, (tq, tk), 1)
        qpos = q0 + lax.broadcasted_iota(I32, (tq, tk), 0)
        adm = (lax.shift_right_logical(kpos, 6) <= lax.shift_right_logical(qpos, 6)) & (kpos < n_keys)
        key_ref[:, pl.ds(koff, tk)] = jnp.where(adm, _sortable(acc * IDX_SCALE), jnp.int32(INT_MIN))
        return carry

    lax.fori_loop(0, nkb, score_block, 0)

    def count(pred):
        def body(kb, cnt):
            koff = pl.multiple_of(kb * tk, tk)
            keys = key_ref[:, pl.ds(koff, tk)]
            kpos = koff + lax.broadcasted_iota(I32, (tq, tk), 1)
            hit = jnp.where(pred(keys, kpos), 1.0, 0.0)
            for c in range(n_rep):
                cnt = cnt + hit[:, c * LANES:(c + 1) * LANES]
            return cnt
        cnt = lax.fori_loop(0, nkb, body, jnp.zeros((tq, LANES), F32))
        return jnp.broadcast_to(jnp.sum(cnt, axis=-1, keepdims=True), (tq, LANES))

    def wide(x):
        return _tile(x, n_rep, 1)

    def bit_step(it, tu):
        cand_u = tu | lax.shift_left(jnp.int32(1), 31 - it)
        cand_s = wide(cand_u ^ jnp.int32(INT_MIN))
        cnt = count(lambda keys, kpos: keys >= cand_s)
        return jnp.where(cnt >= topk, cand_u, tu)

    tu = lax.fori_loop(0, 32, bit_step, jnp.zeros((tq, LANES), I32))
    thr = jnp.maximum(tu ^ jnp.int32(INT_MIN), jnp.int32(INT_MIN + 1))
    thr_w = wide(thr)

    n_ge = count(lambda keys, kpos: keys >= thr_w)
    n_gt = count(lambda keys, kpos: keys > thr_w)
    excess = n_ge > topk

    @pl.when(jnp.max(jnp.where(excess, 1.0, 0.0)) > 0.0)
    def _():
        need = topk - n_gt

        idx_bits = int(key_ref.shape[1]).bit_length()

        def idx_step(it, jm):
            cand = wide(jm | lax.shift_left(jnp.int32(1), idx_bits - 1 - it))
            cnt = count(lambda keys, kpos: (keys == thr_w) & (kpos < cand))
            return jnp.where(cnt < need, cand[:, :LANES], jm)

        jm = lax.fori_loop(0, idx_bits, idx_step, jnp.zeros((tq, LANES), I32))
        jm_w = wide(jnp.where(excess, jm, jnp.int32(2 ** 31 - 1)))

        def drop(kb, carry):
            koff = pl.multiple_of(kb * tk, tk)
            keys = key_ref[:, pl.ds(koff, tk)]
            kpos = koff + lax.broadcasted_iota(I32, (tq, tk), 1)
            key_ref[:, pl.ds(koff, tk)] = jnp.where((keys == thr_w) & (kpos > jm_w), jnp.int32(INT_MIN), keys)
            return carry

        lax.fori_loop(0, nkb, drop, 0)

    m_ref[...] = jnp.full(m_ref.shape, NEG_BIG, F32)
    l_ref[...] = jnp.zeros(l_ref.shape, F32)
    acc_ref[...] = jnp.zeros(acc_ref.shape, F32)
    rows = A_GROUP * tq

    def attend(kb, carry):
        koff = pl.multiple_of(kb * tk, tk)
        bias = _tile(jnp.where(key_ref[:, pl.ds(koff, tk)] >= thr_w, 0.0, NEG_BIG), A_GROUP, 0)
        for n in range(A_KV_HEADS):
            r0 = n * rows
            kn = k_ref[pl.ds(koff, tk), n * LANES:(n + 1) * LANES]
            vn = v_ref[pl.ds(koff, tk), n * LANES:(n + 1) * LANES]
            s = lax.dot_general(qs_ref[r0:r0 + rows, :], kn, nt_dims, preferred_element_type=F32) + bias
            m_prev = m_ref[r0:r0 + rows, :]
            m_new = jnp.maximum(m_prev, jnp.max(s, axis=-1, keepdims=True))
            alpha = jnp.exp2(m_prev - m_new)
            p = jnp.exp2(s - _tile(m_new, n_rep, 1))
            l_ref[r0:r0 + rows, :] = alpha * l_ref[r0:r0 + rows, :] + jnp.sum(p, axis=-1, keepdims=True)
            acc_ref[r0:r0 + rows, :] = alpha * acc_ref[r0:r0 + rows, :] + jnp.dot(
                p.astype(vn.dtype), vn, preferred_element_type=F32)
            m_ref[r0:r0 + rows, :] = m_new
        return carry

    lax.fori_loop(0, nkb, attend, 0)
    for h in range(A_HEADS):
        o_ref[:, h * LANES:(h + 1) * LANES] = (
            acc_ref[h * tq:(h + 1) * tq, :] / l_ref[h * tq:(h + 1) * tq, :]).astype(o_ref.dtype)


def _dsa(q_bf, iq_bf, ikw, k_all, v_all, ik_all, *, n_batch, t_len, pos0, n_keys):
    lp = k_all.shape[1]
    tq = min(t_len, 128)
    tk = min(lp, 512)
    assert lp % tk == 0 and t_len % tq == 0
    nq = t_len // tq
    topk = min(TOPK_MAX, n_keys // 4)
    qrow = lambda w: pl.BlockSpec((tq, w), lambda b, i: (b * nq + i, 0))
    kv_spec = lambda w: pl.BlockSpec((None, lp, w), lambda b, i: (b, 0, 0), pipeline_mode=pl.Buffered(1))
    return pl.pallas_call(
        functools.partial(_dsa_kernel, tq=tq, tk=tk, pos0=pos0, n_keys=n_keys, topk=float(topk)),
        out_shape=jax.ShapeDtypeStruct((n_batch * t_len, A_WIDTH), BF16),
        grid=(n_batch, nq),
        in_specs=[qrow(A_WIDTH), qrow(IQ_W), qrow(LANES), kv_spec(KV_W), kv_spec(KV_W), kv_spec(IDX_DIM)],
        out_specs=qrow(A_WIDTH),
        scratch_shapes=[
            pltpu.VMEM((tq, lp), I32),
            pltpu.VMEM((IDX_HEADS, tq, LANES), F32),
            pltpu.VMEM((A_HEADS * tq, LANES), BF16),
            pltpu.VMEM((A_HEADS * tq, LANES), F32),
            pltpu.VMEM((A_HEADS * tq, LANES), F32),
            pltpu.VMEM((A_HEADS * tq, LANES), F32),
        ],
        compiler_params=_cparams(("parallel", "arbitrary")),
        name="dsa",
    )(q_bf, iq_bf, ikw, k_all, v_all, ik_all)


def _head_sums(x, ones_bd):
    n = x.shape[1] // LANES
    tm = x.shape[0]
    stacked = jnp.concatenate([x[:, c * LANES:(c + 1) * LANES] for c in range(n)], axis=0)
    s = jnp.dot(stacked, ones_bd, precision=HIGHEST, preferred_element_type=F32)
    return jnp.concatenate([s[c * tm:(c + 1) * tm, :] for c in range(n)], axis=1)


def _rwkv_pre_kernel(z_ref, zp_ref, z0_ref, mu_ref, w0_ref, a0_ref, kk_ref, ka_ref, w2_ref, a2_ref, g2_ref,
                     ones_ref, r_out, w_out, k_out, v_out, nkk_out, b_out, g_out, *, tm):
    i = pl.program_id(1)
    row = lax.broadcasted_iota(I32, (tm, 1), 0)

    def mixed(c0, width):
        z = z_ref[:, c0:c0 + width]
        first = jnp.where(i == 0, z0_ref[:, c0:c0 + width], zp_ref[7:8, c0:c0 + width])
        shifted = jnp.where(row == 0, first, pltpu.roll(z, 1, axis=0))
        return z + (shifted - z) * mu_ref[:, c0:c0 + width]

    r = mixed(0, R_WIDTH)
    k = mixed(R_WIDTH, R_WIDTH)
    v = mixed(2 * R_WIDTH, R_WIDTH)
    wd = mixed(RW_WD0, LORA_PAD)
    ad = mixed(RW_AD0, LORA_PAD)
    gd = mixed(RW_GD0, GATE_LORA)
    lora = lambda x, w_ref: jnp.dot(x, w_ref[...], precision=HIGHEST, preferred_element_type=F32)
    y = -(w0_ref[...] + lora(jnp.tanh(wd), w2_ref))
    softplus = jnp.maximum(y, 0.0) + jnp.log(1.0 + jnp.exp(-jnp.abs(y)))
    decay = jnp.exp(-jnp.exp(-softplus - 0.5))
    a = jax.nn.sigmoid(a0_ref[...] + lora(ad, a2_ref))
    g = lora(jax.nn.sigmoid(gd), g2_ref)
    kk = k * kk_ref[...]
    kk = kk / jnp.maximum(jnp.sqrt(_head_sums(kk * kk, ones_ref[...])), 1e-12)
    r_out[...] = r
    w_out[...] = decay
    k_out[...] = k * (1.0 + (a - 1.0) * ka_ref[...])
    v_out[...] = v
    nkk_out[...] = -kk
    b_out[...] = kk * a
    g_out[...] = g


def _rwkv_pre(z_rw, z0, mu, w0, a0, k_k, k_a, w2, a2, g2, ones_bd):
    nb, t_len, _ = z_rw.shape
    tm = min(t_len, 128)
    zrow = pl.BlockSpec((None, tm, RW_COLS), lambda b, i: (b, i, 0))
    zprev = pl.BlockSpec((None, 8, RW_COLS), lambda b, i: (b, jnp.maximum(i * (tm // 8) - 1, 0), 0))
    full = lambda a: pl.BlockSpec(a.shape, lambda b, i: (0,) * a.ndim)
    orow = pl.BlockSpec((None, tm, R_WIDTH), lambda b, i: (b, i, 0))
    params = (mu, w0, a0, k_k, k_a, w2, a2, g2, ones_bd)
    return pl.pallas_call(
        functools.partial(_rwkv_pre_kernel, tm=tm),
        out_shape=tuple(jax.ShapeDtypeStruct((nb, t_len, R_WIDTH), F32) for _ in range(7)),
        grid=(nb, t_len // tm),
        in_specs=[zrow, zprev, pl.BlockSpec((None, 1, RW_COLS), lambda b, i: (b, 0, 0))] + [full(p) for p in params],
        out_specs=tuple(orow for _ in range(7)),
        compiler_params=_cparams(("parallel", "arbitrary")),
        name="rwkv_pre",
    )(z_rw, z_rw, z0, *params)


def _split2(x):
    hi = x.astype(BF16)
    return hi, (x - hi.astype(F32)).astype(BF16)


def _rwkv_scan_kernel(r_ref, w_ref, k_ref, nkk_ref, b_ref, vx_ref, s0_ref, ones2_ref, hot_ref, spread_ref,
                      y_ref, st_ref, s_ref, xs_ref, vc_ref, *, tb, unroll):
    tblk = pl.program_id(1)

    @pl.when(tblk == 0)
    def _():
        s_ref[...] = s0_ref[...]

    for sub in range(tb // SCAN_SUB):
        for g in range(SCAN_GROUPS):
            hi, mid = _split2(vx_ref[sub, g])
            xs_ref[sub, g * R_HEAD_DIM:(g + 1) * R_HEAD_DIM, :] = jnp.concatenate([hi, mid], axis=1)
    ones2 = ones2_ref[...]
    spread = spread_ref[...]
    y_ref[...] = jnp.zeros(y_ref.shape, F32)
    lane_t = lax.rem(lax.broadcasted_iota(I32, (R_HEAD_DIM, 2 * tb), 1), tb)
    cat = lambda xs, ax=0: jnp.concatenate(xs, axis=ax)
    rowp = lambda ref, t, p: ref[t, p:p + 1, :]
    group_pairs = R_PAIRS // SCAN_MATMULS
    half = group_pairs // 2 * R_HEAD_DIM

    groups = [range(g * group_pairs, (g + 1) * group_pairs) for g in range(SCAN_MATMULS)]

    def packed(xs):
        return cat([cat(xs[2 * j:2 * j + 2], 1) for j in range(group_pairs // 2)])

    def pair_tile(res, q):
        return res[(q // 2) * R_HEAD_DIM:(q // 2 + 1) * R_HEAD_DIM, (q % 2) * LANES:(q % 2 + 1) * LANES]

    def head_sums(g, states, t_sa, t_y=None):
        rows = [packed([(s * rowp(nkk_ref, t_sa, p)).astype(BF16) for s, p in zip(states, groups[g])])]
        if t_y is not None:
            rows.append(packed([(s * rowp(r_ref, t_y, p)).astype(BF16) for s, p in zip(states, groups[g])]))
        return jnp.dot(cat(rows), ones2, preferred_element_type=F32)

    def sub_block(sub, sas):
        def bulk(i, c):
            xs = xs_ref[sub]
            n = xs.shape[0]
            res = jnp.dot(cat([xs * hot_ref[SCAN_BULK * i + j, 0:1, :] for j in range(SCAN_BULK)]), spread,
                          preferred_element_type=F32)
            for j in range(SCAN_BULK):
                vc_ref[SCAN_BULK * i + j] = res[j * n:(j + 1) * n]
            return c

        lax.fori_loop(0, SCAN_SUB // SCAN_BULK, bulk, 0)

        def step(tt, sas):
            t = sub * SCAN_SUB + tt
            t_next = jnp.minimum(t + 1, tb - 1)
            hit = lane_t == t
            new_sas = []
            for g in range(SCAN_MATMULS):
                states = []
                for q, p in enumerate(groups[g]):
                    vg, vq = divmod(p, SCAN_GROUP_PAIRS)
                    vc = vc_ref[tt, vg * R_HEAD_DIM:(vg + 1) * R_HEAD_DIM, vq * LANES:(vq + 1) * LANES]
                    s_new = (s_ref[p] * rowp(w_ref, t, p) + pair_tile(sas[g], q) * rowp(b_ref, t, p)
                             + vc * rowp(k_ref, t, p))
                    s_ref[p] = s_new
                    states.append(s_new)
                res = head_sums(g, states, t_next, t)
                new_sas.append(res[:half])
                for q, p in enumerate(groups[g]):
                    y_cols = pair_tile(res[half:], q)[:, R_HEAD_DIM - tb:R_HEAD_DIM + tb]
                    y_ref[p] = jnp.where(hit, y_cols, y_ref[p])
            return tuple(new_sas)

        return lax.fori_loop(0, SCAN_SUB, step, sas, unroll=unroll)

    first = tuple(head_sums(g, [s_ref[p] for p in groups[g]], 0) for g in range(SCAN_MATMULS))
    lax.fori_loop(0, tb // SCAN_SUB, sub_block, first)

    @pl.when(tblk == pl.num_programs(1) - 1)
    def _():
        st_ref[...] = s_ref[...]


def _rwkv_scan(r, w, k, nkk, b, v, s0):
    nb, t_len, _ = r.shape
    tb = min(t_len, 64)
    nblk = t_len // tb
    hd = R_HEAD_DIM
    nsub, sb, ng, gp = tb // SCAN_SUB, SCAN_SUB, SCAN_GROUPS, SCAN_GROUP_PAIRS
    vx = v.reshape(nb, nblk, nsub, sb, ng, gp, 2, hd).transpose(0, 1, 2, 4, 7, 5, 6, 3).reshape(
        nb, nblk, nsub, ng, hd, LANES)
    s0p = s0.reshape(nb, R_PAIRS, 2, hd, hd).transpose(0, 1, 3, 2, 4).reshape(nb, R_PAIRS, hd, LANES)
    lane_h = jnp.arange(LANES) // hd
    ones_bd = lane_h[:, None] == lane_h[None, :]
    lane2_h = jnp.arange(2 * LANES) // hd
    ones2 = (lane2_h[:, None] == lane2_h[None, :]).astype(BF16)
    src = jnp.arange(2 * LANES) % LANES
    src_q, src_h, src_t = src // (2 * sb), (src // sb) % 2, src % sb
    dst = jnp.arange(gp * LANES)
    dst_q, dst_h = dst // LANES, (dst % LANES) // hd
    spread = ((src_q[:, None] == dst_q[None, :]) & (src_h[:, None] == dst_h[None, :])).astype(BF16)
    hot = jnp.broadcast_to((src_t[None, :] == jnp.arange(sb)[:, None])[:, None, :], (sb, 16, 2 * LANES)).astype(BF16)
    trow = pl.BlockSpec((None, tb, R_PAIRS, LANES), lambda bb, i: (bb, i, 0, 0))
    r, w, k, nkk, b = (a.reshape(nb, t_len, R_PAIRS, LANES) for a in (r, w, k, nkk, b))
    col = pl.BlockSpec((None, None, nsub, ng, hd, LANES), lambda bb, i: (bb, i, 0, 0, 0, 0))
    st = pl.BlockSpec((None, R_PAIRS, hd, LANES), lambda bb, i: (bb, 0, 0, 0))
    full = lambda a: pl.BlockSpec(a.shape, lambda bb, i: (0,) * a.ndim)
    ycol_spec = pl.BlockSpec((None, None, R_PAIRS, hd, 2 * tb), lambda bb, i: (bb, i, 0, 0, 0))
    ycol, s_t = pl.pallas_call(
        functools.partial(_rwkv_scan_kernel, tb=tb, unroll=4),
        out_shape=(jax.ShapeDtypeStruct((nb, nblk, R_PAIRS, hd, 2 * tb), F32),
                   jax.ShapeDtypeStruct((nb, R_PAIRS, hd, LANES), F32)),
        grid=(nb, nblk),
        in_specs=[trow, trow, trow, trow, trow, col, st, full(ones2), full(hot), full(spread)],
        out_specs=(ycol_spec, st),
        scratch_shapes=[pltpu.VMEM((R_PAIRS, hd, LANES), F32), pltpu.VMEM((nsub, ng * hd, 2 * LANES), BF16),
                        pltpu.VMEM((sb, ng * hd, gp * LANES), F32)],
        compiler_params=_cparams(("parallel", "arbitrary")),
        name="rwkv_scan",
    )(r, w, k, nkk, b, vx, s0p, ones2, hot, spread)
    y = ycol.reshape(nb, nblk, R_PAIRS, hd, 2, tb).transpose(0, 1, 5, 2, 4, 3).reshape(nb, t_len, R_WIDTH)
    s_t = s_t.reshape(nb, R_PAIRS, hd, 2, hd).transpose(0, 1, 3, 2, 4).reshape(nb, R_HEADS, hd, hd)
    return y, s_t


def _rwkv_post_kernel(y_ref, r_ref, k_ref, v_ref, g_ref, lw_ref, lb_ref, rk_ref, ones_ref, o_ref):
    ones_bd = ones_ref[...]
    y = y_ref[...]
    mean = _head_sums(y, ones_bd) * (1.0 / R_HEAD_DIM)
    d = y - mean
    var = _head_sums(d * d, ones_bd) * (1.0 / R_HEAD_DIM)
    yn = d * lax.rsqrt(var + GN_EPS) * lw_ref[...] + lb_ref[...]
    bonus = _head_sums(r_ref[...] * k_ref[...] * rk_ref[...], ones_bd) * v_ref[...]
    o_ref[...] = ((yn + bonus) * g_ref[...]).astype(o_ref.dtype)


def _rwkv_post(y, r, k, v, g, lnx_w, lnx_b, r_k, ones_bd):
    m = y.shape[0]
    tm = min(m, 128)
    row = pl.BlockSpec((tm, R_WIDTH), lambda i: (i, 0))
    full = lambda a: pl.BlockSpec(a.shape, lambda i: (0,) * a.ndim)
    params = (lnx_w, lnx_b, r_k, ones_bd)
    return pl.pallas_call(
        _rwkv_post_kernel,
        out_shape=jax.ShapeDtypeStruct((m, R_WIDTH), BF16),
        grid=(m // tm,),
        in_specs=[row] * 5 + [full(p) for p in params],
        out_specs=row,
        compiler_params=_cparams(("parallel",)),
        name="rwkv_post",
    )(y, r, k, v, g, *params)


def _ffn_up_kernel(x_ref, xp_ref, wg_ref, wu_ref, c0_ref, cw_ref, cb_ref, act_ref, tail_ref, *, tm, tiles_per_seq):
    first = lax.rem(pl.program_id(0), tiles_per_seq) == 0
    wg = wg_ref[...]
    gate = jnp.dot(x_ref[...], wg, preferred_element_type=F32)
    up = jnp.dot(x_ref[...], wu_ref[...], preferred_element_type=F32)
    gate_prev = jnp.dot(xp_ref[...], wg, preferred_element_type=F32)
    row = lax.broadcasted_iota(I32, (tm, 1), 0)
    prev1 = jnp.where(first, c0_ref[1:2, :], gate_prev[7:8, :])
    prev2 = jnp.where(first, c0_ref[0:1, :], gate_prev[6:7, :])
    g_m1 = jnp.where(row == 0, prev1, pltpu.roll(gate, 1, axis=0))
    g_m2 = jnp.where(row == 0, prev2, jnp.where(row == 1, prev1, pltpu.roll(gate, 2, axis=0)))
    conv = cb_ref[...] + g_m2 * cw_ref[0:1, :]
    conv = conv + g_m1 * cw_ref[1:2, :]
    conv = conv + gate * cw_ref[2:3, :]
    act_ref[...] = (conv * jax.nn.sigmoid(conv) * up).astype(act_ref.dtype)
    tail_ref[...] = gate[tm - 8:, :]


def _ffn_up(x, w_in, conv0, conv_w, conv_b, n_batch, t_len, tm=1024, tn=256):
    m, d = x.shape
    d_ff = w_in.shape[1] // 2
    tm = min(t_len, tm)
    assert t_len % tm == 0 and tm % 8 == 0 and d_ff % tn == 0
    tiles_per_seq = t_len // tm
    n_up = d_ff // tn
    act, tail = pl.pallas_call(
        functools.partial(_ffn_up_kernel, tm=tm, tiles_per_seq=tiles_per_seq),
        out_shape=(jax.ShapeDtypeStruct((m, d_ff), BF16), jax.ShapeDtypeStruct((m // tm, 8, d_ff), F32)),
        grid=(m // tm, n_up),
        in_specs=[pl.BlockSpec((tm, d), lambda i, j: (i, 0)),
                  pl.BlockSpec((8, d), lambda i, j: (jnp.maximum(i * (tm // 8) - 1, 0), 0)),
                  pl.BlockSpec((d, tn), lambda i, j: (0, j)),
                  pl.BlockSpec((d, tn), lambda i, j: (0, j + n_up)),
                  pl.BlockSpec((None, CONV_W - 1, tn), lambda i, j: (i // tiles_per_seq, 0, j)),
                  pl.BlockSpec((CONV_W, tn), lambda i, j: (0, j)),
                  pl.BlockSpec((1, tn), lambda i, j: (0, j))],
        out_specs=(pl.BlockSpec((tm, tn), lambda i, j: (i, j)), pl.BlockSpec((None, 8, tn), lambda i, j: (i, 0, j))),
        compiler_params=_cparams(("parallel", "arbitrary")),
        name="ffn_up",
    )(x, x, w_in, w_in, conv0, conv_w, conv_b.reshape(1, d_ff))
    gate_tail = tail.reshape(n_batch, tiles_per_seq, 8, d_ff)[:, -1, 8 - (CONV_W - 1):, :]
    return act, gate_tail


def _pad_rw_cols(a):
    z = lambda n: jnp.zeros(a.shape[:-1] + (n,), a.dtype)
    wd0, ad0, gd0 = 3 * R_WIDTH, 3 * R_WIDTH + DECAY_LORA, 3 * R_WIDTH + DECAY_LORA + AAA_LORA
    return jnp.concatenate([a[..., :wd0], a[..., wd0:ad0], z(LORA_PAD - DECAY_LORA), a[..., ad0:gd0],
                            z(LORA_PAD - AAA_LORA), a[..., gd0:]], axis=-1)


def _unpad_rw_cols(a):
    return jnp.concatenate([a[..., :RW_WD0 + DECAY_LORA], a[..., RW_AD0:RW_AD0 + AAA_LORA], a[..., RW_GD0:]], axis=-1)


def _prep_weights(norm_mix_g, w_in, rwkv_mu, rwkv_w0, rwkv_w2, rwkv_a0, rwkv_a2, rwkv_g2, rwkv_k_k, rwkv_k_a,
                  rwkv_r_k, rwkv_lnx_w, rwkv_lnx_b, w_out, norm_ffn_g, ffn_w_in, ffn_conv_w, ffn_conv_b,
                  ffn_w_down, norm_final_g, l):
    d = w_in.shape[1]
    w_att = jnp.concatenate([w_in[l][:, :ATT_USED].astype(BF16), jnp.zeros((d, ATT_COLS - ATT_USED), BF16)], axis=1)
    w_rw = _pad_rw_cols(w_in[l][:, ATT_USED:].astype(BF16))
    row = lambda a: a.reshape(1, -1).astype(F32)
    pad_rows = lambda a, n: jnp.concatenate([a, jnp.zeros((n - a.shape[0], a.shape[1]), a.dtype)], axis=0)
    lane_h = jnp.arange(LANES) // R_HEAD_DIM
    return dict(
        norm_mix_g=norm_mix_g[l], w_att=w_att, w_rw=w_rw,
        mu=_pad_rw_cols(row(rwkv_mu[l])), w0=row(rwkv_w0[l]), a0=row(rwkv_a0[l]),
        k_k=row(rwkv_k_k[l]), k_a=row(rwkv_k_a[l]),
        w2=pad_rows(rwkv_w2[l], LORA_PAD), a2=pad_rows(rwkv_a2[l], LORA_PAD), g2=rwkv_g2[l],
        r_k=row(rwkv_r_k[l]), lnx_w=row(rwkv_lnx_w[l]), lnx_b=row(rwkv_lnx_b[l]),
        ones_bd=(lane_h[:, None] == lane_h[None, :]).astype(F32),
        w_out_a=w_out[l][:A_WIDTH].astype(BF16), w_out_r=w_out[l][A_WIDTH:].astype(BF16),
        norm_ffn_g=norm_ffn_g[l], ffn_w_in=ffn_w_in[l].astype(BF16), conv_w=ffn_conv_w[l], conv_b=ffn_conv_b[l],
        ffn_w_down=ffn_w_down[l].astype(BF16), norm_final_g=norm_final_g,
    )


def _trunk(x, past_k, past_v, past_ik, s0, shift0, conv0, wt):
    nb, t_len, d = x.shape
    m = nb * t_len
    p_len = 0 if past_k is None else past_k.shape[1]
    n_keys = p_len + t_len
    x2 = x.reshape(m, d)

    h = _rmsnorm(x2, wt["norm_mix_g"], BF16)
    z_att = _matmul([h], [wt["w_att"]], tm=1024, name="proj_att")
    z_rw = _matmul([h], [wt["w_rw"]], tm=1024, name="proj_rw")

    q_bf, k_f, k_bf, v_f, v_bf, iq_bf, ikw, ik_bf = _rope_split(z_att, t_len, p_len)
    tk = min(-(-n_keys // LANES) * LANES, 512)
    lp = -(-n_keys // tk) * tk

    def with_past(new, past, width):
        new = new.reshape(nb, t_len, width)
        parts = [new] if past is None else [past.reshape(nb, p_len, width).astype(BF16), new]
        if lp > n_keys:
            parts.append(jnp.zeros((nb, lp - n_keys, width), BF16))
        return parts[0] if len(parts) == 1 else jnp.concatenate(parts, axis=1)

    attn = _dsa(q_bf, iq_bf, ikw, with_past(k_bf, past_k, KV_W), with_past(v_bf, past_v, KV_W),
                with_past(ik_bf, past_ik, IDX_DIM), n_batch=nb, t_len=t_len, pos0=p_len, n_keys=n_keys)

    z_rw3 = z_rw.reshape(nb, t_len, RW_COLS)
    r, w, k2, v2, nkk, b, g = _rwkv_pre(z_rw3, _pad_rw_cols(shift0.astype(F32)), wt["mu"], wt["w0"], wt["a0"],
                                       wt["k_k"], wt["k_a"], wt["w2"], wt["a2"], wt["g2"], wt["ones_bd"])
    y, s_t = _rwkv_scan(r, w, k2, nkk, b, v2, s0.astype(F32))
    flat = lambda a: a.reshape(m, R_WIDTH)
    rw = _rwkv_post(flat(y), flat(r), flat(k2), flat(v2), flat(g), wt["lnx_w"], wt["lnx_b"], wt["r_k"], wt["ones_bd"])

    x1 = _matmul([attn, rw], [wt["w_out_a"], wt["w_out_r"]], res=x2, tm=1024, name="out_proj")
    hf = _rmsnorm(x1, wt["norm_ffn_g"], BF16)
    act, conv_t = _ffn_up(hf, wt["ffn_w_in"], conv0.astype(F32), wt["conv_w"], wt["conv_b"], nb, t_len)
    x3 = _matmul([act], [wt["ffn_w_down"]], res=x1, name="ffn_down")
    y_out = _rmsnorm(x3, wt["norm_final_g"], F32).reshape(nb, t_len, d)

    shift_t = _unpad_rw_cols(z_rw3[:, -1:])
    caches = (k_f.reshape(nb, t_len, A_KV_HEADS, A_HEAD_DIM)[None], v_f.reshape(nb, t_len, A_KV_HEADS, A_HEAD_DIM)[None],
              ikw[:, :IDX_DIM].reshape(nb, t_len, IDX_DIM)[None], s_t[None], shift_t[None], conv_t[None])
    return y_out, caches


def kernel(x_prompt, x_sample, cache_k, cache_v, cache_idx_k, state_rwkv, state_rwkv_shift, state_ffn_conv, norm_mix_g, w_in, rwkv_mu, rwkv_w0, rwkv_w2, rwkv_a0, rwkv_a2, rwkv_g2, rwkv_k_k, rwkv_k_a, rwkv_r_k, rwkv_lnx_w, rwkv_lnx_b, w_out, norm_ffn_g, ffn_w_in, ffn_conv_w, ffn_conv_b, ffn_w_down, norm_final_g):
    assert w_in.shape[0] == 1, "single-layer trunk"
    wt = _prep_weights(norm_mix_g, w_in, rwkv_mu, rwkv_w0, rwkv_w2, rwkv_a0, rwkv_a2, rwkv_g2, rwkv_k_k, rwkv_k_a,
                       rwkv_r_k, rwkv_lnx_w, rwkv_lnx_b, w_out, norm_ffn_g, ffn_w_in, ffn_conv_w, ffn_conv_b,
                       ffn_w_down, norm_final_g, 0)
    bp = x_prompt.shape[0]
    d_ff = ffn_conv_w.shape[-1]
    y_p, c_p = _trunk(x_prompt, None, None, None,
                      jnp.zeros((bp, R_HEADS, R_HEAD_DIM, R_HEAD_DIM), F32), jnp.zeros((bp, 1, RWKV_COLS), F32),
                      jnp.zeros((bp, CONV_W - 1, d_ff), F32), wt)
    y_s, c_s = _trunk(x_sample, cache_k[0], cache_v[0], cache_idx_k[0], state_rwkv[0], state_rwkv_shift[0],
                      state_ffn_conv[0], wt)
    return (y_p, y_s) + c_p + c_s
```

```python
import functools

import jax
import jax.numpy as jnp
from jax import lax
from jax.experimental import pallas as pl
from jax.experimental.pallas import tpu as pltpu

F32 = jnp.float32
BF16 = jnp.bfloat16
I32 = jnp.int32

CHUNK = 64
A_HEADS = 16
A_KV_HEADS = 4
A_HEAD_DIM = 128
A_GROUP = A_HEADS // A_KV_HEADS
A_WIDTH = A_HEADS * A_HEAD_DIM
KV_W = A_KV_HEADS * A_HEAD_DIM
IDX_HEADS = 16
IDX_DIM = 64
IQ_W = IDX_HEADS * IDX_DIM
TOPK_MAX = 256
ROPE_THETA = 500000.0
ROPE_FRAC = 4
A_SCALE = A_HEAD_DIM ** -0.5
Q_SCALE = A_SCALE * 1.4426950408889634
IDX_SCALE = (IDX_HEADS ** -0.5) * (IDX_DIM ** -0.5)
R_HEAD_DIM = 64
R_WIDTH = 2048
R_HEADS = R_WIDTH // R_HEAD_DIM
R_PAIRS = R_HEADS // 2
DECAY_LORA = 96
AAA_LORA = 96
GATE_LORA = 256
RWKV_COLS = 3 * R_WIDTH + DECAY_LORA + AAA_LORA + GATE_LORA
GN_EPS = 6.4e-4
CONV_W = 3
RMS_EPS = 1e-6

LANES = 128
VMEM_LIMIT = 56 * 1024 * 1024

ATT_Q0, ATT_K0, ATT_V0, ATT_IQ0, ATT_IK0 = 0, A_WIDTH, A_WIDTH + KV_W, A_WIDTH + 2 * KV_W, A_WIDTH + 2 * KV_W + IQ_W
ATT_USED = ATT_IK0 + IDX_DIM + IDX_HEADS
ATT_COLS = 4608
LORA_PAD = 128
RW_WD0 = 3 * R_WIDTH
RW_AD0 = RW_WD0 + LORA_PAD
RW_GD0 = RW_AD0 + LORA_PAD
RW_COLS = RW_GD0 + GATE_LORA
SCAN_BLOCK = 64
SCAN_SUB = 16
SCAN_GROUP_PAIRS = LANES // (2 * SCAN_SUB)
SCAN_GROUPS = R_PAIRS // SCAN_GROUP_PAIRS
SCAN_MATMULS = 2
SCAN_BULK = 4
INT_MIN = -2 ** 31
NEG_BIG = -1e30
HIGHEST = lax.Precision.HIGHEST


def _cparams(sem):
    return pltpu.CompilerParams(dimension_semantics=sem, vmem_limit_bytes=VMEM_LIMIT)


def _rmsnorm_kernel(x_ref, g_ref, o_ref):
    x = x_ref[...]
    y = x * lax.rsqrt(jnp.mean(x * x, axis=-1, keepdims=True) + RMS_EPS)
    o_ref[...] = (y * g_ref[...]).astype(o_ref.dtype)


def _rmsnorm(x, g, out_dtype):
    m, d = x.shape
    tm = min(m, 256)
    return pl.pallas_call(
        _rmsnorm_kernel,
        out_shape=jax.ShapeDtypeStruct((m, d), out_dtype),
        grid=(m // tm,),
        in_specs=[pl.BlockSpec((tm, d), lambda i: (i, 0)), pl.BlockSpec((1, d), lambda i: (0, 0))],
        out_specs=pl.BlockSpec((tm, d), lambda i: (i, 0)),
        compiler_params=_cparams(("parallel",)),
        name="rmsnorm",
    )(x, g.reshape(1, d).astype(F32))


def _mm_kernel(*refs, n_pairs, has_res):
    o_ref = refs[-1]
    acc = jnp.dot(refs[0][...], refs[n_pairs][...], preferred_element_type=F32)
    for p in range(1, n_pairs):
        acc = acc + jnp.dot(refs[p][...], refs[n_pairs + p][...], preferred_element_type=F32)
    if has_res:
        acc = refs[2 * n_pairs][...] + acc
    o_ref[...] = acc.astype(o_ref.dtype)


def _matmul(a_list, b_list, res=None, tm=512, tn=512, name="matmul"):
    m = a_list[0].shape[0]
    n = b_list[0].shape[1]
    tm = min(tm, m)
    tn = min(tn, n)
    assert m % tm == 0 and n % tn == 0, (m, n, tm, tn)
    in_specs = [pl.BlockSpec((tm, a.shape[1]), lambda i, j: (i, 0)) for a in a_list]
    in_specs += [pl.BlockSpec((b.shape[0], tn), lambda i, j: (0, j)) for b in b_list]
    args = list(a_list) + list(b_list)
    if res is not None:
        in_specs.append(pl.BlockSpec((tm, tn), lambda i, j: (i, j)))
        args.append(res)
    return pl.pallas_call(
        functools.partial(_mm_kernel, n_pairs=len(a_list), has_res=res is not None),
        out_shape=jax.ShapeDtypeStruct((m, n), F32),
        grid=(m // tm, n // tn),
        in_specs=in_specs,
        out_specs=pl.BlockSpec((tm, tn), lambda i, j: (i, j)),
        compiler_params=_cparams(("parallel", "arbitrary")),
        name=name,
    )(*args)


def _rope_tile(x, cos, sin, half, d_in_head):
    lo = d_in_head < half
    hi = (d_in_head >= half) & (d_in_head < 2 * half)
    c = jnp.where(lo | hi, cos, 1.0)
    s_up = jnp.where(lo, -sin, 0.0)
    s_dn = jnp.where(hi, sin, 0.0)
    x_up = pltpu.roll(x, LANES - half, axis=1)
    x_dn = pltpu.roll(x, half, axis=1)
    return x * c + x_up * s_up + x_dn * s_dn


def _rope_kernel(z_ref, invf_ref, q_ref, kf_ref, kb_ref, vf_ref, vb_ref, iq_ref, ikw_ref, ikb_ref, *, tm, t_len, pos0):
    i = pl.program_id(0)
    row = lax.broadcasted_iota(I32, (tm, LANES), 0) + i * tm
    pos = (pos0 + lax.rem(row, t_len)).astype(F32)
    lane = lax.broadcasted_iota(I32, (tm, LANES), 1)
    ang = pos * invf_ref[0:1, :]
    cos_a, sin_a = jnp.cos(ang), jnp.sin(ang)
    half_a = A_HEAD_DIM // ROPE_FRAC // 2
    for h in range(A_HEADS):
        x = z_ref[:, ATT_Q0 + h * LANES:ATT_Q0 + (h + 1) * LANES]
        q_ref[:, h * LANES:(h + 1) * LANES] = (_rope_tile(x, cos_a, sin_a, half_a, lane) * Q_SCALE).astype(q_ref.dtype)
    for h in range(A_KV_HEADS):
        x = z_ref[:, ATT_K0 + h * LANES:ATT_K0 + (h + 1) * LANES]
        y = _rope_tile(x, cos_a, sin_a, half_a, lane)
        kf_ref[:, h * LANES:(h + 1) * LANES] = y
        kb_ref[:, h * LANES:(h + 1) * LANES] = y.astype(kb_ref.dtype)
    v = z_ref[:, ATT_V0:ATT_V0 + KV_W]
    vf_ref[...] = v
    vb_ref[...] = v.astype(vb_ref.dtype)
    ang = pos * invf_ref[1:2, :]
    cos_i, sin_i = jnp.cos(ang), jnp.sin(ang)
    half_i = IDX_DIM // ROPE_FRAC // 2
    d_i = lane & (IDX_DIM - 1)
    for h in range(IQ_W // LANES):
        x = z_ref[:, ATT_IQ0 + h * LANES:ATT_IQ0 + (h + 1) * LANES]
        iq_ref[:, h * LANES:(h + 1) * LANES] = _rope_tile(x, cos_i, sin_i, half_i, d_i).astype(iq_ref.dtype)
    x = z_ref[:, ATT_IK0:ATT_IK0 + LANES]
    d_k = jnp.where(lane < IDX_DIM, lane, IDX_DIM)
    y = _rope_tile(x, cos_i, sin_i, half_i, d_k)
    ikw_ref[...] = y
    ikb_ref[...] = y[:, :IDX_DIM].astype(ikb_ref.dtype)


def _rope_split(z_att, t_len, pos0):
    m = z_att.shape[0]
    tm = min(m, 256)
    lane = jnp.arange(LANES)
    rd_a = A_HEAD_DIM // ROPE_FRAC
    rd_i = IDX_DIM // ROPE_FRAC
    invf_a = ROPE_THETA ** (-((lane % (rd_a // 2)).astype(F32) * 2.0 / rd_a))
    invf_i = ROPE_THETA ** (-((lane % (rd_i // 2)).astype(F32) * 2.0 / rd_i))
    invf = jnp.zeros((8, LANES), F32).at[0].set(invf_a).at[1].set(invf_i)
    row_spec = lambda w: pl.BlockSpec((tm, w), lambda i: (i, 0))
    shp = lambda w, dt: jax.ShapeDtypeStruct((m, w), dt)
    return pl.pallas_call(
        functools.partial(_rope_kernel, tm=tm, t_len=t_len, pos0=pos0),
        out_shape=(shp(A_WIDTH, BF16), shp(KV_W, F32), shp(KV_W, BF16), shp(KV_W, F32), shp(KV_W, BF16),
                   shp(IQ_W, BF16), shp(LANES, F32), shp(IDX_DIM, BF16)),
        grid=(m // tm,),
        in_specs=[row_spec(ATT_COLS), pl.BlockSpec((8, LANES), lambda i: (0, 0))],
        out_specs=(row_spec(A_WIDTH), row_spec(KV_W), row_spec(KV_W), row_spec(KV_W), row_spec(KV_W),
                   row_spec(IQ_W), row_spec(LANES), row_spec(IDX_DIM)),
        compiler_params=_cparams(("parallel",)),
        name="rope_split",
    )(z_att, invf)


def _tile(x, n, axis):
    return x if n == 1 else jnp.concatenate([x] * n, axis=axis)


def _sortable(score):
    u = lax.bitcast_convert_type(score, I32)
    return jnp.where(u < 0, u ^ jnp.int32(0x7FFFFFFF), u)


def _dsa_kernel(q_ref, iq_ref, ikw_ref, k_ref, v_ref, ik_ref, o_ref,
                key_ref, iwb_ref, qs_ref, m_ref, l_ref, acc_ref, *, tq, tk, pos0, n_keys, topk):
    i = pl.program_id(1)
    q0 = pos0 + i * tq
    kmax = jnp.minimum((lax.div(q0 + tq - 1, CHUNK) + 1) * CHUNK, n_keys)
    nkb = lax.div(kmax + tk - 1, tk)
    n_rep = tk // LANES
    nt_dims = (((1,), (1,)), ((), ()))

    for h in range(IDX_HEADS):
        iwb_ref[h] = jnp.broadcast_to(ikw_ref[:, IDX_DIM + h:IDX_DIM + h + 1], (tq, LANES))
    for h in range(A_HEADS):
        qs_ref[h * tq:(h + 1) * tq, :] = q_ref[:, h * LANES:(h + 1) * LANES]

    def score_block(kb, carry):
        koff = pl.multiple_of(kb * tk, tk)
        ikb = ik_ref[pl.ds(koff, tk), :]
        acc = jnp.zeros((tq, tk), F32)
        for h in range(IDX_HEADS):
            d = lax.dot_general(iq_ref[:, h * IDX_DIM:(h + 1) * IDX_DIM], ikb, nt_dims, preferred_element_type=F32)
            acc = acc + jnp.maximum(d, 0.0) * _tile(iwb_ref[h], n_rep, 1)
        kpos = koff + lax.broadcasted_iota(I32, (tq, tk), 1)
        qpos = q0 + lax.broadcasted_iota(I32, (tq, tk), 0)
        adm = (lax.shift_right_logical(kpos, 6) <= lax.shift_right_logical(qpos, 6)) & (kpos < n_keys)
        key_ref[:, pl.ds(koff, tk)] = jnp.where(adm, _sortable(acc * IDX_SCALE), jnp.int32(INT_MIN))
        return carry

    lax.fori_loop(0, nkb, score_block, 0)

    def count(pred):
        def body(kb, cnt):
            koff = pl.multiple_of(kb * tk, tk)
            keys = key_ref[:, pl.ds(koff, tk)]
            kpos = koff + lax.broadcasted_iota(I32, (tq, tk), 1)
            hit = jnp.where(pred(keys, kpos), 1.0, 0.0)
            for c in range(n_rep):
                cnt = cnt + hit[:, c * LANES:(c + 1) * LANES]
            return cnt
        cnt = lax.fori_loop(0, nkb, body, jnp.zeros((tq, LANES), F32))
        return jnp.broadcast_to(jnp.sum(cnt, axis=-1, keepdims=True), (tq, LANES))

    def wide(x):
        return _tile(x, n_rep, 1)

    def bit_step(it, tu):
        cand_u = tu | lax.shift_left(jnp.int32(1), 31 - it)
        cand_s = wide(cand_u ^ jnp.int32(INT_MIN))
        cnt = count(lambda keys, kpos: keys >= cand_s)
        return jnp.where(cnt >= topk, cand_u, tu)

    tu = lax.fori_loop(0, 32, bit_step, jnp.zeros((tq, LANES), I32))
    thr = jnp.maximum(tu ^ jnp.int32(INT_MIN), jnp.int32(INT_MIN + 1))
    thr_w = wide(thr)

    n_ge = count(lambda keys, kpos: keys >= thr_w)
    n_gt = count(lambda keys, kpos: keys > thr_w)
    excess = n_ge > topk

    @pl.when(jnp.max(jnp.where(excess, 1.0, 0.0)) > 0.0)
    def _():
        need = topk - n_gt

        idx_bits = int(key_ref.shape[1]).bit_length()

        def idx_step(it, jm):
            cand = wide(jm | lax.shift_left(jnp.int32(1), idx_bits - 1 - it))
            cnt = count(lambda keys, kpos: (keys == thr_w) & (kpos < cand))
            return jnp.where(cnt < need, cand[:, :LANES], jm)

        jm = lax.fori_loop(0, idx_bits, idx_step, jnp.zeros((tq, LANES), I32))
        jm_w = wide(jnp.where(excess, jm, jnp.int32(2 ** 31 - 1)))

        def drop(kb, carry):
            koff = pl.multiple_of(kb * tk, tk)
            keys = key_ref[:, pl.ds(koff, tk)]
            kpos = koff + lax.broadcasted_iota(I32, (tq, tk), 1)
            key_ref[:, pl.ds(koff, tk)] = jnp.where((keys == thr_w) & (kpos > jm_w), jnp.int32(INT_MIN), keys)
            return carry

        lax.fori_loop(0, nkb, drop, 0)

    m_ref[...] = jnp.full(m_ref.shape, NEG_BIG, F32)
    l_ref[...] = jnp.zeros(l_ref.shape, F32)
    acc_ref[...] = jnp.zeros(acc_ref.shape, F32)
    rows = A_GROUP * tq

    def attend(kb, carry):
        koff = pl.multiple_of(kb * tk, tk)
        bias = _tile(jnp.where(key_ref[:, pl.ds(koff, tk)] >= thr_w, 0.0, NEG_BIG), A_GROUP, 0)
        for n in range(A_KV_HEADS):
            r0 = n * rows
            kn = k_ref[pl.ds(koff, tk), n * LANES:(n + 1) * LANES]
            vn = v_ref[pl.ds(koff, tk), n * LANES:(n + 1) * LANES]
            s = lax.dot_general(qs_ref[r0:r0 + rows, :], kn, nt_dims, preferred_element_type=F32) + bias
            m_prev = m_ref[r0:r0 + rows, :]
            m_new = jnp.maximum(m_prev, jnp.max(s, axis=-1, keepdims=True))
            alpha = jnp.exp2(m_prev - m_new)
            p = jnp.exp2(s - _tile(m_new, n_rep, 1))
            l_ref[r0:r0 + rows, :] = alpha * l_ref[r0:r0 + rows, :] + jnp.sum(p, axis=-1, keepdims=True)
            acc_ref[r0:r0 + rows, :] = alpha * acc_ref[r0:r0 + rows, :] + jnp.dot(
                p.astype(vn.dtype), vn, preferred_element_type=F32)
            m_ref[r0:r0 + rows, :] = m_new
        return carry

    lax.fori_loop(0, nkb, attend, 0)
    for h in range(A_HEADS):
        o_ref[:, h * LANES:(h + 1) * LANES] = (
            acc_ref[h * tq:(h + 1) * tq, :] / l_ref[h * tq:(h + 1) * tq, :]).astype(o_ref.dtype)


def _dsa(q_bf, iq_bf, ikw, k_all, v_all, ik_all, *, n_batch, t_len, pos0, n_keys):
    lp = k_all.shape[1]
    tq = min(t_len, 128)
    tk = min(lp, 512)
    assert lp % tk == 0 and t_len % tq == 0
    nq = t_len // tq
    topk = min(TOPK_MAX, n_keys // 4)
    qrow = lambda w: pl.BlockSpec((tq, w), lambda b, i: (b * nq + i, 0))
    kv_spec = lambda w: pl.BlockSpec((None, lp, w), lambda b, i: (b, 0, 0), pipeline_mode=pl.Buffered(1))
    return pl.pallas_call(
        functools.partial(_dsa_kernel, tq=tq, tk=tk, pos0=pos0, n_keys=n_keys, topk=float(topk)),
        out_shape=jax.ShapeDtypeStruct((n_batch * t_len, A_WIDTH), BF16),
        grid=(n_batch, nq),
        in_specs=[qrow(A_WIDTH), qrow(IQ_W), qrow(LANES), kv_spec(KV_W), kv_spec(KV_W), kv_spec(IDX_DIM)],
        out_specs=qrow(A_WIDTH),
        scratch_shapes=[
            pltpu.VMEM((tq, lp), I32),
            pltpu.VMEM((IDX_HEADS, tq, LANES), F32),
            pltpu.VMEM((A_HEADS * tq, LANES), BF16),
            pltpu.VMEM((A_HEADS * tq, LANES), F32),
            pltpu.VMEM((A_HEADS * tq, LANES), F32),
            pltpu.VMEM((A_HEADS * tq, LANES), F32),
        ],
        compiler_params=_cparams(("parallel", "arbitrary")),
        name="dsa",
    )(q_bf, iq_bf, ikw, k_all, v_all, ik_all)


def _head_sums(x, ones_bd):
    n = x.shape[1] // LANES
    tm = x.shape[0]
    stacked = jnp.concatenate([x[:, c * LANES:(c + 1) * LANES] for c in range(n)], axis=0)
    s = jnp.dot(stacked, ones_bd, precision=HIGHEST, preferred_element_type=F32)
    return jnp.concatenate([s[c * tm:(c + 1) * tm, :] for c in range(n)], axis=1)


def _rwkv_pre_kernel(z_ref, zp_ref, z0_ref, mu_ref, w0_ref, a0_ref, kk_ref, ka_ref, w2_ref, a2_ref, g2_ref,
                     ones_ref, r_out, w_out, k_out, v_out, nkk_out, b_out, g_out, *, tm):
    i = pl.program_id(1)
    row = lax.broadcasted_iota(I32, (tm, 1), 0)

    def mixed(c0, width):
        z = z_ref[:, c0:c0 + width]
        first = jnp.where(i == 0, z0_ref[:, c0:c0 + width], zp_ref[7:8, c0:c0 + width])
        shifted = jnp.where(row == 0, first, pltpu.roll(z, 1, axis=0))
        return z + (shifted - z) * mu_ref[:, c0:c0 + width]

    r = mixed(0, R_WIDTH)
    k = mixed(R_WIDTH, R_WIDTH)
    v = mixed(2 * R_WIDTH, R_WIDTH)
    wd = mixed(RW_WD0, LORA_PAD)
    ad = mixed(RW_AD0, LORA_PAD)
    gd = mixed(RW_GD0, GATE_LORA)
    lora = lambda x, w_ref: jnp.dot(x, w_ref[...], precision=HIGHEST, preferred_element_type=F32)
    y = -(w0_ref[...] + lora(jnp.tanh(wd), w2_ref))
    softplus = jnp.maximum(y, 0.0) + jnp.log(1.0 + jnp.exp(-jnp.abs(y)))
    decay = jnp.exp(-jnp.exp(-softplus - 0.5))
    a = jax.nn.sigmoid(a0_ref[...] + lora(ad, a2_ref))
    g = lora(jax.nn.sigmoid(gd), g2_ref)
    kk = k * kk_ref[...]
    kk = kk / jnp.maximum(jnp.sqrt(_head_sums(kk * kk, ones_ref[...])), 1e-12)
    r_out[...] = r
    w_out[...] = decay
    k_out[...] = k * (1.0 + (a - 1.0) * ka_ref[...])
    v_out[...] = v
    nkk_out[...] = -kk
    b_out[...] = kk * a
    g_out[...] = g


def _rwkv_pre(z_rw, z0, mu, w0, a0, k_k, k_a, w2, a2, g2, ones_bd):
    nb, t_len, _ = z_rw.shape
    tm = min(t_len, 128)
    zrow = pl.BlockSpec((None, tm, RW_COLS), lambda b, i: (b, i, 0))
    zprev = pl.BlockSpec((None, 8, RW_COLS), lambda b, i: (b, jnp.maximum(i * (tm // 8) - 1, 0), 0))
    full = lambda a: pl.BlockSpec(a.shape, lambda b, i: (0,) * a.ndim)
    orow = pl.BlockSpec((None, tm, R_WIDTH), lambda b, i: (b, i, 0))
    params = (mu, w0, a0, k_k, k_a, w2, a2, g2, ones_bd)
    return pl.pallas_call(
        functools.partial(_rwkv_pre_kernel, tm=tm),
        out_shape=tuple(jax.ShapeDtypeStruct((nb, t_len, R_WIDTH), F32) for _ in range(7)),
        grid=(nb, t_len // tm),
        in_specs=[zrow, zprev, pl.BlockSpec((None, 1, RW_COLS), lambda b, i: (b, 0, 0))] + [full(p) for p in params],
        out_specs=tuple(orow for _ in range(7)),
        compiler_params=_cparams(("parallel", "arbitrary")),
        name="rwkv_pre",
    )(z_rw, z_rw, z0, *params)


def _split2(x):
    hi = x.astype(BF16)
    return hi, (x - hi.astype(F32)).astype(BF16)


def _rwkv_scan_kernel(r_ref, w_ref, k_ref, nkk_ref, b_ref, v_ref, s0_ref, ones2_ref, hot_ref, spread_ref,
                      y_ref, st_ref, s_ref, xs_ref, vc_ref, *, tb, unroll):
    tblk = pl.program_id(1)
    cat = lambda xs, ax=0: jnp.concatenate(xs, axis=ax)

    @pl.when(tblk == 0)
    def _():
        s_ref[...] = s0_ref[...]

    pad = jnp.zeros((SCAN_BLOCK - tb, LANES), F32)
    v_cols = []
    for p in range(R_PAIRS):
        vp = v_ref[:, p * LANES:(p + 1) * LANES]
        v_cols.append((cat([vp, pad]) if tb < SCAN_BLOCK else vp).T)
    for sub in range(tb // SCAN_SUB):
        for g in range(SCAN_GROUPS):
            x = cat([v_cols[SCAN_GROUP_PAIRS * g + q][h * R_HEAD_DIM:(h + 1) * R_HEAD_DIM,
                                                      sub * SCAN_SUB:(sub + 1) * SCAN_SUB]
                     for q in range(SCAN_GROUP_PAIRS) for h in range(2)], 1)
            hi, mid = _split2(x)
            xs_ref[sub, g * R_HEAD_DIM:(g + 1) * R_HEAD_DIM, :] = cat([hi, mid], 1)
    ones2 = ones2_ref[...]
    spread = spread_ref[...]
    y_ref[...] = jnp.zeros(y_ref.shape, F32)
    lane_t = lax.rem(lax.broadcasted_iota(I32, (R_HEAD_DIM, LANES), 1), R_HEAD_DIM)
    rowp = lambda ref, t, p: ref[t, p:p + 1, :]
    group_pairs = R_PAIRS // SCAN_MATMULS
    half = group_pairs // 2 * R_HEAD_DIM

    groups = [range(g * group_pairs, (g + 1) * group_pairs) for g in range(SCAN_MATMULS)]

    def packed(xs):
        return cat([cat(xs[2 * j:2 * j + 2], 1) for j in range(group_pairs // 2)])

    def pair_tile(res, q):
        return res[(q // 2) * R_HEAD_DIM:(q // 2 + 1) * R_HEAD_DIM, (q % 2) * LANES:(q % 2 + 1) * LANES]

    def head_sums(g, states, t_sa, t_y=None):
        rows = [packed([(s * rowp(nkk_ref, t_sa, p)).astype(BF16) for s, p in zip(states, groups[g])])]
        if t_y is not None:
            rows.append(packed([(s * rowp(r_ref, t_y, p)).astype(BF16) for s, p in zip(states, groups[g])]))
        return jnp.dot(cat(rows), ones2, preferred_element_type=F32)

    def sub_block(sub, sas):
        def bulk(i, c):
            xs = xs_ref[sub]
            n = xs.shape[0]
            res = jnp.dot(cat([xs * hot_ref[SCAN_BULK * i + j, 0:1, :] for j in range(SCAN_BULK)]), spread,
                          preferred_element_type=F32)
            for j in range(SCAN_BULK):
                vc_ref[SCAN_BULK * i + j] = res[j * n:(j + 1) * n]
            return c

        lax.fori_loop(0, SCAN_SUB // SCAN_BULK, bulk, 0)

        def step(tt, sas):
            t = sub * SCAN_SUB + tt
            t_next = jnp.minimum(t + 1, tb - 1)
            hit = lane_t == t
            new_sas = []
            for g in range(SCAN_MATMULS):
                states = []
                for q, p in enumerate(groups[g]):
                    vg, vq = divmod(p, SCAN_GROUP_PAIRS)
                    vc = vc_ref[tt, vg * R_HEAD_DIM:(vg + 1) * R_HEAD_DIM, vq * LANES:(vq + 1) * LANES]
                    s_new = (s_ref[p] * rowp(w_ref, t, p) + pair_tile(sas[g], q) * rowp(b_ref, t, p)
                             + vc * rowp(k_ref, t, p))
                    s_ref[p] = s_new
                    states.append(s_new)
                res = head_sums(g, states, t_next, t)
                new_sas.append(res[:half])
                for q, p in enumerate(groups[g]):
                    y_ref[p] = jnp.where(hit, pair_tile(res[half:], q), y_ref[p])
            return tuple(new_sas)

        return lax.fori_loop(0, SCAN_SUB, step, sas, unroll=unroll)

    first = tuple(head_sums(g, [s_ref[p] for p in groups[g]], 0) for g in range(SCAN_MATMULS))
    lax.fori_loop(0, tb // SCAN_SUB, sub_block, first)

    @pl.when(tblk == pl.num_programs(1) - 1)
    def _():
        st_ref[...] = s_ref[...]


def _rwkv_scan(r, w, k, nkk, b, v, s0):
    nb, t_len, _ = r.shape
    tb = min(t_len, SCAN_BLOCK)
    nblk = t_len // tb
    hd = R_HEAD_DIM
    nsub, sb, ng, gp = tb // SCAN_SUB, SCAN_SUB, SCAN_GROUPS, SCAN_GROUP_PAIRS
    s0p = s0.reshape(nb, R_PAIRS, 2, hd, hd).transpose(0, 1, 3, 2, 4).reshape(nb, R_PAIRS, hd, LANES)
    lane2_h = jnp.arange(2 * LANES) // hd
    ones2 = (lane2_h[:, None] == lane2_h[None, :]).astype(BF16)
    src = jnp.arange(2 * LANES) % LANES
    src_q, src_h, src_t = src // (2 * sb), (src // sb) % 2, src % sb
    dst = jnp.arange(gp * LANES)
    dst_q, dst_h = dst // LANES, (dst % LANES) // hd
    spread = ((src_q[:, None] == dst_q[None, :]) & (src_h[:, None] == dst_h[None, :])).astype(BF16)
    hot = jnp.broadcast_to((src_t[None, :] == jnp.arange(sb)[:, None])[:, None, :], (sb, 16, 2 * LANES)).astype(BF16)
    trow = pl.BlockSpec((None, tb, R_PAIRS, LANES), lambda bb, i: (bb, i, 0, 0))
    r, w, k, nkk, b = (a.reshape(nb, t_len, R_PAIRS, LANES) for a in (r, w, k, nkk, b))
    st = pl.BlockSpec((None, R_PAIRS, hd, LANES), lambda bb, i: (bb, 0, 0, 0))
    full = lambda a: pl.BlockSpec(a.shape, lambda bb, i: (0,) * a.ndim)
    ycol_spec = pl.BlockSpec((None, None, R_PAIRS, hd, LANES), lambda bb, i: (bb, i, 0, 0, 0))
    ycol, s_t = pl.pallas_call(
        functools.partial(_rwkv_scan_kernel, tb=tb, unroll=4),
        out_shape=(jax.ShapeDtypeStruct((nb, nblk, R_PAIRS, hd, LANES), F32),
                   jax.ShapeDtypeStruct((nb, R_PAIRS, hd, LANES), F32)),
        grid=(nb, nblk),
        in_specs=[trow, trow, trow, trow, trow, pl.BlockSpec((None, tb, R_WIDTH), lambda bb, i: (bb, i, 0)), st,
                  full(ones2), full(hot), full(spread)],
        out_specs=(ycol_spec, st),
        scratch_shapes=[pltpu.VMEM((R_PAIRS, hd, LANES), F32), pltpu.VMEM((nsub, ng * hd, 2 * LANES), BF16),
                        pltpu.VMEM((sb, ng * hd, gp * LANES), F32)],
        compiler_params=_cparams(("parallel", "arbitrary")),
        name="rwkv_scan",
    )(r, w, k, nkk, b, v, s0p, ones2, hot, spread)
    s_t = s_t.reshape(nb, R_PAIRS, hd, 2, hd).transpose(0, 1, 3, 2, 4).reshape(nb, R_HEADS, hd, hd)
    return ycol, s_t


def _rwkv_post_kernel(y_ref, r_ref, k_ref, v_ref, g_ref, lw_ref, lb_ref, rk_ref, ones_ref, o_ref, *, tb):
    ones_bd = ones_ref[...]
    tiles = []
    for p in range(R_PAIRS):
        yt = y_ref[p].T
        tiles.append(jnp.concatenate([yt[0:tb, :], yt[R_HEAD_DIM:R_HEAD_DIM + tb, :]], axis=1))
    y = jnp.concatenate(tiles, axis=1)
    mean = _head_sums(y, ones_bd) * (1.0 / R_HEAD_DIM)
    d = y - mean
    var = _head_sums(d * d, ones_bd) * (1.0 / R_HEAD_DIM)
    yn = d * lax.rsqrt(var + GN_EPS) * lw_ref[...] + lb_ref[...]
    bonus = _head_sums(r_ref[...] * k_ref[...] * rk_ref[...], ones_bd) * v_ref[...]
    o_ref[...] = ((yn + bonus) * g_ref[...]).astype(o_ref.dtype)


def _rwkv_post(ycol, r, k, v, g, lnx_w, lnx_b, r_k, ones_bd):
    m = r.shape[0]
    nblk = ycol.shape[1]
    tb = m // (ycol.shape[0] * nblk)
    row = pl.BlockSpec((tb, R_WIDTH), lambda i: (i, 0))
    full = lambda a: pl.BlockSpec(a.shape, lambda i: (0,) * a.ndim)
    params = (lnx_w, lnx_b, r_k, ones_bd)
    return pl.pallas_call(
        functools.partial(_rwkv_post_kernel, tb=tb),
        out_shape=jax.ShapeDtypeStruct((m, R_WIDTH), BF16),
        grid=(m // tb,),
        in_specs=[pl.BlockSpec((None, None, R_PAIRS, R_HEAD_DIM, LANES), lambda i: (i // nblk, i % nblk, 0, 0, 0))]
        + [row] * 4 + [full(p) for p in params],
        out_specs=row,
        compiler_params=_cparams(("parallel",)),
        name="rwkv_post",
    )(ycol, r, k, v, g, *params)


def _ffn_up_kernel(x_ref, xp_ref, wg_ref, wu_ref, c0_ref, cw_ref, cb_ref, act_ref, tail_ref, *,
                   seq_rows, seqs_per_tile, tiles_per_seq):
    first = lax.rem(pl.program_id(0), tiles_per_seq) == 0
    wg = wg_ref[...]
    gate_all = jnp.dot(x_ref[...], wg, preferred_element_type=F32)
    up_all = jnp.dot(x_ref[...], wu_ref[...], preferred_element_type=F32)
    gate_prev = jnp.dot(xp_ref[...], wg, preferred_element_type=F32)
    row = lax.broadcasted_iota(I32, (seq_rows, 1), 0)
    for s in range(seqs_per_tile):
        rows = slice(s * seq_rows, (s + 1) * seq_rows)
        gate = gate_all[rows]
        prev1 = jnp.where(first, c0_ref[s, 1:2, :], gate_prev[7:8, :])
        prev2 = jnp.where(first, c0_ref[s, 0:1, :], gate_prev[6:7, :])
        g_m1 = jnp.where(row == 0, prev1, pltpu.roll(gate, 1, axis=0))
        g_m2 = jnp.where(row == 0, prev2, jnp.where(row == 1, prev1, pltpu.roll(gate, 2, axis=0)))
        conv = cb_ref[...] + g_m2 * cw_ref[0:1, :]
        conv = conv + g_m1 * cw_ref[1:2, :]
        conv = conv + gate * cw_ref[2:3, :]
        act_ref[rows, :] = (conv * jax.nn.sigmoid(conv) * up_all[rows]).astype(act_ref.dtype)
        tail_ref[s] = gate[seq_rows - 8:, :]


def _ffn_up(x, w_in, conv0, conv_w, conv_b, n_batch, t_len, tm=1024, tn=256):
    m, d = x.shape
    d_ff = w_in.shape[1] // 2
    tm = min(m, tm)
    seq_rows = min(t_len, tm)
    seqs_per_tile, tiles_per_seq = tm // seq_rows, t_len // seq_rows
    assert m % tm == 0 and tm % seq_rows == 0 and t_len % seq_rows == 0 and seq_rows % 8 == 0 and d_ff % tn == 0
    n_up = d_ff // tn
    act, tail = pl.pallas_call(
        functools.partial(_ffn_up_kernel, seq_rows=seq_rows, seqs_per_tile=seqs_per_tile, tiles_per_seq=tiles_per_seq),
        out_shape=(jax.ShapeDtypeStruct((m, d_ff), BF16),
                   jax.ShapeDtypeStruct((n_batch * tiles_per_seq, 8, d_ff), F32)),
        grid=(m // tm, n_up),
        in_specs=[pl.BlockSpec((tm, d), lambda i, j: (i, 0)),
                  pl.BlockSpec((8, d), lambda i, j: (jnp.maximum(i * (tm // 8) - 1, 0), 0)),
                  pl.BlockSpec((d, tn), lambda i, j: (0, j)),
                  pl.BlockSpec((d, tn), lambda i, j: (0, j + n_up)),
                  pl.BlockSpec((seqs_per_tile, CONV_W - 1, tn), lambda i, j: (i // tiles_per_seq, 0, j)),
                  pl.BlockSpec((CONV_W, tn), lambda i, j: (0, j)),
                  pl.BlockSpec((1, tn), lambda i, j: (0, j))],
        out_specs=(pl.BlockSpec((tm, tn), lambda i, j: (i, j)),
                   pl.BlockSpec((seqs_per_tile, 8, tn), lambda i, j: (i, 0, j))),
        compiler_params=_cparams(("parallel", "arbitrary")),
        name="ffn_up",
    )(x, x, w_in, w_in, conv0, conv_w, conv_b.reshape(1, d_ff))
    gate_tail = tail.reshape(n_batch, tiles_per_seq, 8, d_ff)[:, -1, 8 - (CONV_W - 1):, :]
    return act, gate_tail


def _pad_rw_cols(a):
    z = lambda n: jnp.zeros(a.shape[:-1] + (n,), a.dtype)
    wd0, ad0, gd0 = 3 * R_WIDTH, 3 * R_WIDTH + DECAY_LORA, 3 * R_WIDTH + DECAY_LORA + AAA_LORA
    return jnp.concatenate([a[..., :wd0], a[..., wd0:ad0], z(LORA_PAD - DECAY_LORA), a[..., ad0:gd0],
                            z(LORA_PAD - AAA_LORA), a[..., gd0:]], axis=-1)


def _unpad_rw_cols(a):
    return jnp.concatenate([a[..., :RW_WD0 + DECAY_LORA], a[..., RW_AD0:RW_AD0 + AAA_LORA], a[..., RW_GD0:]], axis=-1)


def _prep_weights(norm_mix_g, w_in, rwkv_mu, rwkv_w0, rwkv_w2, rwkv_a0, rwkv_a2, rwkv_g2, rwkv_k_k, rwkv_k_a,
                  rwkv_r_k, rwkv_lnx_w, rwkv_lnx_b, w_out, norm_ffn_g, ffn_w_in, ffn_conv_w, ffn_conv_b,
                  ffn_w_down, norm_final_g, l):
    d = w_in.shape[1]
    w_att = jnp.concatenate([w_in[l][:, :ATT_USED].astype(BF16), jnp.zeros((d, ATT_COLS - ATT_USED), BF16)], axis=1)
    w_rw = _pad_rw_cols(w_in[l][:, ATT_USED:].astype(BF16))
    row = lambda a: a.reshape(1, -1).astype(F32)
    pad_rows = lambda a, n: jnp.concatenate([a, jnp.zeros((n - a.shape[0], a.shape[1]), a.dtype)], axis=0)
    lane_h = jnp.arange(LANES) // R_HEAD_DIM
    return dict(
        norm_mix_g=norm_mix_g[l], w_att=w_att, w_rw=w_rw,
        mu=_pad_rw_cols(row(rwkv_mu[l])), w0=row(rwkv_w0[l]), a0=row(rwkv_a0[l]),
        k_k=row(rwkv_k_k[l]), k_a=row(rwkv_k_a[l]),
        w2=pad_rows(rwkv_w2[l], LORA_PAD), a2=pad_rows(rwkv_a2[l], LORA_PAD), g2=rwkv_g2[l],
        r_k=row(rwkv_r_k[l]), lnx_w=row(rwkv_lnx_w[l]), lnx_b=row(rwkv_lnx_b[l]),
        ones_bd=(lane_h[:, None] == lane_h[None, :]).astype(F32),
        w_out_a=w_out[l][:A_WIDTH].astype(BF16), w_out_r=w_out[l][A_WIDTH:].astype(BF16),
        norm_ffn_g=norm_ffn_g[l], ffn_w_in=ffn_w_in[l].astype(BF16), conv_w=ffn_conv_w[l], conv_b=ffn_conv_b[l],
        ffn_w_down=ffn_w_down[l].astype(BF16), norm_final_g=norm_final_g,
    )


def _trunk(x, past_k, past_v, past_ik, s0, shift0, conv0, wt):
    nb, t_len, d = x.shape
    m = nb * t_len
    p_len = 0 if past_k is None else past_k.shape[1]
    n_keys = p_len + t_len
    x2 = x.reshape(m, d)

    h = _rmsnorm(x2, wt["norm_mix_g"], BF16)
    z_att = _matmul([h], [wt["w_att"]], tm=1024, name="proj_att")
    z_rw = _matmul([h], [wt["w_rw"]], tm=1024, name="proj_rw")

    q_bf, k_f, k_bf, v_f, v_bf, iq_bf, ikw, ik_bf = _rope_split(z_att, t_len, p_len)
    tk = min(-(-n_keys // LANES) * LANES, 512)
    lp = -(-n_keys // tk) * tk

    def with_past(new, past, width):
        new = new.reshape(nb, t_len, width)
        parts = [new] if past is None else [past.reshape(nb, p_len, width).astype(BF16), new]
        if lp > n_keys:
            parts.append(jnp.zeros((nb, lp - n_keys, width), BF16))
        return parts[0] if len(parts) == 1 else jnp.concatenate(parts, axis=1)

    attn = _dsa(q_bf, iq_bf, ikw, with_past(k_bf, past_k, KV_W), with_past(v_bf, past_v, KV_W),
                with_past(ik_bf, past_ik, IDX_DIM), n_batch=nb, t_len=t_len, pos0=p_len, n_keys=n_keys)

    z_rw3 = z_rw.reshape(nb, t_len, RW_COLS)
    r, w, k2, v2, nkk, b, g = _rwkv_pre(z_rw3, _pad_rw_cols(shift0.astype(F32)), wt["mu"], wt["w0"], wt["a0"],
                                       wt["k_k"], wt["k_a"], wt["w2"], wt["a2"], wt["g2"], wt["ones_bd"])
    ycol, s_t = _rwkv_scan(r, w, k2, nkk, b, v2, s0.astype(F32))
    flat = lambda a: a.reshape(m, R_WIDTH)
    rw = _rwkv_post(ycol, flat(r), flat(k2), flat(v2), flat(g), wt["lnx_w"], wt["lnx_b"], wt["r_k"], wt["ones_bd"])

    x1 = _matmul([attn, rw], [wt["w_out_a"], wt["w_out_r"]], res=x2, tm=1024, name="out_proj")
    hf = _rmsnorm(x1, wt["norm_ffn_g"], BF16)
    act, conv_t = _ffn_up(hf, wt["ffn_w_in"], conv0.astype(F32), wt["conv_w"], wt["conv_b"], nb, t_len)
    x3 = _matmul([act], [wt["ffn_w_down"]], res=x1, name="ffn_down")
    y_out = _rmsnorm(x3, wt["norm_final_g"], F32).reshape(nb, t_len, d)

    shift_t = _unpad_rw_cols(z_rw3[:, -1:])
    caches = (k_f.reshape(nb, t_len, A_KV_HEADS, A_HEAD_DIM)[None], v_f.reshape(nb, t_len, A_KV_HEADS, A_HEAD_DIM)[None],
              ikw[:, :IDX_DIM].reshape(nb, t_len, IDX_DIM)[None], s_t[None], shift_t[None], conv_t[None])
    return y_out, caches


def kernel(x_prompt, x_sample, cache_k, cache_v, cache_idx_k, state_rwkv, state_rwkv_shift, state_ffn_conv, norm_mix_g, w_in, rwkv_mu, rwkv_w0, rwkv_w2, rwkv_a0, rwkv_a2, rwkv_g2, rwkv_k_k, rwkv_k_a, rwkv_r_k, rwkv_lnx_w, rwkv_lnx_b, w_out, norm_ffn_g, ffn_w_in, ffn_conv_w, ffn_conv_b, ffn_w_down, norm_final_g):
    assert w_in.shape[0] == 1, "single-layer trunk"
    wt = _prep_weights(norm_mix_g, w_in, rwkv_mu, rwkv_w0, rwkv_w2, rwkv_a0, rwkv_a2, rwkv_g2, rwkv_k_k, rwkv_k_a,
                       rwkv_r_k, rwkv_lnx_w, rwkv_lnx_b, w_out, norm_ffn_g, ffn_w_in, ffn_conv_w, ffn_conv_b,
                       ffn_w_down, norm_final_g, 0)
    bp = x_prompt.shape[0]
    d_ff = ffn_conv_w.shape[-1]
    y_p, c_p = _trunk(x_prompt, None, None, None,
                      jnp.zeros((bp, R_HEADS, R_HEAD_DIM, R_HEAD_DIM), F32), jnp.zeros((bp, 1, RWKV_COLS), F32),
                      jnp.zeros((bp, CONV_W - 1, d_ff), F32), wt)
    y_s, c_s = _trunk(x_sample, cache_k[0], cache_v[0], cache_idx_k[0], state_rwkv[0], state_rwkv_shift[0],
                      state_ffn_conv[0], wt)
    return (y_p, y_s) + c_p + c_s
```

```python
import functools

import jax
import jax.numpy as jnp
from jax import lax
from jax.experimental import pallas as pl
from jax.experimental.pallas import tpu as pltpu

F32 = jnp.float32
BF16 = jnp.bfloat16
I32 = jnp.int32

CHUNK = 64
A_HEADS = 16
A_KV_HEADS = 4
A_HEAD_DIM = 128
A_GROUP = A_HEADS // A_KV_HEADS
A_WIDTH = A_HEADS * A_HEAD_DIM
KV_W = A_KV_HEADS * A_HEAD_DIM
IDX_HEADS = 16
IDX_DIM = 64
IQ_W = IDX_HEADS * IDX_DIM
TOPK_MAX = 256
ROPE_THETA = 500000.0
ROPE_FRAC = 4
A_SCALE = A_HEAD_DIM ** -0.5
Q_SCALE = A_SCALE * 1.4426950408889634
IDX_SCALE = (IDX_HEADS ** -0.5) * (IDX_DIM ** -0.5)
R_HEAD_DIM = 64
R_WIDTH = 2048
R_HEADS = R_WIDTH // R_HEAD_DIM
R_PAIRS = R_HEADS // 2
DECAY_LORA = 96
AAA_LORA = 96
GATE_LORA = 256
RWKV_COLS = 3 * R_WIDTH + DECAY_LORA + AAA_LORA + GATE_LORA
GN_EPS = 6.4e-4
CONV_W = 3
RMS_EPS = 1e-6

LANES = 128
VMEM_LIMIT = 56 * 1024 * 1024

ATT_Q0, ATT_K0, ATT_V0, ATT_IQ0, ATT_IK0 = 0, A_WIDTH, A_WIDTH + KV_W, A_WIDTH + 2 * KV_W, A_WIDTH + 2 * KV_W + IQ_W
ATT_USED = ATT_IK0 + IDX_DIM + IDX_HEADS
ATT_COLS = 4608
LORA_PAD = 128
RW_WD0 = 3 * R_WIDTH
RW_AD0 = RW_WD0 + LORA_PAD
RW_GD0 = RW_AD0 + LORA_PAD
RW_COLS = RW_GD0 + GATE_LORA
SCAN_BLOCK = 64
SCAN_SUB = 16
SCAN_GROUP_PAIRS = LANES // (2 * SCAN_SUB)
SCAN_GROUPS = R_PAIRS // SCAN_GROUP_PAIRS
SCAN_MATMULS = 2
SCAN_BULK = 4
INT_MIN = -2 ** 31
NEG_BIG = -1e30
HIGHEST = lax.Precision.HIGHEST


def _cparams(sem):
    return pltpu.CompilerParams(dimension_semantics=sem, vmem_limit_bytes=VMEM_LIMIT)


def _rmsnorm_kernel(x_ref, g_ref, o_ref):
    x = x_ref[...]
    y = x * lax.rsqrt(jnp.mean(x * x, axis=-1, keepdims=True) + RMS_EPS)
    o_ref[...] = (y * g_ref[...]).astype(o_ref.dtype)


def _rmsnorm(x, g, out_dtype):
    m, d = x.shape
    tm = min(m, 256)
    return pl.pallas_call(
        _rmsnorm_kernel,
        out_shape=jax.ShapeDtypeStruct((m, d), out_dtype),
        grid=(m // tm,),
        in_specs=[pl.BlockSpec((tm, d), lambda i: (i, 0)), pl.BlockSpec((1, d), lambda i: (0, 0))],
        out_specs=pl.BlockSpec((tm, d), lambda i: (i, 0)),
        compiler_params=_cparams(("parallel",)),
        name="rmsnorm",
    )(x, g.reshape(1, d).astype(F32))


def _mm_kernel(*refs, n_pairs, has_res):
    o_ref = refs[-1]
    acc = jnp.dot(refs[0][...], refs[n_pairs][...], preferred_element_type=F32)
    for p in range(1, n_pairs):
        acc = acc + jnp.dot(refs[p][...], refs[n_pairs + p][...], preferred_element_type=F32)
    if has_res:
        acc = refs[2 * n_pairs][...] + acc
    o_ref[...] = acc.astype(o_ref.dtype)


def _matmul(a_list, b_list, res=None, tm=512, tn=512, name="matmul"):
    m = a_list[0].shape[0]
    n = b_list[0].shape[1]
    tm = min(tm, m)
    tn = min(tn, n)
    assert m % tm == 0 and n % tn == 0, (m, n, tm, tn)
    in_specs = [pl.BlockSpec((tm, a.shape[1]), lambda i, j: (i, 0)) for a in a_list]
    in_specs += [pl.BlockSpec((b.shape[0], tn), lambda i, j: (0, j)) for b in b_list]
    args = list(a_list) + list(b_list)
    if res is not None:
        in_specs.append(pl.BlockSpec((tm, tn), lambda i, j: (i, j)))
        args.append(res)
    return pl.pallas_call(
        functools.partial(_mm_kernel, n_pairs=len(a_list), has_res=res is not None),
        out_shape=jax.ShapeDtypeStruct((m, n), F32),
        grid=(m // tm, n // tn),
        in_specs=in_specs,
        out_specs=pl.BlockSpec((tm, tn), lambda i, j: (i, j)),
        compiler_params=_cparams(("parallel", "arbitrary")),
        name=name,
    )(*args)


def _rope_tile(x, cos, sin, half, d_in_head):
    lo = d_in_head < half
    hi = (d_in_head >= half) & (d_in_head < 2 * half)
    c = jnp.where(lo | hi, cos, 1.0)
    s_up = jnp.where(lo, -sin, 0.0)
    s_dn = jnp.where(hi, sin, 0.0)
    x_up = pltpu.roll(x, LANES - half, axis=1)
    x_dn = pltpu.roll(x, half, axis=1)
    return x * c + x_up * s_up + x_dn * s_dn


def _rope_kernel(z_ref, invf_ref, q_ref, kf_ref, kb_ref, vf_ref, vb_ref, iq_ref, ikw_ref, ikb_ref, *, tm, t_len, pos0):
    i = pl.program_id(0)
    row = lax.broadcasted_iota(I32, (tm, LANES), 0) + i * tm
    pos = (pos0 + lax.rem(row, t_len)).astype(F32)
    lane = lax.broadcasted_iota(I32, (tm, LANES), 1)
    ang = pos * invf_ref[0:1, :]
    cos_a, sin_a = jnp.cos(ang), jnp.sin(ang)
    half_a = A_HEAD_DIM // ROPE_FRAC // 2
    for h in range(A_HEADS):
        x = z_ref[:, ATT_Q0 + h * LANES:ATT_Q0 + (h + 1) * LANES]
        q_ref[:, h * LANES:(h + 1) * LANES] = (_rope_tile(x, cos_a, sin_a, half_a, lane) * Q_SCALE).astype(q_ref.dtype)
    for h in range(A_KV_HEADS):
        x = z_ref[:, ATT_K0 + h * LANES:ATT_K0 + (h + 1) * LANES]
        y = _rope_tile(x, cos_a, sin_a, half_a, lane)
        kf_ref[:, h * LANES:(h + 1) * LANES] = y
        kb_ref[:, h * LANES:(h + 1) * LANES] = y.astype(kb_ref.dtype)
    v = z_ref[:, ATT_V0:ATT_V0 + KV_W]
    vf_ref[...] = v
    vb_ref[...] = v.astype(vb_ref.dtype)
    ang = pos * invf_ref[1:2, :]
    cos_i, sin_i = jnp.cos(ang), jnp.sin(ang)
    half_i = IDX_DIM // ROPE_FRAC // 2
    d_i = lane & (IDX_DIM - 1)
    for h in range(IQ_W // LANES):
        x = z_ref[:, ATT_IQ0 + h * LANES:ATT_IQ0 + (h + 1) * LANES]
        iq_ref[:, h * LANES:(h + 1) * LANES] = _rope_tile(x, cos_i, sin_i, half_i, d_i).astype(iq_ref.dtype)
    x = z_ref[:, ATT_IK0:ATT_IK0 + LANES]
    d_k = jnp.where(lane < IDX_DIM, lane, IDX_DIM)
    y = _rope_tile(x, cos_i, sin_i, half_i, d_k)
    ikw_ref[...] = y
    ikb_ref[...] = y[:, :IDX_DIM].astype(ikb_ref.dtype)


def _rope_split(z_att, t_len, pos0):
    m = z_att.shape[0]
    tm = min(m, 256)
    lane = jnp.arange(LANES)
    rd_a = A_HEAD_DIM // ROPE_FRAC
    rd_i = IDX_DIM // ROPE_FRAC
    invf_a = ROPE_THETA ** (-((lane % (rd_a // 2)).astype(F32) * 2.0 / rd_a))
    invf_i = ROPE_THETA ** (-((lane % (rd_i // 2)).astype(F32) * 2.0 / rd_i))
    invf = jnp.zeros((8, LANES), F32).at[0].set(invf_a).at[1].set(invf_i)
    row_spec = lambda w: pl.BlockSpec((tm, w), lambda i: (i, 0))
    shp = lambda w, dt: jax.ShapeDtypeStruct((m, w), dt)
    return pl.pallas_call(
        functools.partial(_rope_kernel, tm=tm, t_len=t_len, pos0=pos0),
        out_shape=(shp(A_WIDTH, BF16), shp(KV_W, F32), shp(KV_W, BF16), shp(KV_W, F32), shp(KV_W, BF16),
                   shp(IQ_W, BF16), shp(LANES, F32), shp(IDX_DIM, BF16)),
        grid=(m // tm,),
        in_specs=[row_spec(ATT_COLS), pl.BlockSpec((8, LANES), lambda i: (0, 0))],
        out_specs=(row_spec(A_WIDTH), row_spec(KV_W), row_spec(KV_W), row_spec(KV_W), row_spec(KV_W),
                   row_spec(IQ_W), row_spec(LANES), row_spec(IDX_DIM)),
        compiler_params=_cparams(("parallel",)),
        name="rope_split",
    )(z_att, invf)


def _tile(x, n, axis):
    return x if n == 1 else jnp.concatenate([x] * n, axis=axis)


def _sortable(score):
    u = lax.bitcast_convert_type(score, I32)
    return jnp.where(u < 0, u ^ jnp.int32(0x7FFFFFFF), u)


def _dsa_kernel(q_ref, iq_ref, ikw_ref, k_ref, v_ref, ik_ref, o_ref,
                key_ref, iwb_ref, qs_ref, m_ref, l_ref, acc_ref, *, tq, tk, pos0, n_keys, topk):
    i = pl.program_id(1)
    q0 = pos0 + i * tq
    kmax = jnp.minimum((lax.div(q0 + tq - 1, CHUNK) + 1) * CHUNK, n_keys)
    nkb = lax.div(kmax + tk - 1, tk)
    n_rep = tk // LANES
    nt_dims = (((1,), (1,)), ((), ()))

    for h in range(IDX_HEADS):
        iwb_ref[h] = jnp.broadcast_to(ikw_ref[:, IDX_DIM + h:IDX_DIM + h + 1], (tq, LANES))
    for h in range(A_HEADS):
        qs_ref[h * tq:(h + 1) * tq, :] = q_ref[:, h * LANES:(h + 1) * LANES]

    def score_block(kb, carry):
        koff = pl.multiple_of(kb * tk, tk)
        ikb = ik_ref[pl.ds(koff, tk), :]
        acc = jnp.zeros((tq, tk), F32)
        for h in range(IDX_HEADS):
            d = lax.dot_general(iq_ref[:, h * IDX_DIM:(h + 1) * IDX_DIM], ikb, nt_dims, preferred_element_type=F32)
            acc = acc + jnp.maximum(d, 0.0) * _tile(iwb_ref[h], n_rep, 1)
        kpos = koff + lax.broadcasted_iota(I32, (tq, tk), 1)
        qpos = q0 + lax.broadcasted_iota(I32, (tq, tk), 0)
        adm = (lax.shift_right_logical(kpos, 6) <= lax.shift_right_logical(qpos, 6)) & (kpos < n_keys)
        key_ref[:, pl.ds(koff, tk)] = jnp.where(adm, _sortable(acc * IDX_SCALE), jnp.int32(INT_MIN))
        return carry

    lax.fori_loop(0, nkb, score_block, 0)

    def count(pred):
        def body(kb, cnt):
            koff = pl.multiple_of(kb * tk, tk)
            keys = key_ref[:, pl.ds(koff, tk)]
            kpos = koff + lax.broadcasted_iota(I32, (tq, tk), 1)
            hit = jnp.where(pred(keys, kpos), 1.0, 0.0)
            for c in range(n_rep):
                cnt = cnt + hit[:, c * LANES:(c + 1) * LANES]
            return cnt
        cnt = lax.fori_loop(0, nkb, body, jnp.zeros((tq, LANES), F32))
        return jnp.broadcast_to(jnp.sum(cnt, axis=-1, keepdims=True), (tq, LANES))

    def wide(x):
        return _tile(x, n_rep, 1)

    def bit_step(it, tu):
        cand_u = tu | lax.shift_left(jnp.int32(1), 31 - it)
        cand_s = wide(cand_u ^ jnp.int32(INT_MIN))
        cnt = count(lambda keys, kpos: keys >= cand_s)
        return jnp.where(cnt >= topk, cand_u, tu)

    tu = lax.fori_loop(0, 32, bit_step, jnp.zeros((tq, LANES), I32))
    thr = jnp.maximum(tu ^ jnp.int32(INT_MIN), jnp.int32(INT_MIN + 1))
    thr_w = wide(thr)

    n_ge = count(lambda keys, kpos: keys >= thr_w)
    n_gt = count(lambda keys, kpos: keys > thr_w)
    excess = n_ge > topk

    @pl.when(jnp.max(jnp.where(excess, 1.0, 0.0)) > 0.0)
    def _():
        need = topk - n_gt

        idx_bits = int(key_ref.shape[1]).bit_length()

        def idx_step(it, jm):
            cand = wide(jm | lax.shift_left(jnp.int32(1), idx_bits - 1 - it))
            cnt = count(lambda keys, kpos: (keys == thr_w) & (kpos < cand))
            return jnp.where(cnt < need, cand[:, :LANES], jm)

        jm = lax.fori_loop(0, idx_bits, idx_step, jnp.zeros((tq, LANES), I32))
        jm_w = wide(jnp.where(excess, jm, jnp.int32(2 ** 31 - 1)))

        def drop(kb, carry):
            koff = pl.multiple_of(kb * tk, tk)
            keys = key_ref[:, pl.ds(koff, tk)]
            kpos = koff + lax.broadcasted_iota(I32, (tq, tk), 1)
            key_ref[:, pl.ds(koff, tk)] = jnp.where((keys == thr_w) & (kpos > jm_w), jnp.int32(INT_MIN), keys)
            return carry

        lax.fori_loop(0, nkb, drop, 0)

    m_ref[...] = jnp.full(m_ref.shape, NEG_BIG, F32)
    l_ref[...] = jnp.zeros(l_ref.shape, F32)
    acc_ref[...] = jnp.zeros(acc_ref.shape, F32)
    rows = A_GROUP * tq

    def attend(kb, carry):
        koff = pl.multiple_of(kb * tk, tk)
        bias = _tile(jnp.where(key_ref[:, pl.ds(koff, tk)] >= thr_w, 0.0, NEG_BIG), A_GROUP, 0)
        for n in range(A_KV_HEADS):
            r0 = n * rows
            kn = k_ref[pl.ds(koff, tk), n * LANES:(n + 1) * LANES]
            vn = v_ref[pl.ds(koff, tk), n * LANES:(n + 1) * LANES]
            s = lax.dot_general(qs_ref[r0:r0 + rows, :], kn, nt_dims, preferred_element_type=F32) + bias
            m_prev = m_ref[r0:r0 + rows, :]
            m_new = jnp.maximum(m_prev, jnp.max(s, axis=-1, keepdims=True))
            alpha = jnp.exp2(m_prev - m_new)
            p = jnp.exp2(s - _tile(m_new, n_rep, 1))
            l_ref[r0:r0 + rows, :] = alpha * l_ref[r0:r0 + rows, :] + jnp.sum(p, axis=-1, keepdims=True)
            acc_ref[r0:r0 + rows, :] = alpha * acc_ref[r0:r0 + rows, :] + jnp.dot(
                p.astype(vn.dtype), vn, preferred_element_type=F32)
            m_ref[r0:r0 + rows, :] = m_new
        return carry

    lax.fori_loop(0, nkb, attend, 0)
    for h in range(A_HEADS):
        o_ref[:, h * LANES:(h + 1) * LANES] = (
            acc_ref[h * tq:(h + 1) * tq, :] / l_ref[h * tq:(h + 1) * tq, :]).astype(o_ref.dtype)


def _dsa(q_bf, iq_bf, ikw, k_all, v_all, ik_all, *, n_batch, t_len, pos0, n_keys):
    lp = k_all.shape[1]
    tq = min(t_len, 128)
    tk = min(lp, 512)
    assert lp % tk == 0 and t_len % tq == 0
    nq = t_len // tq
    topk = min(TOPK_MAX, n_keys // 4)
    qrow = lambda w: pl.BlockSpec((tq, w), lambda b, i: (b * nq + i, 0))
    kv_spec = lambda w: pl.BlockSpec((None, lp, w), lambda b, i: (b, 0, 0), pipeline_mode=pl.Buffered(1))
    return pl.pallas_call(
        functools.partial(_dsa_kernel, tq=tq, tk=tk, pos0=pos0, n_keys=n_keys, topk=float(topk)),
        out_shape=jax.ShapeDtypeStruct((n_batch * t_len, A_WIDTH), BF16),
        grid=(n_batch, nq),
        in_specs=[qrow(A_WIDTH), qrow(IQ_W), qrow(LANES), kv_spec(KV_W), kv_spec(KV_W), kv_spec(IDX_DIM)],
        out_specs=qrow(A_WIDTH),
        scratch_shapes=[
            pltpu.VMEM((tq, lp), I32),
            pltpu.VMEM((IDX_HEADS, tq, LANES), F32),
            pltpu.VMEM((A_HEADS * tq, LANES), BF16),
            pltpu.VMEM((A_HEADS * tq, LANES), F32),
            pltpu.VMEM((A_HEADS * tq, LANES), F32),
            pltpu.VMEM((A_HEADS * tq, LANES), F32),
        ],
        compiler_params=_cparams(("parallel", "arbitrary")),
        name="dsa",
    )(q_bf, iq_bf, ikw, k_all, v_all, ik_all)


def _head_sums(x, ones_bd):
    n = x.shape[1] // LANES
    tm = x.shape[0]
    stacked = jnp.concatenate([x[:, c * LANES:(c + 1) * LANES] for c in range(n)], axis=0)
    s = jnp.dot(stacked, ones_bd, precision=HIGHEST, preferred_element_type=F32)
    return jnp.concatenate([s[c * tm:(c + 1) * tm, :] for c in range(n)], axis=1)


def _rwkv_pre_kernel(z_ref, zp_ref, z0_ref, mu_ref, w0_ref, a0_ref, kk_ref, ka_ref, w2_ref, a2_ref, g2_ref,
                     ones_ref, r_out, w_out, k_out, v_out, nkk_out, b_out, g_out, *, tm):
    i = pl.program_id(1)
    row = lax.broadcasted_iota(I32, (tm, 1), 0)

    def mixed(c0, width):
        z = z_ref[:, c0:c0 + width]
        first = jnp.where(i == 0, z0_ref[:, c0:c0 + width], zp_ref[7:8, c0:c0 + width])
        shifted = jnp.where(row == 0, first, pltpu.roll(z, 1, axis=0))
        return z + (shifted - z) * mu_ref[:, c0:c0 + width]

    r = mixed(0, R_WIDTH)
    k = mixed(R_WIDTH, R_WIDTH)
    v = mixed(2 * R_WIDTH, R_WIDTH)
    wd = mixed(RW_WD0, LORA_PAD)
    ad = mixed(RW_AD0, LORA_PAD)
    gd = mixed(RW_GD0, GATE_LORA)
    lora = lambda x, w_ref: jnp.dot(x, w_ref[...], precision=HIGHEST, preferred_element_type=F32)
    y = -(w0_ref[...] + lora(jnp.tanh(wd), w2_ref))
    softplus = jnp.maximum(y, 0.0) + jnp.log(1.0 + jnp.exp(-jnp.abs(y)))
    decay = jnp.exp(-jnp.exp(-softplus - 0.5))
    a = jax.nn.sigmoid(a0_ref[...] + lora(ad, a2_ref))
    g = lora(jax.nn.sigmoid(gd), g2_ref)
    kk = k * kk_ref[...]
    kk = kk / jnp.maximum(jnp.sqrt(_head_sums(kk * kk, ones_ref[...])), 1e-12)
    r_out[...] = r
    w_out[...] = decay
    k_out[...] = k * (1.0 + (a - 1.0) * ka_ref[...])
    v_out[...] = v
    nkk_out[...] = -kk
    b_out[...] = kk * a
    g_out[...] = g


def _rwkv_pre(z_rw, z0, mu, w0, a0, k_k, k_a, w2, a2, g2, ones_bd):
    nb, t_len, _ = z_rw.shape
    tm = min(t_len, 128)
    zrow = pl.BlockSpec((None, tm, RW_COLS), lambda b, i: (b, i, 0))
    zprev = pl.BlockSpec((None, 8, RW_COLS), lambda b, i: (b, jnp.maximum(i * (tm // 8) - 1, 0), 0))
    full = lambda a: pl.BlockSpec(a.shape, lambda b, i: (0,) * a.ndim)
    orow = pl.BlockSpec((None, tm, R_WIDTH), lambda b, i: (b, i, 0))
    params = (mu, w0, a0, k_k, k_a, w2, a2, g2, ones_bd)
    return pl.pallas_call(
        functools.partial(_rwkv_pre_kernel, tm=tm),
        out_shape=tuple(jax.ShapeDtypeStruct((nb, t_len, R_WIDTH), F32) for _ in range(7)),
        grid=(nb, t_len // tm),
        in_specs=[zrow, zprev, pl.BlockSpec((None, 1, RW_COLS), lambda b, i: (b, 0, 0))] + [full(p) for p in params],
        out_specs=tuple(orow for _ in range(7)),
        compiler_params=_cparams(("parallel", "arbitrary")),
        name="rwkv_pre",
    )(z_rw, z_rw, z0, *params)


def _split2(x):
    hi = x.astype(BF16)
    return hi, (x - hi.astype(F32)).astype(BF16)


def _rwkv_scan_kernel(r_ref, w_ref, k_ref, nkk_ref, b_ref, v_ref, s0_ref, ones2_ref, hot_ref, spread_ref,
                      y_ref, st_ref, s_ref, xs_ref, vc_ref, vc2_ref, *, tb):
    tblk = pl.program_id(1)
    cat = lambda xs, ax=0: jnp.concatenate(xs, axis=ax)

    @pl.when(tblk == 0)
    def _():
        s_ref[...] = s0_ref[...]

    pad = jnp.zeros((SCAN_BLOCK - tb, LANES), F32)
    v_cols = []
    for p in range(R_PAIRS):
        vp = v_ref[:, p * LANES:(p + 1) * LANES]
        v_cols.append((cat([vp, pad]) if tb < SCAN_BLOCK else vp).T)
    for sub in range(tb // SCAN_SUB):
        for g in range(SCAN_GROUPS):
            x = cat([v_cols[SCAN_GROUP_PAIRS * g + q][h * R_HEAD_DIM:(h + 1) * R_HEAD_DIM,
                                                      sub * SCAN_SUB:(sub + 1) * SCAN_SUB]
                     for q in range(SCAN_GROUP_PAIRS) for h in range(2)], 1)
            hi, mid = _split2(x)
            xs_ref[sub, g * R_HEAD_DIM:(g + 1) * R_HEAD_DIM, :] = cat([hi, mid], 1)
    ones2 = ones2_ref[...]
    spread = spread_ref[...]
    y_ref[...] = jnp.zeros(y_ref.shape, F32)
    lane_t = lax.rem(lax.broadcasted_iota(I32, (R_HEAD_DIM, LANES), 1), R_HEAD_DIM)
    rowp = lambda ref, t, p: ref[t, p:p + 1, :]
    group_pairs = R_PAIRS // SCAN_MATMULS
    half = group_pairs // 2 * R_HEAD_DIM

    groups = [range(g * group_pairs, (g + 1) * group_pairs) for g in range(SCAN_MATMULS)]

    def packed(xs):
        return cat([cat(xs[2 * j:2 * j + 2], 1) for j in range(group_pairs // 2)])

    def pair_tile(res, q):
        return res[(q // 2) * R_HEAD_DIM:(q // 2 + 1) * R_HEAD_DIM, (q % 2) * LANES:(q % 2 + 1) * LANES]

    def head_sums(g, states, t_sa, t_y=None):
        rows = [packed([(s * rowp(nkk_ref, t_sa, p)).astype(BF16) for s, p in zip(states, groups[g])])]
        if t_y is not None:
            rows.append(packed([(s * rowp(r_ref, t_y, p)).astype(BF16) for s, p in zip(states, groups[g])]))
        return jnp.dot(cat(rows), ones2, preferred_element_type=F32)

    def value_columns(sub, i, vc_out):
        xs = xs_ref[sub]
        n = xs.shape[0]
        for j0 in range(0, SCAN_BULK, SCAN_BULK // 2):
            steps = [SCAN_BULK * i + j0 + j for j in range(SCAN_BULK // 2)]
            res = jnp.dot(cat([xs * hot_ref[tt, 0:1, :] for tt in steps]), spread, preferred_element_type=F32)
            for j, tt in enumerate(steps):
                vc_out[tt] = res[j * n:(j + 1) * n]

    def step(sub, tt, vc_in, sas):
        t = sub * SCAN_SUB + tt
        t_next = jnp.minimum(t + 1, tb - 1)
        hit = lane_t == t
        new_sas = []
        for g in range(SCAN_MATMULS):
            states = []
            for q, p in enumerate(groups[g]):
                vg, vq = divmod(p, SCAN_GROUP_PAIRS)
                vc = vc_in[tt, vg * R_HEAD_DIM:(vg + 1) * R_HEAD_DIM, vq * LANES:(vq + 1) * LANES]
                s_new = (s_ref[p] * rowp(w_ref, t, p) + pair_tile(sas[g], q) * rowp(b_ref, t, p)
                         + vc * rowp(k_ref, t, p))
                s_ref[p] = s_new
                states.append(s_new)
            res = head_sums(g, states, t_next, t)
            new_sas.append(res[:half])
            for q, p in enumerate(groups[g]):
                y_ref[p] = jnp.where(hit, pair_tile(res[half:], q), y_ref[p])
        return tuple(new_sas)

    def sub_block(sub, vc_in, vc_out, sas):
        nxt = jnp.minimum(sub + 1, n_sub - 1)

        def four_steps(i, sas):
            sas = step(sub, SCAN_BULK * i, vc_in, sas)
            if n_sub > 1:
                value_columns(nxt, i, vc_out)
            for u in range(1, SCAN_BULK):
                sas = step(sub, SCAN_BULK * i + u, vc_in, sas)
            return sas

        return lax.fori_loop(0, SCAN_SUB // SCAN_BULK, four_steps, sas)

    n_sub = tb // SCAN_SUB
    for i in range(SCAN_SUB // SCAN_BULK):
        value_columns(0, i, vc_ref)
    sas = tuple(head_sums(g, [s_ref[p] for p in groups[g]], 0) for g in range(SCAN_MATMULS))
    if n_sub == 1:
        sub_block(0, vc_ref, vc2_ref, sas)
    else:
        def two_sub_blocks(i, sas):
            return sub_block(2 * i + 1, vc2_ref, vc_ref, sub_block(2 * i, vc_ref, vc2_ref, sas))

        lax.fori_loop(0, n_sub // 2, two_sub_blocks, sas)

    @pl.when(tblk == pl.num_programs(1) - 1)
    def _():
        st_ref[...] = s_ref[...]


def _rwkv_scan(r, w, k, nkk, b, v, s0):
    nb, t_len, _ = r.shape
    tb = min(t_len, SCAN_BLOCK)
    nblk = t_len // tb
    hd = R_HEAD_DIM
    nsub, sb, ng, gp = tb // SCAN_SUB, SCAN_SUB, SCAN_GROUPS, SCAN_GROUP_PAIRS
    s0p = s0.reshape(nb, R_PAIRS, 2, hd, hd).transpose(0, 1, 3, 2, 4).reshape(nb, R_PAIRS, hd, LANES)
    lane2_h = jnp.arange(2 * LANES) // hd
    ones2 = (lane2_h[:, None] == lane2_h[None, :]).astype(BF16)
    src = jnp.arange(2 * LANES) % LANES
    src_q, src_h, src_t = src // (2 * sb), (src // sb) % 2, src % sb
    dst = jnp.arange(gp * LANES)
    dst_q, dst_h = dst // LANES, (dst % LANES) // hd
    spread = ((src_q[:, None] == dst_q[None, :]) & (src_h[:, None] == dst_h[None, :])).astype(BF16)
    hot = jnp.broadcast_to((src_t[None, :] == jnp.arange(sb)[:, None])[:, None, :], (sb, 16, 2 * LANES)).astype(BF16)
    trow = pl.BlockSpec((None, tb, R_PAIRS, LANES), lambda bb, i: (bb, i, 0, 0))
    r, w, k, nkk, b = (a.reshape(nb, t_len, R_PAIRS, LANES) for a in (r, w, k, nkk, b))
    st = pl.BlockSpec((None, R_PAIRS, hd, LANES), lambda bb, i: (bb, 0, 0, 0))
    full = lambda a: pl.BlockSpec(a.shape, lambda bb, i: (0,) * a.ndim)
    ycol_spec = pl.BlockSpec((None, None, R_PAIRS, hd, LANES), lambda bb, i: (bb, i, 0, 0, 0))
    ycol, s_t = pl.pallas_call(
        functools.partial(_rwkv_scan_kernel, tb=tb),
        out_shape=(jax.ShapeDtypeStruct((nb, nblk, R_PAIRS, hd, LANES), F32),
                   jax.ShapeDtypeStruct((nb, R_PAIRS, hd, LANES), F32)),
        grid=(nb, nblk),
        in_specs=[trow, trow, trow, trow, trow, pl.BlockSpec((None, tb, R_WIDTH), lambda bb, i: (bb, i, 0)), st,
                  full(ones2), full(hot), full(spread)],
        out_specs=(ycol_spec, st),
        scratch_shapes=[pltpu.VMEM((R_PAIRS, hd, LANES), F32), pltpu.VMEM((nsub, ng * hd, 2 * LANES), BF16),
                        pltpu.VMEM((sb, ng * hd, gp * LANES), F32), pltpu.VMEM((sb, ng * hd, gp * LANES), F32)],
        compiler_params=_cparams(("parallel", "arbitrary")),
        name="rwkv_scan",
    )(r, w, k, nkk, b, v, s0p, ones2, hot, spread)
    s_t = s_t.reshape(nb, R_PAIRS, hd, 2, hd).transpose(0, 1, 3, 2, 4).reshape(nb, R_HEADS, hd, hd)
    return ycol, s_t


def _rwkv_post_kernel(y_ref, r_ref, k_ref, v_ref, g_ref, lw_ref, lb_ref, rk_ref, ones_ref, o_ref, *, tb):
    ones_bd = ones_ref[...]
    tiles = []
    for p in range(R_PAIRS):
        yt = y_ref[p].T
        tiles.append(jnp.concatenate([yt[0:tb, :], yt[R_HEAD_DIM:R_HEAD_DIM + tb, :]], axis=1))
    y = jnp.concatenate(tiles, axis=1)
    mean = _head_sums(y, ones_bd) * (1.0 / R_HEAD_DIM)
    d = y - mean
    var = _head_sums(d * d, ones_bd) * (1.0 / R_HEAD_DIM)
    yn = d * lax.rsqrt(var + GN_EPS) * lw_ref[...] + lb_ref[...]
    bonus = _head_sums(r_ref[...] * k_ref[...] * rk_ref[...], ones_bd) * v_ref[...]
    o_ref[...] = ((yn + bonus) * g_ref[...]).astype(o_ref.dtype)


def _rwkv_post(ycol, r, k, v, g, lnx_w, lnx_b, r_k, ones_bd):
    m = r.shape[0]
    nblk = ycol.shape[1]
    tb = m // (ycol.shape[0] * nblk)
    row = pl.BlockSpec((tb, R_WIDTH), lambda i: (i, 0))
    full = lambda a: pl.BlockSpec(a.shape, lambda i: (0,) * a.ndim)
    params = (lnx_w, lnx_b, r_k, ones_bd)
    return pl.pallas_call(
        functools.partial(_rwkv_post_kernel, tb=tb),
        out_shape=jax.ShapeDtypeStruct((m, R_WIDTH), BF16),
        grid=(m // tb,),
        in_specs=[pl.BlockSpec((None, None, R_PAIRS, R_HEAD_DIM, LANES), lambda i: (i // nblk, i % nblk, 0, 0, 0))]
        + [row] * 4 + [full(p) for p in params],
        out_specs=row,
        compiler_params=_cparams(("parallel",)),
        name="rwkv_post",
    )(ycol, r, k, v, g, *params)


def _ffn_up_kernel(x_ref, xp_ref, wg_ref, wu_ref, c0_ref, cw_ref, cb_ref, act_ref, tail_ref, *,
                   seq_rows, seqs_per_tile, tiles_per_seq):
    first = lax.rem(pl.program_id(0), tiles_per_seq) == 0
    wg = wg_ref[...].astype(BF16)
    gate_all = jnp.dot(x_ref[...], wg, preferred_element_type=F32)
    up_all = jnp.dot(x_ref[...], wu_ref[...].astype(BF16), preferred_element_type=F32)
    gate_prev = jnp.dot(xp_ref[...], wg, preferred_element_type=F32)
    row = lax.broadcasted_iota(I32, (seq_rows, 1), 0)
    for s in range(seqs_per_tile):
        rows = slice(s * seq_rows, (s + 1) * seq_rows)
        gate = gate_all[rows]
        prev1 = jnp.where(first, c0_ref[s, 1:2, :], gate_prev[7:8, :])
        prev2 = jnp.where(first, c0_ref[s, 0:1, :], gate_prev[6:7, :])
        g_m1 = jnp.where(row == 0, prev1, pltpu.roll(gate, 1, axis=0))
        g_m2 = jnp.where(row == 0, prev2, jnp.where(row == 1, prev1, pltpu.roll(gate, 2, axis=0)))
        conv = cb_ref[...] + g_m2 * cw_ref[0:1, :]
        conv = conv + g_m1 * cw_ref[1:2, :]
        conv = conv + gate * cw_ref[2:3, :]
        act_ref[rows, :] = (conv * jax.nn.sigmoid(conv) * up_all[rows]).astype(act_ref.dtype)
        tail_ref[s] = gate[seq_rows - 8:, :]


def _ffn_up(x, w_in, conv0, conv_w, conv_b, n_batch, t_len, tm=1024, tn=256):
    m, d = x.shape
    d_ff = w_in.shape[1] // 2
    tm = min(m, tm)
    seq_rows = min(t_len, tm)
    seqs_per_tile, tiles_per_seq = tm // seq_rows, t_len // seq_rows
    assert m % tm == 0 and tm % seq_rows == 0 and t_len % seq_rows == 0 and seq_rows % 8 == 0 and d_ff % tn == 0
    n_up = d_ff // tn
    act, tail = pl.pallas_call(
        functools.partial(_ffn_up_kernel, seq_rows=seq_rows, seqs_per_tile=seqs_per_tile, tiles_per_seq=tiles_per_seq),
        out_shape=(jax.ShapeDtypeStruct((m, d_ff), BF16),
                   jax.ShapeDtypeStruct((n_batch * tiles_per_seq, 8, d_ff), F32)),
        grid=(m // tm, n_up),
        in_specs=[pl.BlockSpec((tm, d), lambda i, j: (i, 0)),
                  pl.BlockSpec((8, d), lambda i, j: (jnp.maximum(i * (tm // 8) - 1, 0), 0)),
                  pl.BlockSpec((d, tn), lambda i, j: (0, j)),
                  pl.BlockSpec((d, tn), lambda i, j: (0, j + n_up)),
                  pl.BlockSpec((seqs_per_tile, CONV_W - 1, tn), lambda i, j: (i // tiles_per_seq, 0, j)),
                  pl.BlockSpec((CONV_W, tn), lambda i, j: (0, j)),
                  pl.BlockSpec((1, tn), lambda i, j: (0, j))],
        out_specs=(pl.BlockSpec((tm, tn), lambda i, j: (i, j)),
                   pl.BlockSpec((seqs_per_tile, 8, tn), lambda i, j: (i, 0, j))),
        compiler_params=_cparams(("parallel", "arbitrary")),
        name="ffn_up",
    )(x, x, w_in, w_in, conv0, conv_w, conv_b.reshape(1, d_ff))
    gate_tail = tail.reshape(n_batch, tiles_per_seq, 8, d_ff)[:, -1, 8 - (CONV_W - 1):, :]
    return act, gate_tail


def _pad_rw_cols(a):
    z = lambda n: jnp.zeros(a.shape[:-1] + (n,), a.dtype)
    wd0, ad0, gd0 = 3 * R_WIDTH, 3 * R_WIDTH + DECAY_LORA, 3 * R_WIDTH + DECAY_LORA + AAA_LORA
    return jnp.concatenate([a[..., :wd0], a[..., wd0:ad0], z(LORA_PAD - DECAY_LORA), a[..., ad0:gd0],
                            z(LORA_PAD - AAA_LORA), a[..., gd0:]], axis=-1)


def _unpad_rw_cols(a):
    return jnp.concatenate([a[..., :RW_WD0 + DECAY_LORA], a[..., RW_AD0:RW_AD0 + AAA_LORA], a[..., RW_GD0:]], axis=-1)


def _prep_weights(norm_mix_g, w_in, rwkv_mu, rwkv_w0, rwkv_w2, rwkv_a0, rwkv_a2, rwkv_g2, rwkv_k_k, rwkv_k_a,
                  rwkv_r_k, rwkv_lnx_w, rwkv_lnx_b, w_out, norm_ffn_g, ffn_w_in, ffn_conv_w, ffn_conv_b,
                  ffn_w_down, norm_final_g, l):
    d = w_in.shape[1]
    w_att = jnp.concatenate([w_in[l][:, :ATT_USED].astype(BF16), jnp.zeros((d, ATT_COLS - ATT_USED), BF16)], axis=1)
    w_rw = _pad_rw_cols(w_in[l][:, ATT_USED:].astype(BF16))
    row = lambda a: a.reshape(1, -1).astype(F32)
    pad_rows = lambda a, n: jnp.concatenate([a, jnp.zeros((n - a.shape[0], a.shape[1]), a.dtype)], axis=0)
    lane_h = jnp.arange(LANES) // R_HEAD_DIM
    return dict(
        norm_mix_g=norm_mix_g[l], w_att=w_att, w_rw=w_rw,
        mu=_pad_rw_cols(row(rwkv_mu[l])), w0=row(rwkv_w0[l]), a0=row(rwkv_a0[l]),
        k_k=row(rwkv_k_k[l]), k_a=row(rwkv_k_a[l]),
        w2=pad_rows(rwkv_w2[l], LORA_PAD), a2=pad_rows(rwkv_a2[l], LORA_PAD), g2=rwkv_g2[l],
        r_k=row(rwkv_r_k[l]), lnx_w=row(rwkv_lnx_w[l]), lnx_b=row(rwkv_lnx_b[l]),
        ones_bd=(lane_h[:, None] == lane_h[None, :]).astype(F32),
        w_out_a=w_out[l][:A_WIDTH].astype(BF16), w_out_r=w_out[l][A_WIDTH:].astype(BF16),
        norm_ffn_g=norm_ffn_g[l], ffn_w_in=ffn_w_in[l], conv_w=ffn_conv_w[l], conv_b=ffn_conv_b[l],
        ffn_w_down=ffn_w_down[l].astype(BF16), norm_final_g=norm_final_g,
    )


def _trunk(x, past_k, past_v, past_ik, s0, shift0, conv0, wt):
    nb, t_len, d = x.shape
    m = nb * t_len
    p_len = 0 if past_k is None else past_k.shape[1]
    n_keys = p_len + t_len
    x2 = x.reshape(m, d)

    h = _rmsnorm(x2, wt["norm_mix_g"], BF16)
    z_att = _matmul([h], [wt["w_att"]], tm=1024, name="proj_att")
    z_rw = _matmul([h], [wt["w_rw"]], tm=1024, name="proj_rw")

    q_bf, k_f, k_bf, v_f, v_bf, iq_bf, ikw, ik_bf = _rope_split(z_att, t_len, p_len)
    tk = min(-(-n_keys // LANES) * LANES, 512)
    lp = -(-n_keys // tk) * tk

    def with_past(new, past, width):
        new = new.reshape(nb, t_len, width)
        parts = [new] if past is None else [past.reshape(nb, p_len, width).astype(BF16), new]
        if lp > n_keys:
            parts.append(jnp.zeros((nb, lp - n_keys, width), BF16))
        return parts[0] if len(parts) == 1 else jnp.concatenate(parts, axis=1)

    attn = _dsa(q_bf, iq_bf, ikw, with_past(k_bf, past_k, KV_W), with_past(v_bf, past_v, KV_W),
                with_past(ik_bf, past_ik, IDX_DIM), n_batch=nb, t_len=t_len, pos0=p_len, n_keys=n_keys)

    z_rw3 = z_rw.reshape(nb, t_len, RW_COLS)
    r, w, k2, v2, nkk, b, g = _rwkv_pre(z_rw3, _pad_rw_cols(shift0.astype(F32)), wt["mu"], wt["w0"], wt["a0"],
                                       wt["k_k"], wt["k_a"], wt["w2"], wt["a2"], wt["g2"], wt["ones_bd"])
    ycol, s_t = _rwkv_scan(r, w, k2, nkk, b, v2, s0.astype(F32))
    flat = lambda a: a.reshape(m, R_WIDTH)
    rw = _rwkv_post(ycol, flat(r), flat(k2), flat(v2), flat(g), wt["lnx_w"], wt["lnx_b"], wt["r_k"], wt["ones_bd"])

    x1 = _matmul([attn, rw], [wt["w_out_a"], wt["w_out_r"]], res=x2, tm=1024, name="out_proj")
    hf = _rmsnorm(x1, wt["norm_ffn_g"], BF16)
    act, conv_t = _ffn_up(hf, wt["ffn_w_in"], conv0.astype(F32), wt["conv_w"], wt["conv_b"], nb, t_len)
    x3 = _matmul([act], [wt["ffn_w_down"]], res=x1, name="ffn_down")
    y_out = _rmsnorm(x3, wt["norm_final_g"], F32).reshape(nb, t_len, d)

    shift_t = _unpad_rw_cols(z_rw3[:, -1:])
    caches = (k_f.reshape(nb, t_len, A_KV_HEADS, A_HEAD_DIM)[None], v_f.reshape(nb, t_len, A_KV_HEADS, A_HEAD_DIM)[None],
              ikw[:, :IDX_DIM].reshape(nb, t_len, IDX_DIM)[None], s_t[None], shift_t[None], conv_t[None])
    return y_out, caches


def kernel(x_prompt, x_sample, cache_k, cache_v, cache_idx_k, state_rwkv, state_rwkv_shift, state_ffn_conv, norm_mix_g, w_in, rwkv_mu, rwkv_w0, rwkv_w2, rwkv_a0, rwkv_a2, rwkv_g2, rwkv_k_k, rwkv_k_a, rwkv_r_k, rwkv_lnx_w, rwkv_lnx_b, w_out, norm_ffn_g, ffn_w_in, ffn_conv_w, ffn_conv_b, ffn_w_down, norm_final_g):
    assert w_in.shape[0] == 1, "single-layer trunk"
    wt = _prep_weights(norm_mix_g, w_in, rwkv_mu, rwkv_w0, rwkv_w2, rwkv_a0, rwkv_a2, rwkv_g2, rwkv_k_k, rwkv_k_a,
                       rwkv_r_k, rwkv_lnx_w, rwkv_lnx_b, w_out, norm_ffn_g, ffn_w_in, ffn_conv_w, ffn_conv_b,
                       ffn_w_down, norm_final_g, 0)
    bp = x_prompt.shape[0]
    d_ff = ffn_conv_w.shape[-1]
    y_p, c_p = _trunk(x_prompt, None, None, None,
                      jnp.zeros((bp, R_HEADS, R_HEAD_DIM, R_HEAD_DIM), F32), jnp.zeros((bp, 1, RWKV_COLS), F32),
                      jnp.zeros((bp, CONV_W - 1, d_ff), F32), wt)
    y_s, c_s = _trunk(x_sample, cache_k[0], cache_v[0], cache_idx_k[0], state_rwkv[0], state_rwkv_shift[0],
                      state_ffn_conv[0], wt)
    return (y_p, y_s) + c_p + c_s
```

```python
import functools

import jax
import jax.numpy as jnp
from jax import lax
from jax.experimental import pallas as pl
from jax.experimental.pallas import tpu as pltpu

F32 = jnp.float32
BF16 = jnp.bfloat16
I32 = jnp.int32

CHUNK = 64
A_HEADS = 16
A_KV_HEADS = 4
A_HEAD_DIM = 128
A_GROUP = A_HEADS // A_KV_HEADS
A_WIDTH = A_HEADS * A_HEAD_DIM
KV_W = A_KV_HEADS * A_HEAD_DIM
IDX_HEADS = 16
IDX_DIM = 64
IQ_W = IDX_HEADS * IDX_DIM
TOPK_MAX = 256
ROPE_THETA = 500000.0
ROPE_FRAC = 4
A_SCALE = A_HEAD_DIM ** -0.5
Q_SCALE = A_SCALE * 1.4426950408889634
IDX_SCALE = (IDX_HEADS ** -0.5) * (IDX_DIM ** -0.5)
R_HEAD_DIM = 64
R_WIDTH = 2048
R_HEADS = R_WIDTH // R_HEAD_DIM
R_PAIRS = R_HEADS // 2
DECAY_LORA = 96
AAA_LORA = 96
GATE_LORA = 256
RWKV_COLS = 3 * R_WIDTH + DECAY_LORA + AAA_LORA + GATE_LORA
GN_EPS = 6.4e-4
CONV_W = 3
RMS_EPS = 1e-6

LANES = 128
VMEM_LIMIT = 56 * 1024 * 1024

ATT_Q0, ATT_K0, ATT_V0, ATT_IQ0, ATT_IK0 = 0, A_WIDTH, A_WIDTH + KV_W, A_WIDTH + 2 * KV_W, A_WIDTH + 2 * KV_W + IQ_W
ATT_USED = ATT_IK0 + IDX_DIM + IDX_HEADS
ATT_COLS = 4608
LORA_PAD = 128
RW_WD0 = 3 * R_WIDTH
RW_AD0 = RW_WD0 + LORA_PAD
RW_GD0 = RW_AD0 + LORA_PAD
RW_COLS = RW_GD0 + GATE_LORA
SCAN_BLOCK = 64
SCAN_SUB = 16
SCAN_GROUP_PAIRS = LANES // (2 * SCAN_SUB)
SCAN_GROUPS = R_PAIRS // SCAN_GROUP_PAIRS
SCAN_MATMULS = 2
SCAN_BULK = 4
INT_MIN = -2 ** 31
NEG_BIG = -1e30


def _cparams(sem):
    return pltpu.CompilerParams(dimension_semantics=sem, vmem_limit_bytes=VMEM_LIMIT)


def _rmsnorm_kernel(x_ref, g_ref, o_ref):
    x = x_ref[...]
    y = x * lax.rsqrt(jnp.mean(x * x, axis=-1, keepdims=True) + RMS_EPS)
    o_ref[...] = (y * g_ref[...]).astype(o_ref.dtype)


def _rmsnorm(x, g, out_dtype):
    m, d = x.shape
    tm = min(m, 256)
    return pl.pallas_call(
        _rmsnorm_kernel,
        out_shape=jax.ShapeDtypeStruct((m, d), out_dtype),
        grid=(m // tm,),
        in_specs=[pl.BlockSpec((tm, d), lambda i: (i, 0)), pl.BlockSpec((1, d), lambda i: (0, 0))],
        out_specs=pl.BlockSpec((tm, d), lambda i: (i, 0)),
        compiler_params=_cparams(("parallel",)),
        name="rmsnorm",
    )(x, g.reshape(1, d).astype(F32))


def _mm_kernel(*refs, n_pairs, has_res):
    o_ref = refs[-1]
    acc = jnp.dot(refs[0][...], refs[n_pairs][...], preferred_element_type=F32)
    for p in range(1, n_pairs):
        acc = acc + jnp.dot(refs[p][...], refs[n_pairs + p][...], preferred_element_type=F32)
    if has_res:
        acc = refs[2 * n_pairs][...] + acc
    o_ref[...] = acc.astype(o_ref.dtype)


def _matmul(a_list, b_list, res=None, tm=512, tn=512, name="matmul"):
    m = a_list[0].shape[0]
    n = b_list[0].shape[1]
    tm = min(tm, m)
    tn = min(tn, n)
    assert m % tm == 0 and n % tn == 0, (m, n, tm, tn)
    in_specs = [pl.BlockSpec((tm, a.shape[1]), lambda i, j: (i, 0)) for a in a_list]
    in_specs += [pl.BlockSpec((b.shape[0], tn), lambda i, j: (0, j)) for b in b_list]
    args = list(a_list) + list(b_list)
    if res is not None:
        in_specs.append(pl.BlockSpec((tm, tn), lambda i, j: (i, j)))
        args.append(res)
    return pl.pallas_call(
        functools.partial(_mm_kernel, n_pairs=len(a_list), has_res=res is not None),
        out_shape=jax.ShapeDtypeStruct((m, n), F32),
        grid=(m // tm, n // tn),
        in_specs=in_specs,
        out_specs=pl.BlockSpec((tm, tn), lambda i, j: (i, j)),
        compiler_params=_cparams(("parallel", "arbitrary")),
        name=name,
    )(*args)


def _rope_tile(x, cos, sin, half, d_in_head):
    lo = d_in_head < half
    hi = (d_in_head >= half) & (d_in_head < 2 * half)
    c = jnp.where(lo | hi, cos, 1.0)
    s_up = jnp.where(lo, -sin, 0.0)
    s_dn = jnp.where(hi, sin, 0.0)
    x_up = pltpu.roll(x, LANES - half, axis=1)
    x_dn = pltpu.roll(x, half, axis=1)
    return x * c + x_up * s_up + x_dn * s_dn


def _rope_kernel(z_ref, invf_ref, q_ref, kf_ref, kb_ref, vf_ref, vb_ref, iq_ref, ikw_ref, ikb_ref, *, tm, t_len, pos0):
    i = pl.program_id(0)
    row = lax.broadcasted_iota(I32, (tm, LANES), 0) + i * tm
    pos = (pos0 + lax.rem(row, t_len)).astype(F32)
    lane = lax.broadcasted_iota(I32, (tm, LANES), 1)
    ang = pos * invf_ref[0:1, :]
    cos_a, sin_a = jnp.cos(ang), jnp.sin(ang)
    half_a = A_HEAD_DIM // ROPE_FRAC // 2
    for h in range(A_HEADS):
        x = z_ref[:, ATT_Q0 + h * LANES:ATT_Q0 + (h + 1) * LANES]
        q_ref[:, h * LANES:(h + 1) * LANES] = (_rope_tile(x, cos_a, sin_a, half_a, lane) * Q_SCALE).astype(q_ref.dtype)
    for h in range(A_KV_HEADS):
        x = z_ref[:, ATT_K0 + h * LANES:ATT_K0 + (h + 1) * LANES]
        y = _rope_tile(x, cos_a, sin_a, half_a, lane)
        kf_ref[:, h * LANES:(h + 1) * LANES] = y
        kb_ref[:, h * LANES:(h + 1) * LANES] = y.astype(kb_ref.dtype)
    v = z_ref[:, ATT_V0:ATT_V0 + KV_W]
    vf_ref[...] = v
    vb_ref[...] = v.astype(vb_ref.dtype)
    ang = pos * invf_ref[1:2, :]
    cos_i, sin_i = jnp.cos(ang), jnp.sin(ang)
    half_i = IDX_DIM // ROPE_FRAC // 2
    d_i = lane & (IDX_DIM - 1)
    for h in range(IQ_W // LANES):
        x = z_ref[:, ATT_IQ0 + h * LANES:ATT_IQ0 + (h + 1) * LANES]
        iq_ref[:, h * LANES:(h + 1) * LANES] = _rope_tile(x, cos_i, sin_i, half_i, d_i).astype(iq_ref.dtype)
    x = z_ref[:, ATT_IK0:ATT_IK0 + LANES]
    d_k = jnp.where(lane < IDX_DIM, lane, IDX_DIM)
    y = _rope_tile(x, cos_i, sin_i, half_i, d_k)
    ikw_ref[...] = y
    ikb_ref[...] = y[:, :IDX_DIM].astype(ikb_ref.dtype)


def _rope_split(z_att, t_len, pos0):
    m = z_att.shape[0]
    tm = min(m, 256)
    lane = jnp.arange(LANES)
    rd_a = A_HEAD_DIM // ROPE_FRAC
    rd_i = IDX_DIM // ROPE_FRAC
    invf_a = ROPE_THETA ** (-((lane % (rd_a // 2)).astype(F32) * 2.0 / rd_a))
    invf_i = ROPE_THETA ** (-((lane % (rd_i // 2)).astype(F32) * 2.0 / rd_i))
    invf = jnp.zeros((8, LANES), F32).at[0].set(invf_a).at[1].set(invf_i)
    row_spec = lambda w: pl.BlockSpec((tm, w), lambda i: (i, 0))
    shp = lambda w, dt: jax.ShapeDtypeStruct((m, w), dt)
    return pl.pallas_call(
        functools.partial(_rope_kernel, tm=tm, t_len=t_len, pos0=pos0),
        out_shape=(shp(A_WIDTH, BF16), shp(KV_W, F32), shp(KV_W, BF16), shp(KV_W, F32), shp(KV_W, BF16),
                   shp(IQ_W, BF16), shp(LANES, F32), shp(IDX_DIM, BF16)),
        grid=(m // tm,),
        in_specs=[row_spec(ATT_COLS), pl.BlockSpec((8, LANES), lambda i: (0, 0))],
        out_specs=(row_spec(A_WIDTH), row_spec(KV_W), row_spec(KV_W), row_spec(KV_W), row_spec(KV_W),
                   row_spec(IQ_W), row_spec(LANES), row_spec(IDX_DIM)),
        compiler_params=_cparams(("parallel",)),
        name="rope_split",
    )(z_att, invf)


def _tile(x, n, axis):
    return x if n == 1 else jnp.concatenate([x] * n, axis=axis)


def _sortable(score):
    u = lax.bitcast_convert_type(score, I32)
    return jnp.where(u < 0, u ^ jnp.int32(0x7FFFFFFF), u)


def _dsa_kernel(q_ref, iq_ref, ikw_ref, k_ref, v_ref, ik_ref, o_ref,
                key_ref, iwb_ref, qs_ref, m_ref, l_ref, acc_ref, *, tq, tk, pos0, n_keys, topk):
    i = pl.program_id(1)
    q0 = pos0 + i * tq
    kmax = jnp.minimum((lax.div(q0 + tq - 1, CHUNK) + 1) * CHUNK, n_keys)
    nkb = lax.div(kmax + tk - 1, tk)
    n_rep = tk // LANES
    nt_dims = (((1,), (1,)), ((), ()))

    for h in range(IDX_HEADS):
        iwb_ref[h] = jnp.broadcast_to(ikw_ref[:, IDX_DIM + h:IDX_DIM + h + 1], (tq, LANES))
    for h in range(A_HEADS):
        qs_ref[h * tq:(h + 1) * tq, :] = q_ref[:, h * LANES:(h + 1) * LANES]

    def score_block(kb, carry):
        koff = pl.multiple_of(kb * tk, tk)
        ikb = ik_ref[pl.ds(koff, tk), :]
        acc = jnp.zeros((tq, tk), F32)
        for h in range(IDX_HEADS):
            d = lax.dot_general(iq_ref[:, h * IDX_DIM:(h + 1) * IDX_DIM], ikb, nt_dims, preferred_element_type=F32)
            acc = acc + jnp.maximum(d, 0.0) * _tile(iwb_ref[h], n_rep, 1)
        kpos = koff + lax.broadcasted_iota(I32, (tq, tk), 1)
        qpos = q0 + lax.broadcasted_iota(I32, (tq, tk), 0)
        adm = (lax.shift_right_logical(kpos, 6) <= lax.shift_right_logical(qpos, 6)) & (kpos < n_keys)
        key_ref[:, pl.ds(koff, tk)] = jnp.where(adm, _sortable(acc * IDX_SCALE), jnp.int32(INT_MIN))
        return carry

    lax.fori_loop(0, nkb, score_block, 0)

    def count(pred):
        def body(kb, cnt):
            koff = pl.multiple_of(kb * tk, tk)
            keys = key_ref[:, pl.ds(koff, tk)]
            kpos = koff + lax.broadcasted_iota(I32, (tq, tk), 1)
            hit = jnp.where(pred(keys, kpos), 1.0, 0.0)
            for c in range(n_rep):
                cnt = cnt + hit[:, c * LANES:(c + 1) * LANES]
            return cnt
        cnt = lax.fori_loop(0, nkb, body, jnp.zeros((tq, LANES), F32))
        return jnp.broadcast_to(jnp.sum(cnt, axis=-1, keepdims=True), (tq, LANES))

    def wide(x):
        return _tile(x, n_rep, 1)

    def bit_step(it, tu):
        cand_u = tu | lax.shift_left(jnp.int32(1), 31 - it)
        cand_s = wide(cand_u ^ jnp.int32(INT_MIN))
        cnt = count(lambda keys, kpos: keys >= cand_s)
        return jnp.where(cnt >= topk, cand_u, tu)

    tu = lax.fori_loop(0, 32, bit_step, jnp.zeros((tq, LANES), I32))
    thr = jnp.maximum(tu ^ jnp.int32(INT_MIN), jnp.int32(INT_MIN + 1))
    thr_w = wide(thr)

    n_ge = count(lambda keys, kpos: keys >= thr_w)
    n_gt = count(lambda keys, kpos: keys > thr_w)
    excess = n_ge > topk

    @pl.when(jnp.max(jnp.where(excess, 1.0, 0.0)) > 0.0)
    def _():
        need = topk - n_gt

        idx_bits = int(key_ref.shape[1]).bit_length()

        def idx_step(it, jm):
            cand = wide(jm | lax.shift_left(jnp.int32(1), idx_bits - 1 - it))
            cnt = count(lambda keys, kpos: (keys == thr_w) & (kpos < cand))
            return jnp.where(cnt < need, cand[:, :LANES], jm)

        jm = lax.fori_loop(0, idx_bits, idx_step, jnp.zeros((tq, LANES), I32))
        jm_w = wide(jnp.where(excess, jm, jnp.int32(2 ** 31 - 1)))

        def drop(kb, carry):
            koff = pl.multiple_of(kb * tk, tk)
            keys = key_ref[:, pl.ds(koff, tk)]
            kpos = koff + lax.broadcasted_iota(I32, (tq, tk), 1)
            key_ref[:, pl.ds(koff, tk)] = jnp.where((keys == thr_w) & (kpos > jm_w), jnp.int32(INT_MIN), keys)
            return carry

        lax.fori_loop(0, nkb, drop, 0)

    m_ref[...] = jnp.full(m_ref.shape, NEG_BIG, F32)
    l_ref[...] = jnp.zeros(l_ref.shape, F32)
    acc_ref[...] = jnp.zeros(acc_ref.shape, F32)
    rows = A_GROUP * tq

    def attend(kb, carry):
        koff = pl.multiple_of(kb * tk, tk)
        bias = _tile(jnp.where(key_ref[:, pl.ds(koff, tk)] >= thr_w, 0.0, NEG_BIG), A_GROUP, 0)
        for n in range(A_KV_HEADS):
            r0 = n * rows
            kn = k_ref[pl.ds(koff, tk), n * LANES:(n + 1) * LANES]
            vn = v_ref[pl.ds(koff, tk), n * LANES:(n + 1) * LANES]
            s = lax.dot_general(qs_ref[r0:r0 + rows, :], kn, nt_dims, preferred_element_type=F32) + bias
            m_prev = m_ref[r0:r0 + rows, :]
            m_new = jnp.maximum(m_prev, jnp.max(s, axis=-1, keepdims=True))
            alpha = jnp.exp2(m_prev - m_new)
            p = jnp.exp2(s - _tile(m_new, n_rep, 1))
            l_ref[r0:r0 + rows, :] = alpha * l_ref[r0:r0 + rows, :] + jnp.sum(p, axis=-1, keepdims=True)
            acc_ref[r0:r0 + rows, :] = alpha * acc_ref[r0:r0 + rows, :] + jnp.dot(
                p.astype(vn.dtype), vn, preferred_element_type=F32)
            m_ref[r0:r0 + rows, :] = m_new
        return carry

    lax.fori_loop(0, nkb, attend, 0)
    for h in range(A_HEADS):
        o_ref[:, h * LANES:(h + 1) * LANES] = (
            acc_ref[h * tq:(h + 1) * tq, :] / l_ref[h * tq:(h + 1) * tq, :]).astype(o_ref.dtype)


def _dsa(q_bf, iq_bf, ikw, k_all, v_all, ik_all, *, n_batch, t_len, pos0, n_keys):
    lp = k_all.shape[1]
    tq = min(t_len, 128)
    tk = min(lp, 512)
    assert lp % tk == 0 and t_len % tq == 0
    nq = t_len // tq
    topk = min(TOPK_MAX, n_keys // 4)
    qrow = lambda w: pl.BlockSpec((tq, w), lambda b, i: (b * nq + i, 0))
    kv_spec = lambda w: pl.BlockSpec((None, lp, w), lambda b, i: (b, 0, 0), pipeline_mode=pl.Buffered(1))
    return pl.pallas_call(
        functools.partial(_dsa_kernel, tq=tq, tk=tk, pos0=pos0, n_keys=n_keys, topk=float(topk)),
        out_shape=jax.ShapeDtypeStruct((n_batch * t_len, A_WIDTH), BF16),
        grid=(n_batch, nq),
        in_specs=[qrow(A_WIDTH), qrow(IQ_W), qrow(LANES), kv_spec(KV_W), kv_spec(KV_W), kv_spec(IDX_DIM)],
        out_specs=qrow(A_WIDTH),
        scratch_shapes=[
            pltpu.VMEM((tq, lp), I32),
            pltpu.VMEM((IDX_HEADS, tq, LANES), F32),
            pltpu.VMEM((A_HEADS * tq, LANES), BF16),
            pltpu.VMEM((A_HEADS * tq, LANES), F32),
            pltpu.VMEM((A_HEADS * tq, LANES), F32),
            pltpu.VMEM((A_HEADS * tq, LANES), F32),
        ],
        compiler_params=_cparams(("parallel", "arbitrary")),
        name="dsa",
    )(q_bf, iq_bf, ikw, k_all, v_all, ik_all)


def _head_sums(x, ones2):
    n = x.shape[1] // LANES
    tm = x.shape[0]
    hi, mid = _split2(jnp.concatenate([x[:, c * LANES:(c + 1) * LANES] for c in range(n)], axis=0))
    s = jnp.dot(jnp.concatenate([hi, mid], axis=1), ones2, preferred_element_type=F32)
    return jnp.concatenate([s[c * tm:(c + 1) * tm, :] for c in range(n)], axis=1)


def _rwkv_pre_kernel(z_ref, zp_ref, z0_ref, mu_ref, w0_ref, a0_ref, kk_ref, ka_ref, w2_ref, a2_ref, g2_ref,
                     ones_ref, r_out, w_out, k_out, v_out, nkk_out, b_out, g_out, *, tm):
    i = pl.program_id(1)
    row = lax.broadcasted_iota(I32, (tm, 1), 0)

    def mixed(c0, width):
        z = z_ref[:, c0:c0 + width]
        first = jnp.where(i == 0, z0_ref[:, c0:c0 + width], zp_ref[7:8, c0:c0 + width])
        shifted = jnp.where(row == 0, first, pltpu.roll(z, 1, axis=0))
        return z + (shifted - z) * mu_ref[:, c0:c0 + width]

    r = mixed(0, R_WIDTH)
    k = mixed(R_WIDTH, R_WIDTH)
    v = mixed(2 * R_WIDTH, R_WIDTH)
    wd = mixed(RW_WD0, LORA_PAD)
    ad = mixed(RW_AD0, LORA_PAD)
    gd = mixed(RW_GD0, GATE_LORA)
    lora = lambda x, w_ref: jnp.dot(x.astype(BF16), w_ref[...], preferred_element_type=F32)
    t_hi, t_mid = _split2(jnp.tanh(wd))
    y = -(w0_ref[...] + jnp.dot(jnp.concatenate([t_hi, t_hi, t_mid], axis=1), w2_ref[...], preferred_element_type=F32))
    softplus = jnp.maximum(y, 0.0) + jnp.log(1.0 + jnp.exp(-jnp.abs(y)))
    decay = jnp.exp(-jnp.exp(-softplus - 0.5))
    a = jax.nn.sigmoid(a0_ref[...] + lora(ad, a2_ref))
    g = lora(jax.nn.sigmoid(gd), g2_ref)
    kk = k * kk_ref[...]
    kk = kk / jnp.maximum(jnp.sqrt(_head_sums(kk * kk, ones_ref[...])), 1e-12)
    r_out[...] = r
    w_out[...] = decay
    k_out[...] = k * (1.0 + (a - 1.0) * ka_ref[...])
    v_out[...] = v
    nkk_out[...] = -kk
    b_out[...] = kk * a
    g_out[...] = g


def _rwkv_pre(z_rw, z0, mu, w0, a0, k_k, k_a, w2, a2, g2, ones_bd):
    nb, t_len, _ = z_rw.shape
    tm = min(t_len, 128)
    zrow = pl.BlockSpec((None, tm, RW_COLS), lambda b, i: (b, i, 0))
    zprev = pl.BlockSpec((None, 8, RW_COLS), lambda b, i: (b, jnp.maximum(i * (tm // 8) - 1, 0), 0))
    full = lambda a: pl.BlockSpec(a.shape, lambda b, i: (0,) * a.ndim)
    orow = pl.BlockSpec((None, tm, R_WIDTH), lambda b, i: (b, i, 0))
    params = (mu, w0, a0, k_k, k_a, w2, a2, g2, ones_bd)
    return pl.pallas_call(
        functools.partial(_rwkv_pre_kernel, tm=tm),
        out_shape=tuple(jax.ShapeDtypeStruct((nb, t_len, R_WIDTH), F32) for _ in range(7)),
        grid=(nb, t_len // tm),
        in_specs=[zrow, zprev, pl.BlockSpec((None, 1, RW_COLS), lambda b, i: (b, 0, 0))] + [full(p) for p in params],
        out_specs=tuple(orow for _ in range(7)),
        compiler_params=_cparams(("parallel", "arbitrary")),
        name="rwkv_pre",
    )(z_rw, z_rw, z0, *params)


def _split2(x):
    hi = x.astype(BF16)
    return hi, (x - hi.astype(F32)).astype(BF16)


def _rwkv_scan_kernel(r_ref, w_ref, k_ref, nkk_ref, b_ref, v_ref, s0_ref, ones2_ref, hot_ref, spread_ref,
                      y_ref, st_ref, s_ref, xs_ref, vc_ref, vc2_ref, *, tb):
    tblk = pl.program_id(1)
    cat = lambda xs, ax=0: jnp.concatenate(xs, axis=ax)

    @pl.when(tblk == 0)
    def _():
        s_ref[...] = s0_ref[...]

    pad = jnp.zeros((SCAN_BLOCK - tb, LANES), F32)
    v_cols = []
    for p in range(R_PAIRS):
        vp = v_ref[:, p * LANES:(p + 1) * LANES]
        v_cols.append((cat([vp, pad]) if tb < SCAN_BLOCK else vp).T)
    for sub in range(tb // SCAN_SUB):
        for g in range(SCAN_GROUPS):
            x = cat([v_cols[SCAN_GROUP_PAIRS * g + q][h * R_HEAD_DIM:(h + 1) * R_HEAD_DIM,
                                                      sub * SCAN_SUB:(sub + 1) * SCAN_SUB]
                     for q in range(SCAN_GROUP_PAIRS) for h in range(2)], 1)
            hi, mid = _split2(x)
            xs_ref[sub, g * R_HEAD_DIM:(g + 1) * R_HEAD_DIM, :] = cat([hi, mid], 1)
    ones2 = ones2_ref[...]
    spread = spread_ref[...]
    y_ref[...] = jnp.zeros(y_ref.shape, F32)
    lane_t = lax.rem(lax.broadcasted_iota(I32, (R_HEAD_DIM, LANES), 1), R_HEAD_DIM)
    rowp = lambda ref, t, p: ref[t, p:p + 1, :]
    group_pairs = R_PAIRS // SCAN_MATMULS
    half = group_pairs // 2 * R_HEAD_DIM

    groups = [range(g * group_pairs, (g + 1) * group_pairs) for g in range(SCAN_MATMULS)]

    def packed(xs):
        return cat([cat(xs[2 * j:2 * j + 2], 1) for j in range(group_pairs // 2)])

    def pair_tile(res, q):
        return res[(q // 2) * R_HEAD_DIM:(q // 2 + 1) * R_HEAD_DIM, (q % 2) * LANES:(q % 2 + 1) * LANES]

    def head_sums(g, states, t_sa, t_y=None):
        rows = [packed([(s * rowp(nkk_ref, t_sa, p)).astype(BF16) for s, p in zip(states, groups[g])])]
        if t_y is not None:
            rows.append(packed([(s * rowp(r_ref, t_y, p)).astype(BF16) for s, p in zip(states, groups[g])]))
        return jnp.dot(cat(rows), ones2, preferred_element_type=F32)

    def value_columns(sub, i, vc_out):
        xs = xs_ref[sub]
        n = xs.shape[0]
        for j0 in range(0, SCAN_BULK, SCAN_BULK // 2):
            steps = [SCAN_BULK * i + j0 + j for j in range(SCAN_BULK // 2)]
            res = jnp.dot(cat([xs * hot_ref[tt, 0:1, :] for tt in steps]), spread, preferred_element_type=F32)
            for j, tt in enumerate(steps):
                vc_out[tt] = res[j * n:(j + 1) * n]

    def step(sub, tt, vc_in, sas):
        t = sub * SCAN_SUB + tt
        t_next = jnp.minimum(t + 1, tb - 1)
        hit = lane_t == t
        new_sas = []
        for g in range(SCAN_MATMULS):
            states = []
            for q, p in enumerate(groups[g]):
                vg, vq = divmod(p, SCAN_GROUP_PAIRS)
                vc = vc_in[tt, vg * R_HEAD_DIM:(vg + 1) * R_HEAD_DIM, vq * LANES:(vq + 1) * LANES]
                s_new = (s_ref[p] * rowp(w_ref, t, p) + pair_tile(sas[g], q) * rowp(b_ref, t, p)
                         + vc * rowp(k_ref, t, p))
                s_ref[p] = s_new
                states.append(s_new)
            res = head_sums(g, states, t_next, t)
            new_sas.append(res[:half])
            for q, p in enumerate(groups[g]):
                y_ref[p] = jnp.where(hit, pair_tile(res[half:], q), y_ref[p])
        return tuple(new_sas)

    def sub_block(sub, vc_in, vc_out, sas):
        nxt = jnp.minimum(sub + 1, n_sub - 1)

        def four_steps(i, sas):
            sas = step(sub, SCAN_BULK * i, vc_in, sas)
            if n_sub > 1:
                value_columns(nxt, i, vc_out)
            for u in range(1, SCAN_BULK):
                sas = step(sub, SCAN_BULK * i + u, vc_in, sas)
            return sas

        return lax.fori_loop(0, SCAN_SUB // SCAN_BULK, four_steps, sas)

    n_sub = tb // SCAN_SUB
    for i in range(SCAN_SUB // SCAN_BULK):
        value_columns(0, i, vc_ref)
    sas = tuple(head_sums(g, [s_ref[p] for p in groups[g]], 0) for g in range(SCAN_MATMULS))
    if n_sub == 1:
        sub_block(0, vc_ref, vc2_ref, sas)
    else:
        def two_sub_blocks(i, sas):
            return sub_block(2 * i + 1, vc2_ref, vc_ref, sub_block(2 * i, vc_ref, vc2_ref, sas))

        lax.fori_loop(0, n_sub // 2, two_sub_blocks, sas)

    @pl.when(tblk == pl.num_programs(1) - 1)
    def _():
        st_ref[...] = s_ref[...]


def _rwkv_scan(r, w, k, nkk, b, v, s0):
    nb, t_len, _ = r.shape
    tb = min(t_len, SCAN_BLOCK)
    nblk = t_len // tb
    hd = R_HEAD_DIM
    nsub, sb, ng, gp = tb // SCAN_SUB, SCAN_SUB, SCAN_GROUPS, SCAN_GROUP_PAIRS
    s0p = s0.reshape(nb, R_PAIRS, 2, hd, hd).transpose(0, 1, 3, 2, 4).reshape(nb, R_PAIRS, hd, LANES)
    lane2_h = jnp.arange(2 * LANES) // hd
    ones2 = (lane2_h[:, None] == lane2_h[None, :]).astype(BF16)
    src = jnp.arange(2 * LANES) % LANES
    src_q, src_h, src_t = src // (2 * sb), (src // sb) % 2, src % sb
    dst = jnp.arange(gp * LANES)
    dst_q, dst_h = dst // LANES, (dst % LANES) // hd
    spread = ((src_q[:, None] == dst_q[None, :]) & (src_h[:, None] == dst_h[None, :])).astype(BF16)
    hot = jnp.broadcast_to((src_t[None, :] == jnp.arange(sb)[:, None])[:, None, :], (sb, 16, 2 * LANES)).astype(BF16)
    trow = pl.BlockSpec((None, tb, R_PAIRS, LANES), lambda bb, i: (bb, i, 0, 0))
    r, w, k, nkk, b = (a.reshape(nb, t_len, R_PAIRS, LANES) for a in (r, w, k, nkk, b))
    st = pl.BlockSpec((None, R_PAIRS, hd, LANES), lambda bb, i: (bb, 0, 0, 0))
    full = lambda a: pl.BlockSpec(a.shape, lambda bb, i: (0,) * a.ndim)
    ycol_spec = pl.BlockSpec((None, None, R_PAIRS, hd, LANES), lambda bb, i: (bb, i, 0, 0, 0))
    ycol, s_t = pl.pallas_call(
        functools.partial(_rwkv_scan_kernel, tb=tb),
        out_shape=(jax.ShapeDtypeStruct((nb, nblk, R_PAIRS, hd, LANES), F32),
                   jax.ShapeDtypeStruct((nb, R_PAIRS, hd, LANES), F32)),
        grid=(nb, nblk),
        in_specs=[trow, trow, trow, trow, trow, pl.BlockSpec((None, tb, R_WIDTH), lambda bb, i: (bb, i, 0)), st,
                  full(ones2), full(hot), full(spread)],
        out_specs=(ycol_spec, st),
        scratch_shapes=[pltpu.VMEM((R_PAIRS, hd, LANES), F32), pltpu.VMEM((nsub, ng * hd, 2 * LANES), BF16),
                        pltpu.VMEM((sb, ng * hd, gp * LANES), F32), pltpu.VMEM((sb, ng * hd, gp * LANES), F32)],
        compiler_params=_cparams(("parallel", "arbitrary")),
        name="rwkv_scan",
    )(r, w, k, nkk, b, v, s0p, ones2, hot, spread)
    s_t = s_t.reshape(nb, R_PAIRS, hd, 2, hd).transpose(0, 1, 3, 2, 4).reshape(nb, R_HEADS, hd, hd)
    return ycol, s_t


def _rwkv_post_kernel(y_ref, r_ref, k_ref, v_ref, g_ref, lw_ref, lb_ref, rk_ref, ones_ref, o_ref, *, tb):
    ones_bd = ones_ref[...]
    tiles = []
    for p in range(R_PAIRS):
        yt = y_ref[p].T
        tiles.append(jnp.concatenate([yt[0:tb, :], yt[R_HEAD_DIM:R_HEAD_DIM + tb, :]], axis=1))
    y = jnp.concatenate(tiles, axis=1)
    mean = _head_sums(y, ones_bd) * (1.0 / R_HEAD_DIM)
    d = y - mean
    var = _head_sums(d * d, ones_bd) * (1.0 / R_HEAD_DIM)
    yn = d * lax.rsqrt(var + GN_EPS) * lw_ref[...] + lb_ref[...]
    bonus = _head_sums(r_ref[...] * k_ref[...] * rk_ref[...], ones_bd) * v_ref[...]
    o_ref[...] = ((yn + bonus) * g_ref[...]).astype(o_ref.dtype)


def _rwkv_post(ycol, r, k, v, g, lnx_w, lnx_b, r_k, ones_bd):
    m = r.shape[0]
    nblk = ycol.shape[1]
    tb = m // (ycol.shape[0] * nblk)
    row = pl.BlockSpec((tb, R_WIDTH), lambda i: (i, 0))
    full = lambda a: pl.BlockSpec(a.shape, lambda i: (0,) * a.ndim)
    params = (lnx_w, lnx_b, r_k, ones_bd)
    return pl.pallas_call(
        functools.partial(_rwkv_post_kernel, tb=tb),
        out_shape=jax.ShapeDtypeStruct((m, R_WIDTH), BF16),
        grid=(m // tb,),
        in_specs=[pl.BlockSpec((None, None, R_PAIRS, R_HEAD_DIM, LANES), lambda i: (i // nblk, i % nblk, 0, 0, 0))]
        + [row] * 4 + [full(p) for p in params],
        out_specs=row,
        compiler_params=_cparams(("parallel",)),
        name="rwkv_post",
    )(ycol, r, k, v, g, *params)


def _ffn_up_kernel(x_ref, xp_ref, wg_ref, wu_ref, c0_ref, cw_ref, cb_ref, act_ref, tail_ref, *,
                   seq_rows, seqs_per_tile, tiles_per_seq):
    first = lax.rem(pl.program_id(0), tiles_per_seq) == 0
    wg = wg_ref[...].astype(BF16)
    gate_all = jnp.dot(x_ref[...], wg, preferred_element_type=F32)
    up_all = jnp.dot(x_ref[...], wu_ref[...].astype(BF16), preferred_element_type=F32)
    gate_prev = jnp.dot(xp_ref[...], wg, preferred_element_type=F32)
    row = lax.broadcasted_iota(I32, (seq_rows, 1), 0)
    for s in range(seqs_per_tile):
        rows = slice(s * seq_rows, (s + 1) * seq_rows)
        gate = gate_all[rows]
        prev1 = jnp.where(first, c0_ref[s, 1:2, :], gate_prev[7:8, :])
        prev2 = jnp.where(first, c0_ref[s, 0:1, :], gate_prev[6:7, :])
        g_m1 = jnp.where(row == 0, prev1, pltpu.roll(gate, 1, axis=0))
        g_m2 = jnp.where(row == 0, prev2, jnp.where(row == 1, prev1, pltpu.roll(gate, 2, axis=0)))
        conv = cb_ref[...] + g_m2 * cw_ref[0:1, :]
        conv = conv + g_m1 * cw_ref[1:2, :]
        conv = conv + gate * cw_ref[2:3, :]
        act_ref[rows, :] = (conv * jax.nn.sigmoid(conv) * up_all[rows]).astype(act_ref.dtype)
        tail_ref[s] = gate[seq_rows - 8:, :]


def _ffn_up(x, w_in, conv0, conv_w, conv_b, n_batch, t_len, tm=1024, tn=256):
    m, d = x.shape
    d_ff = w_in.shape[1] // 2
    tm = min(m, tm)
    seq_rows = min(t_len, tm)
    seqs_per_tile, tiles_per_seq = tm // seq_rows, t_len // seq_rows
    assert m % tm == 0 and tm % seq_rows == 0 and t_len % seq_rows == 0 and seq_rows % 8 == 0 and d_ff % tn == 0
    n_up = d_ff // tn
    act, tail = pl.pallas_call(
        functools.partial(_ffn_up_kernel, seq_rows=seq_rows, seqs_per_tile=seqs_per_tile, tiles_per_seq=tiles_per_seq),
        out_shape=(jax.ShapeDtypeStruct((m, d_ff), BF16),
                   jax.ShapeDtypeStruct((n_batch * tiles_per_seq, 8, d_ff), F32)),
        grid=(m // tm, n_up),
        in_specs=[pl.BlockSpec((tm, d), lambda i, j: (i, 0)),
                  pl.BlockSpec((8, d), lambda i, j: (jnp.maximum(i * (tm // 8) - 1, 0), 0)),
                  pl.BlockSpec((d, tn), lambda i, j: (0, j)),
                  pl.BlockSpec((d, tn), lambda i, j: (0, j + n_up)),
                  pl.BlockSpec((seqs_per_tile, CONV_W - 1, tn), lambda i, j: (i // tiles_per_seq, 0, j)),
                  pl.BlockSpec((CONV_W, tn), lambda i, j: (0, j)),
                  pl.BlockSpec((1, tn), lambda i, j: (0, j))],
        out_specs=(pl.BlockSpec((tm, tn), lambda i, j: (i, j)),
                   pl.BlockSpec((seqs_per_tile, 8, tn), lambda i, j: (i, 0, j))),
        compiler_params=_cparams(("parallel", "arbitrary")),
        name="ffn_up",
    )(x, x, w_in, w_in, conv0, conv_w, conv_b.reshape(1, d_ff))
    gate_tail = tail.reshape(n_batch, tiles_per_seq, 8, d_ff)[:, -1, 8 - (CONV_W - 1):, :]
    return act, gate_tail


def _pad_rw_cols(a):
    z = lambda n: jnp.zeros(a.shape[:-1] + (n,), a.dtype)
    wd0, ad0, gd0 = 3 * R_WIDTH, 3 * R_WIDTH + DECAY_LORA, 3 * R_WIDTH + DECAY_LORA + AAA_LORA
    return jnp.concatenate([a[..., :wd0], a[..., wd0:ad0], z(LORA_PAD - DECAY_LORA), a[..., ad0:gd0],
                            z(LORA_PAD - AAA_LORA), a[..., gd0:]], axis=-1)


def _unpad_rw_cols(a):
    return jnp.concatenate([a[..., :RW_WD0 + DECAY_LORA], a[..., RW_AD0:RW_AD0 + AAA_LORA], a[..., RW_GD0:]], axis=-1)


def _prep_weights(norm_mix_g, w_in, rwkv_mu, rwkv_w0, rwkv_w2, rwkv_a0, rwkv_a2, rwkv_g2, rwkv_k_k, rwkv_k_a,
                  rwkv_r_k, rwkv_lnx_w, rwkv_lnx_b, w_out, norm_ffn_g, ffn_w_in, ffn_conv_w, ffn_conv_b,
                  ffn_w_down, norm_final_g, l):
    d = w_in.shape[1]
    w_att = jnp.concatenate([w_in[l][:, :ATT_USED].astype(BF16), jnp.zeros((d, ATT_COLS - ATT_USED), BF16)], axis=1)
    w_rw = _pad_rw_cols(w_in[l][:, ATT_USED:].astype(BF16))
    row = lambda a: a.reshape(1, -1).astype(F32)
    pad_rows = lambda a, n: jnp.concatenate([a, jnp.zeros((n - a.shape[0], a.shape[1]), a.dtype)], axis=0)
    lane_h = jnp.arange(LANES) // R_HEAD_DIM
    w2_pad = pad_rows(rwkv_w2[l].astype(F32), LORA_PAD)
    w2_hi = w2_pad.astype(BF16)
    w2_mid = (w2_pad - w2_hi.astype(F32)).astype(BF16)
    return dict(
        norm_mix_g=norm_mix_g[l], w_att=w_att, w_rw=w_rw,
        mu=_pad_rw_cols(row(rwkv_mu[l])), w0=row(rwkv_w0[l]), a0=row(rwkv_a0[l]),
        k_k=row(rwkv_k_k[l]), k_a=row(rwkv_k_a[l]),
        w2=jnp.concatenate([w2_hi, w2_mid, w2_hi], axis=0), a2=pad_rows(rwkv_a2[l], LORA_PAD).astype(BF16),
        g2=rwkv_g2[l].astype(BF16),
        r_k=row(rwkv_r_k[l]), lnx_w=row(rwkv_lnx_w[l]), lnx_b=row(rwkv_lnx_b[l]),
        ones_bd=jnp.concatenate([lane_h[:, None] == lane_h[None, :]] * 2, axis=0).astype(BF16),
        w_out_a=w_out[l][:A_WIDTH].astype(BF16), w_out_r=w_out[l][A_WIDTH:].astype(BF16),
        norm_ffn_g=norm_ffn_g[l], ffn_w_in=ffn_w_in[l], conv_w=ffn_conv_w[l], conv_b=ffn_conv_b[l],
        ffn_w_down=ffn_w_down[l].astype(BF16), norm_final_g=norm_final_g,
    )


def _trunk(x, past_k, past_v, past_ik, s0, shift0, conv0, wt):
    nb, t_len, d = x.shape
    m = nb * t_len
    p_len = 0 if past_k is None else past_k.shape[1]
    n_keys = p_len + t_len
    x2 = x.reshape(m, d)

    h = _rmsnorm(x2, wt["norm_mix_g"], BF16)
    z_att = _matmul([h], [wt["w_att"]], tm=1024, name="proj_att")
    z_rw = _matmul([h], [wt["w_rw"]], tm=1024, name="proj_rw")

    q_bf, k_f, k_bf, v_f, v_bf, iq_bf, ikw, ik_bf = _rope_split(z_att, t_len, p_len)
    tk = min(-(-n_keys // LANES) * LANES, 512)
    lp = -(-n_keys // tk) * tk

    def with_past(new, past, width):
        new = new.reshape(nb, t_len, width)
        parts = [new] if past is None else [past.reshape(nb, p_len, width).astype(BF16), new]
        if lp > n_keys:
            parts.append(jnp.zeros((nb, lp - n_keys, width), BF16))
        return parts[0] if len(parts) == 1 else jnp.concatenate(parts, axis=1)

    attn = _dsa(q_bf, iq_bf, ikw, with_past(k_bf, past_k, KV_W), with_past(v_bf, past_v, KV_W),
                with_past(ik_bf, past_ik, IDX_DIM), n_batch=nb, t_len=t_len, pos0=p_len, n_keys=n_keys)

    z_rw3 = z_rw.reshape(nb, t_len, RW_COLS)
    r, w, k2, v2, nkk, b, g = _rwkv_pre(z_rw3, _pad_rw_cols(shift0.astype(F32)), wt["mu"], wt["w0"], wt["a0"],
                                       wt["k_k"], wt["k_a"], wt["w2"], wt["a2"], wt["g2"], wt["ones_bd"])
    ycol, s_t = _rwkv_scan(r, w, k2, nkk, b, v2, s0.astype(F32))
    flat = lambda a: a.reshape(m, R_WIDTH)
    rw = _rwkv_post(ycol, flat(r), flat(k2), flat(v2), flat(g), wt["lnx_w"], wt["lnx_b"], wt["r_k"], wt["ones_bd"])

    x1 = _matmul([attn, rw], [wt["w_out_a"], wt["w_out_r"]], res=x2, tm=1024, name="out_proj")
    hf = _rmsnorm(x1, wt["norm_ffn_g"], BF16)
    act, conv_t = _ffn_up(hf, wt["ffn_w_in"], conv0.astype(F32), wt["conv_w"], wt["conv_b"], nb, t_len)
    x3 = _matmul([act], [wt["ffn_w_down"]], res=x1, name="ffn_down")
    y_out = _rmsnorm(x3, wt["norm_final_g"], F32).reshape(nb, t_len, d)

    shift_t = _unpad_rw_cols(z_rw3[:, -1:])
    caches = (k_f.reshape(nb, t_len, A_KV_HEADS, A_HEAD_DIM)[None], v_f.reshape(nb, t_len, A_KV_HEADS, A_HEAD_DIM)[None],
              ikw[:, :IDX_DIM].reshape(nb, t_len, IDX_DIM)[None], s_t[None], shift_t[None], conv_t[None])
    return y_out, caches


def kernel(x_prompt, x_sample, cache_k, cache_v, cache_idx_k, state_rwkv, state_rwkv_shift, state_ffn_conv, norm_mix_g, w_in, rwkv_mu, rwkv_w0, rwkv_w2, rwkv_a0, rwkv_a2, rwkv_g2, rwkv_k_k, rwkv_k_a, rwkv_r_k, rwkv_lnx_w, rwkv_lnx_b, w_out, norm_ffn_g, ffn_w_in, ffn_conv_w, ffn_conv_b, ffn_w_down, norm_final_g):
    assert w_in.shape[0] == 1, "single-layer trunk"
    wt = _prep_weights(norm_mix_g, w_in, rwkv_mu, rwkv_w0, rwkv_w2, rwkv_a0, rwkv_a2, rwkv_g2, rwkv_k_k, rwkv_k_a,
                       rwkv_r_k, rwkv_lnx_w, rwkv_lnx_b, w_out, norm_ffn_g, ffn_w_in, ffn_conv_w, ffn_conv_b,
                       ffn_w_down, norm_final_g, 0)
    bp = x_prompt.shape[0]
    d_ff = ffn_conv_w.shape[-1]
    y_p, c_p = _trunk(x_prompt, None, None, None,
                      jnp.zeros((bp, R_HEADS, R_HEAD_DIM, R_HEAD_DIM), F32), jnp.zeros((bp, 1, RWKV_COLS), F32),
                      jnp.zeros((bp, CONV_W - 1, d_ff), F32), wt)
    y_s, c_s = _trunk(x_sample, cache_k[0], cache_v[0], cache_idx_k[0], state_rwkv[0], state_rwkv_shift[0],
                      state_ffn_conv[0], wt)
    return (y_p, y_s) + c_p + c_s
```

```python
import functools

import jax
import jax.numpy as jnp
from jax import lax
from jax.experimental import pallas as pl
from jax.experimental.pallas import tpu as pltpu

F32 = jnp.float32
BF16 = jnp.bfloat16
I32 = jnp.int32

CHUNK = 64
A_HEADS = 16
A_KV_HEADS = 4
A_HEAD_DIM = 128
A_GROUP = A_HEADS // A_KV_HEADS
A_WIDTH = A_HEADS * A_HEAD_DIM
KV_W = A_KV_HEADS * A_HEAD_DIM
IDX_HEADS = 16
IDX_DIM = 64
IQ_W = IDX_HEADS * IDX_DIM
TOPK_MAX = 256
ROPE_THETA = 500000.0
ROPE_FRAC = 4
A_SCALE = A_HEAD_DIM ** -0.5
Q_SCALE = A_SCALE * 1.4426950408889634
IDX_SCALE = (IDX_HEADS ** -0.5) * (IDX_DIM ** -0.5)
R_HEAD_DIM = 64
R_WIDTH = 2048
R_HEADS = R_WIDTH // R_HEAD_DIM
R_PAIRS = R_HEADS // 2
DECAY_LORA = 96
AAA_LORA = 96
GATE_LORA = 256
RWKV_COLS = 3 * R_WIDTH + DECAY_LORA + AAA_LORA + GATE_LORA
GN_EPS = 6.4e-4
CONV_W = 3
RMS_EPS = 1e-6

LANES = 128
VMEM_LIMIT = 56 * 1024 * 1024

ATT_Q0, ATT_K0, ATT_V0, ATT_IQ0, ATT_IK0 = 0, A_WIDTH, A_WIDTH + KV_W, A_WIDTH + 2 * KV_W, A_WIDTH + 2 * KV_W + IQ_W
ATT_USED = ATT_IK0 + IDX_DIM + IDX_HEADS
ATT_COLS = 4608
LORA_PAD = 128
RW_WD0 = 3 * R_WIDTH
RW_AD0 = RW_WD0 + LORA_PAD
RW_GD0 = RW_AD0 + LORA_PAD
RW_COLS = RW_GD0 + GATE_LORA
COUNT_ROWS = 128
SCAN_BLOCK = 64
SCAN_SUB = 16
SCAN_GROUP_PAIRS = LANES // (2 * SCAN_SUB)
SCAN_GROUPS = R_PAIRS // SCAN_GROUP_PAIRS
SCAN_MATMULS = 2
SCAN_BULK = 4
INT_MIN = -2 ** 31
NEG_BIG = -1e30


def _cparams(sem):
    return pltpu.CompilerParams(dimension_semantics=sem, vmem_limit_bytes=VMEM_LIMIT)


def _rmsnorm_kernel(x_ref, g_ref, o_ref):
    x = x_ref[...]
    y = x * lax.rsqrt(jnp.mean(x * x, axis=-1, keepdims=True) + RMS_EPS)
    o_ref[...] = (y * g_ref[...]).astype(o_ref.dtype)


def _rmsnorm(x, g, out_dtype):
    m, d = x.shape
    tm = min(m, 256)
    return pl.pallas_call(
        _rmsnorm_kernel,
        out_shape=jax.ShapeDtypeStruct((m, d), out_dtype),
        grid=(m // tm,),
        in_specs=[pl.BlockSpec((tm, d), lambda i: (i, 0)), pl.BlockSpec((1, d), lambda i: (0, 0))],
        out_specs=pl.BlockSpec((tm, d), lambda i: (i, 0)),
        compiler_params=_cparams(("parallel",)),
        name="rmsnorm",
    )(x, g.reshape(1, d).astype(F32))


def _mm_kernel(*refs, n_pairs, has_res):
    o_ref = refs[-1]
    acc = jnp.dot(refs[0][...], refs[n_pairs][...], preferred_element_type=F32)
    for p in range(1, n_pairs):
        acc = acc + jnp.dot(refs[p][...], refs[n_pairs + p][...], preferred_element_type=F32)
    if has_res:
        acc = refs[2 * n_pairs][...] + acc
    o_ref[...] = acc.astype(o_ref.dtype)


def _matmul(a_list, b_list, res=None, tm=512, tn=512, name="matmul"):
    m = a_list[0].shape[0]
    n = b_list[0].shape[1]
    tm = min(tm, m)
    tn = min(tn, n)
    assert m % tm == 0 and n % tn == 0, (m, n, tm, tn)
    in_specs = [pl.BlockSpec((tm, a.shape[1]), lambda i, j: (i, 0)) for a in a_list]
    in_specs += [pl.BlockSpec((b.shape[0], tn), lambda i, j: (0, j)) for b in b_list]
    args = list(a_list) + list(b_list)
    if res is not None:
        in_specs.append(pl.BlockSpec((tm, tn), lambda i, j: (i, j)))
        args.append(res)
    return pl.pallas_call(
        functools.partial(_mm_kernel, n_pairs=len(a_list), has_res=res is not None),
        out_shape=jax.ShapeDtypeStruct((m, n), F32),
        grid=(m // tm, n // tn),
        in_specs=in_specs,
        out_specs=pl.BlockSpec((tm, tn), lambda i, j: (i, j)),
        compiler_params=_cparams(("parallel", "arbitrary")),
        name=name,
    )(*args)


def _rope_tile(x, cos, sin, half, d_in_head):
    lo = d_in_head < half
    hi = (d_in_head >= half) & (d_in_head < 2 * half)
    c = jnp.where(lo | hi, cos, 1.0)
    s_up = jnp.where(lo, -sin, 0.0)
    s_dn = jnp.where(hi, sin, 0.0)
    x_up = pltpu.roll(x, LANES - half, axis=1)
    x_dn = pltpu.roll(x, half, axis=1)
    return x * c + x_up * s_up + x_dn * s_dn


def _rope_kernel(z_ref, invf_ref, q_ref, kf_ref, kb_ref, vf_ref, vb_ref, iq_ref, ikw_ref, ikb_ref, *, tm, t_len, pos0):
    i = pl.program_id(0)
    row = lax.broadcasted_iota(I32, (tm, LANES), 0) + i * tm
    pos = (pos0 + lax.rem(row, t_len)).astype(F32)
    lane = lax.broadcasted_iota(I32, (tm, LANES), 1)
    ang = pos * invf_ref[0:1, :]
    cos_a, sin_a = jnp.cos(ang), jnp.sin(ang)
    half_a = A_HEAD_DIM // ROPE_FRAC // 2
    for h in range(A_HEADS):
        x = z_ref[:, ATT_Q0 + h * LANES:ATT_Q0 + (h + 1) * LANES]
        q_ref[:, h * LANES:(h + 1) * LANES] = (_rope_tile(x, cos_a, sin_a, half_a, lane) * Q_SCALE).astype(q_ref.dtype)
    for h in range(A_KV_HEADS):
        x = z_ref[:, ATT_K0 + h * LANES:ATT_K0 + (h + 1) * LANES]
        y = _rope_tile(x, cos_a, sin_a, half_a, lane)
        kf_ref[:, h * LANES:(h + 1) * LANES] = y
        kb_ref[:, h * LANES:(h + 1) * LANES] = y.astype(kb_ref.dtype)
    v = z_ref[:, ATT_V0:ATT_V0 + KV_W]
    vf_ref[...] = v
    vb_ref[...] = v.astype(vb_ref.dtype)
    ang = pos * invf_ref[1:2, :]
    cos_i, sin_i = jnp.cos(ang), jnp.sin(ang)
    half_i = IDX_DIM // ROPE_FRAC // 2
    d_i = lane & (IDX_DIM - 1)
    for h in range(IQ_W // LANES):
        x = z_ref[:, ATT_IQ0 + h * LANES:ATT_IQ0 + (h + 1) * LANES]
        iq_ref[:, h * LANES:(h + 1) * LANES] = _rope_tile(x, cos_i, sin_i, half_i, d_i).astype(iq_ref.dtype)
    x = z_ref[:, ATT_IK0:ATT_IK0 + LANES]
    d_k = jnp.where(lane < IDX_DIM, lane, IDX_DIM)
    y = _rope_tile(x, cos_i, sin_i, half_i, d_k)
    ikw_ref[...] = y
    ikb_ref[...] = y[:, :IDX_DIM].astype(ikb_ref.dtype)


def _rope_split(z_att, t_len, pos0):
    m = z_att.shape[0]
    tm = min(m, 256)
    lane = jnp.arange(LANES)
    rd_a = A_HEAD_DIM // ROPE_FRAC
    rd_i = IDX_DIM // ROPE_FRAC
    invf_a = ROPE_THETA ** (-((lane % (rd_a // 2)).astype(F32) * 2.0 / rd_a))
    invf_i = ROPE_THETA ** (-((lane % (rd_i // 2)).astype(F32) * 2.0 / rd_i))
    invf = jnp.zeros((8, LANES), F32).at[0].set(invf_a).at[1].set(invf_i)
    row_spec = lambda w: pl.BlockSpec((tm, w), lambda i: (i, 0))
    shp = lambda w, dt: jax.ShapeDtypeStruct((m, w), dt)
    return pl.pallas_call(
        functools.partial(_rope_kernel, tm=tm, t_len=t_len, pos0=pos0),
        out_shape=(shp(A_WIDTH, BF16), shp(KV_W, F32), shp(KV_W, BF16), shp(KV_W, F32), shp(KV_W, BF16),
                   shp(IQ_W, BF16), shp(LANES, F32), shp(IDX_DIM, BF16)),
        grid=(m // tm,),
        in_specs=[row_spec(ATT_COLS), pl.BlockSpec((8, LANES), lambda i: (0, 0))],
        out_specs=(row_spec(A_WIDTH), row_spec(KV_W), row_spec(KV_W), row_spec(KV_W), row_spec(KV_W),
                   row_spec(IQ_W), row_spec(LANES), row_spec(IDX_DIM)),
        compiler_params=_cparams(("parallel",)),
        name="rope_split",
    )(z_att, invf)


def _tile(x, n, axis):
    return x if n == 1 else jnp.concatenate([x] * n, axis=axis)


def _sortable(score):
    u = lax.bitcast_convert_type(score, I32)
    return jnp.where(u < 0, u ^ jnp.int32(0x7FFFFFFF), u)


def _dsa_kernel(q_ref, iq_ref, ikw_ref, k_ref, v_ref, ik_ref, o_ref,
                key_ref, iwb_ref, qs_ref, m_ref, l_ref, acc_ref, *, tq, tk, pos0, n_keys, topk):
    i = pl.program_id(1)
    q0 = pos0 + i * tq
    kmax = jnp.minimum((lax.div(q0 + tq - 1, CHUNK) + 1) * CHUNK, n_keys)
    nkb = lax.div(kmax + tk - 1, tk)
    n_rep = tk // LANES
    nt_dims = (((1,), (1,)), ((), ()))

    for h in range(IDX_HEADS):
        iwb_ref[h] = jnp.broadcast_to(ikw_ref[:, IDX_DIM + h:IDX_DIM + h + 1], (tq, LANES))
    for h in range(A_HEADS):
        qs_ref[h * tq:(h + 1) * tq, :] = q_ref[:, h * LANES:(h + 1) * LANES]

    def score_block(kb, carry):
        koff = pl.multiple_of(kb * tk, tk)
        ikb = ik_ref[pl.ds(koff, tk), :]
        acc = jnp.zeros((tq, tk), F32)
        for h in range(IDX_HEADS):
            d = lax.dot_general(iq_ref[:, h * IDX_DIM:(h + 1) * IDX_DIM], ikb, nt_dims, preferred_element_type=F32)
            acc = acc + jnp.maximum(d, 0.0) * _tile(iwb_ref[h], n_rep, 1)
        kpos = koff + lax.broadcasted_iota(I32, (tq, tk), 1)
        qpos = q0 + lax.broadcasted_iota(I32, (tq, tk), 0)
        adm = (lax.shift_right_logical(kpos, 6) <= lax.shift_right_logical(qpos, 6)) & (kpos < n_keys)
        key_ref[:, pl.ds(koff, tk)] = jnp.where(adm, _sortable(acc * IDX_SCALE), jnp.int32(INT_MIN))
        return carry

    lax.fori_loop(0, nkb, score_block, 0)

    def count(pred, *row_args):
        cr = min(tq, COUNT_ROWS)

        def chunk(r0):
            def body(kb, cnt):
                koff = pl.multiple_of(kb * tk, tk)
                keys = key_ref[r0:r0 + cr, pl.ds(koff, tk)]
                kpos = koff + lax.broadcasted_iota(I32, (cr, tk), 1)
                hit = jnp.where(pred(keys, kpos, *[a[r0:r0 + cr] for a in row_args]), 1.0, 0.0)
                for c in range(n_rep):
                    cnt = cnt + hit[:, c * LANES:(c + 1) * LANES]
                return cnt
            cnt = lax.fori_loop(0, nkb, body, jnp.zeros((cr, LANES), F32))
            return jnp.broadcast_to(jnp.sum(cnt, axis=-1, keepdims=True), (cr, LANES))

        return jnp.concatenate([chunk(r0) for r0 in range(0, tq, cr)], axis=0)

    def wide(x):
        return _tile(x, n_rep, 1)

    def bit_step(it, tu):
        cand_u = tu | lax.shift_left(jnp.int32(1), 31 - it)
        cand_s = wide(cand_u ^ jnp.int32(INT_MIN))
        cnt = count(lambda keys, kpos, c: keys >= c, cand_s)
        return jnp.where(cnt >= topk, cand_u, tu)

    tu = lax.fori_loop(0, 32, bit_step, jnp.zeros((tq, LANES), I32))
    thr = jnp.maximum(tu ^ jnp.int32(INT_MIN), jnp.int32(INT_MIN + 1))
    thr_w = wide(thr)

    n_ge = count(lambda keys, kpos, t: keys >= t, thr_w)
    n_gt = count(lambda keys, kpos, t: keys > t, thr_w)
    excess = n_ge > topk

    @pl.when(jnp.max(jnp.where(excess, 1.0, 0.0)) > 0.0)
    def _():
        need = topk - n_gt

        idx_bits = int(key_ref.shape[1]).bit_length()

        def idx_step(it, jm):
            cand = wide(jm | lax.shift_left(jnp.int32(1), idx_bits - 1 - it))
            cnt = count(lambda keys, kpos, t, c: (keys == t) & (kpos < c), thr_w, cand)
            return jnp.where(cnt < need, cand[:, :LANES], jm)

        jm = lax.fori_loop(0, idx_bits, idx_step, jnp.zeros((tq, LANES), I32))
        jm_w = wide(jnp.where(excess, jm, jnp.int32(2 ** 31 - 1)))

        def drop(kb, carry):
            koff = pl.multiple_of(kb * tk, tk)
            keys = key_ref[:, pl.ds(koff, tk)]
            kpos = koff + lax.broadcasted_iota(I32, (tq, tk), 1)
            key_ref[:, pl.ds(koff, tk)] = jnp.where((keys == thr_w) & (kpos > jm_w), jnp.int32(INT_MIN), keys)
            return carry

        lax.fori_loop(0, nkb, drop, 0)

    m_ref[...] = jnp.full(m_ref.shape, NEG_BIG, F32)
    l_ref[...] = jnp.zeros(l_ref.shape, F32)
    acc_ref[...] = jnp.zeros(acc_ref.shape, F32)
    rows = A_GROUP * tq

    def attend(kb, carry):
        koff = pl.multiple_of(kb * tk, tk)
        bias = _tile(jnp.where(key_ref[:, pl.ds(koff, tk)] >= thr_w, 0.0, NEG_BIG), A_GROUP, 0)
        for n in range(A_KV_HEADS):
            r0 = n * rows
            kn = k_ref[pl.ds(koff, tk), n * LANES:(n + 1) * LANES]
            vn = v_ref[pl.ds(koff, tk), n * LANES:(n + 1) * LANES]
            s = lax.dot_general(qs_ref[r0:r0 + rows, :], kn, nt_dims, preferred_element_type=F32) + bias
            m_prev = m_ref[r0:r0 + rows, :]
            m_new = jnp.maximum(m_prev, jnp.max(s, axis=-1, keepdims=True))
            alpha = jnp.exp2(m_prev - m_new)
            p = jnp.exp2(s - _tile(m_new, n_rep, 1))
            l_ref[r0:r0 + rows, :] = alpha * l_ref[r0:r0 + rows, :] + jnp.sum(p, axis=-1, keepdims=True)
            acc_ref[r0:r0 + rows, :] = alpha * acc_ref[r0:r0 + rows, :] + jnp.dot(
                p.astype(vn.dtype), vn, preferred_element_type=F32)
            m_ref[r0:r0 + rows, :] = m_new
        return carry

    lax.fori_loop(0, nkb, attend, 0)
    for h in range(A_HEADS):
        o_ref[:, h * LANES:(h + 1) * LANES] = (
            acc_ref[h * tq:(h + 1) * tq, :] / l_ref[h * tq:(h + 1) * tq, :]).astype(o_ref.dtype)


def _dsa(q_bf, iq_bf, ikw, k_all, v_all, ik_all, *, n_batch, t_len, pos0, n_keys):
    lp = k_all.shape[1]
    tq = min(t_len, 256)
    tk = min(lp, 512)
    assert lp % tk == 0 and t_len % tq == 0
    nq = t_len // tq
    topk = min(TOPK_MAX, n_keys // 4)
    qrow = lambda w: pl.BlockSpec((tq, w), lambda b, i: (b * nq + i, 0))
    kv_spec = lambda w: pl.BlockSpec((None, lp, w), lambda b, i: (b, 0, 0), pipeline_mode=pl.Buffered(1))
    return pl.pallas_call(
        functools.partial(_dsa_kernel, tq=tq, tk=tk, pos0=pos0, n_keys=n_keys, topk=float(topk)),
        out_shape=jax.ShapeDtypeStruct((n_batch * t_len, A_WIDTH), BF16),
        grid=(n_batch, nq),
        in_specs=[qrow(A_WIDTH), qrow(IQ_W), qrow(LANES), kv_spec(KV_W), kv_spec(KV_W), kv_spec(IDX_DIM)],
        out_specs=qrow(A_WIDTH),
        scratch_shapes=[
            pltpu.VMEM((tq, lp), I32),
            pltpu.VMEM((IDX_HEADS, tq, LANES), F32),
            pltpu.VMEM((A_HEADS * tq, LANES), BF16),
            pltpu.VMEM((A_HEADS * tq, LANES), F32),
            pltpu.VMEM((A_HEADS * tq, LANES), F32),
            pltpu.VMEM((A_HEADS * tq, LANES), F32),
        ],
        compiler_params=_cparams(("parallel", "arbitrary")),
        name="dsa",
    )(q_bf, iq_bf, ikw, k_all, v_all, ik_all)


def _head_sums(x, ones2):
    n = x.shape[1] // LANES
    tm = x.shape[0]
    hi, mid = _split2(jnp.concatenate([x[:, c * LANES:(c + 1) * LANES] for c in range(n)], axis=0))
    s = jnp.dot(jnp.concatenate([hi, mid], axis=1), ones2, preferred_element_type=F32)
    return jnp.concatenate([s[c * tm:(c + 1) * tm, :] for c in range(n)], axis=1)


def _rwkv_pre_kernel(z_ref, zp_ref, z0_ref, mu_ref, w0_ref, a0_ref, kk_ref, ka_ref, w2_ref, a2_ref, g2_ref,
                     ones_ref, r_out, w_out, k_out, v_out, nkk_out, b_out, g_out, *, tm):
    i = pl.program_id(1)
    row = lax.broadcasted_iota(I32, (tm, 1), 0)

    def mixed(c0, width):
        z = z_ref[:, c0:c0 + width]
        first = jnp.where(i == 0, z0_ref[:, c0:c0 + width], zp_ref[7:8, c0:c0 + width])
        shifted = jnp.where(row == 0, first, pltpu.roll(z, 1, axis=0))
        return z + (shifted - z) * mu_ref[:, c0:c0 + width]

    r = mixed(0, R_WIDTH)
    k = mixed(R_WIDTH, R_WIDTH)
    v = mixed(2 * R_WIDTH, R_WIDTH)
    wd = mixed(RW_WD0, LORA_PAD)
    ad = mixed(RW_AD0, LORA_PAD)
    gd = mixed(RW_GD0, GATE_LORA)
    lora = lambda x, w_ref: jnp.dot(x.astype(BF16), w_ref[...], preferred_element_type=F32)
    t_hi, t_mid = _split2(jnp.tanh(wd))
    y = -(w0_ref[...] + jnp.dot(jnp.concatenate([t_hi, t_hi, t_mid], axis=1), w2_ref[...], preferred_element_type=F32))
    softplus = jnp.maximum(y, 0.0) + jnp.log(1.0 + jnp.exp(-jnp.abs(y)))
    decay = jnp.exp(-jnp.exp(-softplus - 0.5))
    a = jax.nn.sigmoid(a0_ref[...] + lora(ad, a2_ref))
    g = lora(jax.nn.sigmoid(gd), g2_ref)
    kk = k * kk_ref[...]
    kk = kk / jnp.maximum(jnp.sqrt(_head_sums(kk * kk, ones_ref[...])), 1e-12)
    r_out[...] = r
    w_out[...] = decay
    k_out[...] = k * (1.0 + (a - 1.0) * ka_ref[...])
    v_out[...] = v
    nkk_out[...] = -kk
    b_out[...] = kk * a
    g_out[...] = g


def _rwkv_pre(z_rw, z0, mu, w0, a0, k_k, k_a, w2, a2, g2, ones_bd):
    nb, t_len, _ = z_rw.shape
    tm = min(t_len, 128)
    zrow = pl.BlockSpec((None, tm, RW_COLS), lambda b, i: (b, i, 0))
    zprev = pl.BlockSpec((None, 8, RW_COLS), lambda b, i: (b, jnp.maximum(i * (tm // 8) - 1, 0), 0))
    full = lambda a: pl.BlockSpec(a.shape, lambda b, i: (0,) * a.ndim)
    orow = pl.BlockSpec((None, tm, R_WIDTH), lambda b, i: (b, i, 0))
    params = (mu, w0, a0, k_k, k_a, w2, a2, g2, ones_bd)
    return pl.pallas_call(
        functools.partial(_rwkv_pre_kernel, tm=tm),
        out_shape=tuple(jax.ShapeDtypeStruct((nb, t_len, R_WIDTH), F32) for _ in range(7)),
        grid=(nb, t_len // tm),
        in_specs=[zrow, zprev, pl.BlockSpec((None, 1, RW_COLS), lambda b, i: (b, 0, 0))] + [full(p) for p in params],
        out_specs=tuple(orow for _ in range(7)),
        compiler_params=_cparams(("parallel", "arbitrary")),
        name="rwkv_pre",
    )(z_rw, z_rw, z0, *params)


def _split2(x):
    hi = x.astype(BF16)
    return hi, (x - hi.astype(F32)).astype(BF16)


def _rwkv_scan_kernel(r_ref, w_ref, k_ref, nkk_ref, b_ref, v_ref, s0_ref, ones2_ref, hot_ref, spread_ref,
                      y_ref, st_ref, s_ref, xs_ref, vc_ref, vc2_ref, *, tb):
    tblk = pl.program_id(1)
    cat = lambda xs, ax=0: jnp.concatenate(xs, axis=ax)

    @pl.when(tblk == 0)
    def _():
        s_ref[...] = s0_ref[...]

    pad = jnp.zeros((SCAN_BLOCK - tb, LANES), F32)
    v_cols = []
    for p in range(R_PAIRS):
        vp = v_ref[:, p * LANES:(p + 1) * LANES]
        v_cols.append((cat([vp, pad]) if tb < SCAN_BLOCK else vp).T)
    for sub in range(tb // SCAN_SUB):
        for g in range(SCAN_GROUPS):
            x = cat([v_cols[SCAN_GROUP_PAIRS * g + q][h * R_HEAD_DIM:(h + 1) * R_HEAD_DIM,
                                                      sub * SCAN_SUB:(sub + 1) * SCAN_SUB]
                     for q in range(SCAN_GROUP_PAIRS) for h in range(2)], 1)
            hi, mid = _split2(x)
            xs_ref[sub, g * R_HEAD_DIM:(g + 1) * R_HEAD_DIM, :] = cat([hi, mid], 1)
    ones2 = ones2_ref[...]
    spread = spread_ref[...]
    y_ref[...] = jnp.zeros(y_ref.shape, F32)
    lane_t = lax.rem(lax.broadcasted_iota(I32, (R_HEAD_DIM, LANES), 1), R_HEAD_DIM)
    rowp = lambda ref, t, p: ref[t, p:p + 1, :]
    group_pairs = R_PAIRS // SCAN_MATMULS
    half = group_pairs // 2 * R_HEAD_DIM

    groups = [range(g * group_pairs, (g + 1) * group_pairs) for g in range(SCAN_MATMULS)]

    def packed(xs):
        return cat([cat(xs[2 * j:2 * j + 2], 1) for j in range(group_pairs // 2)])

    def pair_tile(res, q):
        return res[(q // 2) * R_HEAD_DIM:(q // 2 + 1) * R_HEAD_DIM, (q % 2) * LANES:(q % 2 + 1) * LANES]

    def head_sums(g, states, t_sa, t_y=None):
        rows = [packed([(s * rowp(nkk_ref, t_sa, p)).astype(BF16) for s, p in zip(states, groups[g])])]
        if t_y is not None:
            rows.append(packed([(s * rowp(r_ref, t_y, p)).astype(BF16) for s, p in zip(states, groups[g])]))
        return jnp.dot(cat(rows), ones2, preferred_element_type=F32)

    def value_columns(sub, i, vc_out):
        xs = xs_ref[sub]
        n = xs.shape[0]
        for j0 in range(0, SCAN_BULK, SCAN_BULK // 2):
            steps = [SCAN_BULK * i + j0 + j for j in range(SCAN_BULK // 2)]
            res = jnp.dot(cat([xs * hot_ref[tt, 0:1, :] for tt in steps]), spread, preferred_element_type=F32)
            for j, tt in enumerate(steps):
                vc_out[tt] = res[j * n:(j + 1) * n]

    def step(sub, tt, vc_in, sas):
        t = sub * SCAN_SUB + tt
        t_next = jnp.minimum(t + 1, tb - 1)
        hit = lane_t == t
        new_sas = []
        for g in range(SCAN_MATMULS):
            states = []
            for q, p in enumerate(groups[g]):
                vg, vq = divmod(p, SCAN_GROUP_PAIRS)
                vc = vc_in[tt, vg * R_HEAD_DIM:(vg + 1) * R_HEAD_DIM, vq * LANES:(vq + 1) * LANES]
                s_new = (s_ref[p] * rowp(w_ref, t, p) + pair_tile(sas[g], q) * rowp(b_ref, t, p)
                         + vc * rowp(k_ref, t, p))
                s_ref[p] = s_new
                states.append(s_new)
            res = head_sums(g, states, t_next, t)
            new_sas.append(res[:half])
            for q, p in enumerate(groups[g]):
                y_ref[p] = jnp.where(hit, pair_tile(res[half:], q), y_ref[p])
        return tuple(new_sas)

    def sub_block(sub, vc_in, vc_out, sas):
        nxt = jnp.minimum(sub + 1, n_sub - 1)

        def four_steps(i, sas):
            sas = step(sub, SCAN_BULK * i, vc_in, sas)
            if n_sub > 1:
                value_columns(nxt, i, vc_out)
            for u in range(1, SCAN_BULK):
                sas = step(sub, SCAN_BULK * i + u, vc_in, sas)
            return sas

        return lax.fori_loop(0, SCAN_SUB // SCAN_BULK, four_steps, sas)

    n_sub = tb // SCAN_SUB
    for i in range(SCAN_SUB // SCAN_BULK):
        value_columns(0, i, vc_ref)
    sas = tuple(head_sums(g, [s_ref[p] for p in groups[g]], 0) for g in range(SCAN_MATMULS))
    if n_sub == 1:
        sub_block(0, vc_ref, vc2_ref, sas)
    else:
        def two_sub_blocks(i, sas):
            return sub_block(2 * i + 1, vc2_ref, vc_ref, sub_block(2 * i, vc_ref, vc2_ref, sas))

        lax.fori_loop(0, n_sub // 2, two_sub_blocks, sas)

    @pl.when(tblk == pl.num_programs(1) - 1)
    def _():
        st_ref[...] = s_ref[...]


def _rwkv_scan(r, w, k, nkk, b, v, s0):
    nb, t_len, _ = r.shape
    tb = min(t_len, SCAN_BLOCK)
    nblk = t_len // tb
    hd = R_HEAD_DIM
    nsub, sb, ng, gp = tb // SCAN_SUB, SCAN_SUB, SCAN_GROUPS, SCAN_GROUP_PAIRS
    s0p = s0.reshape(nb, R_PAIRS, 2, hd, hd).transpose(0, 1, 3, 2, 4).reshape(nb, R_PAIRS, hd, LANES)
    lane2_h = jnp.arange(2 * LANES) // hd
    ones2 = (lane2_h[:, None] == lane2_h[None, :]).astype(BF16)
    src = jnp.arange(2 * LANES) % LANES
    src_q, src_h, src_t = src // (2 * sb), (src // sb) % 2, src % sb
    dst = jnp.arange(gp * LANES)
    dst_q, dst_h = dst // LANES, (dst % LANES) // hd
    spread = ((src_q[:, None] == dst_q[None, :]) & (src_h[:, None] == dst_h[None, :])).astype(BF16)
    hot = jnp.broadcast_to((src_t[None, :] == jnp.arange(sb)[:, None])[:, None, :], (sb, 16, 2 * LANES)).astype(BF16)
    trow = pl.BlockSpec((None, tb, R_PAIRS, LANES), lambda bb, i: (bb, i, 0, 0))
    r, w, k, nkk, b = (a.reshape(nb, t_len, R_PAIRS, LANES) for a in (r, w, k, nkk, b))
    st = pl.BlockSpec((None, R_PAIRS, hd, LANES), lambda bb, i: (bb, 0, 0, 0))
    full = lambda a: pl.BlockSpec(a.shape, lambda bb, i: (0,) * a.ndim)
    ycol_spec = pl.BlockSpec((None, None, R_PAIRS, hd, LANES), lambda bb, i: (bb, i, 0, 0, 0))
    ycol, s_t = pl.pallas_call(
        functools.partial(_rwkv_scan_kernel, tb=tb),
        out_shape=(jax.ShapeDtypeStruct((nb, nblk, R_PAIRS, hd, LANES), F32),
                   jax.ShapeDtypeStruct((nb, R_PAIRS, hd, LANES), F32)),
        grid=(nb, nblk),
        in_specs=[trow, trow, trow, trow, trow, pl.BlockSpec((None, tb, R_WIDTH), lambda bb, i: (bb, i, 0)), st,
                  full(ones2), full(hot), full(spread)],
        out_specs=(ycol_spec, st),
        scratch_shapes=[pltpu.VMEM((R_PAIRS, hd, LANES), F32), pltpu.VMEM((nsub, ng * hd, 2 * LANES), BF16),
                        pltpu.VMEM((sb, ng * hd, gp * LANES), F32), pltpu.VMEM((sb, ng * hd, gp * LANES), F32)],
        compiler_params=_cparams(("parallel", "arbitrary")),
        name="rwkv_scan",
    )(r, w, k, nkk, b, v, s0p, ones2, hot, spread)
    s_t = s_t.reshape(nb, R_PAIRS, hd, 2, hd).transpose(0, 1, 3, 2, 4).reshape(nb, R_HEADS, hd, hd)
    return ycol, s_t


def _rwkv_post_kernel(y_ref, r_ref, k_ref, v_ref, g_ref, lw_ref, lb_ref, rk_ref, ones_ref, o_ref, *, tb):
    ones_bd = ones_ref[...]
    tiles = []
    for p in range(R_PAIRS):
        yt = y_ref[p].T
        tiles.append(jnp.concatenate([yt[0:tb, :], yt[R_HEAD_DIM:R_HEAD_DIM + tb, :]], axis=1))
    y = jnp.concatenate(tiles, axis=1)
    mean = _head_sums(y, ones_bd) * (1.0 / R_HEAD_DIM)
    d = y - mean
    var = _head_sums(d * d, ones_bd) * (1.0 / R_HEAD_DIM)
    yn = d * lax.rsqrt(var + GN_EPS) * lw_ref[...] + lb_ref[...]
    bonus = _head_sums(r_ref[...] * k_ref[...] * rk_ref[...], ones_bd) * v_ref[...]
    o_ref[...] = ((yn + bonus) * g_ref[...]).astype(o_ref.dtype)


def _rwkv_post(ycol, r, k, v, g, lnx_w, lnx_b, r_k, ones_bd):
    m = r.shape[0]
    nblk = ycol.shape[1]
    tb = m // (ycol.shape[0] * nblk)
    row = pl.BlockSpec((tb, R_WIDTH), lambda i: (i, 0))
    full = lambda a: pl.BlockSpec(a.shape, lambda i: (0,) * a.ndim)
    params = (lnx_w, lnx_b, r_k, ones_bd)
    return pl.pallas_call(
        functools.partial(_rwkv_post_kernel, tb=tb),
        out_shape=jax.ShapeDtypeStruct((m, R_WIDTH), BF16),
        grid=(m // tb,),
        in_specs=[pl.BlockSpec((None, None, R_PAIRS, R_HEAD_DIM, LANES), lambda i: (i // nblk, i % nblk, 0, 0, 0))]
        + [row] * 4 + [full(p) for p in params],
        out_specs=row,
        compiler_params=_cparams(("parallel",)),
        name="rwkv_post",
    )(ycol, r, k, v, g, *params)


def _ffn_up_kernel(x_ref, xp_ref, wg_ref, wu_ref, c0_ref, cw_ref, cb_ref, act_ref, tail_ref, *,
                   seq_rows, seqs_per_tile, tiles_per_seq):
    first = lax.rem(pl.program_id(0), tiles_per_seq) == 0
    wg = wg_ref[...].astype(BF16)
    gate_all = jnp.dot(x_ref[...], wg, preferred_element_type=F32)
    up_all = jnp.dot(x_ref[...], wu_ref[...].astype(BF16), preferred_element_type=F32)
    gate_prev = jnp.dot(xp_ref[...], wg, preferred_element_type=F32)
    row = lax.broadcasted_iota(I32, (seq_rows, 1), 0)
    for s in range(seqs_per_tile):
        rows = slice(s * seq_rows, (s + 1) * seq_rows)
        gate = gate_all[rows]
        prev1 = jnp.where(first, c0_ref[s, 1:2, :], gate_prev[7:8, :])
        prev2 = jnp.where(first, c0_ref[s, 0:1, :], gate_prev[6:7, :])
        g_m1 = jnp.where(row == 0, prev1, pltpu.roll(gate, 1, axis=0))
        g_m2 = jnp.where(row == 0, prev2, jnp.where(row == 1, prev1, pltpu.roll(gate, 2, axis=0)))
        conv = cb_ref[...] + g_m2 * cw_ref[0:1, :]
        conv = conv + g_m1 * cw_ref[1:2, :]
        conv = conv + gate * cw_ref[2:3, :]
        act_ref[rows, :] = (conv * jax.nn.sigmoid(conv) * up_all[rows]).astype(act_ref.dtype)
        tail_ref[s] = gate[seq_rows - 8:, :]


def _ffn_up(x, w_in, conv0, conv_w, conv_b, n_batch, t_len, tm=1024, tn=256):
    m, d = x.shape
    d_ff = w_in.shape[1] // 2
    tm = min(m, tm)
    seq_rows = min(t_len, tm)
    seqs_per_tile, tiles_per_seq = tm // seq_rows, t_len // seq_rows
    assert m % tm == 0 and tm % seq_rows == 0 and t_len % seq_rows == 0 and seq_rows % 8 == 0 and d_ff % tn == 0
    n_up = d_ff // tn
    act, tail = pl.pallas_call(
        functools.partial(_ffn_up_kernel, seq_rows=seq_rows, seqs_per_tile=seqs_per_tile, tiles_per_seq=tiles_per_seq),
        out_shape=(jax.ShapeDtypeStruct((m, d_ff), BF16),
                   jax.ShapeDtypeStruct((n_batch * tiles_per_seq, 8, d_ff), F32)),
        grid=(m // tm, n_up),
        in_specs=[pl.BlockSpec((tm, d), lambda i, j: (i, 0)),
                  pl.BlockSpec((8, d), lambda i, j: (jnp.maximum(i * (tm // 8) - 1, 0), 0)),
                  pl.BlockSpec((d, tn), lambda i, j: (0, j)),
                  pl.BlockSpec((d, tn), lambda i, j: (0, j + n_up)),
                  pl.BlockSpec((seqs_per_tile, CONV_W - 1, tn), lambda i, j: (i // tiles_per_seq, 0, j)),
                  pl.BlockSpec((CONV_W, tn), lambda i, j: (0, j)),
                  pl.BlockSpec((1, tn), lambda i, j: (0, j))],
        out_specs=(pl.BlockSpec((tm, tn), lambda i, j: (i, j)),
                   pl.BlockSpec((seqs_per_tile, 8, tn), lambda i, j: (i, 0, j))),
        compiler_params=_cparams(("parallel", "arbitrary")),
        name="ffn_up",
    )(x, x, w_in, w_in, conv0, conv_w, conv_b.reshape(1, d_ff))
    gate_tail = tail.reshape(n_batch, tiles_per_seq, 8, d_ff)[:, -1, 8 - (CONV_W - 1):, :]
    return act, gate_tail


def _pad_rw_cols(a):
    z = lambda n: jnp.zeros(a.shape[:-1] + (n,), a.dtype)
    wd0, ad0, gd0 = 3 * R_WIDTH, 3 * R_WIDTH + DECAY_LORA, 3 * R_WIDTH + DECAY_LORA + AAA_LORA
    return jnp.concatenate([a[..., :wd0], a[..., wd0:ad0], z(LORA_PAD - DECAY_LORA), a[..., ad0:gd0],
                            z(LORA_PAD - AAA_LORA), a[..., gd0:]], axis=-1)


def _unpad_rw_cols(a):
    return jnp.concatenate([a[..., :RW_WD0 + DECAY_LORA], a[..., RW_AD0:RW_AD0 + AAA_LORA], a[..., RW_GD0:]], axis=-1)


def _prep_weights(norm_mix_g, w_in, rwkv_mu, rwkv_w0, rwkv_w2, rwkv_a0, rwkv_a2, rwkv_g2, rwkv_k_k, rwkv_k_a,
                  rwkv_r_k, rwkv_lnx_w, rwkv_lnx_b, w_out, norm_ffn_g, ffn_w_in, ffn_conv_w, ffn_conv_b,
                  ffn_w_down, norm_final_g, l):
    d = w_in.shape[1]
    w_att = jnp.concatenate([w_in[l][:, :ATT_USED].astype(BF16), jnp.zeros((d, ATT_COLS - ATT_USED), BF16)], axis=1)
    w_rw = _pad_rw_cols(w_in[l][:, ATT_USED:].astype(BF16))
    row = lambda a: a.reshape(1, -1).astype(F32)
    pad_rows = lambda a, n: jnp.concatenate([a, jnp.zeros((n - a.shape[0], a.shape[1]), a.dtype)], axis=0)
    lane_h = jnp.arange(LANES) // R_HEAD_DIM
    w2_pad = pad_rows(rwkv_w2[l].astype(F32), LORA_PAD)
    w2_hi = w2_pad.astype(BF16)
    w2_mid = (w2_pad - w2_hi.astype(F32)).astype(BF16)
    return dict(
        norm_mix_g=norm_mix_g[l], w_att=w_att, w_rw=w_rw,
        mu=_pad_rw_cols(row(rwkv_mu[l])), w0=row(rwkv_w0[l]), a0=row(rwkv_a0[l]),
        k_k=row(rwkv_k_k[l]), k_a=row(rwkv_k_a[l]),
        w2=jnp.concatenate([w2_hi, w2_mid, w2_hi], axis=0), a2=pad_rows(rwkv_a2[l], LORA_PAD).astype(BF16),
        g2=rwkv_g2[l].astype(BF16),
        r_k=row(rwkv_r_k[l]), lnx_w=row(rwkv_lnx_w[l]), lnx_b=row(rwkv_lnx_b[l]),
        ones_bd=jnp.concatenate([lane_h[:, None] == lane_h[None, :]] * 2, axis=0).astype(BF16),
        w_out_a=w_out[l][:A_WIDTH].astype(BF16), w_out_r=w_out[l][A_WIDTH:].astype(BF16),
        norm_ffn_g=norm_ffn_g[l], ffn_w_in=ffn_w_in[l], conv_w=ffn_conv_w[l], conv_b=ffn_conv_b[l],
        ffn_w_down=ffn_w_down[l].astype(BF16), norm_final_g=norm_final_g,
    )


def _trunk(x, past_k, past_v, past_ik, s0, shift0, conv0, wt):
    nb, t_len, d = x.shape
    m = nb * t_len
    p_len = 0 if past_k is None else past_k.shape[1]
    n_keys = p_len + t_len
    x2 = x.reshape(m, d)

    h = _rmsnorm(x2, wt["norm_mix_g"], BF16)
    z_att = _matmul([h], [wt["w_att"]], tm=1024, name="proj_att")
    z_rw = _matmul([h], [wt["w_rw"]], tm=1024, name="proj_rw")

    q_bf, k_f, k_bf, v_f, v_bf, iq_bf, ikw, ik_bf = _rope_split(z_att, t_len, p_len)
    tk = min(-(-n_keys // LANES) * LANES, 512)
    lp = -(-n_keys // tk) * tk

    def with_past(new, past, width):
        new = new.reshape(nb, t_len, width)
        parts = [new] if past is None else [past.reshape(nb, p_len, width).astype(BF16), new]
        if lp > n_keys:
            parts.append(jnp.zeros((nb, lp - n_keys, width), BF16))
        return parts[0] if len(parts) == 1 else jnp.concatenate(parts, axis=1)

    attn = _dsa(q_bf, iq_bf, ikw, with_past(k_bf, past_k, KV_W), with_past(v_bf, past_v, KV_W),
                with_past(ik_bf, past_ik, IDX_DIM), n_batch=nb, t_len=t_len, pos0=p_len, n_keys=n_keys)

    z_rw3 = z_rw.reshape(nb, t_len, RW_COLS)
    r, w, k2, v2, nkk, b, g = _rwkv_pre(z_rw3, _pad_rw_cols(shift0.astype(F32)), wt["mu"], wt["w0"], wt["a0"],
                                       wt["k_k"], wt["k_a"], wt["w2"], wt["a2"], wt["g2"], wt["ones_bd"])
    ycol, s_t = _rwkv_scan(r, w, k2, nkk, b, v2, s0.astype(F32))
    flat = lambda a: a.reshape(m, R_WIDTH)
    rw = _rwkv_post(ycol, flat(r), flat(k2), flat(v2), flat(g), wt["lnx_w"], wt["lnx_b"], wt["r_k"], wt["ones_bd"])

    x1 = _matmul([attn, rw], [wt["w_out_a"], wt["w_out_r"]], res=x2, tm=1024, name="out_proj")
    hf = _rmsnorm(x1, wt["norm_ffn_g"], BF16)
    act, conv_t = _ffn_up(hf, wt["ffn_w_in"], conv0.astype(F32), wt["conv_w"], wt["conv_b"], nb, t_len)
    x3 = _matmul([act], [wt["ffn_w_down"]], res=x1, name="ffn_down")
    y_out = _rmsnorm(x3, wt["norm_final_g"], F32).reshape(nb, t_len, d)

    shift_t = _unpad_rw_cols(z_rw3[:, -1:])
    caches = (k_f.reshape(nb, t_len, A_KV_HEADS, A_HEAD_DIM)[None], v_f.reshape(nb, t_len, A_KV_HEADS, A_HEAD_DIM)[None],
              ikw[:, :IDX_DIM].reshape(nb, t_len, IDX_DIM)[None], s_t[None], shift_t[None], conv_t[None])
    return y_out, caches


def kernel(x_prompt, x_sample, cache_k, cache_v, cache_idx_k, state_rwkv, state_rwkv_shift, state_ffn_conv, norm_mix_g, w_in, rwkv_mu, rwkv_w0, rwkv_w2, rwkv_a0, rwkv_a2, rwkv_g2, rwkv_k_k, rwkv_k_a, rwkv_r_k, rwkv_lnx_w, rwkv_lnx_b, w_out, norm_ffn_g, ffn_w_in, ffn_conv_w, ffn_conv_b, ffn_w_down, norm_final_g):
    assert w_in.shape[0] == 1, "single-layer trunk"
    wt = _prep_weights(norm_mix_g, w_in, rwkv_mu, rwkv_w0, rwkv_w2, rwkv_a0, rwkv_a2, rwkv_g2, rwkv_k_k, rwkv_k_a,
                       rwkv_r_k, rwkv_lnx_w, rwkv_lnx_b, w_out, norm_ffn_g, ffn_w_in, ffn_conv_w, ffn_conv_b,
                       ffn_w_down, norm_final_g, 0)
    bp = x_prompt.shape[0]
    d_ff = ffn_conv_w.shape[-1]
    y_p, c_p = _trunk(x_prompt, None, None, None,
                      jnp.zeros((bp, R_HEADS, R_HEAD_DIM, R_HEAD_DIM), F32), jnp.zeros((bp, 1, RWKV_COLS), F32),
                      jnp.zeros((bp, CONV_W - 1, d_ff), F32), wt)
    y_s, c_s = _trunk(x_sample, cache_k[0], cache_v[0], cache_idx_k[0], state_rwkv[0], state_rwkv_shift[0],
                      state_ffn_conv[0], wt)
    return (y_p, y_s) + c_p + c_s
```

```python
import functools

import jax
import jax.numpy as jnp
from jax import lax
from jax.experimental import pallas as pl
from jax.experimental.pallas import tpu as pltpu

F32 = jnp.float32
BF16 = jnp.bfloat16
I32 = jnp.int32

CHUNK = 64
A_HEADS = 16
A_KV_HEADS = 4
A_HEAD_DIM = 128
A_GROUP = A_HEADS // A_KV_HEADS
A_WIDTH = A_HEADS * A_HEAD_DIM
KV_W = A_KV_HEADS * A_HEAD_DIM
IDX_HEADS = 16
IDX_DIM = 64
IQ_W = IDX_HEADS * IDX_DIM
TOPK_MAX = 256
ROPE_THETA = 500000.0
ROPE_FRAC = 4
A_SCALE = A_HEAD_DIM ** -0.5
Q_SCALE = A_SCALE * 1.4426950408889634
IDX_SCALE = (IDX_HEADS ** -0.5) * (IDX_DIM ** -0.5)
R_HEAD_DIM = 64
R_WIDTH = 2048
R_HEADS = R_WIDTH // R_HEAD_DIM
R_PAIRS = R_HEADS // 2
DECAY_LORA = 96
AAA_LORA = 96
GATE_LORA = 256
RWKV_COLS = 3 * R_WIDTH + DECAY_LORA + AAA_LORA + GATE_LORA
GN_EPS = 6.4e-4
CONV_W = 3
RMS_EPS = 1e-6

LANES = 128
VMEM_LIMIT = 56 * 1024 * 1024

ATT_Q0, ATT_K0, ATT_V0, ATT_IQ0, ATT_IK0 = 0, A_WIDTH, A_WIDTH + KV_W, A_WIDTH + 2 * KV_W, A_WIDTH + 2 * KV_W + IQ_W
ATT_USED = ATT_IK0 + IDX_DIM + IDX_HEADS
ATT_COLS = 4608
LORA_PAD = 128
RW_WD0 = 3 * R_WIDTH
RW_AD0 = RW_WD0 + LORA_PAD
RW_GD0 = RW_AD0 + LORA_PAD
RW_COLS = RW_GD0 + GATE_LORA
DSA_TQ, DSA_TK = 256, 512
COUNT_ROWS = 128
SCAN_BLOCK = 64
SCAN_SUB = 16
SCAN_GROUP_PAIRS = LANES // (2 * SCAN_SUB)
SCAN_GROUPS = R_PAIRS // SCAN_GROUP_PAIRS
SCAN_MATMULS = 4
SCAN_BULK = 4
INT_MIN = -2 ** 31
NEG_BIG = -1e30


def _cparams(sem):
    return pltpu.CompilerParams(dimension_semantics=sem, vmem_limit_bytes=VMEM_LIMIT)


def _rmsnorm_kernel(x_ref, g_ref, o_ref):
    x = x_ref[...]
    y = x * lax.rsqrt(jnp.mean(x * x, axis=-1, keepdims=True) + RMS_EPS)
    o_ref[...] = (y * g_ref[...]).astype(o_ref.dtype)


def _rmsnorm(x, g, out_dtype):
    m, d = x.shape
    tm = min(m, 256)
    return pl.pallas_call(
        _rmsnorm_kernel,
        out_shape=jax.ShapeDtypeStruct((m, d), out_dtype),
        grid=(m // tm,),
        in_specs=[pl.BlockSpec((tm, d), lambda i: (i, 0)), pl.BlockSpec((1, d), lambda i: (0, 0))],
        out_specs=pl.BlockSpec((tm, d), lambda i: (i, 0)),
        compiler_params=_cparams(("parallel",)),
        name="rmsnorm",
    )(x, g.reshape(1, d).astype(F32))


def _mm_kernel(*refs, n_pairs, has_res):
    o_ref = refs[-1]
    acc = jnp.dot(refs[0][...], refs[n_pairs][...], preferred_element_type=F32)
    for p in range(1, n_pairs):
        acc = acc + jnp.dot(refs[p][...], refs[n_pairs + p][...], preferred_element_type=F32)
    if has_res:
        acc = refs[2 * n_pairs][...] + acc
    o_ref[...] = acc.astype(o_ref.dtype)


def _matmul(a_list, b_list, res=None, tm=512, tn=512, name="matmul"):
    m = a_list[0].shape[0]
    n = b_list[0].shape[1]
    tm = min(tm, m)
    tn = min(tn, n)
    assert m % tm == 0 and n % tn == 0, (m, n, tm, tn)
    in_specs = [pl.BlockSpec((tm, a.shape[1]), lambda i, j: (i, 0)) for a in a_list]
    in_specs += [pl.BlockSpec((b.shape[0], tn), lambda i, j: (0, j)) for b in b_list]
    args = list(a_list) + list(b_list)
    if res is not None:
        in_specs.append(pl.BlockSpec((tm, tn), lambda i, j: (i, j)))
        args.append(res)
    return pl.pallas_call(
        functools.partial(_mm_kernel, n_pairs=len(a_list), has_res=res is not None),
        out_shape=jax.ShapeDtypeStruct((m, n), F32),
        grid=(m // tm, n // tn),
        in_specs=in_specs,
        out_specs=pl.BlockSpec((tm, tn), lambda i, j: (i, j)),
        compiler_params=_cparams(("parallel", "arbitrary")),
        name=name,
    )(*args)


def _rope_tile(x, cos, sin, half, d_in_head):
    lo = d_in_head < half
    hi = (d_in_head >= half) & (d_in_head < 2 * half)
    c = jnp.where(lo | hi, cos, 1.0)
    s_up = jnp.where(lo, -sin, 0.0)
    s_dn = jnp.where(hi, sin, 0.0)
    x_up = pltpu.roll(x, LANES - half, axis=1)
    x_dn = pltpu.roll(x, half, axis=1)
    return x * c + x_up * s_up + x_dn * s_dn


def _rope_kernel(z_ref, invf_ref, q_ref, kf_ref, kb_ref, vf_ref, vb_ref, iq_ref, ikw_ref, ikb_ref, *, tm, t_len, pos0):
    i = pl.program_id(0)
    row = lax.broadcasted_iota(I32, (tm, LANES), 0) + i * tm
    pos = (pos0 + lax.rem(row, t_len)).astype(F32)
    lane = lax.broadcasted_iota(I32, (tm, LANES), 1)
    ang = pos * invf_ref[0:1, :]
    cos_a, sin_a = jnp.cos(ang), jnp.sin(ang)
    half_a = A_HEAD_DIM // ROPE_FRAC // 2
    for h in range(A_HEADS):
        x = z_ref[:, ATT_Q0 + h * LANES:ATT_Q0 + (h + 1) * LANES]
        q_ref[:, h * LANES:(h + 1) * LANES] = (_rope_tile(x, cos_a, sin_a, half_a, lane) * Q_SCALE).astype(q_ref.dtype)
    for h in range(A_KV_HEADS):
        x = z_ref[:, ATT_K0 + h * LANES:ATT_K0 + (h + 1) * LANES]
        y = _rope_tile(x, cos_a, sin_a, half_a, lane)
        kf_ref[:, h * LANES:(h + 1) * LANES] = y
        kb_ref[:, h * LANES:(h + 1) * LANES] = y.astype(kb_ref.dtype)
    v = z_ref[:, ATT_V0:ATT_V0 + KV_W]
    vf_ref[...] = v
    vb_ref[...] = v.astype(vb_ref.dtype)
    ang = pos * invf_ref[1:2, :]
    cos_i, sin_i = jnp.cos(ang), jnp.sin(ang)
    half_i = IDX_DIM // ROPE_FRAC // 2
    d_i = lane & (IDX_DIM - 1)
    for h in range(IQ_W // LANES):
        x = z_ref[:, ATT_IQ0 + h * LANES:ATT_IQ0 + (h + 1) * LANES]
        iq_ref[:, h * LANES:(h + 1) * LANES] = _rope_tile(x, cos_i, sin_i, half_i, d_i).astype(iq_ref.dtype)
    x = z_ref[:, ATT_IK0:ATT_IK0 + LANES]
    d_k = jnp.where(lane < IDX_DIM, lane, IDX_DIM)
    y = _rope_tile(x, cos_i, sin_i, half_i, d_k)
    ikw_ref[...] = y
    ikb_ref[...] = y[:, :IDX_DIM].astype(ikb_ref.dtype)


def _rope_split(z_att, t_len, pos0):
    m = z_att.shape[0]
    tm = min(m, 256)
    lane = jnp.arange(LANES)
    rd_a = A_HEAD_DIM // ROPE_FRAC
    rd_i = IDX_DIM // ROPE_FRAC
    invf_a = ROPE_THETA ** (-((lane % (rd_a // 2)).astype(F32) * 2.0 / rd_a))
    invf_i = ROPE_THETA ** (-((lane % (rd_i // 2)).astype(F32) * 2.0 / rd_i))
    invf = jnp.zeros((8, LANES), F32).at[0].set(invf_a).at[1].set(invf_i)
    row_spec = lambda w: pl.BlockSpec((tm, w), lambda i: (i, 0))
    shp = lambda w, dt: jax.ShapeDtypeStruct((m, w), dt)
    return pl.pallas_call(
        functools.partial(_rope_kernel, tm=tm, t_len=t_len, pos0=pos0),
        out_shape=(shp(A_WIDTH, BF16), shp(KV_W, F32), shp(KV_W, BF16), shp(KV_W, F32), shp(KV_W, BF16),
                   shp(IQ_W, BF16), shp(LANES, F32), shp(IDX_DIM, BF16)),
        grid=(m // tm,),
        in_specs=[row_spec(ATT_COLS), pl.BlockSpec((8, LANES), lambda i: (0, 0))],
        out_specs=(row_spec(A_WIDTH), row_spec(KV_W), row_spec(KV_W), row_spec(KV_W), row_spec(KV_W),
                   row_spec(IQ_W), row_spec(LANES), row_spec(IDX_DIM)),
        compiler_params=_cparams(("parallel",)),
        name="rope_split",
    )(z_att, invf)


def _tile(x, n, axis):
    return x if n == 1 else jnp.concatenate([x] * n, axis=axis)


def _sortable(score):
    u = lax.bitcast_convert_type(score, I32)
    return jnp.where(u < 0, u ^ jnp.int32(0x7FFFFFFF), u)


def _dsa_kernel(q_ref, iq_ref, ikw_ref, k_ref, v_ref, ik_ref, o_ref,
                key_ref, iwb_ref, qs_ref, m_ref, l_ref, acc_ref, *, tq, tk, pos0, n_keys, topk):
    i = pl.program_id(1)
    q0 = pos0 + i * tq
    kmax = jnp.minimum((lax.div(q0 + tq - 1, CHUNK) + 1) * CHUNK, n_keys)
    nkb = lax.div(kmax + tk - 1, tk)
    n_rep = tk // LANES
    nt_dims = (((1,), (1,)), ((), ()))

    for h in range(IDX_HEADS):
        iwb_ref[h] = jnp.broadcast_to(ikw_ref[:, IDX_DIM + h:IDX_DIM + h + 1], (tq, LANES))
    for h in range(A_HEADS):
        qs_ref[h * tq:(h + 1) * tq, :] = q_ref[:, h * LANES:(h + 1) * LANES]

    def score_block(kb, carry):
        koff = pl.multiple_of(kb * tk, tk)
        ikb = ik_ref[pl.ds(koff, tk), :]
        acc = jnp.zeros((tq, tk), F32)
        for h in range(IDX_HEADS):
            d = lax.dot_general(iq_ref[:, h * IDX_DIM:(h + 1) * IDX_DIM], ikb, nt_dims, preferred_element_type=F32)
            acc = acc + jnp.maximum(d, 0.0) * _tile(iwb_ref[h], n_rep, 1)
        kpos = koff + lax.broadcasted_iota(I32, (tq, tk), 1)
        qpos = q0 + lax.broadcasted_iota(I32, (tq, tk), 0)
        adm = (lax.shift_right_logical(kpos, 6) <= lax.shift_right_logical(qpos, 6)) & (kpos < n_keys)
        key_ref[:, pl.ds(koff, tk)] = jnp.where(adm, _sortable(acc * IDX_SCALE), jnp.int32(INT_MIN))
        return carry

    lax.fori_loop(0, nkb, score_block, 0)

    def count(pred, *row_args):
        cr = min(tq, COUNT_ROWS)

        def chunk(r0):
            def body(kb, cnt):
                koff = pl.multiple_of(kb * tk, tk)
                keys = key_ref[r0:r0 + cr, pl.ds(koff, tk)]
                kpos = koff + lax.broadcasted_iota(I32, (cr, tk), 1)
                hit = jnp.where(pred(keys, kpos, *[a[r0:r0 + cr] for a in row_args]), 1.0, 0.0)
                for c in range(n_rep):
                    cnt = cnt + hit[:, c * LANES:(c + 1) * LANES]
                return cnt
            cnt = lax.fori_loop(0, nkb, body, jnp.zeros((cr, LANES), F32))
            return jnp.broadcast_to(jnp.sum(cnt, axis=-1, keepdims=True), (cr, LANES))

        return jnp.concatenate([chunk(r0) for r0 in range(0, tq, cr)], axis=0)

    def wide(x):
        return _tile(x, n_rep, 1)

    def bit_step(it, tu):
        cand_u = tu | lax.shift_left(jnp.int32(1), 31 - it)
        cand_s = wide(cand_u ^ jnp.int32(INT_MIN))
        cnt = count(lambda keys, kpos, c: keys >= c, cand_s)
        return jnp.where(cnt >= topk, cand_u, tu)

    tu = lax.fori_loop(0, 32, bit_step, jnp.zeros((tq, LANES), I32))
    thr = jnp.maximum(tu ^ jnp.int32(INT_MIN), jnp.int32(INT_MIN + 1))
    thr_w = wide(thr)

    n_ge = count(lambda keys, kpos, t: keys >= t, thr_w)
    n_gt = count(lambda keys, kpos, t: keys > t, thr_w)
    excess = n_ge > topk

    @pl.when(jnp.max(jnp.where(excess, 1.0, 0.0)) > 0.0)
    def _():
        need = topk - n_gt

        idx_bits = int(key_ref.shape[1]).bit_length()

        def idx_step(it, jm):
            cand = wide(jm | lax.shift_left(jnp.int32(1), idx_bits - 1 - it))
            cnt = count(lambda keys, kpos, t, c: (keys == t) & (kpos < c), thr_w, cand)
            return jnp.where(cnt < need, cand[:, :LANES], jm)

        jm = lax.fori_loop(0, idx_bits, idx_step, jnp.zeros((tq, LANES), I32))
        jm_w = wide(jnp.where(excess, jm, jnp.int32(2 ** 31 - 1)))

        def drop(kb, carry):
            koff = pl.multiple_of(kb * tk, tk)
            keys = key_ref[:, pl.ds(koff, tk)]
            kpos = koff + lax.broadcasted_iota(I32, (tq, tk), 1)
            key_ref[:, pl.ds(koff, tk)] = jnp.where((keys == thr_w) & (kpos > jm_w), jnp.int32(INT_MIN), keys)
            return carry

        lax.fori_loop(0, nkb, drop, 0)

    m_ref[...] = jnp.full(m_ref.shape, NEG_BIG, F32)
    l_ref[...] = jnp.zeros(l_ref.shape, F32)
    acc_ref[...] = jnp.zeros(acc_ref.shape, F32)
    rows = A_GROUP * tq

    def attend(kb, carry):
        koff = pl.multiple_of(kb * tk, tk)
        bias = _tile(jnp.where(key_ref[:, pl.ds(koff, tk)] >= thr_w, 0.0, NEG_BIG), A_GROUP, 0)
        for n in range(A_KV_HEADS):
            r0 = n * rows
            kn = k_ref[pl.ds(koff, tk), n * LANES:(n + 1) * LANES]
            vn = v_ref[pl.ds(koff, tk), n * LANES:(n + 1) * LANES]
            s = lax.dot_general(qs_ref[r0:r0 + rows, :], kn, nt_dims, preferred_element_type=F32) + bias
            m_prev = m_ref[r0:r0 + rows, :]
            m_new = jnp.maximum(m_prev, jnp.max(s, axis=-1, keepdims=True))
            alpha = jnp.exp2(m_prev - m_new)
            p = jnp.exp2(s - _tile(m_new, n_rep, 1))
            l_ref[r0:r0 + rows, :] = alpha * l_ref[r0:r0 + rows, :] + jnp.sum(p, axis=-1, keepdims=True)
            acc_ref[r0:r0 + rows, :] = alpha * acc_ref[r0:r0 + rows, :] + jnp.dot(
                p.astype(vn.dtype), vn, preferred_element_type=F32)
            m_ref[r0:r0 + rows, :] = m_new
        return carry

    lax.fori_loop(0, nkb, attend, 0)
    for h in range(A_HEADS):
        o_ref[:, h * LANES:(h + 1) * LANES] = (
            acc_ref[h * tq:(h + 1) * tq, :] / l_ref[h * tq:(h + 1) * tq, :]).astype(o_ref.dtype)


def _dsa(q_bf, iq_bf, ikw, k_all, v_all, ik_all, *, n_batch, t_len, pos0, n_keys):
    lp = k_all.shape[1]
    tq = min(t_len, DSA_TQ)
    tk = lp if tq * lp <= DSA_TQ * DSA_TK else DSA_TK
    assert lp % tk == 0 and t_len % tq == 0
    nq = t_len // tq
    topk = min(TOPK_MAX, n_keys // 4)
    qrow = lambda w: pl.BlockSpec((tq, w), lambda b, i: (b * nq + i, 0))
    kv_spec = lambda w: pl.BlockSpec((None, lp, w), lambda b, i: (b, 0, 0), pipeline_mode=pl.Buffered(1))
    return pl.pallas_call(
        functools.partial(_dsa_kernel, tq=tq, tk=tk, pos0=pos0, n_keys=n_keys, topk=float(topk)),
        out_shape=jax.ShapeDtypeStruct((n_batch * t_len, A_WIDTH), BF16),
        grid=(n_batch, nq),
        in_specs=[qrow(A_WIDTH), qrow(IQ_W), qrow(LANES), kv_spec(KV_W), kv_spec(KV_W), kv_spec(IDX_DIM)],
        out_specs=qrow(A_WIDTH),
        scratch_shapes=[
            pltpu.VMEM((tq, lp), I32),
            pltpu.VMEM((IDX_HEADS, tq, LANES), F32),
            pltpu.VMEM((A_HEADS * tq, LANES), BF16),
            pltpu.VMEM((A_HEADS * tq, LANES), F32),
            pltpu.VMEM((A_HEADS * tq, LANES), F32),
            pltpu.VMEM((A_HEADS * tq, LANES), F32),
        ],
        compiler_params=_cparams(("parallel", "arbitrary")),
        name="dsa",
    )(q_bf, iq_bf, ikw, k_all, v_all, ik_all)


def _head_sums(x, ones2):
    n = x.shape[1] // LANES
    tm = x.shape[0]
    hi, mid = _split2(jnp.concatenate([x[:, c * LANES:(c + 1) * LANES] for c in range(n)], axis=0))
    s = jnp.dot(jnp.concatenate([hi, mid], axis=1), ones2, preferred_element_type=F32)
    return jnp.concatenate([s[c * tm:(c + 1) * tm, :] for c in range(n)], axis=1)


def _rwkv_pre_kernel(z_ref, zp_ref, z0_ref, mu_ref, w0_ref, a0_ref, kk_ref, ka_ref, w2_ref, a2_ref, g2_ref,
                     ones_ref, r_out, w_out, k_out, v_out, nkk_out, b_out, g_out, *, tm):
    i = pl.program_id(1)
    row = lax.broadcasted_iota(I32, (tm, 1), 0)

    def mixed(c0, width):
        z = z_ref[:, c0:c0 + width]
        first = jnp.where(i == 0, z0_ref[:, c0:c0 + width], zp_ref[7:8, c0:c0 + width])
        shifted = jnp.where(row == 0, first, pltpu.roll(z, 1, axis=0))
        return z + (shifted - z) * mu_ref[:, c0:c0 + width]

    r = mixed(0, R_WIDTH)
    k = mixed(R_WIDTH, R_WIDTH)
    v = mixed(2 * R_WIDTH, R_WIDTH)
    wd = mixed(RW_WD0, LORA_PAD)
    ad = mixed(RW_AD0, LORA_PAD)
    gd = mixed(RW_GD0, GATE_LORA)
    lora = lambda x, w_ref: jnp.dot(x.astype(BF16), w_ref[...], preferred_element_type=F32)
    t_hi, t_mid = _split2(jnp.tanh(wd))
    y = -(w0_ref[...] + jnp.dot(jnp.concatenate([t_hi, t_hi, t_mid], axis=1), w2_ref[...], preferred_element_type=F32))
    softplus = jnp.maximum(y, 0.0) + jnp.log(1.0 + jnp.exp(-jnp.abs(y)))
    decay = jnp.exp(-jnp.exp(-softplus - 0.5))
    a = jax.nn.sigmoid(a0_ref[...] + lora(ad, a2_ref))
    g = lora(jax.nn.sigmoid(gd), g2_ref)
    kk = k * kk_ref[...]
    kk = kk / jnp.maximum(jnp.sqrt(_head_sums(kk * kk, ones_ref[...])), 1e-12)
    r_out[...] = r
    w_out[...] = decay
    k_out[...] = k * (1.0 + (a - 1.0) * ka_ref[...])
    v_out[...] = v
    nkk_out[...] = -kk
    b_out[...] = kk * a
    g_out[...] = g


def _rwkv_pre(z_rw, z0, mu, w0, a0, k_k, k_a, w2, a2, g2, ones_bd):
    nb, t_len, _ = z_rw.shape
    tm = min(t_len, 128)
    zrow = pl.BlockSpec((None, tm, RW_COLS), lambda b, i: (b, i, 0))
    zprev = pl.BlockSpec((None, 8, RW_COLS), lambda b, i: (b, jnp.maximum(i * (tm // 8) - 1, 0), 0))
    full = lambda a: pl.BlockSpec(a.shape, lambda b, i: (0,) * a.ndim)
    orow = pl.BlockSpec((None, tm, R_WIDTH), lambda b, i: (b, i, 0))
    params = (mu, w0, a0, k_k, k_a, w2, a2, g2, ones_bd)
    return pl.pallas_call(
        functools.partial(_rwkv_pre_kernel, tm=tm),
        out_shape=tuple(jax.ShapeDtypeStruct((nb, t_len, R_WIDTH), F32) for _ in range(7)),
        grid=(nb, t_len // tm),
        in_specs=[zrow, zprev, pl.BlockSpec((None, 1, RW_COLS), lambda b, i: (b, 0, 0))] + [full(p) for p in params],
        out_specs=tuple(orow for _ in range(7)),
        compiler_params=_cparams(("parallel", "arbitrary")),
        name="rwkv_pre",
    )(z_rw, z_rw, z0, *params)


def _split2(x):
    hi = x.astype(BF16)
    return hi, (x - hi.astype(F32)).astype(BF16)


def _rwkv_scan_kernel(r_ref, w_ref, k_ref, nkk_ref, b_ref, v_ref, s0_ref, ones2_ref, hot_ref, spread_ref,
                      y_ref, st_ref, s_ref, xs_ref, vc_ref, vc2_ref, *, tb):
    tblk = pl.program_id(1)
    cat = lambda xs, ax=0: jnp.concatenate(xs, axis=ax)

    @pl.when(tblk == 0)
    def _():
        s_ref[...] = s0_ref[...]

    pad = jnp.zeros((SCAN_BLOCK - tb, LANES), F32)
    v_cols = []
    for p in range(R_PAIRS):
        vp = v_ref[:, p * LANES:(p + 1) * LANES]
        v_cols.append((cat([vp, pad]) if tb < SCAN_BLOCK else vp).T)
    for sub in range(tb // SCAN_SUB):
        for g in range(SCAN_GROUPS):
            x = cat([v_cols[SCAN_GROUP_PAIRS * g + q][h * R_HEAD_DIM:(h + 1) * R_HEAD_DIM,
                                                      sub * SCAN_SUB:(sub + 1) * SCAN_SUB]
                     for q in range(SCAN_GROUP_PAIRS) for h in range(2)], 1)
            hi, mid = _split2(x)
            xs_ref[sub, g * R_HEAD_DIM:(g + 1) * R_HEAD_DIM, :] = cat([hi, mid], 1)
    ones2 = ones2_ref[...]
    spread = spread_ref[...]
    y_ref[...] = jnp.zeros(y_ref.shape, F32)
    lane_t = lax.rem(lax.broadcasted_iota(I32, (R_HEAD_DIM, LANES), 1), R_HEAD_DIM)
    rowp = lambda ref, t, p: ref[t, p:p + 1, :]
    group_pairs = R_PAIRS // SCAN_MATMULS
    half = group_pairs // 2 * R_HEAD_DIM

    groups = [range(g * group_pairs, (g + 1) * group_pairs) for g in range(SCAN_MATMULS)]

    def packed(xs):
        return cat([cat(xs[2 * j:2 * j + 2], 1) for j in range(group_pairs // 2)])

    def pair_tile(res, q):
        return res[(q // 2) * R_HEAD_DIM:(q // 2 + 1) * R_HEAD_DIM, (q % 2) * LANES:(q % 2 + 1) * LANES]

    def head_sums(g, states, t_sa, t_y=None):
        rows = [packed([(s * rowp(nkk_ref, t_sa, p)).astype(BF16) for s, p in zip(states, groups[g])])]
        if t_y is not None:
            rows.append(packed([(s * rowp(r_ref, t_y, p)).astype(BF16) for s, p in zip(states, groups[g])]))
        return jnp.dot(cat(rows), ones2, preferred_element_type=F32)

    def value_columns(sub, i, vc_out):
        xs = xs_ref[sub]
        n = xs.shape[0]
        for j0 in range(0, SCAN_BULK, SCAN_BULK // 2):
            steps = [SCAN_BULK * i + j0 + j for j in range(SCAN_BULK // 2)]
            res = jnp.dot(cat([xs * hot_ref[tt, 0:1, :] for tt in steps]), spread, preferred_element_type=F32)
            for j, tt in enumerate(steps):
                vc_out[tt] = res[j * n:(j + 1) * n]

    def step(sub, tt, vc_in, sas):
        t = sub * SCAN_SUB + tt
        t_next = jnp.minimum(t + 1, tb - 1)
        hit = lane_t == t
        new_sas = []
        for g in range(SCAN_MATMULS):
            states = []
            for q, p in enumerate(groups[g]):
                vg, vq = divmod(p, SCAN_GROUP_PAIRS)
                vc = vc_in[tt, vg * R_HEAD_DIM:(vg + 1) * R_HEAD_DIM, vq * LANES:(vq + 1) * LANES]
                s_new = (s_ref[p] * rowp(w_ref, t, p) + pair_tile(sas[g], q) * rowp(b_ref, t, p)
                         + vc * rowp(k_ref, t, p))
                s_ref[p] = s_new
                states.append(s_new)
            res = head_sums(g, states, t_next, t)
            new_sas.append(res[:half])
            for q, p in enumerate(groups[g]):
                y_ref[p] = jnp.where(hit, pair_tile(res[half:], q), y_ref[p])
        return tuple(new_sas)

    def sub_block(sub, vc_in, vc_out, sas):
        nxt = jnp.minimum(sub + 1, n_sub - 1)

        def four_steps(i, sas):
            sas = step(sub, SCAN_BULK * i, vc_in, sas)
            if n_sub > 1:
                value_columns(nxt, i, vc_out)
            for u in range(1, SCAN_BULK):
                sas = step(sub, SCAN_BULK * i + u, vc_in, sas)
            return sas

        return lax.fori_loop(0, SCAN_SUB // SCAN_BULK, four_steps, sas)

    n_sub = tb // SCAN_SUB
    for i in range(SCAN_SUB // SCAN_BULK):
        value_columns(0, i, vc_ref)
    sas = tuple(head_sums(g, [s_ref[p] for p in groups[g]], 0) for g in range(SCAN_MATMULS))
    if n_sub == 1:
        sub_block(0, vc_ref, vc2_ref, sas)
    else:
        def two_sub_blocks(i, sas):
            return sub_block(2 * i + 1, vc2_ref, vc_ref, sub_block(2 * i, vc_ref, vc2_ref, sas))

        lax.fori_loop(0, n_sub // 2, two_sub_blocks, sas)

    @pl.when(tblk == pl.num_programs(1) - 1)
    def _():
        st_ref[...] = s_ref[...]


def _rwkv_scan(r, w, k, nkk, b, v, s0):
    nb, t_len, _ = r.shape
    tb = min(t_len, SCAN_BLOCK)
    nblk = t_len // tb
    hd = R_HEAD_DIM
    nsub, sb, ng, gp = tb // SCAN_SUB, SCAN_SUB, SCAN_GROUPS, SCAN_GROUP_PAIRS
    s0p = s0.reshape(nb, R_PAIRS, 2, hd, hd).transpose(0, 1, 3, 2, 4).reshape(nb, R_PAIRS, hd, LANES)
    lane2_h = jnp.arange(2 * LANES) // hd
    ones2 = (lane2_h[:, None] == lane2_h[None, :]).astype(BF16)
    src = jnp.arange(2 * LANES) % LANES
    src_q, src_h, src_t = src // (2 * sb), (src // sb) % 2, src % sb
    dst = jnp.arange(gp * LANES)
    dst_q, dst_h = dst // LANES, (dst % LANES) // hd
    spread = ((src_q[:, None] == dst_q[None, :]) & (src_h[:, None] == dst_h[None, :])).astype(BF16)
    hot = jnp.broadcast_to((src_t[None, :] == jnp.arange(sb)[:, None])[:, None, :], (sb, 16, 2 * LANES)).astype(BF16)
    trow = pl.BlockSpec((None, tb, R_PAIRS, LANES), lambda bb, i: (bb, i, 0, 0))
    r, w, k, nkk, b = (a.reshape(nb, t_len, R_PAIRS, LANES) for a in (r, w, k, nkk, b))
    st = pl.BlockSpec((None, R_PAIRS, hd, LANES), lambda bb, i: (bb, 0, 0, 0))
    full = lambda a: pl.BlockSpec(a.shape, lambda bb, i: (0,) * a.ndim)
    ycol_spec = pl.BlockSpec((None, None, R_PAIRS, hd, LANES), lambda bb, i: (bb, i, 0, 0, 0))
    ycol, s_t = pl.pallas_call(
        functools.partial(_rwkv_scan_kernel, tb=tb),
        out_shape=(jax.ShapeDtypeStruct((nb, nblk, R_PAIRS, hd, LANES), F32),
                   jax.ShapeDtypeStruct((nb, R_PAIRS, hd, LANES), F32)),
        grid=(nb, nblk),
        in_specs=[trow, trow, trow, trow, trow, pl.BlockSpec((None, tb, R_WIDTH), lambda bb, i: (bb, i, 0)), st,
                  full(ones2), full(hot), full(spread)],
        out_specs=(ycol_spec, st),
        scratch_shapes=[pltpu.VMEM((R_PAIRS, hd, LANES), F32), pltpu.VMEM((nsub, ng * hd, 2 * LANES), BF16),
                        pltpu.VMEM((sb, ng * hd, gp * LANES), F32), pltpu.VMEM((sb, ng * hd, gp * LANES), F32)],
        compiler_params=_cparams(("parallel", "arbitrary")),
        name="rwkv_scan",
    )(r, w, k, nkk, b, v, s0p, ones2, hot, spread)
    s_t = s_t.reshape(nb, R_PAIRS, hd, 2, hd).transpose(0, 1, 3, 2, 4).reshape(nb, R_HEADS, hd, hd)
    return ycol, s_t


def _rwkv_post_kernel(y_ref, r_ref, k_ref, v_ref, g_ref, lw_ref, lb_ref, rk_ref, ones_ref, o_ref, *, tb):
    ones_bd = ones_ref[...]
    tiles = []
    for p in range(R_PAIRS):
        yt = y_ref[p].T
        tiles.append(jnp.concatenate([yt[0:tb, :], yt[R_HEAD_DIM:R_HEAD_DIM + tb, :]], axis=1))
    y = jnp.concatenate(tiles, axis=1)
    mean = _head_sums(y, ones_bd) * (1.0 / R_HEAD_DIM)
    d = y - mean
    var = _head_sums(d * d, ones_bd) * (1.0 / R_HEAD_DIM)
    yn = d * lax.rsqrt(var + GN_EPS) * lw_ref[...] + lb_ref[...]
    bonus = _head_sums(r_ref[...] * k_ref[...] * rk_ref[...], ones_bd) * v_ref[...]
    o_ref[...] = ((yn + bonus) * g_ref[...]).astype(o_ref.dtype)


def _rwkv_post(ycol, r, k, v, g, lnx_w, lnx_b, r_k, ones_bd):
    m = r.shape[0]
    nblk = ycol.shape[1]
    tb = m // (ycol.shape[0] * nblk)
    row = pl.BlockSpec((tb, R_WIDTH), lambda i: (i, 0))
    full = lambda a: pl.BlockSpec(a.shape, lambda i: (0,) * a.ndim)
    params = (lnx_w, lnx_b, r_k, ones_bd)
    return pl.pallas_call(
        functools.partial(_rwkv_post_kernel, tb=tb),
        out_shape=jax.ShapeDtypeStruct((m, R_WIDTH), BF16),
        grid=(m // tb,),
        in_specs=[pl.BlockSpec((None, None, R_PAIRS, R_HEAD_DIM, LANES), lambda i: (i // nblk, i % nblk, 0, 0, 0))]
        + [row] * 4 + [full(p) for p in params],
        out_specs=row,
        compiler_params=_cparams(("parallel",)),
        name="rwkv_post",
    )(ycol, r, k, v, g, *params)


def _ffn_up_kernel(x_ref, xp_ref, wg_ref, wu_ref, c0_ref, cw_ref, cb_ref, act_ref, tail_ref, *,
                   seq_rows, seqs_per_tile, tiles_per_seq):
    first = lax.rem(pl.program_id(0), tiles_per_seq) == 0
    wg = wg_ref[...].astype(BF16)
    gate_all = jnp.dot(x_ref[...], wg, preferred_element_type=F32)
    up_all = jnp.dot(x_ref[...], wu_ref[...].astype(BF16), preferred_element_type=F32)
    gate_prev = jnp.dot(xp_ref[...], wg, preferred_element_type=F32)
    row = lax.broadcasted_iota(I32, (seq_rows, 1), 0)
    for s in range(seqs_per_tile):
        rows = slice(s * seq_rows, (s + 1) * seq_rows)
        gate = gate_all[rows]
        prev1 = jnp.where(first, c0_ref[s, 1:2, :], gate_prev[7:8, :])
        prev2 = jnp.where(first, c0_ref[s, 0:1, :], gate_prev[6:7, :])
        g_m1 = jnp.where(row == 0, prev1, pltpu.roll(gate, 1, axis=0))
        g_m2 = jnp.where(row == 0, prev2, jnp.where(row == 1, prev1, pltpu.roll(gate, 2, axis=0)))
        conv = cb_ref[...] + g_m2 * cw_ref[0:1, :]
        conv = conv + g_m1 * cw_ref[1:2, :]
        conv = conv + gate * cw_ref[2:3, :]
        act_ref[rows, :] = (conv * jax.nn.sigmoid(conv) * up_all[rows]).astype(act_ref.dtype)
        tail_ref[s] = gate[seq_rows - 8:, :]


def _ffn_up(x, w_in, conv0, conv_w, conv_b, n_batch, t_len, tm=1024, tn=256):
    m, d = x.shape
    d_ff = w_in.shape[1] // 2
    tm = min(m, tm)
    seq_rows = min(t_len, tm)
    seqs_per_tile, tiles_per_seq = tm // seq_rows, t_len // seq_rows
    assert m % tm == 0 and tm % seq_rows == 0 and t_len % seq_rows == 0 and seq_rows % 8 == 0 and d_ff % tn == 0
    n_up = d_ff // tn
    act, tail = pl.pallas_call(
        functools.partial(_ffn_up_kernel, seq_rows=seq_rows, seqs_per_tile=seqs_per_tile, tiles_per_seq=tiles_per_seq),
        out_shape=(jax.ShapeDtypeStruct((m, d_ff), BF16),
                   jax.ShapeDtypeStruct((n_batch * tiles_per_seq, 8, d_ff), F32)),
        grid=(m // tm, n_up),
        in_specs=[pl.BlockSpec((tm, d), lambda i, j: (i, 0)),
                  pl.BlockSpec((8, d), lambda i, j: (jnp.maximum(i * (tm // 8) - 1, 0), 0)),
                  pl.BlockSpec((d, tn), lambda i, j: (0, j)),
                  pl.BlockSpec((d, tn), lambda i, j: (0, j + n_up)),
                  pl.BlockSpec((seqs_per_tile, CONV_W - 1, tn), lambda i, j: (i // tiles_per_seq, 0, j)),
                  pl.BlockSpec((CONV_W, tn), lambda i, j: (0, j)),
                  pl.BlockSpec((1, tn), lambda i, j: (0, j))],
        out_specs=(pl.BlockSpec((tm, tn), lambda i, j: (i, j)),
                   pl.BlockSpec((seqs_per_tile, 8, tn), lambda i, j: (i, 0, j))),
        compiler_params=_cparams(("parallel", "arbitrary")),
        name="ffn_up",
    )(x, x, w_in, w_in, conv0, conv_w, conv_b.reshape(1, d_ff))
    gate_tail = tail.reshape(n_batch, tiles_per_seq, 8, d_ff)[:, -1, 8 - (CONV_W - 1):, :]
    return act, gate_tail


def _pad_rw_cols(a):
    z = lambda n: jnp.zeros(a.shape[:-1] + (n,), a.dtype)
    wd0, ad0, gd0 = 3 * R_WIDTH, 3 * R_WIDTH + DECAY_LORA, 3 * R_WIDTH + DECAY_LORA + AAA_LORA
    return jnp.concatenate([a[..., :wd0], a[..., wd0:ad0], z(LORA_PAD - DECAY_LORA), a[..., ad0:gd0],
                            z(LORA_PAD - AAA_LORA), a[..., gd0:]], axis=-1)


def _unpad_rw_cols(a):
    return jnp.concatenate([a[..., :RW_WD0 + DECAY_LORA], a[..., RW_AD0:RW_AD0 + AAA_LORA], a[..., RW_GD0:]], axis=-1)


def _prep_weights(norm_mix_g, w_in, rwkv_mu, rwkv_w0, rwkv_w2, rwkv_a0, rwkv_a2, rwkv_g2, rwkv_k_k, rwkv_k_a,
                  rwkv_r_k, rwkv_lnx_w, rwkv_lnx_b, w_out, norm_ffn_g, ffn_w_in, ffn_conv_w, ffn_conv_b,
                  ffn_w_down, norm_final_g, l):
    d = w_in.shape[1]
    w_att = jnp.concatenate([w_in[l][:, :ATT_USED].astype(BF16), jnp.zeros((d, ATT_COLS - ATT_USED), BF16)], axis=1)
    w_rw = _pad_rw_cols(w_in[l][:, ATT_USED:].astype(BF16))
    row = lambda a: a.reshape(1, -1).astype(F32)
    pad_rows = lambda a, n: jnp.concatenate([a, jnp.zeros((n - a.shape[0], a.shape[1]), a.dtype)], axis=0)
    lane_h = jnp.arange(LANES) // R_HEAD_DIM
    w2_pad = pad_rows(rwkv_w2[l].astype(F32), LORA_PAD)
    w2_hi = w2_pad.astype(BF16)
    w2_mid = (w2_pad - w2_hi.astype(F32)).astype(BF16)
    return dict(
        norm_mix_g=norm_mix_g[l], w_att=w_att, w_rw=w_rw,
        mu=_pad_rw_cols(row(rwkv_mu[l])), w0=row(rwkv_w0[l]), a0=row(rwkv_a0[l]),
        k_k=row(rwkv_k_k[l]), k_a=row(rwkv_k_a[l]),
        w2=jnp.concatenate([w2_hi, w2_mid, w2_hi], axis=0), a2=pad_rows(rwkv_a2[l], LORA_PAD).astype(BF16),
        g2=rwkv_g2[l].astype(BF16),
        r_k=row(rwkv_r_k[l]), lnx_w=row(rwkv_lnx_w[l]), lnx_b=row(rwkv_lnx_b[l]),
        ones_bd=jnp.concatenate([lane_h[:, None] == lane_h[None, :]] * 2, axis=0).astype(BF16),
        w_out_a=w_out[l][:A_WIDTH].astype(BF16), w_out_r=w_out[l][A_WIDTH:].astype(BF16),
        norm_ffn_g=norm_ffn_g[l], ffn_w_in=ffn_w_in[l], conv_w=ffn_conv_w[l], conv_b=ffn_conv_b[l],
        ffn_w_down=ffn_w_down[l].astype(BF16), norm_final_g=norm_final_g,
    )


def _trunk(x, past_k, past_v, past_ik, s0, shift0, conv0, wt):
    nb, t_len, d = x.shape
    m = nb * t_len
    p_len = 0 if past_k is None else past_k.shape[1]
    n_keys = p_len + t_len
    x2 = x.reshape(m, d)

    h = _rmsnorm(x2, wt["norm_mix_g"], BF16)
    z_att = _matmul([h], [wt["w_att"]], tm=1024, name="proj_att")
    z_rw = _matmul([h], [wt["w_rw"]], tm=1024, name="proj_rw")

    q_bf, k_f, k_bf, v_f, v_bf, iq_bf, ikw, ik_bf = _rope_split(z_att, t_len, p_len)
    tk = min(-(-n_keys // LANES) * LANES, 512)
    lp = -(-n_keys // tk) * tk

    def with_past(new, past, width):
        new = new.reshape(nb, t_len, width)
        parts = [new] if past is None else [past.reshape(nb, p_len, width).astype(BF16), new]
        if lp > n_keys:
            parts.append(jnp.zeros((nb, lp - n_keys, width), BF16))
        return parts[0] if len(parts) == 1 else jnp.concatenate(parts, axis=1)

    attn = _dsa(q_bf, iq_bf, ikw, with_past(k_bf, past_k, KV_W), with_past(v_bf, past_v, KV_W),
                with_past(ik_bf, past_ik, IDX_DIM), n_batch=nb, t_len=t_len, pos0=p_len, n_keys=n_keys)

    z_rw3 = z_rw.reshape(nb, t_len, RW_COLS)
    r, w, k2, v2, nkk, b, g = _rwkv_pre(z_rw3, _pad_rw_cols(shift0.astype(F32)), wt["mu"], wt["w0"], wt["a0"],
                                       wt["k_k"], wt["k_a"], wt["w2"], wt["a2"], wt["g2"], wt["ones_bd"])
    ycol, s_t = _rwkv_scan(r, w, k2, nkk, b, v2, s0.astype(F32))
    flat = lambda a: a.reshape(m, R_WIDTH)
    rw = _rwkv_post(ycol, flat(r), flat(k2), flat(v2), flat(g), wt["lnx_w"], wt["lnx_b"], wt["r_k"], wt["ones_bd"])

    x1 = _matmul([attn, rw], [wt["w_out_a"], wt["w_out_r"]], res=x2, tm=1024, name="out_proj")
    hf = _rmsnorm(x1, wt["norm_ffn_g"], BF16)
    act, conv_t = _ffn_up(hf, wt["ffn_w_in"], conv0.astype(F32), wt["conv_w"], wt["conv_b"], nb, t_len)
    x3 = _matmul([act], [wt["ffn_w_down"]], res=x1, name="ffn_down")
    y_out = _rmsnorm(x3, wt["norm_final_g"], F32).reshape(nb, t_len, d)

    shift_t = _unpad_rw_cols(z_rw3[:, -1:])
    caches = (k_f.reshape(nb, t_len, A_KV_HEADS, A_HEAD_DIM)[None], v_f.reshape(nb, t_len, A_KV_HEADS, A_HEAD_DIM)[None],
              ikw[:, :IDX_DIM].reshape(nb, t_len, IDX_DIM)[None], s_t[None], shift_t[None], conv_t[None])
    return y_out, caches


def kernel(x_prompt, x_sample, cache_k, cache_v, cache_idx_k, state_rwkv, state_rwkv_shift, state_ffn_conv, norm_mix_g, w_in, rwkv_mu, rwkv_w0, rwkv_w2, rwkv_a0, rwkv_a2, rwkv_g2, rwkv_k_k, rwkv_k_a, rwkv_r_k, rwkv_lnx_w, rwkv_lnx_b, w_out, norm_ffn_g, ffn_w_in, ffn_conv_w, ffn_conv_b, ffn_w_down, norm_final_g):
    assert w_in.shape[0] == 1, "single-layer trunk"
    wt = _prep_weights(norm_mix_g, w_in, rwkv_mu, rwkv_w0, rwkv_w2, rwkv_a0, rwkv_a2, rwkv_g2, rwkv_k_k, rwkv_k_a,
                       rwkv_r_k, rwkv_lnx_w, rwkv_lnx_b, w_out, norm_ffn_g, ffn_w_in, ffn_conv_w, ffn_conv_b,
                       ffn_w_down, norm_final_g, 0)
    bp = x_prompt.shape[0]
    d_ff = ffn_conv_w.shape[-1]
    y_p, c_p = _trunk(x_prompt, None, None, None,
                      jnp.zeros((bp, R_HEADS, R_HEAD_DIM, R_HEAD_DIM), F32), jnp.zeros((bp, 1, RWKV_COLS), F32),
                      jnp.zeros((bp, CONV_W - 1, d_ff), F32), wt)
    y_s, c_s = _trunk(x_sample, cache_k[0], cache_v[0], cache_idx_k[0], state_rwkv[0], state_rwkv_shift[0],
                      state_ffn_conv[0], wt)
    return (y_p, y_s) + c_p + c_s
```

```python
import functools

import jax
import jax.numpy as jnp
from jax import lax
from jax.experimental import pallas as pl
from jax.experimental.pallas import tpu as pltpu

F32 = jnp.float32
BF16 = jnp.bfloat16
I32 = jnp.int32

CHUNK = 64
A_HEADS = 16
A_KV_HEADS = 4
A_HEAD_DIM = 128
A_GROUP = A_HEADS // A_KV_HEADS
A_WIDTH = A_HEADS * A_HEAD_DIM
KV_W = A_KV_HEADS * A_HEAD_DIM
IDX_HEADS = 16
IDX_DIM = 64
IQ_W = IDX_HEADS * IDX_DIM
TOPK_MAX = 256
ROPE_THETA = 500000.0
ROPE_FRAC = 4
A_SCALE = A_HEAD_DIM ** -0.5
Q_SCALE = A_SCALE * 1.4426950408889634
IDX_SCALE = (IDX_HEADS ** -0.5) * (IDX_DIM ** -0.5)
R_HEAD_DIM = 64
R_WIDTH = 2048
R_HEADS = R_WIDTH // R_HEAD_DIM
R_PAIRS = R_HEADS // 2
DECAY_LORA = 96
AAA_LORA = 96
GATE_LORA = 256
RWKV_COLS = 3 * R_WIDTH + DECAY_LORA + AAA_LORA + GATE_LORA
GN_EPS = 6.4e-4
CONV_W = 3
RMS_EPS = 1e-6

LANES = 128
VMEM_LIMIT = 56 * 1024 * 1024

ATT_Q0, ATT_K0, ATT_V0, ATT_IQ0, ATT_IK0 = 0, A_WIDTH, A_WIDTH + KV_W, A_WIDTH + 2 * KV_W, A_WIDTH + 2 * KV_W + IQ_W
ATT_USED = ATT_IK0 + IDX_DIM + IDX_HEADS
ATT_COLS = 4608
LORA_PAD = 128
RW_WD0 = 3 * R_WIDTH
RW_AD0 = RW_WD0 + LORA_PAD
RW_GD0 = RW_AD0 + LORA_PAD
RW_COLS = RW_GD0 + GATE_LORA
DSA_TQ, DSA_TK = 256, 512
COUNT_ROWS = 128
SCAN_BLOCK = 64
SCAN_SUB = 16
SCAN_GROUP_PAIRS = LANES // (2 * SCAN_SUB)
SCAN_GROUPS = R_PAIRS // SCAN_GROUP_PAIRS
SCAN_MATMULS = 4
SCAN_BULK = 4
INT_MIN = -2 ** 31
NEG_BIG = -1e30


def _cparams(sem):
    return pltpu.CompilerParams(dimension_semantics=sem, vmem_limit_bytes=VMEM_LIMIT)


def _rmsnorm_kernel(x_ref, g_ref, o_ref):
    x = x_ref[...]
    y = x * lax.rsqrt(jnp.mean(x * x, axis=-1, keepdims=True) + RMS_EPS)
    o_ref[...] = (y * g_ref[...]).astype(o_ref.dtype)


def _rmsnorm(x, g, out_dtype):
    m, d = x.shape
    tm = min(m, 256)
    return pl.pallas_call(
        _rmsnorm_kernel,
        out_shape=jax.ShapeDtypeStruct((m, d), out_dtype),
        grid=(m // tm,),
        in_specs=[pl.BlockSpec((tm, d), lambda i: (i, 0)), pl.BlockSpec((1, d), lambda i: (0, 0))],
        out_specs=pl.BlockSpec((tm, d), lambda i: (i, 0)),
        compiler_params=_cparams(("parallel",)),
        name="rmsnorm",
    )(x, g.reshape(1, d).astype(F32))


def _mm_kernel(*refs, n_pairs, has_res):
    o_ref = refs[-1]
    acc = jnp.dot(refs[0][...], refs[n_pairs][...], preferred_element_type=F32)
    for p in range(1, n_pairs):
        acc = acc + jnp.dot(refs[p][...], refs[n_pairs + p][...], preferred_element_type=F32)
    if has_res:
        acc = refs[2 * n_pairs][...] + acc
    o_ref[...] = acc.astype(o_ref.dtype)


def _matmul(a_list, b_list, res=None, tm=512, tn=512, name="matmul"):
    m = a_list[0].shape[0]
    n = b_list[0].shape[1]
    tm = min(tm, m)
    tn = min(tn, n)
    assert m % tm == 0 and n % tn == 0, (m, n, tm, tn)
    in_specs = [pl.BlockSpec((tm, a.shape[1]), lambda i, j: (i, 0)) for a in a_list]
    in_specs += [pl.BlockSpec((b.shape[0], tn), lambda i, j: (0, j)) for b in b_list]
    args = list(a_list) + list(b_list)
    if res is not None:
        in_specs.append(pl.BlockSpec((tm, tn), lambda i, j: (i, j)))
        args.append(res)
    return pl.pallas_call(
        functools.partial(_mm_kernel, n_pairs=len(a_list), has_res=res is not None),
        out_shape=jax.ShapeDtypeStruct((m, n), F32),
        grid=(m // tm, n // tn),
        in_specs=in_specs,
        out_specs=pl.BlockSpec((tm, tn), lambda i, j: (i, j)),
        compiler_params=_cparams(("parallel", "arbitrary")),
        name=name,
    )(*args)


def _rope_tile(x, cos, sin, half, d_in_head):
    lo = d_in_head < half
    hi = (d_in_head >= half) & (d_in_head < 2 * half)
    c = jnp.where(lo | hi, cos, 1.0)
    s_up = jnp.where(lo, -sin, 0.0)
    s_dn = jnp.where(hi, sin, 0.0)
    x_up = pltpu.roll(x, LANES - half, axis=1)
    x_dn = pltpu.roll(x, half, axis=1)
    return x * c + x_up * s_up + x_dn * s_dn


def _rope_kernel(z_ref, invf_ref, q_ref, kf_ref, kb_ref, vf_ref, vb_ref, iq_ref, ikw_ref, ikb_ref, *, tm, t_len, pos0):
    i = pl.program_id(0)
    row = lax.broadcasted_iota(I32, (tm, LANES), 0) + i * tm
    pos = (pos0 + lax.rem(row, t_len)).astype(F32)
    lane = lax.broadcasted_iota(I32, (tm, LANES), 1)
    ang = pos * invf_ref[0:1, :]
    cos_a, sin_a = jnp.cos(ang), jnp.sin(ang)
    half_a = A_HEAD_DIM // ROPE_FRAC // 2
    for h in range(A_HEADS):
        x = z_ref[:, ATT_Q0 + h * LANES:ATT_Q0 + (h + 1) * LANES]
        q_ref[:, h * LANES:(h + 1) * LANES] = (_rope_tile(x, cos_a, sin_a, half_a, lane) * Q_SCALE).astype(q_ref.dtype)
    for h in range(A_KV_HEADS):
        x = z_ref[:, ATT_K0 + h * LANES:ATT_K0 + (h + 1) * LANES]
        y = _rope_tile(x, cos_a, sin_a, half_a, lane)
        kf_ref[:, h * LANES:(h + 1) * LANES] = y
        kb_ref[:, h * LANES:(h + 1) * LANES] = y.astype(kb_ref.dtype)
    v = z_ref[:, ATT_V0:ATT_V0 + KV_W]
    vf_ref[...] = v
    vb_ref[...] = v.astype(vb_ref.dtype)
    ang = pos * invf_ref[1:2, :]
    cos_i, sin_i = jnp.cos(ang), jnp.sin(ang)
    half_i = IDX_DIM // ROPE_FRAC // 2
    d_i = lane & (IDX_DIM - 1)
    for h in range(IQ_W // LANES):
        x = z_ref[:, ATT_IQ0 + h * LANES:ATT_IQ0 + (h + 1) * LANES]
        iq_ref[:, h * LANES:(h + 1) * LANES] = _rope_tile(x, cos_i, sin_i, half_i, d_i).astype(iq_ref.dtype)
    x = z_ref[:, ATT_IK0:ATT_IK0 + LANES]
    d_k = jnp.where(lane < IDX_DIM, lane, IDX_DIM)
    y = _rope_tile(x, cos_i, sin_i, half_i, d_k)
    ikw_ref[...] = y
    ikb_ref[...] = y[:, :IDX_DIM].astype(ikb_ref.dtype)


def _rope_split(z_att, t_len, pos0):
    m = z_att.shape[0]
    tm = min(m, 256)
    lane = jnp.arange(LANES)
    rd_a = A_HEAD_DIM // ROPE_FRAC
    rd_i = IDX_DIM // ROPE_FRAC
    invf_a = ROPE_THETA ** (-((lane % (rd_a // 2)).astype(F32) * 2.0 / rd_a))
    invf_i = ROPE_THETA ** (-((lane % (rd_i // 2)).astype(F32) * 2.0 / rd_i))
    invf = jnp.zeros((8, LANES), F32).at[0].set(invf_a).at[1].set(invf_i)
    row_spec = lambda w: pl.BlockSpec((tm, w), lambda i: (i, 0))
    shp = lambda w, dt: jax.ShapeDtypeStruct((m, w), dt)
    return pl.pallas_call(
        functools.partial(_rope_kernel, tm=tm, t_len=t_len, pos0=pos0),
        out_shape=(shp(A_WIDTH, BF16), shp(KV_W, F32), shp(KV_W, BF16), shp(KV_W, F32), shp(KV_W, BF16),
                   shp(IQ_W, BF16), shp(LANES, F32), shp(IDX_DIM, BF16)),
        grid=(m // tm,),
        in_specs=[row_spec(ATT_COLS), pl.BlockSpec((8, LANES), lambda i: (0, 0))],
        out_specs=(row_spec(A_WIDTH), row_spec(KV_W), row_spec(KV_W), row_spec(KV_W), row_spec(KV_W),
                   row_spec(IQ_W), row_spec(LANES), row_spec(IDX_DIM)),
        compiler_params=_cparams(("parallel",)),
        name="rope_split",
    )(z_att, invf)


def _tile(x, n, axis):
    return x if n == 1 else jnp.concatenate([x] * n, axis=axis)


def _sortable(score):
    u = lax.bitcast_convert_type(score, I32)
    return jnp.where(u < 0, u ^ jnp.int32(0x7FFFFFFF), u)


def _dsa_kernel(*refs, tq, tk, pos0, n_keys, topk):
    i = pl.program_id(1)
    if pos0 > 0:
        (q_ref, iq_ref, ikw_ref, kn_ref, vn_ref, ikn_ref, pk_ref, pv_ref, pik_ref, o_ref,
         key_ref, iwb_ref, qs_ref, m_ref, l_ref, acc_ref, k_ref, v_ref, ik_ref) = refs
        t_new = kn_ref.shape[0]

        @pl.when(i == 0)
        def _():
            for new_ref, past_ref, all_ref in ((kn_ref, pk_ref, k_ref), (vn_ref, pv_ref, v_ref), (ikn_ref, pik_ref, ik_ref)):
                for r0 in range(0, pos0, DSA_TK):
                    r1 = min(r0 + DSA_TK, pos0)
                    all_ref[r0:r1, :] = past_ref[r0:r1, :].astype(all_ref.dtype)
                all_ref[pos0:pos0 + t_new, :] = new_ref[...]
                tail = all_ref.shape[0] - pos0 - t_new
                if tail:
                    all_ref[pos0 + t_new:, :] = jnp.zeros((tail, all_ref.shape[1]), all_ref.dtype)
    else:
        (q_ref, iq_ref, ikw_ref, k_ref, v_ref, ik_ref, o_ref,
         key_ref, iwb_ref, qs_ref, m_ref, l_ref, acc_ref) = refs
    q0 = pos0 + i * tq
    kmax = jnp.minimum((lax.div(q0 + tq - 1, CHUNK) + 1) * CHUNK, n_keys)
    nkb = lax.div(kmax + tk - 1, tk)
    n_rep = tk // LANES
    nt_dims = (((1,), (1,)), ((), ()))

    for h in range(IDX_HEADS):
        iwb_ref[h] = jnp.broadcast_to(ikw_ref[:, IDX_DIM + h:IDX_DIM + h + 1], (tq, LANES))
    for h in range(A_HEADS):
        qs_ref[h * tq:(h + 1) * tq, :] = q_ref[:, h * LANES:(h + 1) * LANES]

    def score_block(kb, carry):
        koff = pl.multiple_of(kb * tk, tk)
        ikb = ik_ref[pl.ds(koff, tk), :]
        acc = jnp.zeros((tq, tk), F32)
        for h in range(IDX_HEADS):
            d = lax.dot_general(iq_ref[:, h * IDX_DIM:(h + 1) * IDX_DIM], ikb, nt_dims, preferred_element_type=F32)
            acc = acc + jnp.maximum(d, 0.0) * _tile(iwb_ref[h], n_rep, 1)
        kpos = koff + lax.broadcasted_iota(I32, (tq, tk), 1)
        qpos = q0 + lax.broadcasted_iota(I32, (tq, tk), 0)
        adm = (lax.shift_right_logical(kpos, 6) <= lax.shift_right_logical(qpos, 6)) & (kpos < n_keys)
        key_ref[:, pl.ds(koff, tk)] = jnp.where(adm, _sortable(acc * IDX_SCALE), jnp.int32(INT_MIN))
        return carry

    lax.fori_loop(0, nkb, score_block, 0)

    def count(pred, *row_args):
        cr = min(tq, COUNT_ROWS)

        def chunk(r0):
            def body(kb, cnt):
                koff = pl.multiple_of(kb * tk, tk)
                keys = key_ref[r0:r0 + cr, pl.ds(koff, tk)]
                kpos = koff + lax.broadcasted_iota(I32, (cr, tk), 1)
                hit = jnp.where(pred(keys, kpos, *[a[r0:r0 + cr] for a in row_args]), 1.0, 0.0)
                for c in range(n_rep):
                    cnt = cnt + hit[:, c * LANES:(c + 1) * LANES]
                return cnt
            cnt = lax.fori_loop(0, nkb, body, jnp.zeros((cr, LANES), F32))
            return jnp.broadcast_to(jnp.sum(cnt, axis=-1, keepdims=True), (cr, LANES))

        return jnp.concatenate([chunk(r0) for r0 in range(0, tq, cr)], axis=0)

    def wide(x):
        return _tile(x, n_rep, 1)

    def bit_step(it, tu):
        cand_u = tu | lax.shift_left(jnp.int32(1), 31 - it)
        cand_s = wide(cand_u ^ jnp.int32(INT_MIN))
        cnt = count(lambda keys, kpos, c: keys >= c, cand_s)
        return jnp.where(cnt >= topk, cand_u, tu)

    tu = lax.fori_loop(0, 32, bit_step, jnp.zeros((tq, LANES), I32))
    thr = jnp.maximum(tu ^ jnp.int32(INT_MIN), jnp.int32(INT_MIN + 1))
    thr_w = wide(thr)

    n_ge = count(lambda keys, kpos, t: keys >= t, thr_w)
    n_gt = count(lambda keys, kpos, t: keys > t, thr_w)
    excess = n_ge > topk

    @pl.when(jnp.max(jnp.where(excess, 1.0, 0.0)) > 0.0)
    def _():
        need = topk - n_gt

        idx_bits = int(key_ref.shape[1]).bit_length()

        def idx_step(it, jm):
            cand = wide(jm | lax.shift_left(jnp.int32(1), idx_bits - 1 - it))
            cnt = count(lambda keys, kpos, t, c: (keys == t) & (kpos < c), thr_w, cand)
            return jnp.where(cnt < need, cand[:, :LANES], jm)

        jm = lax.fori_loop(0, idx_bits, idx_step, jnp.zeros((tq, LANES), I32))
        jm_w = wide(jnp.where(excess, jm, jnp.int32(2 ** 31 - 1)))

        def drop(kb, carry):
            koff = pl.multiple_of(kb * tk, tk)
            keys = key_ref[:, pl.ds(koff, tk)]
            kpos = koff + lax.broadcasted_iota(I32, (tq, tk), 1)
            key_ref[:, pl.ds(koff, tk)] = jnp.where((keys == thr_w) & (kpos > jm_w), jnp.int32(INT_MIN), keys)
            return carry

        lax.fori_loop(0, nkb, drop, 0)

    m_ref[...] = jnp.full(m_ref.shape, NEG_BIG, F32)
    l_ref[...] = jnp.zeros(l_ref.shape, F32)
    acc_ref[...] = jnp.zeros(acc_ref.shape, F32)
    rows = A_GROUP * tq

    def attend(kb, carry):
        koff = pl.multiple_of(kb * tk, tk)
        bias = _tile(jnp.where(key_ref[:, pl.ds(koff, tk)] >= thr_w, 0.0, NEG_BIG), A_GROUP, 0)
        for n in range(A_KV_HEADS):
            r0 = n * rows
            kn = k_ref[pl.ds(koff, tk), n * LANES:(n + 1) * LANES]
            vn = v_ref[pl.ds(koff, tk), n * LANES:(n + 1) * LANES]
            s = lax.dot_general(qs_ref[r0:r0 + rows, :], kn, nt_dims, preferred_element_type=F32) + bias
            m_prev = m_ref[r0:r0 + rows, :]
            m_new = jnp.maximum(m_prev, jnp.max(s, axis=-1, keepdims=True))
            alpha = jnp.exp2(m_prev - m_new)
            p = jnp.exp2(s - _tile(m_new, n_rep, 1))
            l_ref[r0:r0 + rows, :] = alpha * l_ref[r0:r0 + rows, :] + jnp.sum(p, axis=-1, keepdims=True)
            acc_ref[r0:r0 + rows, :] = alpha * acc_ref[r0:r0 + rows, :] + jnp.dot(
                p.astype(vn.dtype), vn, preferred_element_type=F32)
            m_ref[r0:r0 + rows, :] = m_new
        return carry

    lax.fori_loop(0, nkb, attend, 0)
    for h in range(A_HEADS):
        o_ref[:, h * LANES:(h + 1) * LANES] = (
            acc_ref[h * tq:(h + 1) * tq, :] / l_ref[h * tq:(h + 1) * tq, :]).astype(o_ref.dtype)


def _dsa(q_bf, iq_bf, ikw, k_new, v_new, ik_new, past, *, n_batch, t_len):
    pos0 = 0 if past is None else past[0].shape[1]
    n_keys = pos0 + t_len
    tq = min(t_len, DSA_TQ)
    lp = -(-n_keys // LANES) * LANES
    tk = lp if tq * lp <= DSA_TQ * DSA_TK else DSA_TK
    lp = -(-lp // tk) * tk
    assert t_len % tq == 0 and (past is None or (lp == tk and pos0 % 16 == 0 and t_len % 16 == 0)) \
        and (past is not None or lp == n_keys)
    nq = t_len // tq
    topk = min(TOPK_MAX, n_keys // 4)
    qrow = lambda w: pl.BlockSpec((tq, w), lambda b, i: (b * nq + i, 0))
    new3 = lambda a: a.reshape(n_batch, t_len, a.shape[-1])
    whole = lambda rows, w, **kw: pl.BlockSpec((None, rows, w), lambda b, i: (b, 0, 0), **kw)
    in_specs = [qrow(A_WIDTH), qrow(IQ_W), qrow(LANES)]
    args = [q_bf, iq_bf, ikw, new3(k_new), new3(v_new), new3(ik_new)]
    scratch = [
        pltpu.VMEM((tq, lp), I32),
        pltpu.VMEM((IDX_HEADS, tq, LANES), F32),
        pltpu.VMEM((A_HEADS * tq, LANES), BF16),
        pltpu.VMEM((A_HEADS * tq, LANES), F32),
        pltpu.VMEM((A_HEADS * tq, LANES), F32),
        pltpu.VMEM((A_HEADS * tq, LANES), F32),
    ]
    if past is None:
        in_specs += [whole(lp, w, pipeline_mode=pl.Buffered(1)) for w in (KV_W, KV_W, IDX_DIM)]
    else:
        in_specs += [whole(t_len, w) for w in (KV_W, KV_W, IDX_DIM)] + [whole(pos0, w) for w in (KV_W, KV_W, IDX_DIM)]
        args += list(past)
        scratch += [pltpu.VMEM((lp, w), BF16) for w in (KV_W, KV_W, IDX_DIM)]
    return pl.pallas_call(
        functools.partial(_dsa_kernel, tq=tq, tk=tk, pos0=pos0, n_keys=n_keys, topk=float(topk)),
        out_shape=jax.ShapeDtypeStruct((n_batch * t_len, A_WIDTH), BF16),
        grid=(n_batch, nq),
        in_specs=in_specs,
        out_specs=qrow(A_WIDTH),
        scratch_shapes=scratch,
        compiler_params=_cparams(("parallel", "arbitrary")),
        name="dsa",
    )(*args)


def _head_sums(x, ones2):
    n = x.shape[1] // LANES
    tm = x.shape[0]
    hi, mid = _split2(jnp.concatenate([x[:, c * LANES:(c + 1) * LANES] for c in range(n)], axis=0))
    s = jnp.dot(jnp.concatenate([hi, mid], axis=1), ones2, preferred_element_type=F32)
    return jnp.concatenate([s[c * tm:(c + 1) * tm, :] for c in range(n)], axis=1)


def _rwkv_pre_kernel(z_ref, zp_ref, z0_ref, mu_ref, w0_ref, a0_ref, kk_ref, ka_ref, w2_ref, a2_ref, g2_ref,
                     ones_ref, r_out, w_out, k_out, v_out, nkk_out, b_out, g_out, *, tm):
    i = pl.program_id(1)
    row = lax.broadcasted_iota(I32, (tm, 1), 0)

    def mixed(c0, width):
        z = z_ref[:, c0:c0 + width]
        first = jnp.where(i == 0, z0_ref[:, c0:c0 + width], zp_ref[7:8, c0:c0 + width])
        shifted = jnp.where(row == 0, first, pltpu.roll(z, 1, axis=0))
        return z + (shifted - z) * mu_ref[:, c0:c0 + width]

    r = mixed(0, R_WIDTH)
    k = mixed(R_WIDTH, R_WIDTH)
    v = mixed(2 * R_WIDTH, R_WIDTH)
    wd = mixed(RW_WD0, LORA_PAD)
    ad = mixed(RW_AD0, LORA_PAD)
    gd = mixed(RW_GD0, GATE_LORA)
    lora = lambda x, w_ref: jnp.dot(x.astype(BF16), w_ref[...], preferred_element_type=F32)
    t_hi, t_mid = _split2(jnp.tanh(wd))
    y = -(w0_ref[...] + jnp.dot(jnp.concatenate([t_hi, t_hi, t_mid], axis=1), w2_ref[...], preferred_element_type=F32))
    softplus = jnp.maximum(y, 0.0) + jnp.log(1.0 + jnp.exp(-jnp.abs(y)))
    decay = jnp.exp(-jnp.exp(-softplus - 0.5))
    a = jax.nn.sigmoid(a0_ref[...] + lora(ad, a2_ref))
    g = lora(jax.nn.sigmoid(gd), g2_ref)
    kk = k * kk_ref[...]
    kk = kk / jnp.maximum(jnp.sqrt(_head_sums(kk * kk, ones_ref[...])), 1e-12)
    r_out[...] = r
    w_out[...] = decay
    k_out[...] = k * (1.0 + (a - 1.0) * ka_ref[...])
    v_out[...] = v
    nkk_out[...] = -kk
    b_out[...] = kk * a
    g_out[...] = g


def _rwkv_pre(z_rw, z0, mu, w0, a0, k_k, k_a, w2, a2, g2, ones_bd):
    nb, t_len, _ = z_rw.shape
    tm = min(t_len, 128)
    zrow = pl.BlockSpec((None, tm, RW_COLS), lambda b, i: (b, i, 0))
    zprev = pl.BlockSpec((None, 8, RW_COLS), lambda b, i: (b, jnp.maximum(i * (tm // 8) - 1, 0), 0))
    full = lambda a: pl.BlockSpec(a.shape, lambda b, i: (0,) * a.ndim)
    orow = pl.BlockSpec((None, tm, R_WIDTH), lambda b, i: (b, i, 0))
    params = (mu, w0, a0, k_k, k_a, w2, a2, g2, ones_bd)
    return pl.pallas_call(
        functools.partial(_rwkv_pre_kernel, tm=tm),
        out_shape=tuple(jax.ShapeDtypeStruct((nb, t_len, R_WIDTH), F32) for _ in range(7)),
        grid=(nb, t_len // tm),
        in_specs=[zrow, zprev, pl.BlockSpec((None, 1, RW_COLS), lambda b, i: (b, 0, 0))] + [full(p) for p in params],
        out_specs=tuple(orow for _ in range(7)),
        compiler_params=_cparams(("parallel", "arbitrary")),
        name="rwkv_pre",
    )(z_rw, z_rw, z0, *params)


def _split2(x):
    hi = x.astype(BF16)
    return hi, (x - hi.astype(F32)).astype(BF16)


def _rwkv_scan_kernel(r_ref, w_ref, k_ref, nkk_ref, b_ref, v_ref, s0_ref, ones2_ref, hot_ref, spread_ref,
                      y_ref, st_ref, s_ref, xs_ref, vc_ref, vc2_ref, *, tb):
    tblk = pl.program_id(1)
    cat = lambda xs, ax=0: jnp.concatenate(xs, axis=ax)

    @pl.when(tblk == 0)
    def _():
        s_ref[...] = s0_ref[...]

    pad = jnp.zeros((SCAN_BLOCK - tb, LANES), F32)
    v_cols = []
    for p in range(R_PAIRS):
        vp = v_ref[:, p * LANES:(p + 1) * LANES]
        v_cols.append((cat([vp, pad]) if tb < SCAN_BLOCK else vp).T)
    for sub in range(tb // SCAN_SUB):
        for g in range(SCAN_GROUPS):
            x = cat([v_cols[SCAN_GROUP_PAIRS * g + q][h * R_HEAD_DIM:(h + 1) * R_HEAD_DIM,
                                                      sub * SCAN_SUB:(sub + 1) * SCAN_SUB]
                     for q in range(SCAN_GROUP_PAIRS) for h in range(2)], 1)
            hi, mid = _split2(x)
            xs_ref[sub, g * R_HEAD_DIM:(g + 1) * R_HEAD_DIM, :] = cat([hi, mid], 1)
    ones2 = ones2_ref[...]
    spread = spread_ref[...]
    y_ref[...] = jnp.zeros(y_ref.shape, F32)
    lane_t = lax.rem(lax.broadcasted_iota(I32, (R_HEAD_DIM, LANES), 1), R_HEAD_DIM)
    rowp = lambda ref, t, p: ref[t, p:p + 1, :]
    group_pairs = R_PAIRS // SCAN_MATMULS
    half = group_pairs // 2 * R_HEAD_DIM

    groups = [range(g * group_pairs, (g + 1) * group_pairs) for g in range(SCAN_MATMULS)]

    def packed(xs):
        return cat([cat(xs[2 * j:2 * j + 2], 1) for j in range(group_pairs // 2)])

    def pair_tile(res, q):
        return res[(q // 2) * R_HEAD_DIM:(q // 2 + 1) * R_HEAD_DIM, (q % 2) * LANES:(q % 2 + 1) * LANES]

    def head_sums(g, states, t_sa, t_y=None):
        rows = [packed([(s * rowp(nkk_ref, t_sa, p)).astype(BF16) for s, p in zip(states, groups[g])])]
        if t_y is not None:
            rows.append(packed([(s * rowp(r_ref, t_y, p)).astype(BF16) for s, p in zip(states, groups[g])]))
        return jnp.dot(cat(rows), ones2, preferred_element_type=F32)

    def value_columns(sub, i, vc_out):
        xs = xs_ref[sub]
        n = xs.shape[0]
        for j0 in range(0, SCAN_BULK, SCAN_BULK // 2):
            steps = [SCAN_BULK * i + j0 + j for j in range(SCAN_BULK // 2)]
            res = jnp.dot(cat([xs * hot_ref[tt, 0:1, :] for tt in steps]), spread, preferred_element_type=F32)
            for j, tt in enumerate(steps):
                vc_out[tt] = res[j * n:(j + 1) * n]

    def step(sub, tt, vc_in, sas):
        t = sub * SCAN_SUB + tt
        t_next = jnp.minimum(t + 1, tb - 1)
        hit = lane_t == t
        new_sas = []
        for g in range(SCAN_MATMULS):
            states = []
            for q, p in enumerate(groups[g]):
                vg, vq = divmod(p, SCAN_GROUP_PAIRS)
                vc = vc_in[tt, vg * R_HEAD_DIM:(vg + 1) * R_HEAD_DIM, vq * LANES:(vq + 1) * LANES]
                s_new = (s_ref[p] * rowp(w_ref, t, p) + pair_tile(sas[g], q) * rowp(b_ref, t, p)
                         + vc * rowp(k_ref, t, p))
                s_ref[p] = s_new
                states.append(s_new)
            res = head_sums(g, states, t_next, t)
            new_sas.append(res[:half])
            for q, p in enumerate(groups[g]):
                y_ref[p] = jnp.where(hit, pair_tile(res[half:], q), y_ref[p])
        return tuple(new_sas)

    def sub_block(sub, vc_in, vc_out, sas):
        nxt = jnp.minimum(sub + 1, n_sub - 1)

        def four_steps(i, sas):
            sas = step(sub, SCAN_BULK * i, vc_in, sas)
            if n_sub > 1:
                value_columns(nxt, i, vc_out)
            for u in range(1, SCAN_BULK):
                sas = step(sub, SCAN_BULK * i + u, vc_in, sas)
            return sas

        return lax.fori_loop(0, SCAN_SUB // SCAN_BULK, four_steps, sas)

    n_sub = tb // SCAN_SUB
    for i in range(SCAN_SUB // SCAN_BULK):
        value_columns(0, i, vc_ref)
    sas = tuple(head_sums(g, [s_ref[p] for p in groups[g]], 0) for g in range(SCAN_MATMULS))
    if n_sub == 1:
        sub_block(0, vc_ref, vc2_ref, sas)
    else:
        def two_sub_blocks(i, sas):
            return sub_block(2 * i + 1, vc2_ref, vc_ref, sub_block(2 * i, vc_ref, vc2_ref, sas))

        lax.fori_loop(0, n_sub // 2, two_sub_blocks, sas)

    @pl.when(tblk == pl.num_programs(1) - 1)
    def _():
        st_ref[...] = s_ref[...]


def _rwkv_scan(r, w, k, nkk, b, v, s0):
    nb, t_len, _ = r.shape
    tb = min(t_len, SCAN_BLOCK)
    nblk = t_len // tb
    hd = R_HEAD_DIM
    nsub, sb, ng, gp = tb // SCAN_SUB, SCAN_SUB, SCAN_GROUPS, SCAN_GROUP_PAIRS
    s0p = s0.reshape(nb, R_PAIRS, 2, hd, hd).transpose(0, 1, 3, 2, 4).reshape(nb, R_PAIRS, hd, LANES)
    lane2_h = jnp.arange(2 * LANES) // hd
    ones2 = (lane2_h[:, None] == lane2_h[None, :]).astype(BF16)
    src = jnp.arange(2 * LANES) % LANES
    src_q, src_h, src_t = src // (2 * sb), (src // sb) % 2, src % sb
    dst = jnp.arange(gp * LANES)
    dst_q, dst_h = dst // LANES, (dst % LANES) // hd
    spread = ((src_q[:, None] == dst_q[None, :]) & (src_h[:, None] == dst_h[None, :])).astype(BF16)
    hot = jnp.broadcast_to((src_t[None, :] == jnp.arange(sb)[:, None])[:, None, :], (sb, 16, 2 * LANES)).astype(BF16)
    trow = pl.BlockSpec((None, tb, R_PAIRS, LANES), lambda bb, i: (bb, i, 0, 0))
    r, w, k, nkk, b = (a.reshape(nb, t_len, R_PAIRS, LANES) for a in (r, w, k, nkk, b))
    st = pl.BlockSpec((None, R_PAIRS, hd, LANES), lambda bb, i: (bb, 0, 0, 0))
    full = lambda a: pl.BlockSpec(a.shape, lambda bb, i: (0,) * a.ndim)
    ycol_spec = pl.BlockSpec((None, None, R_PAIRS, hd, LANES), lambda bb, i: (bb, i, 0, 0, 0))
    ycol, s_t = pl.pallas_call(
        functools.partial(_rwkv_scan_kernel, tb=tb),
        out_shape=(jax.ShapeDtypeStruct((nb, nblk, R_PAIRS, hd, LANES), F32),
                   jax.ShapeDtypeStruct((nb, R_PAIRS, hd, LANES), F32)),
        grid=(nb, nblk),
        in_specs=[trow, trow, trow, trow, trow, pl.BlockSpec((None, tb, R_WIDTH), lambda bb, i: (bb, i, 0)), st,
                  full(ones2), full(hot), full(spread)],
        out_specs=(ycol_spec, st),
        scratch_shapes=[pltpu.VMEM((R_PAIRS, hd, LANES), F32), pltpu.VMEM((nsub, ng * hd, 2 * LANES), BF16),
                        pltpu.VMEM((sb, ng * hd, gp * LANES), F32), pltpu.VMEM((sb, ng * hd, gp * LANES), F32)],
        compiler_params=_cparams(("parallel", "arbitrary")),
        name="rwkv_scan",
    )(r, w, k, nkk, b, v, s0p, ones2, hot, spread)
    s_t = s_t.reshape(nb, R_PAIRS, hd, 2, hd).transpose(0, 1, 3, 2, 4).reshape(nb, R_HEADS, hd, hd)
    return ycol, s_t


def _rwkv_post_kernel(y_ref, r_ref, k_ref, v_ref, g_ref, lw_ref, lb_ref, rk_ref, ones_ref, o_ref, *, tb):
    ones_bd = ones_ref[...]
    tiles = []
    for p in range(R_PAIRS):
        yt = y_ref[p].T
        tiles.append(jnp.concatenate([yt[0:tb, :], yt[R_HEAD_DIM:R_HEAD_DIM + tb, :]], axis=1))
    y = jnp.concatenate(tiles, axis=1)
    mean = _head_sums(y, ones_bd) * (1.0 / R_HEAD_DIM)
    d = y - mean
    var = _head_sums(d * d, ones_bd) * (1.0 / R_HEAD_DIM)
    yn = d * lax.rsqrt(var + GN_EPS) * lw_ref[...] + lb_ref[...]
    bonus = _head_sums(r_ref[...] * k_ref[...] * rk_ref[...], ones_bd) * v_ref[...]
    o_ref[...] = ((yn + bonus) * g_ref[...]).astype(o_ref.dtype)


def _rwkv_post(ycol, r, k, v, g, lnx_w, lnx_b, r_k, ones_bd):
    m = r.shape[0]
    nblk = ycol.shape[1]
    tb = m // (ycol.shape[0] * nblk)
    row = pl.BlockSpec((tb, R_WIDTH), lambda i: (i, 0))
    full = lambda a: pl.BlockSpec(a.shape, lambda i: (0,) * a.ndim)
    params = (lnx_w, lnx_b, r_k, ones_bd)
    return pl.pallas_call(
        functools.partial(_rwkv_post_kernel, tb=tb),
        out_shape=jax.ShapeDtypeStruct((m, R_WIDTH), BF16),
        grid=(m // tb,),
        in_specs=[pl.BlockSpec((None, None, R_PAIRS, R_HEAD_DIM, LANES), lambda i: (i // nblk, i % nblk, 0, 0, 0))]
        + [row] * 4 + [full(p) for p in params],
        out_specs=row,
        compiler_params=_cparams(("parallel",)),
        name="rwkv_post",
    )(ycol, r, k, v, g, *params)


def _ffn_up_kernel(x_ref, xp_ref, wg_ref, wu_ref, c0_ref, cw_ref, cb_ref, act_ref, tail_ref, *,
                   seq_rows, seqs_per_tile, tiles_per_seq):
    first = lax.rem(pl.program_id(0), tiles_per_seq) == 0
    wg = wg_ref[...].astype(BF16)
    gate_all = jnp.dot(x_ref[...], wg, preferred_element_type=F32)
    up_all = jnp.dot(x_ref[...], wu_ref[...].astype(BF16), preferred_element_type=F32)
    gate_prev = jnp.dot(xp_ref[...], wg, preferred_element_type=F32)
    row = lax.broadcasted_iota(I32, (seq_rows, 1), 0)
    for s in range(seqs_per_tile):
        rows = slice(s * seq_rows, (s + 1) * seq_rows)
        gate = gate_all[rows]
        prev1 = jnp.where(first, c0_ref[s, 1:2, :], gate_prev[7:8, :])
        prev2 = jnp.where(first, c0_ref[s, 0:1, :], gate_prev[6:7, :])
        g_m1 = jnp.where(row == 0, prev1, pltpu.roll(gate, 1, axis=0))
        g_m2 = jnp.where(row == 0, prev2, jnp.where(row == 1, prev1, pltpu.roll(gate, 2, axis=0)))
        conv = cb_ref[...] + g_m2 * cw_ref[0:1, :]
        conv = conv + g_m1 * cw_ref[1:2, :]
        conv = conv + gate * cw_ref[2:3, :]
        act_ref[rows, :] = (conv * jax.nn.sigmoid(conv) * up_all[rows]).astype(act_ref.dtype)
        tail_ref[s] = gate[seq_rows - 8:, :]


def _ffn_up(x, w_in, conv0, conv_w, conv_b, n_batch, t_len, tm=1024, tn=256):
    m, d = x.shape
    d_ff = w_in.shape[1] // 2
    tm = min(m, tm)
    seq_rows = min(t_len, tm)
    seqs_per_tile, tiles_per_seq = tm // seq_rows, t_len // seq_rows
    assert m % tm == 0 and tm % seq_rows == 0 and t_len % seq_rows == 0 and seq_rows % 8 == 0 and d_ff % tn == 0
    n_up = d_ff // tn
    act, tail = pl.pallas_call(
        functools.partial(_ffn_up_kernel, seq_rows=seq_rows, seqs_per_tile=seqs_per_tile, tiles_per_seq=tiles_per_seq),
        out_shape=(jax.ShapeDtypeStruct((m, d_ff), BF16),
                   jax.ShapeDtypeStruct((n_batch * tiles_per_seq, 8, d_ff), F32)),
        grid=(m // tm, n_up),
        in_specs=[pl.BlockSpec((tm, d), lambda i, j: (i, 0)),
                  pl.BlockSpec((8, d), lambda i, j: (jnp.maximum(i * (tm // 8) - 1, 0), 0)),
                  pl.BlockSpec((d, tn), lambda i, j: (0, j)),
                  pl.BlockSpec((d, tn), lambda i, j: (0, j + n_up)),
                  pl.BlockSpec((seqs_per_tile, CONV_W - 1, tn), lambda i, j: (i // tiles_per_seq, 0, j)),
                  pl.BlockSpec((CONV_W, tn), lambda i, j: (0, j)),
                  pl.BlockSpec((1, tn), lambda i, j: (0, j))],
        out_specs=(pl.BlockSpec((tm, tn), lambda i, j: (i, j)),
                   pl.BlockSpec((seqs_per_tile, 8, tn), lambda i, j: (i, 0, j))),
        compiler_params=_cparams(("parallel", "arbitrary")),
        name="ffn_up",
    )(x, x, w_in, w_in, conv0, conv_w, conv_b.reshape(1, d_ff))
    gate_tail = tail.reshape(n_batch, tiles_per_seq, 8, d_ff)[:, -1, 8 - (CONV_W - 1):, :]
    return act, gate_tail


def _pad_rw_cols(a):
    z = lambda n: jnp.zeros(a.shape[:-1] + (n,), a.dtype)
    wd0, ad0, gd0 = 3 * R_WIDTH, 3 * R_WIDTH + DECAY_LORA, 3 * R_WIDTH + DECAY_LORA + AAA_LORA
    return jnp.concatenate([a[..., :wd0], a[..., wd0:ad0], z(LORA_PAD - DECAY_LORA), a[..., ad0:gd0],
                            z(LORA_PAD - AAA_LORA), a[..., gd0:]], axis=-1)


def _unpad_rw_cols(a):
    return jnp.concatenate([a[..., :RW_WD0 + DECAY_LORA], a[..., RW_AD0:RW_AD0 + AAA_LORA], a[..., RW_GD0:]], axis=-1)


def _prep_weights(norm_mix_g, w_in, rwkv_mu, rwkv_w0, rwkv_w2, rwkv_a0, rwkv_a2, rwkv_g2, rwkv_k_k, rwkv_k_a,
                  rwkv_r_k, rwkv_lnx_w, rwkv_lnx_b, w_out, norm_ffn_g, ffn_w_in, ffn_conv_w, ffn_conv_b,
                  ffn_w_down, norm_final_g, l):
    d = w_in.shape[1]
    w_att = jnp.concatenate([w_in[l][:, :ATT_USED].astype(BF16), jnp.zeros((d, ATT_COLS - ATT_USED), BF16)], axis=1)
    w_rw = _pad_rw_cols(w_in[l][:, ATT_USED:].astype(BF16))
    row = lambda a: a.reshape(1, -1).astype(F32)
    pad_rows = lambda a, n: jnp.concatenate([a, jnp.zeros((n - a.shape[0], a.shape[1]), a.dtype)], axis=0)
    lane_h = jnp.arange(LANES) // R_HEAD_DIM
    w2_pad = pad_rows(rwkv_w2[l].astype(F32), LORA_PAD)
    w2_hi = w2_pad.astype(BF16)
    w2_mid = (w2_pad - w2_hi.astype(F32)).astype(BF16)
    return dict(
        norm_mix_g=norm_mix_g[l], w_att=w_att, w_rw=w_rw,
        mu=_pad_rw_cols(row(rwkv_mu[l])), w0=row(rwkv_w0[l]), a0=row(rwkv_a0[l]),
        k_k=row(rwkv_k_k[l]), k_a=row(rwkv_k_a[l]),
        w2=jnp.concatenate([w2_hi, w2_mid, w2_hi], axis=0), a2=pad_rows(rwkv_a2[l], LORA_PAD).astype(BF16),
        g2=rwkv_g2[l].astype(BF16),
        r_k=row(rwkv_r_k[l]), lnx_w=row(rwkv_lnx_w[l]), lnx_b=row(rwkv_lnx_b[l]),
        ones_bd=jnp.concatenate([lane_h[:, None] == lane_h[None, :]] * 2, axis=0).astype(BF16),
        w_out_a=w_out[l][:A_WIDTH].astype(BF16), w_out_r=w_out[l][A_WIDTH:].astype(BF16),
        norm_ffn_g=norm_ffn_g[l], ffn_w_in=ffn_w_in[l], conv_w=ffn_conv_w[l], conv_b=ffn_conv_b[l],
        ffn_w_down=ffn_w_down[l].astype(BF16), norm_final_g=norm_final_g,
    )


def _trunk(x, past_k, past_v, past_ik, s0, shift0, conv0, wt):
    nb, t_len, d = x.shape
    m = nb * t_len
    p_len = 0 if past_k is None else past_k.shape[1]
    x2 = x.reshape(m, d)

    h = _rmsnorm(x2, wt["norm_mix_g"], BF16)
    z_att = _matmul([h], [wt["w_att"]], tm=1024, name="proj_att")
    z_rw = _matmul([h], [wt["w_rw"]], tm=1024, name="proj_rw")

    q_bf, k_f, k_bf, v_f, v_bf, iq_bf, ikw, ik_bf = _rope_split(z_att, t_len, p_len)
    past = None if past_k is None else (past_k.reshape(nb, p_len, KV_W), past_v.reshape(nb, p_len, KV_W), past_ik)
    attn = _dsa(q_bf, iq_bf, ikw, k_bf, v_bf, ik_bf, past, n_batch=nb, t_len=t_len)

    z_rw3 = z_rw.reshape(nb, t_len, RW_COLS)
    r, w, k2, v2, nkk, b, g = _rwkv_pre(z_rw3, _pad_rw_cols(shift0.astype(F32)), wt["mu"], wt["w0"], wt["a0"],
                                       wt["k_k"], wt["k_a"], wt["w2"], wt["a2"], wt["g2"], wt["ones_bd"])
    ycol, s_t = _rwkv_scan(r, w, k2, nkk, b, v2, s0.astype(F32))
    flat = lambda a: a.reshape(m, R_WIDTH)
    rw = _rwkv_post(ycol, flat(r), flat(k2), flat(v2), flat(g), wt["lnx_w"], wt["lnx_b"], wt["r_k"], wt["ones_bd"])

    x1 = _matmul([attn, rw], [wt["w_out_a"], wt["w_out_r"]], res=x2, tm=1024, name="out_proj")
    hf = _rmsnorm(x1, wt["norm_ffn_g"], BF16)
    act, conv_t = _ffn_up(hf, wt["ffn_w_in"], conv0.astype(F32), wt["conv_w"], wt["conv_b"], nb, t_len)
    x3 = _matmul([act], [wt["ffn_w_down"]], res=x1, name="ffn_down")
    y_out = _rmsnorm(x3, wt["norm_final_g"], F32).reshape(nb, t_len, d)

    shift_t = _unpad_rw_cols(z_rw3[:, -1:])
    caches = (k_f.reshape(nb, t_len, A_KV_HEADS, A_HEAD_DIM)[None], v_f.reshape(nb, t_len, A_KV_HEADS, A_HEAD_DIM)[None],
              ikw[:, :IDX_DIM].reshape(nb, t_len, IDX_DIM)[None], s_t[None], shift_t[None], conv_t[None])
    return y_out, caches


def kernel(x_prompt, x_sample, cache_k, cache_v, cache_idx_k, state_rwkv, state_rwkv_shift, state_ffn_conv, norm_mix_g, w_in, rwkv_mu, rwkv_w0, rwkv_w2, rwkv_a0, rwkv_a2, rwkv_g2, rwkv_k_k, rwkv_k_a, rwkv_r_k, rwkv_lnx_w, rwkv_lnx_b, w_out, norm_ffn_g, ffn_w_in, ffn_conv_w, ffn_conv_b, ffn_w_down, norm_final_g):
    assert w_in.shape[0] == 1, "single-layer trunk"
    wt = _prep_weights(norm_mix_g, w_in, rwkv_mu, rwkv_w0, rwkv_w2, rwkv_a0, rwkv_a2, rwkv_g2, rwkv_k_k, rwkv_k_a,
                       rwkv_r_k, rwkv_lnx_w, rwkv_lnx_b, w_out, norm_ffn_g, ffn_w_in, ffn_conv_w, ffn_conv_b,
                       ffn_w_down, norm_final_g, 0)
    bp = x_prompt.shape[0]
    d_ff = ffn_conv_w.shape[-1]
    y_p, c_p = _trunk(x_prompt, None, None, None,
                      jnp.zeros((bp, R_HEADS, R_HEAD_DIM, R_HEAD_DIM), F32), jnp.zeros((bp, 1, RWKV_COLS), F32),
                      jnp.zeros((bp, CONV_W - 1, d_ff), F32), wt)
    y_s, c_s = _trunk(x_sample, cache_k[0], cache_v[0], cache_idx_k[0], state_rwkv[0], state_rwkv_shift[0],
                      state_ffn_conv[0], wt)
    return (y_p, y_s) + c_p + c_s
```

```python
import functools

import jax
import jax.numpy as jnp
from jax import lax
from jax.experimental import pallas as pl
from jax.experimental.pallas import tpu as pltpu

F32 = jnp.float32
BF16 = jnp.bfloat16
I32 = jnp.int32

CHUNK = 64
A_HEADS = 16
A_KV_HEADS = 4
A_HEAD_DIM = 128
A_GROUP = A_HEADS // A_KV_HEADS
A_WIDTH = A_HEADS * A_HEAD_DIM
KV_W = A_KV_HEADS * A_HEAD_DIM
IDX_HEADS = 16
IDX_DIM = 64
IQ_W = IDX_HEADS * IDX_DIM
TOPK_MAX = 256
ROPE_THETA = 500000.0
ROPE_FRAC = 4
A_SCALE = A_HEAD_DIM ** -0.5
Q_SCALE = A_SCALE * 1.4426950408889634
IDX_SCALE = (IDX_HEADS ** -0.5) * (IDX_DIM ** -0.5)
R_HEAD_DIM = 64
R_WIDTH = 2048
R_HEADS = R_WIDTH // R_HEAD_DIM
R_PAIRS = R_HEADS // 2
DECAY_LORA = 96
AAA_LORA = 96
GATE_LORA = 256
RWKV_COLS = 3 * R_WIDTH + DECAY_LORA + AAA_LORA + GATE_LORA
GN_EPS = 6.4e-4
CONV_W = 3
RMS_EPS = 1e-6

LANES = 128
VMEM_LIMIT = 56 * 1024 * 1024

ATT_Q0, ATT_K0, ATT_V0, ATT_IQ0, ATT_IK0 = 0, A_WIDTH, A_WIDTH + KV_W, A_WIDTH + 2 * KV_W, A_WIDTH + 2 * KV_W + IQ_W
ATT_USED = ATT_IK0 + IDX_DIM + IDX_HEADS
ATT_COLS = 4608
LORA_PAD = 128
RW_WD0 = 3 * R_WIDTH
RW_AD0 = RW_WD0 + LORA_PAD
RW_GD0 = RW_AD0 + LORA_PAD
RW_COLS = RW_GD0 + GATE_LORA
DSA_TQ, DSA_TK = 256, 512
COUNT_ROWS = 128
SCAN_BLOCK = 64
SCAN_SUB = 16
SCAN_GROUP_PAIRS = LANES // (2 * SCAN_SUB)
SCAN_GROUPS = R_PAIRS // SCAN_GROUP_PAIRS
SCAN_MATMULS = 4
SCAN_BULK = 4
INT_MIN = -2 ** 31
NEG_BIG = -1e30


def _cparams(sem):
    return pltpu.CompilerParams(dimension_semantics=sem, vmem_limit_bytes=VMEM_LIMIT)


def _rmsnorm_kernel(x_ref, g_ref, o_ref):
    x = x_ref[...]
    y = x * lax.rsqrt(jnp.mean(x * x, axis=-1, keepdims=True) + RMS_EPS)
    o_ref[...] = (y * g_ref[...]).astype(o_ref.dtype)


def _rmsnorm(x, g, out_dtype):
    m, d = x.shape
    tm = min(m, 256)
    return pl.pallas_call(
        _rmsnorm_kernel,
        out_shape=jax.ShapeDtypeStruct((m, d), out_dtype),
        grid=(m // tm,),
        in_specs=[pl.BlockSpec((tm, d), lambda i: (i, 0)), pl.BlockSpec((1, d), lambda i: (0, 0))],
        out_specs=pl.BlockSpec((tm, d), lambda i: (i, 0)),
        compiler_params=_cparams(("parallel",)),
        name="rmsnorm",
    )(x, g.reshape(1, d).astype(F32))


def _mm_kernel(*refs, n_pairs, has_res):
    o_ref = refs[-1]
    acc = jnp.dot(refs[0][...], refs[n_pairs][...], preferred_element_type=F32)
    for p in range(1, n_pairs):
        acc = acc + jnp.dot(refs[p][...], refs[n_pairs + p][...], preferred_element_type=F32)
    if has_res:
        acc = refs[2 * n_pairs][...] + acc
    o_ref[...] = acc.astype(o_ref.dtype)


def _matmul(a_list, b_list, res=None, tm=512, tn=512, name="matmul"):
    m = a_list[0].shape[0]
    n = b_list[0].shape[1]
    tm = min(tm, m)
    tn = min(tn, n)
    assert m % tm == 0 and n % tn == 0, (m, n, tm, tn)
    in_specs = [pl.BlockSpec((tm, a.shape[1]), lambda i, j: (i, 0)) for a in a_list]
    in_specs += [pl.BlockSpec((b.shape[0], tn), lambda i, j: (0, j)) for b in b_list]
    args = list(a_list) + list(b_list)
    if res is not None:
        in_specs.append(pl.BlockSpec((tm, tn), lambda i, j: (i, j)))
        args.append(res)
    return pl.pallas_call(
        functools.partial(_mm_kernel, n_pairs=len(a_list), has_res=res is not None),
        out_shape=jax.ShapeDtypeStruct((m, n), F32),
        grid=(m // tm, n // tn),
        in_specs=in_specs,
        out_specs=pl.BlockSpec((tm, tn), lambda i, j: (i, j)),
        compiler_params=_cparams(("parallel", "arbitrary")),
        name=name,
    )(*args)


def _rope_tile(x, cos, sin, half, d_in_head):
    lo = d_in_head < half
    hi = (d_in_head >= half) & (d_in_head < 2 * half)
    c = jnp.where(lo | hi, cos, 1.0)
    s_up = jnp.where(lo, -sin, 0.0)
    s_dn = jnp.where(hi, sin, 0.0)
    x_up = pltpu.roll(x, LANES - half, axis=1)
    x_dn = pltpu.roll(x, half, axis=1)
    return x * c + x_up * s_up + x_dn * s_dn


def _rope_kernel(z_ref, invf_ref, q_ref, kf_ref, kb_ref, vf_ref, vb_ref, iq_ref, ikw_ref, ikb_ref, *, tm, t_len, pos0):
    i = pl.program_id(0)
    row = lax.broadcasted_iota(I32, (tm, LANES), 0) + i * tm
    pos = (pos0 + lax.rem(row, t_len)).astype(F32)
    lane = lax.broadcasted_iota(I32, (tm, LANES), 1)
    ang = pos * invf_ref[0:1, :]
    cos_a, sin_a = jnp.cos(ang), jnp.sin(ang)
    half_a = A_HEAD_DIM // ROPE_FRAC // 2
    for h in range(A_HEADS):
        x = z_ref[:, ATT_Q0 + h * LANES:ATT_Q0 + (h + 1) * LANES]
        q_ref[:, h * LANES:(h + 1) * LANES] = (_rope_tile(x, cos_a, sin_a, half_a, lane) * Q_SCALE).astype(q_ref.dtype)
    for h in range(A_KV_HEADS):
        x = z_ref[:, ATT_K0 + h * LANES:ATT_K0 + (h + 1) * LANES]
        y = _rope_tile(x, cos_a, sin_a, half_a, lane)
        kf_ref[:, h * LANES:(h + 1) * LANES] = y
        kb_ref[:, h * LANES:(h + 1) * LANES] = y.astype(kb_ref.dtype)
    v = z_ref[:, ATT_V0:ATT_V0 + KV_W]
    vf_ref[...] = v
    vb_ref[...] = v.astype(vb_ref.dtype)
    ang = pos * invf_ref[1:2, :]
    cos_i, sin_i = jnp.cos(ang), jnp.sin(ang)
    half_i = IDX_DIM // ROPE_FRAC // 2
    d_i = lane & (IDX_DIM - 1)
    for h in range(IQ_W // LANES):
        x = z_ref[:, ATT_IQ0 + h * LANES:ATT_IQ0 + (h + 1) * LANES]
        iq_ref[:, h * LANES:(h + 1) * LANES] = _rope_tile(x, cos_i, sin_i, half_i, d_i).astype(iq_ref.dtype)
    x = z_ref[:, ATT_IK0:ATT_IK0 + LANES]
    d_k = jnp.where(lane < IDX_DIM, lane, IDX_DIM)
    y = _rope_tile(x, cos_i, sin_i, half_i, d_k)
    ikw_ref[...] = y
    ikb_ref[...] = y[:, :IDX_DIM].astype(ikb_ref.dtype)


def _rope_split(z_att, t_len, pos0):
    m = z_att.shape[0]
    tm = min(m, 256)
    lane = jnp.arange(LANES)
    rd_a = A_HEAD_DIM // ROPE_FRAC
    rd_i = IDX_DIM // ROPE_FRAC
    invf_a = ROPE_THETA ** (-((lane % (rd_a // 2)).astype(F32) * 2.0 / rd_a))
    invf_i = ROPE_THETA ** (-((lane % (rd_i // 2)).astype(F32) * 2.0 / rd_i))
    invf = jnp.zeros((8, LANES), F32).at[0].set(invf_a).at[1].set(invf_i)
    row_spec = lambda w: pl.BlockSpec((tm, w), lambda i: (i, 0))
    shp = lambda w, dt: jax.ShapeDtypeStruct((m, w), dt)
    return pl.pallas_call(
        functools.partial(_rope_kernel, tm=tm, t_len=t_len, pos0=pos0),
        out_shape=(shp(A_WIDTH, BF16), shp(KV_W, F32), shp(KV_W, BF16), shp(KV_W, F32), shp(KV_W, BF16),
                   shp(IQ_W, BF16), shp(LANES, F32), shp(IDX_DIM, BF16)),
        grid=(m // tm,),
        in_specs=[row_spec(ATT_COLS), pl.BlockSpec((8, LANES), lambda i: (0, 0))],
        out_specs=(row_spec(A_WIDTH), row_spec(KV_W), row_spec(KV_W), row_spec(KV_W), row_spec(KV_W),
                   row_spec(IQ_W), row_spec(LANES), row_spec(IDX_DIM)),
        compiler_params=_cparams(("parallel",)),
        name="rope_split",
    )(z_att, invf)


def _tile(x, n, axis):
    return x if n == 1 else jnp.concatenate([x] * n, axis=axis)


def _sortable(score):
    u = lax.bitcast_convert_type(score, I32)
    return jnp.where(u < 0, u ^ jnp.int32(0x7FFFFFFF), u)


def _dsa_kernel(*refs, tq, tk, pos0, n_keys, topk):
    i = pl.program_id(1)
    if pos0 > 0:
        (q_ref, iq_ref, ikw_ref, kn_ref, vn_ref, ikn_ref, pk_ref, pv_ref, pik_ref, o_ref,
         key_ref, iwb_ref, qs_ref, m_ref, l_ref, acc_ref, k_ref, v_ref, ik_ref) = refs
        t_new = kn_ref.shape[0]

        @pl.when(i == 0)
        def _():
            for new_ref, past_ref, all_ref in ((kn_ref, pk_ref, k_ref), (vn_ref, pv_ref, v_ref), (ikn_ref, pik_ref, ik_ref)):
                for r0 in range(0, pos0, DSA_TK):
                    r1 = min(r0 + DSA_TK, pos0)
                    all_ref[r0:r1, :] = past_ref[r0:r1, :].astype(all_ref.dtype)
                all_ref[pos0:pos0 + t_new, :] = new_ref[...]
                tail = all_ref.shape[0] - pos0 - t_new
                if tail:
                    all_ref[pos0 + t_new:, :] = jnp.zeros((tail, all_ref.shape[1]), all_ref.dtype)
    else:
        (q_ref, iq_ref, ikw_ref, k_ref, v_ref, ik_ref, o_ref,
         key_ref, iwb_ref, qs_ref, m_ref, l_ref, acc_ref) = refs
    q0 = pos0 + i * tq
    kmax = jnp.minimum((lax.div(q0 + tq - 1, CHUNK) + 1) * CHUNK, n_keys)
    nkb = lax.div(kmax + tk - 1, tk)
    n_rep = tk // LANES
    nt_dims = (((1,), (1,)), ((), ()))

    for h in range(IDX_HEADS):
        iwb_ref[h] = jnp.broadcast_to(ikw_ref[:, IDX_DIM + h:IDX_DIM + h + 1], (tq, LANES))
    for h in range(A_HEADS):
        qs_ref[h * tq:(h + 1) * tq, :] = q_ref[:, h * LANES:(h + 1) * LANES]

    def score_block(kb, carry):
        koff = pl.multiple_of(kb * tk, tk)
        ikb = ik_ref[pl.ds(koff, tk), :]
        acc = jnp.zeros((tq, tk), F32)
        for h in range(IDX_HEADS):
            d = lax.dot_general(iq_ref[:, h * IDX_DIM:(h + 1) * IDX_DIM], ikb, nt_dims, preferred_element_type=F32)
            acc = acc + jnp.maximum(d, 0.0) * _tile(iwb_ref[h], n_rep, 1)
        kpos = koff + lax.broadcasted_iota(I32, (tq, tk), 1)
        qpos = q0 + lax.broadcasted_iota(I32, (tq, tk), 0)
        adm = (lax.shift_right_logical(kpos, 6) <= lax.shift_right_logical(qpos, 6)) & (kpos < n_keys)
        key_ref[:, pl.ds(koff, tk)] = jnp.where(adm, _sortable(acc * IDX_SCALE), jnp.int32(INT_MIN))
        return carry

    lax.fori_loop(0, nkb, score_block, 0)

    def count(pred, *row_args):
        cr = min(tq, COUNT_ROWS)

        def chunk(r0):
            def body(kb, cnt):
                koff = pl.multiple_of(kb * tk, tk)
                keys = key_ref[r0:r0 + cr, pl.ds(koff, tk)]
                kpos = koff + lax.broadcasted_iota(I32, (cr, tk), 1)
                hit = jnp.where(pred(keys, kpos, *[a[r0:r0 + cr] for a in row_args]), 1.0, 0.0)
                for c in range(n_rep):
                    cnt = cnt + hit[:, c * LANES:(c + 1) * LANES]
                return cnt
            cnt = lax.fori_loop(0, nkb, body, jnp.zeros((cr, LANES), F32))
            return jnp.broadcast_to(jnp.sum(cnt, axis=-1, keepdims=True), (cr, LANES))

        return jnp.concatenate([chunk(r0) for r0 in range(0, tq, cr)], axis=0)

    def wide(x):
        return _tile(x, n_rep, 1)

    def bit_step(it, tu):
        cand_u = tu | lax.shift_left(jnp.int32(1), 31 - it)
        cand_s = wide(cand_u ^ jnp.int32(INT_MIN))
        cnt = count(lambda keys, kpos, c: keys >= c, cand_s)
        return jnp.where(cnt >= topk, cand_u, tu)

    tu = lax.fori_loop(0, 32, bit_step, jnp.zeros((tq, LANES), I32))
    thr = jnp.maximum(tu ^ jnp.int32(INT_MIN), jnp.int32(INT_MIN + 1))
    thr_w = wide(thr)

    n_ge = count(lambda keys, kpos, t: keys >= t, thr_w)
    n_gt = count(lambda keys, kpos, t: keys > t, thr_w)
    excess = n_ge > topk

    @pl.when(jnp.max(jnp.where(excess, 1.0, 0.0)) > 0.0)
    def _():
        need = topk - n_gt

        idx_bits = int(key_ref.shape[1]).bit_length()

        def idx_step(it, jm):
            cand = wide(jm | lax.shift_left(jnp.int32(1), idx_bits - 1 - it))
            cnt = count(lambda keys, kpos, t, c: (keys == t) & (kpos < c), thr_w, cand)
            return jnp.where(cnt < need, cand[:, :LANES], jm)

        jm = lax.fori_loop(0, idx_bits, idx_step, jnp.zeros((tq, LANES), I32))
        jm_w = wide(jnp.where(excess, jm, jnp.int32(2 ** 31 - 1)))

        def drop(kb, carry):
            koff = pl.multiple_of(kb * tk, tk)
            keys = key_ref[:, pl.ds(koff, tk)]
            kpos = koff + lax.broadcasted_iota(I32, (tq, tk), 1)
            key_ref[:, pl.ds(koff, tk)] = jnp.where((keys == thr_w) & (kpos > jm_w), jnp.int32(INT_MIN), keys)
            return carry

        lax.fori_loop(0, nkb, drop, 0)

    m_ref[...] = jnp.full(m_ref.shape, NEG_BIG, F32)
    l_ref[...] = jnp.zeros(l_ref.shape, F32)
    acc_ref[...] = jnp.zeros(acc_ref.shape, F32)
    rows = A_GROUP * tq

    def attend(kb, carry):
        koff = pl.multiple_of(kb * tk, tk)
        bias = _tile(jnp.where(key_ref[:, pl.ds(koff, tk)] >= thr_w, 0.0, NEG_BIG), A_GROUP, 0)
        for n in range(A_KV_HEADS):
            r0 = n * rows
            kn = k_ref[pl.ds(koff, tk), n * LANES:(n + 1) * LANES]
            vn = v_ref[pl.ds(koff, tk), n * LANES:(n + 1) * LANES]
            s = lax.dot_general(qs_ref[r0:r0 + rows, :], kn, nt_dims, preferred_element_type=F32) + bias
            m_prev = m_ref[r0:r0 + rows, :]
            m_new = jnp.maximum(m_prev, jnp.max(s, axis=-1, keepdims=True))
            alpha = jnp.exp2(m_prev - m_new)
            p = jnp.exp2(s - _tile(m_new, n_rep, 1))
            l_ref[r0:r0 + rows, :] = alpha * l_ref[r0:r0 + rows, :] + jnp.sum(p, axis=-1, keepdims=True)
            acc_ref[r0:r0 + rows, :] = alpha * acc_ref[r0:r0 + rows, :] + jnp.dot(
                p.astype(vn.dtype), vn, preferred_element_type=F32)
            m_ref[r0:r0 + rows, :] = m_new
        return carry

    lax.fori_loop(0, nkb, attend, 0)
    for h in range(A_HEADS):
        o_ref[:, h * LANES:(h + 1) * LANES] = (
            acc_ref[h * tq:(h + 1) * tq, :] / l_ref[h * tq:(h + 1) * tq, :]).astype(o_ref.dtype)


def _dsa(q_bf, iq_bf, ikw, k_new, v_new, ik_new, past, *, n_batch, t_len):
    pos0 = 0 if past is None else past[0].shape[1]
    n_keys = pos0 + t_len
    tq = min(t_len, DSA_TQ)
    lp = -(-n_keys // LANES) * LANES
    tk = lp if tq * lp <= DSA_TQ * DSA_TK else DSA_TK
    lp = -(-lp // tk) * tk
    assert t_len % tq == 0 and (past is None or (lp == tk and pos0 % 16 == 0 and t_len % 16 == 0)) \
        and (past is not None or lp == n_keys)
    nq = t_len // tq
    topk = min(TOPK_MAX, n_keys // 4)
    qrow = lambda w: pl.BlockSpec((tq, w), lambda b, i: (b * nq + i, 0))
    new3 = lambda a: a.reshape(n_batch, t_len, a.shape[-1])
    whole = lambda rows, w, **kw: pl.BlockSpec((None, rows, w), lambda b, i: (b, 0, 0), **kw)
    in_specs = [qrow(A_WIDTH), qrow(IQ_W), qrow(LANES)]
    args = [q_bf, iq_bf, ikw, new3(k_new), new3(v_new), new3(ik_new)]
    scratch = [
        pltpu.VMEM((tq, lp), I32),
        pltpu.VMEM((IDX_HEADS, tq, LANES), F32),
        pltpu.VMEM((A_HEADS * tq, LANES), BF16),
        pltpu.VMEM((A_HEADS * tq, LANES), F32),
        pltpu.VMEM((A_HEADS * tq, LANES), F32),
        pltpu.VMEM((A_HEADS * tq, LANES), F32),
    ]
    if past is None:
        in_specs += [whole(lp, w, pipeline_mode=pl.Buffered(1)) for w in (KV_W, KV_W, IDX_DIM)]
    else:
        in_specs += [whole(t_len, w) for w in (KV_W, KV_W, IDX_DIM)] + [whole(pos0, w) for w in (KV_W, KV_W, IDX_DIM)]
        args += list(past)
        scratch += [pltpu.VMEM((lp, w), BF16) for w in (KV_W, KV_W, IDX_DIM)]
    return pl.pallas_call(
        functools.partial(_dsa_kernel, tq=tq, tk=tk, pos0=pos0, n_keys=n_keys, topk=float(topk)),
        out_shape=jax.ShapeDtypeStruct((n_batch * t_len, A_WIDTH), BF16),
        grid=(n_batch, nq),
        in_specs=in_specs,
        out_specs=qrow(A_WIDTH),
        scratch_shapes=scratch,
        compiler_params=_cparams(("parallel", "arbitrary")),
        name="dsa",
    )(*args)


def _head_sums(x, ones2):
    n = x.shape[1] // LANES
    tm = x.shape[0]
    hi, mid = _split2(jnp.concatenate([x[:, c * LANES:(c + 1) * LANES] for c in range(n)], axis=0))
    s = jnp.dot(jnp.concatenate([hi, mid], axis=1), ones2, preferred_element_type=F32)
    return jnp.concatenate([s[c * tm:(c + 1) * tm, :] for c in range(n)], axis=1)


def _rwkv_pre_kernel(z_ref, zp_ref, z0_ref, mu_ref, w0_ref, a0_ref, kk_ref, ka_ref, w2_ref, a2_ref, g2_ref,
                     ones_ref, r_out, w_out, k_out, v_out, nkk_out, b_out, g_out, *, tm):
    i = pl.program_id(1)
    row = lax.broadcasted_iota(I32, (tm, 1), 0)

    def mixed(c0, width):
        z = z_ref[:, c0:c0 + width]
        first = jnp.where(i == 0, z0_ref[:, c0:c0 + width], zp_ref[7:8, c0:c0 + width])
        shifted = jnp.where(row == 0, first, pltpu.roll(z, 1, axis=0))
        return z + (shifted - z) * mu_ref[:, c0:c0 + width]

    r = mixed(0, R_WIDTH)
    k = mixed(R_WIDTH, R_WIDTH)
    v = mixed(2 * R_WIDTH, R_WIDTH)
    wd = mixed(RW_WD0, LORA_PAD)
    ad = mixed(RW_AD0, LORA_PAD)
    gd = mixed(RW_GD0, GATE_LORA)
    lora = lambda x, w_ref: jnp.dot(x.astype(BF16), w_ref[...], preferred_element_type=F32)
    t_hi, t_mid = _split2(jnp.tanh(wd))
    y = -(w0_ref[...] + jnp.dot(jnp.concatenate([t_hi, t_hi, t_mid], axis=1), w2_ref[...], preferred_element_type=F32))
    softplus = jnp.maximum(y, 0.0) + jnp.log(1.0 + jnp.exp(-jnp.abs(y)))
    decay = jnp.exp(-jnp.exp(-softplus - 0.5))
    a = jax.nn.sigmoid(a0_ref[...] + lora(ad, a2_ref))
    g = lora(jax.nn.sigmoid(gd), g2_ref)
    kk = k * kk_ref[...]
    kk = kk / jnp.maximum(jnp.sqrt(_head_sums(kk * kk, ones_ref[...])), 1e-12)
    r_out[...] = r
    w_out[...] = decay
    k_out[...] = k * (1.0 + (a - 1.0) * ka_ref[...])
    v_out[...] = v
    nkk_out[...] = -kk
    b_out[...] = kk * a
    g_out[...] = g


def _rwkv_pre(z_rw, z0, mu, w0, a0, k_k, k_a, w2, a2, g2, ones_bd):
    nb, t_len, _ = z_rw.shape
    tm = min(t_len, 128)
    zrow = pl.BlockSpec((None, tm, RW_COLS), lambda b, i: (b, i, 0))
    zprev = pl.BlockSpec((None, 8, RW_COLS), lambda b, i: (b, jnp.maximum(i * (tm // 8) - 1, 0), 0))
    full = lambda a: pl.BlockSpec(a.shape, lambda b, i: (0,) * a.ndim)
    orow = pl.BlockSpec((None, tm, R_WIDTH), lambda b, i: (b, i, 0))
    params = (mu, w0, a0, k_k, k_a, w2, a2, g2, ones_bd)
    return pl.pallas_call(
        functools.partial(_rwkv_pre_kernel, tm=tm),
        out_shape=tuple(jax.ShapeDtypeStruct((nb, t_len, R_WIDTH), F32) for _ in range(7)),
        grid=(nb, t_len // tm),
        in_specs=[zrow, zprev, pl.BlockSpec((None, 1, RW_COLS), lambda b, i: (b, 0, 0))] + [full(p) for p in params],
        out_specs=tuple(orow for _ in range(7)),
        compiler_params=_cparams(("parallel", "arbitrary")),
        name="rwkv_pre",
    )(z_rw, z_rw, z0, *params)


def _split2(x):
    hi = x.astype(BF16)
    return hi, (x - hi.astype(F32)).astype(BF16)


def _rwkv_scan_kernel(r_ref, w_ref, k_ref, nkk_ref, b_ref, v_ref, s0_ref, ones2_ref, hot_ref, spread_ref,
                      y_ref, st_ref, s_ref, xs_ref, vc_ref, vc2_ref, *, tb):
    tblk = pl.program_id(1)
    cat = lambda xs, ax=0: jnp.concatenate(xs, axis=ax)

    @pl.when(tblk == 0)
    def _():
        s_ref[...] = s0_ref[...]

    pad = jnp.zeros((SCAN_BLOCK - tb, LANES), F32)
    v_cols = []
    for p in range(R_PAIRS):
        vp = v_ref[:, p * LANES:(p + 1) * LANES]
        v_cols.append((cat([vp, pad]) if tb < SCAN_BLOCK else vp).T)
    for sub in range(tb // SCAN_SUB):
        for g in range(SCAN_GROUPS):
            x = cat([v_cols[SCAN_GROUP_PAIRS * g + q][h * R_HEAD_DIM:(h + 1) * R_HEAD_DIM,
                                                      sub * SCAN_SUB:(sub + 1) * SCAN_SUB]
                     for q in range(SCAN_GROUP_PAIRS) for h in range(2)], 1)
            hi, mid = _split2(x)
            xs_ref[sub, g * R_HEAD_DIM:(g + 1) * R_HEAD_DIM, :] = cat([hi, mid], 1)
    ones2 = ones2_ref[...]
    spread = spread_ref[...]
    y_ref[...] = jnp.zeros(y_ref.shape, F32)
    lane_t = lax.rem(lax.broadcasted_iota(I32, (R_HEAD_DIM, LANES), 1), R_HEAD_DIM)
    rowp = lambda ref, t, p: ref[t, p:p + 1, :]
    group_pairs = R_PAIRS // SCAN_MATMULS
    half = group_pairs // 2 * R_HEAD_DIM

    groups = [range(g * group_pairs, (g + 1) * group_pairs) for g in range(SCAN_MATMULS)]

    def packed(xs):
        return cat([cat(xs[2 * j:2 * j + 2], 1) for j in range(group_pairs // 2)])

    def pair_tile(res, q):
        return res[(q // 2) * R_HEAD_DIM:(q // 2 + 1) * R_HEAD_DIM, (q % 2) * LANES:(q % 2 + 1) * LANES]

    def head_sums(g, states, t_sa, t_y=None):
        rows = [packed([(s * rowp(nkk_ref, t_sa, p)).astype(BF16) for s, p in zip(states, groups[g])])]
        if t_y is not None:
            rows.append(packed([(s * rowp(r_ref, t_y, p)).astype(BF16) for s, p in zip(states, groups[g])]))
        return jnp.dot(cat(rows), ones2, preferred_element_type=F32)

    def value_columns(sub, i, vc_out):
        xs = xs_ref[sub]
        n = xs.shape[0]
        for j0 in range(0, SCAN_BULK, SCAN_BULK // 2):
            steps = [SCAN_BULK * i + j0 + j for j in range(SCAN_BULK // 2)]
            res = jnp.dot(cat([xs * hot_ref[tt, 0:1, :] for tt in steps]), spread, preferred_element_type=F32)
            for j, tt in enumerate(steps):
                vc_out[tt] = res[j * n:(j + 1) * n]

    def step(sub, tt, vc_in, sas):
        t = sub * SCAN_SUB + tt
        t_next = jnp.minimum(t + 1, tb - 1)
        hit = lane_t == t
        new_sas = []
        for g in range(SCAN_MATMULS):
            states = []
            for q, p in enumerate(groups[g]):
                vg, vq = divmod(p, SCAN_GROUP_PAIRS)
                vc = vc_in[tt, vg * R_HEAD_DIM:(vg + 1) * R_HEAD_DIM, vq * LANES:(vq + 1) * LANES]
                s_new = (s_ref[p] * rowp(w_ref, t, p) + pair_tile(sas[g], q) * rowp(b_ref, t, p)
                         + vc * rowp(k_ref, t, p))
                s_ref[p] = s_new
                states.append(s_new)
            res = head_sums(g, states, t_next, t)
            new_sas.append(res[:half])
            for q, p in enumerate(groups[g]):
                y_ref[p] = jnp.where(hit, pair_tile(res[half:], q), y_ref[p])
        return tuple(new_sas)

    def sub_block(sub, vc_in, vc_out, sas):
        nxt = jnp.minimum(sub + 1, n_sub - 1)

        def four_steps(i, sas):
            sas = step(sub, SCAN_BULK * i, vc_in, sas)
            if n_sub > 1:
                value_columns(nxt, i, vc_out)
            for u in range(1, SCAN_BULK):
                sas = step(sub, SCAN_BULK * i + u, vc_in, sas)
            return sas

        return lax.fori_loop(0, SCAN_SUB // SCAN_BULK, four_steps, sas)

    n_sub = tb // SCAN_SUB
    for i in range(SCAN_SUB // SCAN_BULK):
        value_columns(0, i, vc_ref)
    sas = tuple(head_sums(g, [s_ref[p] for p in groups[g]], 0) for g in range(SCAN_MATMULS))
    if n_sub == 1:
        sub_block(0, vc_ref, vc2_ref, sas)
    else:
        def two_sub_blocks(i, sas):
            return sub_block(2 * i + 1, vc2_ref, vc_ref, sub_block(2 * i, vc_ref, vc2_ref, sas))

        lax.fori_loop(0, n_sub // 2, two_sub_blocks, sas)

    @pl.when(tblk == pl.num_programs(1) - 1)
    def _():
        st_ref[...] = s_ref[...]


def _rwkv_scan(r, w, k, nkk, b, v, s0):
    nb, t_len, _ = r.shape
    tb = min(t_len, SCAN_BLOCK)
    nblk = t_len // tb
    hd = R_HEAD_DIM
    nsub, sb, ng, gp = tb // SCAN_SUB, SCAN_SUB, SCAN_GROUPS, SCAN_GROUP_PAIRS
    s0p = s0.reshape(nb, R_PAIRS, 2, hd, hd).transpose(0, 1, 3, 2, 4).reshape(nb, R_PAIRS, hd, LANES)
    lane2_h = jnp.arange(2 * LANES) // hd
    ones2 = (lane2_h[:, None] == lane2_h[None, :]).astype(BF16)
    src = jnp.arange(2 * LANES) % LANES
    src_q, src_h, src_t = src // (2 * sb), (src // sb) % 2, src % sb
    dst = jnp.arange(gp * LANES)
    dst_q, dst_h = dst // LANES, (dst % LANES) // hd
    spread = ((src_q[:, None] == dst_q[None, :]) & (src_h[:, None] == dst_h[None, :])).astype(BF16)
    hot = jnp.broadcast_to((src_t[None, :] == jnp.arange(sb)[:, None])[:, None, :], (sb, 16, 2 * LANES)).astype(BF16)
    trow = pl.BlockSpec((None, tb, R_PAIRS, LANES), lambda bb, i: (bb, i, 0, 0))
    r, w, k, nkk, b = (a.reshape(nb, t_len, R_PAIRS, LANES) for a in (r, w, k, nkk, b))
    st = pl.BlockSpec((None, R_PAIRS, hd, LANES), lambda bb, i: (bb, 0, 0, 0))
    full = lambda a: pl.BlockSpec(a.shape, lambda bb, i: (0,) * a.ndim)
    ycol_spec = pl.BlockSpec((None, None, R_PAIRS, hd, LANES), lambda bb, i: (bb, i, 0, 0, 0))
    ycol, s_t = pl.pallas_call(
        functools.partial(_rwkv_scan_kernel, tb=tb),
        out_shape=(jax.ShapeDtypeStruct((nb, nblk, R_PAIRS, hd, LANES), F32),
                   jax.ShapeDtypeStruct((nb, R_PAIRS, hd, LANES), F32)),
        grid=(nb, nblk),
        in_specs=[trow, trow, trow, trow, trow, pl.BlockSpec((None, tb, R_WIDTH), lambda bb, i: (bb, i, 0)), st,
                  full(ones2), full(hot), full(spread)],
        out_specs=(ycol_spec, st),
        scratch_shapes=[pltpu.VMEM((R_PAIRS, hd, LANES), F32), pltpu.VMEM((nsub, ng * hd, 2 * LANES), BF16),
                        pltpu.VMEM((sb, ng * hd, gp * LANES), F32), pltpu.VMEM((sb, ng * hd, gp * LANES), F32)],
        compiler_params=_cparams(("parallel", "arbitrary")),
        name="rwkv_scan",
    )(r, w, k, nkk, b, v, s0p, ones2, hot, spread)
    s_t = s_t.reshape(nb, R_PAIRS, hd, 2, hd).transpose(0, 1, 3, 2, 4).reshape(nb, R_HEADS, hd, hd)
    return ycol, s_t


def _rwkv_post_kernel(y_ref, r_ref, k_ref, v_ref, g_ref, lw_ref, lb_ref, rk_ref, ones_ref, o_ref, *, tb):
    ones_bd = ones_ref[...]
    tiles = []
    for p in range(R_PAIRS):
        yt = y_ref[p].T
        tiles.append(jnp.concatenate([yt[0:tb, :], yt[R_HEAD_DIM:R_HEAD_DIM + tb, :]], axis=1))
    y = jnp.concatenate(tiles, axis=1)
    mean = _head_sums(y, ones_bd) * (1.0 / R_HEAD_DIM)
    d = y - mean
    var = _head_sums(d * d, ones_bd) * (1.0 / R_HEAD_DIM)
    yn = d * lax.rsqrt(var + GN_EPS) * lw_ref[...] + lb_ref[...]
    bonus = _head_sums(r_ref[...] * k_ref[...] * rk_ref[...], ones_bd) * v_ref[...]
    o_ref[...] = ((yn + bonus) * g_ref[...]).astype(o_ref.dtype)


def _rwkv_post(ycol, r, k, v, g, lnx_w, lnx_b, r_k, ones_bd):
    m = r.shape[0]
    nblk = ycol.shape[1]
    tb = m // (ycol.shape[0] * nblk)
    row = pl.BlockSpec((tb, R_WIDTH), lambda i: (i, 0))
    full = lambda a: pl.BlockSpec(a.shape, lambda i: (0,) * a.ndim)
    params = (lnx_w, lnx_b, r_k, ones_bd)
    return pl.pallas_call(
        functools.partial(_rwkv_post_kernel, tb=tb),
        out_shape=jax.ShapeDtypeStruct((m, R_WIDTH), BF16),
        grid=(m // tb,),
        in_specs=[pl.BlockSpec((None, None, R_PAIRS, R_HEAD_DIM, LANES), lambda i: (i // nblk, i % nblk, 0, 0, 0))]
        + [row] * 4 + [full(p) for p in params],
        out_specs=row,
        compiler_params=_cparams(("parallel",)),
        name="rwkv_post",
    )(ycol, r, k, v, g, *params)


def _ffn_up_kernel(x_ref, xp_ref, wg_ref, wu_ref, c0_ref, cw_ref, cb_ref, act_ref, tail_ref, *,
                   seq_rows, seqs_per_tile, tiles_per_seq):
    first = lax.rem(pl.program_id(0), tiles_per_seq) == 0
    wg = wg_ref[...].astype(BF16)
    gate_all = jnp.dot(x_ref[...], wg, preferred_element_type=F32)
    up_all = jnp.dot(x_ref[...], wu_ref[...].astype(BF16), preferred_element_type=F32)
    gate_prev = jnp.dot(xp_ref[...], wg, preferred_element_type=F32)
    row = lax.broadcasted_iota(I32, (seq_rows, 1), 0)
    for s in range(seqs_per_tile):
        rows = slice(s * seq_rows, (s + 1) * seq_rows)
        gate = gate_all[rows]
        prev1 = jnp.where(first, c0_ref[s, 1:2, :], gate_prev[7:8, :])
        prev2 = jnp.where(first, c0_ref[s, 0:1, :], gate_prev[6:7, :])
        g_m1 = jnp.where(row == 0, prev1, pltpu.roll(gate, 1, axis=0))
        g_m2 = jnp.where(row == 0, prev2, jnp.where(row == 1, prev1, pltpu.roll(gate, 2, axis=0)))
        conv = cb_ref[...] + g_m2 * cw_ref[0:1, :]
        conv = conv + g_m1 * cw_ref[1:2, :]
        conv = conv + gate * cw_ref[2:3, :]
        act_ref[rows, :] = (conv * jax.nn.sigmoid(conv) * up_all[rows]).astype(act_ref.dtype)
        tail_ref[s] = gate[seq_rows - 8:, :]


def _ffn_up(x, w_in, conv0, conv_w, conv_b, n_batch, t_len, tm=1024, tn=256):
    m, d = x.shape
    d_ff = w_in.shape[1] // 2
    tm = min(m, tm)
    seq_rows = min(t_len, tm)
    seqs_per_tile, tiles_per_seq = tm // seq_rows, t_len // seq_rows
    assert m % tm == 0 and tm % seq_rows == 0 and t_len % seq_rows == 0 and seq_rows % 8 == 0 and d_ff % tn == 0
    n_up = d_ff // tn
    act, tail = pl.pallas_call(
        functools.partial(_ffn_up_kernel, seq_rows=seq_rows, seqs_per_tile=seqs_per_tile, tiles_per_seq=tiles_per_seq),
        out_shape=(jax.ShapeDtypeStruct((m, d_ff), BF16),
                   jax.ShapeDtypeStruct((n_batch * tiles_per_seq, 8, d_ff), F32)),
        grid=(m // tm, n_up),
        in_specs=[pl.BlockSpec((tm, d), lambda i, j: (i, 0)),
                  pl.BlockSpec((8, d), lambda i, j: (jnp.maximum(i * (tm // 8) - 1, 0), 0)),
                  pl.BlockSpec((d, tn), lambda i, j: (0, j)),
                  pl.BlockSpec((d, tn), lambda i, j: (0, j + n_up)),
                  pl.BlockSpec((seqs_per_tile, CONV_W - 1, tn), lambda i, j: (i // tiles_per_seq, 0, j)),
                  pl.BlockSpec((CONV_W, tn), lambda i, j: (0, j)),
                  pl.BlockSpec((1, tn), lambda i, j: (0, j))],
        out_specs=(pl.BlockSpec((tm, tn), lambda i, j: (i, j)),
                   pl.BlockSpec((seqs_per_tile, 8, tn), lambda i, j: (i, 0, j))),
        compiler_params=_cparams(("parallel", "arbitrary")),
        name="ffn_up",
    )(x, x, w_in, w_in, conv0, conv_w, conv_b.reshape(1, d_ff))
    gate_tail = tail.reshape(n_batch, tiles_per_seq, 8, d_ff)[:, -1, 8 - (CONV_W - 1):, :]
    return act, gate_tail


def _pad_rw_cols(a):
    z = lambda n: jnp.zeros(a.shape[:-1] + (n,), a.dtype)
    wd0, ad0, gd0 = 3 * R_WIDTH, 3 * R_WIDTH + DECAY_LORA, 3 * R_WIDTH + DECAY_LORA + AAA_LORA
    return jnp.concatenate([a[..., :wd0], a[..., wd0:ad0], z(LORA_PAD - DECAY_LORA), a[..., ad0:gd0],
                            z(LORA_PAD - AAA_LORA), a[..., gd0:]], axis=-1)


def _unpad_rw_cols(a):
    return jnp.concatenate([a[..., :RW_WD0 + DECAY_LORA], a[..., RW_AD0:RW_AD0 + AAA_LORA], a[..., RW_GD0:]], axis=-1)


def _prep_weights(norm_mix_g, w_in, rwkv_mu, rwkv_w0, rwkv_w2, rwkv_a0, rwkv_a2, rwkv_g2, rwkv_k_k, rwkv_k_a,
                  rwkv_r_k, rwkv_lnx_w, rwkv_lnx_b, w_out, norm_ffn_g, ffn_w_in, ffn_conv_w, ffn_conv_b,
                  ffn_w_down, norm_final_g, l):
    d = w_in.shape[1]
    w_att = jnp.concatenate([w_in[l][:, :ATT_USED].astype(BF16), jnp.zeros((d, ATT_COLS - ATT_USED), BF16)], axis=1)
    w_rw = _pad_rw_cols(w_in[l][:, ATT_USED:].astype(BF16))
    row = lambda a: a.reshape(1, -1).astype(F32)
    pad_rows = lambda a, n: jnp.concatenate([a, jnp.zeros((n - a.shape[0], a.shape[1]), a.dtype)], axis=0)
    lane_h = jnp.arange(LANES) // R_HEAD_DIM
    w2_pad = pad_rows(rwkv_w2[l].astype(F32), LORA_PAD)
    w2_hi = w2_pad.astype(BF16)
    w2_mid = (w2_pad - w2_hi.astype(F32)).astype(BF16)
    return dict(
        norm_mix_g=norm_mix_g[l], w_att=w_att, w_rw=w_rw,
        mu=_pad_rw_cols(row(rwkv_mu[l])), w0=row(rwkv_w0[l]), a0=row(rwkv_a0[l]),
        k_k=row(rwkv_k_k[l]), k_a=row(rwkv_k_a[l]),
        w2=jnp.concatenate([w2_hi, w2_mid, w2_hi], axis=0), a2=pad_rows(rwkv_a2[l], LORA_PAD).astype(BF16),
        g2=rwkv_g2[l].astype(BF16),
        r_k=row(rwkv_r_k[l]), lnx_w=row(rwkv_lnx_w[l]), lnx_b=row(rwkv_lnx_b[l]),
        ones_bd=jnp.concatenate([lane_h[:, None] == lane_h[None, :]] * 2, axis=0).astype(BF16),
        w_out_a=w_out[l][:A_WIDTH].astype(BF16), w_out_r=w_out[l][A_WIDTH:].astype(BF16),
        norm_ffn_g=norm_ffn_g[l], ffn_w_in=ffn_w_in[l], conv_w=ffn_conv_w[l], conv_b=ffn_conv_b[l],
        ffn_w_down=ffn_w_down[l].astype(BF16), norm_final_g=norm_final_g,
    )


def _trunk(x, past_k, past_v, past_ik, s0, shift0, conv0, wt):
    nb, t_len, d = x.shape
    m = nb * t_len
    p_len = 0 if past_k is None else past_k.shape[1]
    x2 = x.reshape(m, d)

    h = _rmsnorm(x2, wt["norm_mix_g"], BF16)
    z_att = _matmul([h], [wt["w_att"]], tm=2048, name="proj_att")
    z_rw = _matmul([h], [wt["w_rw"]], tm=2048, name="proj_rw")

    q_bf, k_f, k_bf, v_f, v_bf, iq_bf, ikw, ik_bf = _rope_split(z_att, t_len, p_len)
    past = None if past_k is None else (past_k.reshape(nb, p_len, KV_W), past_v.reshape(nb, p_len, KV_W), past_ik)
    attn = _dsa(q_bf, iq_bf, ikw, k_bf, v_bf, ik_bf, past, n_batch=nb, t_len=t_len)

    z_rw3 = z_rw.reshape(nb, t_len, RW_COLS)
    r, w, k2, v2, nkk, b, g = _rwkv_pre(z_rw3, _pad_rw_cols(shift0.astype(F32)), wt["mu"], wt["w0"], wt["a0"],
                                       wt["k_k"], wt["k_a"], wt["w2"], wt["a2"], wt["g2"], wt["ones_bd"])
    ycol, s_t = _rwkv_scan(r, w, k2, nkk, b, v2, s0.astype(F32))
    flat = lambda a: a.reshape(m, R_WIDTH)
    rw = _rwkv_post(ycol, flat(r), flat(k2), flat(v2), flat(g), wt["lnx_w"], wt["lnx_b"], wt["r_k"], wt["ones_bd"])

    x1 = _matmul([attn, rw], [wt["w_out_a"], wt["w_out_r"]], res=x2, tm=1024, name="out_proj")
    hf = _rmsnorm(x1, wt["norm_ffn_g"], BF16)
    act, conv_t = _ffn_up(hf, wt["ffn_w_in"], conv0.astype(F32), wt["conv_w"], wt["conv_b"], nb, t_len)
    x3 = _matmul([act], [wt["ffn_w_down"]], res=x1, name="ffn_down")
    y_out = _rmsnorm(x3, wt["norm_final_g"], F32).reshape(nb, t_len, d)

    shift_t = _unpad_rw_cols(z_rw3[:, -1:])
    caches = (k_f.reshape(nb, t_len, A_KV_HEADS, A_HEAD_DIM)[None], v_f.reshape(nb, t_len, A_KV_HEADS, A_HEAD_DIM)[None],
              ikw[:, :IDX_DIM].reshape(nb, t_len, IDX_DIM)[None], s_t[None], shift_t[None], conv_t[None])
    return y_out, caches


def kernel(x_prompt, x_sample, cache_k, cache_v, cache_idx_k, state_rwkv, state_rwkv_shift, state_ffn_conv, norm_mix_g, w_in, rwkv_mu, rwkv_w0, rwkv_w2, rwkv_a0, rwkv_a2, rwkv_g2, rwkv_k_k, rwkv_k_a, rwkv_r_k, rwkv_lnx_w, rwkv_lnx_b, w_out, norm_ffn_g, ffn_w_in, ffn_conv_w, ffn_conv_b, ffn_w_down, norm_final_g):
    assert w_in.shape[0] == 1, "single-layer trunk"
    wt = _prep_weights(norm_mix_g, w_in, rwkv_mu, rwkv_w0, rwkv_w2, rwkv_a0, rwkv_a2, rwkv_g2, rwkv_k_k, rwkv_k_a,
                       rwkv_r_k, rwkv_lnx_w, rwkv_lnx_b, w_out, norm_ffn_g, ffn_w_in, ffn_conv_w, ffn_conv_b,
                       ffn_w_down, norm_final_g, 0)
    bp = x_prompt.shape[0]
    d_ff = ffn_conv_w.shape[-1]
    y_p, c_p = _trunk(x_prompt, None, None, None,
                      jnp.zeros((bp, R_HEADS, R_HEAD_DIM, R_HEAD_DIM), F32), jnp.zeros((bp, 1, RWKV_COLS), F32),
                      jnp.zeros((bp, CONV_W - 1, d_ff), F32), wt)
    y_s, c_s = _trunk(x_sample, cache_k[0], cache_v[0], cache_idx_k[0], state_rwkv[0], state_rwkv_shift[0],
                      state_ffn_conv[0], wt)
    return (y_p, y_s) + c_p + c_s
```

```python
import functools

import jax
import jax.numpy as jnp
from jax import lax
from jax.experimental import pallas as pl
from jax.experimental.pallas import tpu as pltpu

F32 = jnp.float32
BF16 = jnp.bfloat16
I32 = jnp.int32

CHUNK = 64
A_HEADS = 16
A_KV_HEADS = 4
A_HEAD_DIM = 128
A_GROUP = A_HEADS // A_KV_HEADS
A_WIDTH = A_HEADS * A_HEAD_DIM
KV_W = A_KV_HEADS * A_HEAD_DIM
IDX_HEADS = 16
IDX_DIM = 64
IQ_W = IDX_HEADS * IDX_DIM
TOPK_MAX = 256
ROPE_THETA = 500000.0
ROPE_FRAC = 4
A_SCALE = A_HEAD_DIM ** -0.5
Q_SCALE = A_SCALE * 1.4426950408889634
IDX_SCALE = (IDX_HEADS ** -0.5) * (IDX_DIM ** -0.5)
R_HEAD_DIM = 64
R_WIDTH = 2048
R_HEADS = R_WIDTH // R_HEAD_DIM
R_PAIRS = R_HEADS // 2
DECAY_LORA = 96
AAA_LORA = 96
GATE_LORA = 256
RWKV_COLS = 3 * R_WIDTH + DECAY_LORA + AAA_LORA + GATE_LORA
GN_EPS = 6.4e-4
CONV_W = 3
RMS_EPS = 1e-6

LANES = 128
VMEM_LIMIT = 56 * 1024 * 1024

ATT_Q0, ATT_K0, ATT_V0, ATT_IQ0, ATT_IK0 = 0, A_WIDTH, A_WIDTH + KV_W, A_WIDTH + 2 * KV_W, A_WIDTH + 2 * KV_W + IQ_W
ATT_USED = ATT_IK0 + IDX_DIM + IDX_HEADS
ATT_COLS = 4608
LORA_PAD = 128
RW_WD0 = 3 * R_WIDTH
RW_AD0 = RW_WD0 + LORA_PAD
RW_GD0 = RW_AD0 + LORA_PAD
RW_COLS = RW_GD0 + GATE_LORA
DSA_TQ, DSA_TK = 256, 512
COUNT_ROWS = 128
SCAN_BLOCK = 64
SCAN_SUB = 16
SCAN_GROUP_PAIRS = LANES // (2 * SCAN_SUB)
SCAN_GROUPS = R_PAIRS // SCAN_GROUP_PAIRS
SCAN_MATMULS = 4
SCAN_BULK = 8
INT_MIN = -2 ** 31
NEG_BIG = -1e30


def _cparams(sem):
    return pltpu.CompilerParams(dimension_semantics=sem, vmem_limit_bytes=VMEM_LIMIT)


def _rmsnorm_kernel(x_ref, g_ref, o_ref):
    x = x_ref[...]
    y = x * lax.rsqrt(jnp.mean(x * x, axis=-1, keepdims=True) + RMS_EPS)
    o_ref[...] = (y * g_ref[...]).astype(o_ref.dtype)


def _rmsnorm(x, g, out_dtype):
    m, d = x.shape
    tm = min(m, 256)
    return pl.pallas_call(
        _rmsnorm_kernel,
        out_shape=jax.ShapeDtypeStruct((m, d), out_dtype),
        grid=(m // tm,),
        in_specs=[pl.BlockSpec((tm, d), lambda i: (i, 0)), pl.BlockSpec((1, d), lambda i: (0, 0))],
        out_specs=pl.BlockSpec((tm, d), lambda i: (i, 0)),
        compiler_params=_cparams(("parallel",)),
        name="rmsnorm",
    )(x, g.reshape(1, d).astype(F32))


def _mm_kernel(*refs, n_pairs, has_res):
    o_ref = refs[-1]
    acc = jnp.dot(refs[0][...], refs[n_pairs][...], preferred_element_type=F32)
    for p in range(1, n_pairs):
        acc = acc + jnp.dot(refs[p][...], refs[n_pairs + p][...], preferred_element_type=F32)
    if has_res:
        acc = refs[2 * n_pairs][...] + acc
    o_ref[...] = acc.astype(o_ref.dtype)


def _matmul(a_list, b_list, res=None, tm=512, tn=512, name="matmul"):
    m = a_list[0].shape[0]
    n = b_list[0].shape[1]
    tm = min(tm, m)
    tn = min(tn, n)
    assert m % tm == 0 and n % tn == 0, (m, n, tm, tn)
    in_specs = [pl.BlockSpec((tm, a.shape[1]), lambda i, j: (i, 0)) for a in a_list]
    in_specs += [pl.BlockSpec((b.shape[0], tn), lambda i, j: (0, j)) for b in b_list]
    args = list(a_list) + list(b_list)
    if res is not None:
        in_specs.append(pl.BlockSpec((tm, tn), lambda i, j: (i, j)))
        args.append(res)
    return pl.pallas_call(
        functools.partial(_mm_kernel, n_pairs=len(a_list), has_res=res is not None),
        out_shape=jax.ShapeDtypeStruct((m, n), F32),
        grid=(m // tm, n // tn),
        in_specs=in_specs,
        out_specs=pl.BlockSpec((tm, tn), lambda i, j: (i, j)),
        compiler_params=_cparams(("parallel", "arbitrary")),
        name=name,
    )(*args)


def _rope_tile(x, cos, sin, half, d_in_head):
    lo = d_in_head < half
    hi = (d_in_head >= half) & (d_in_head < 2 * half)
    c = jnp.where(lo | hi, cos, 1.0)
    s_up = jnp.where(lo, -sin, 0.0)
    s_dn = jnp.where(hi, sin, 0.0)
    x_up = pltpu.roll(x, LANES - half, axis=1)
    x_dn = pltpu.roll(x, half, axis=1)
    return x * c + x_up * s_up + x_dn * s_dn


def _rope_kernel(z_ref, invf_ref, q_ref, kf_ref, kb_ref, vf_ref, vb_ref, iq_ref, ikw_ref, ikb_ref, *, tm, t_len, pos0):
    i = pl.program_id(0)
    row = lax.broadcasted_iota(I32, (tm, LANES), 0) + i * tm
    pos = (pos0 + lax.rem(row, t_len)).astype(F32)
    lane = lax.broadcasted_iota(I32, (tm, LANES), 1)
    ang = pos * invf_ref[0:1, :]
    cos_a, sin_a = jnp.cos(ang), jnp.sin(ang)
    half_a = A_HEAD_DIM // ROPE_FRAC // 2
    for h in range(A_HEADS):
        x = z_ref[:, ATT_Q0 + h * LANES:ATT_Q0 + (h + 1) * LANES]
        q_ref[:, h * LANES:(h + 1) * LANES] = (_rope_tile(x, cos_a, sin_a, half_a, lane) * Q_SCALE).astype(q_ref.dtype)
    for h in range(A_KV_HEADS):
        x = z_ref[:, ATT_K0 + h * LANES:ATT_K0 + (h + 1) * LANES]
        y = _rope_tile(x, cos_a, sin_a, half_a, lane)
        kf_ref[:, h * LANES:(h + 1) * LANES] = y
        kb_ref[:, h * LANES:(h + 1) * LANES] = y.astype(kb_ref.dtype)
    v = z_ref[:, ATT_V0:ATT_V0 + KV_W]
    vf_ref[...] = v
    vb_ref[...] = v.astype(vb_ref.dtype)
    ang = pos * invf_ref[1:2, :]
    cos_i, sin_i = jnp.cos(ang), jnp.sin(ang)
    half_i = IDX_DIM // ROPE_FRAC // 2
    d_i = lane & (IDX_DIM - 1)
    for h in range(IQ_W // LANES):
        x = z_ref[:, ATT_IQ0 + h * LANES:ATT_IQ0 + (h + 1) * LANES]
        iq_ref[:, h * LANES:(h + 1) * LANES] = _rope_tile(x, cos_i, sin_i, half_i, d_i).astype(iq_ref.dtype)
    x = z_ref[:, ATT_IK0:ATT_IK0 + LANES]
    d_k = jnp.where(lane < IDX_DIM, lane, IDX_DIM)
    y = _rope_tile(x, cos_i, sin_i, half_i, d_k)
    ikw_ref[...] = y
    ikb_ref[...] = y[:, :IDX_DIM].astype(ikb_ref.dtype)


def _rope_split(z_att, t_len, pos0):
    m = z_att.shape[0]
    tm = min(m, 256)
    lane = jnp.arange(LANES)
    rd_a = A_HEAD_DIM // ROPE_FRAC
    rd_i = IDX_DIM // ROPE_FRAC
    invf_a = ROPE_THETA ** (-((lane % (rd_a // 2)).astype(F32) * 2.0 / rd_a))
    invf_i = ROPE_THETA ** (-((lane % (rd_i // 2)).astype(F32) * 2.0 / rd_i))
    invf = jnp.zeros((8, LANES), F32).at[0].set(invf_a).at[1].set(invf_i)
    row_spec = lambda w: pl.BlockSpec((tm, w), lambda i: (i, 0))
    shp = lambda w, dt: jax.ShapeDtypeStruct((m, w), dt)
    return pl.pallas_call(
        functools.partial(_rope_kernel, tm=tm, t_len=t_len, pos0=pos0),
        out_shape=(shp(A_WIDTH, BF16), shp(KV_W, F32), shp(KV_W, BF16), shp(KV_W, F32), shp(KV_W, BF16),
                   shp(IQ_W, BF16), shp(LANES, F32), shp(IDX_DIM, BF16)),
        grid=(m // tm,),
        in_specs=[row_spec(ATT_COLS), pl.BlockSpec((8, LANES), lambda i: (0, 0))],
        out_specs=(row_spec(A_WIDTH), row_spec(KV_W), row_spec(KV_W), row_spec(KV_W), row_spec(KV_W),
                   row_spec(IQ_W), row_spec(LANES), row_spec(IDX_DIM)),
        compiler_params=_cparams(("parallel",)),
        name="rope_split",
    )(z_att, invf)


def _tile(x, n, axis):
    return x if n == 1 else jnp.concatenate([x] * n, axis=axis)


def _sortable(score):
    u = lax.bitcast_convert_type(score, I32)
    return jnp.where(u < 0, u ^ jnp.int32(0x7FFFFFFF), u)


def _dsa_kernel(*refs, tq, tk, pos0, n_keys, topk):
    i = pl.program_id(1)
    if pos0 > 0:
        (q_ref, iq_ref, ikw_ref, kn_ref, vn_ref, ikn_ref, pk_ref, pv_ref, pik_ref, o_ref,
         key_ref, iwb_ref, qs_ref, m_ref, l_ref, acc_ref, k_ref, v_ref, ik_ref) = refs
        t_new = kn_ref.shape[0]

        @pl.when(i == 0)
        def _():
            for new_ref, past_ref, all_ref in ((kn_ref, pk_ref, k_ref), (vn_ref, pv_ref, v_ref), (ikn_ref, pik_ref, ik_ref)):
                for r0 in range(0, pos0, DSA_TK):
                    r1 = min(r0 + DSA_TK, pos0)
                    all_ref[r0:r1, :] = past_ref[r0:r1, :].astype(all_ref.dtype)
                all_ref[pos0:pos0 + t_new, :] = new_ref[...]
                tail = all_ref.shape[0] - pos0 - t_new
                if tail:
                    all_ref[pos0 + t_new:, :] = jnp.zeros((tail, all_ref.shape[1]), all_ref.dtype)
    else:
        (q_ref, iq_ref, ikw_ref, k_ref, v_ref, ik_ref, o_ref,
         key_ref, iwb_ref, qs_ref, m_ref, l_ref, acc_ref) = refs
    q0 = pos0 + i * tq
    kmax = jnp.minimum((lax.div(q0 + tq - 1, CHUNK) + 1) * CHUNK, n_keys)
    nkb = lax.div(kmax + tk - 1, tk)
    n_rep = tk // LANES
    nt_dims = (((1,), (1,)), ((), ()))

    for h in range(IDX_HEADS):
        iwb_ref[h] = jnp.broadcast_to(ikw_ref[:, IDX_DIM + h:IDX_DIM + h + 1], (tq, LANES))
    for h in range(A_HEADS):
        qs_ref[h * tq:(h + 1) * tq, :] = q_ref[:, h * LANES:(h + 1) * LANES]

    def score_block(kb, carry):
        koff = pl.multiple_of(kb * tk, tk)
        ikb = ik_ref[pl.ds(koff, tk), :]
        acc = jnp.zeros((tq, tk), F32)
        for h in range(IDX_HEADS):
            d = lax.dot_general(iq_ref[:, h * IDX_DIM:(h + 1) * IDX_DIM], ikb, nt_dims, preferred_element_type=F32)
            acc = acc + jnp.maximum(d, 0.0) * _tile(iwb_ref[h], n_rep, 1)
        kpos = koff + lax.broadcasted_iota(I32, (tq, tk), 1)
        qpos = q0 + lax.broadcasted_iota(I32, (tq, tk), 0)
        adm = (lax.shift_right_logical(kpos, 6) <= lax.shift_right_logical(qpos, 6)) & (kpos < n_keys)
        key_ref[:, pl.ds(koff, tk)] = jnp.where(adm, _sortable(acc * IDX_SCALE), jnp.int32(INT_MIN))
        return carry

    lax.fori_loop(0, nkb, score_block, 0)

    def count(pred, *row_args):
        cr = min(tq, COUNT_ROWS)

        def chunk(r0):
            def body(kb, cnt):
                koff = pl.multiple_of(kb * tk, tk)
                keys = key_ref[r0:r0 + cr, pl.ds(koff, tk)]
                kpos = koff + lax.broadcasted_iota(I32, (cr, tk), 1)
                hit = jnp.where(pred(keys, kpos, *[a[r0:r0 + cr] for a in row_args]), 1.0, 0.0)
                for c in range(n_rep):
                    cnt = cnt + hit[:, c * LANES:(c + 1) * LANES]
                return cnt
            cnt = lax.fori_loop(0, nkb, body, jnp.zeros((cr, LANES), F32))
            return jnp.broadcast_to(jnp.sum(cnt, axis=-1, keepdims=True), (cr, LANES))

        return jnp.concatenate([chunk(r0) for r0 in range(0, tq, cr)], axis=0)

    def wide(x):
        return _tile(x, n_rep, 1)

    def bit_step(it, tu):
        cand_u = tu | lax.shift_left(jnp.int32(1), 31 - it)
        cand_s = wide(cand_u ^ jnp.int32(INT_MIN))
        cnt = count(lambda keys, kpos, c: keys >= c, cand_s)
        return jnp.where(cnt >= topk, cand_u, tu)

    tu = lax.fori_loop(0, 32, bit_step, jnp.zeros((tq, LANES), I32))
    thr = jnp.maximum(tu ^ jnp.int32(INT_MIN), jnp.int32(INT_MIN + 1))
    thr_w = wide(thr)

    n_ge = count(lambda keys, kpos, t: keys >= t, thr_w)
    n_gt = count(lambda keys, kpos, t: keys > t, thr_w)
    excess = n_ge > topk

    @pl.when(jnp.max(jnp.where(excess, 1.0, 0.0)) > 0.0)
    def _():
        need = topk - n_gt

        idx_bits = int(key_ref.shape[1]).bit_length()

        def idx_step(it, jm):
            cand = wide(jm | lax.shift_left(jnp.int32(1), idx_bits - 1 - it))
            cnt = count(lambda keys, kpos, t, c: (keys == t) & (kpos < c), thr_w, cand)
            return jnp.where(cnt < need, cand[:, :LANES], jm)

        jm = lax.fori_loop(0, idx_bits, idx_step, jnp.zeros((tq, LANES), I32))
        jm_w = wide(jnp.where(excess, jm, jnp.int32(2 ** 31 - 1)))

        def drop(kb, carry):
            koff = pl.multiple_of(kb * tk, tk)
            keys = key_ref[:, pl.ds(koff, tk)]
            kpos = koff + lax.broadcasted_iota(I32, (tq, tk), 1)
            key_ref[:, pl.ds(koff, tk)] = jnp.where((keys == thr_w) & (kpos > jm_w), jnp.int32(INT_MIN), keys)
            return carry

        lax.fori_loop(0, nkb, drop, 0)

    m_ref[...] = jnp.full(m_ref.shape, NEG_BIG, F32)
    l_ref[...] = jnp.zeros(l_ref.shape, F32)
    acc_ref[...] = jnp.zeros(acc_ref.shape, F32)
    rows = A_GROUP * tq

    def attend(kb, carry):
        koff = pl.multiple_of(kb * tk, tk)
        bias = _tile(jnp.where(key_ref[:, pl.ds(koff, tk)] >= thr_w, 0.0, NEG_BIG), A_GROUP, 0)
        for n in range(A_KV_HEADS):
            r0 = n * rows
            kn = k_ref[pl.ds(koff, tk), n * LANES:(n + 1) * LANES]
            vn = v_ref[pl.ds(koff, tk), n * LANES:(n + 1) * LANES]
            s = lax.dot_general(qs_ref[r0:r0 + rows, :], kn, nt_dims, preferred_element_type=F32) + bias
            m_prev = m_ref[r0:r0 + rows, :]
            m_new = jnp.maximum(m_prev, jnp.max(s, axis=-1, keepdims=True))
            alpha = jnp.exp2(m_prev - m_new)
            p = jnp.exp2(s - _tile(m_new, n_rep, 1))
            l_ref[r0:r0 + rows, :] = alpha * l_ref[r0:r0 + rows, :] + jnp.sum(p, axis=-1, keepdims=True)
            acc_ref[r0:r0 + rows, :] = alpha * acc_ref[r0:r0 + rows, :] + jnp.dot(
                p.astype(vn.dtype), vn, preferred_element_type=F32)
            m_ref[r0:r0 + rows, :] = m_new
        return carry

    lax.fori_loop(0, nkb, attend, 0)
    for h in range(A_HEADS):
        o_ref[:, h * LANES:(h + 1) * LANES] = (
            acc_ref[h * tq:(h + 1) * tq, :] / l_ref[h * tq:(h + 1) * tq, :]).astype(o_ref.dtype)


def _dsa(q_bf, iq_bf, ikw, k_new, v_new, ik_new, past, *, n_batch, t_len):
    pos0 = 0 if past is None else past[0].shape[1]
    n_keys = pos0 + t_len
    tq = min(t_len, DSA_TQ)
    lp = -(-n_keys // LANES) * LANES
    tk = lp if tq * lp <= DSA_TQ * DSA_TK else DSA_TK
    lp = -(-lp // tk) * tk
    assert t_len % tq == 0 and (past is None or (lp == tk and pos0 % 16 == 0 and t_len % 16 == 0)) \
        and (past is not None or lp == n_keys)
    nq = t_len // tq
    topk = min(TOPK_MAX, n_keys // 4)
    qrow = lambda w: pl.BlockSpec((tq, w), lambda b, i: (b * nq + i, 0))
    new3 = lambda a: a.reshape(n_batch, t_len, a.shape[-1])
    whole = lambda rows, w, **kw: pl.BlockSpec((None, rows, w), lambda b, i: (b, 0, 0), **kw)
    in_specs = [qrow(A_WIDTH), qrow(IQ_W), qrow(LANES)]
    args = [q_bf, iq_bf, ikw, new3(k_new), new3(v_new), new3(ik_new)]
    scratch = [
        pltpu.VMEM((tq, lp), I32),
        pltpu.VMEM((IDX_HEADS, tq, LANES), F32),
        pltpu.VMEM((A_HEADS * tq, LANES), BF16),
        pltpu.VMEM((A_HEADS * tq, LANES), F32),
        pltpu.VMEM((A_HEADS * tq, LANES), F32),
        pltpu.VMEM((A_HEADS * tq, LANES), F32),
    ]
    if past is None:
        in_specs += [whole(lp, w, pipeline_mode=pl.Buffered(1)) for w in (KV_W, KV_W, IDX_DIM)]
    else:
        in_specs += [whole(t_len, w) for w in (KV_W, KV_W, IDX_DIM)] + [whole(pos0, w) for w in (KV_W, KV_W, IDX_DIM)]
        args += list(past)
        scratch += [pltpu.VMEM((lp, w), BF16) for w in (KV_W, KV_W, IDX_DIM)]
    return pl.pallas_call(
        functools.partial(_dsa_kernel, tq=tq, tk=tk, pos0=pos0, n_keys=n_keys, topk=float(topk)),
        out_shape=jax.ShapeDtypeStruct((n_batch * t_len, A_WIDTH), BF16),
        grid=(n_batch, nq),
        in_specs=in_specs,
        out_specs=qrow(A_WIDTH),
        scratch_shapes=scratch,
        compiler_params=_cparams(("parallel", "arbitrary")),
        name="dsa",
    )(*args)


def _head_sums(x, ones2):
    n = x.shape[1] // LANES
    tm = x.shape[0]
    hi, mid = _split2(jnp.concatenate([x[:, c * LANES:(c + 1) * LANES] for c in range(n)], axis=0))
    s = jnp.dot(jnp.concatenate([hi, mid], axis=1), ones2, preferred_element_type=F32)
    return jnp.concatenate([s[c * tm:(c + 1) * tm, :] for c in range(n)], axis=1)


def _rwkv_pre_kernel(z_ref, zp_ref, z0_ref, mu_ref, w0_ref, a0_ref, kk_ref, ka_ref, w2_ref, a2_ref, g2_ref,
                     ones_ref, r_out, w_out, k_out, v_out, nkk_out, b_out, g_out, *, tm):
    i = pl.program_id(1)
    row = lax.broadcasted_iota(I32, (tm, 1), 0)

    def mixed(c0, width):
        z = z_ref[:, c0:c0 + width]
        first = jnp.where(i == 0, z0_ref[:, c0:c0 + width], zp_ref[7:8, c0:c0 + width])
        shifted = jnp.where(row == 0, first, pltpu.roll(z, 1, axis=0))
        return z + (shifted - z) * mu_ref[:, c0:c0 + width]

    r = mixed(0, R_WIDTH)
    k = mixed(R_WIDTH, R_WIDTH)
    v = mixed(2 * R_WIDTH, R_WIDTH)
    wd = mixed(RW_WD0, LORA_PAD)
    ad = mixed(RW_AD0, LORA_PAD)
    gd = mixed(RW_GD0, GATE_LORA)
    lora = lambda x, w_ref: jnp.dot(x.astype(BF16), w_ref[...], preferred_element_type=F32)
    t_hi, t_mid = _split2(jnp.tanh(wd))
    y = -(w0_ref[...] + jnp.dot(jnp.concatenate([t_hi, t_hi, t_mid], axis=1), w2_ref[...], preferred_element_type=F32))
    softplus = jnp.maximum(y, 0.0) + jnp.log(1.0 + jnp.exp(-jnp.abs(y)))
    decay = jnp.exp(-jnp.exp(-softplus - 0.5))
    a = jax.nn.sigmoid(a0_ref[...] + lora(ad, a2_ref))
    g = lora(jax.nn.sigmoid(gd), g2_ref)
    kk = k * kk_ref[...]
    kk = kk / jnp.maximum(jnp.sqrt(_head_sums(kk * kk, ones_ref[...])), 1e-12)
    r_out[...] = r
    w_out[...] = decay
    k_out[...] = k * (1.0 + (a - 1.0) * ka_ref[...])
    v_out[...] = v
    nkk_out[...] = -kk
    b_out[...] = kk * a
    g_out[...] = g


def _rwkv_pre(z_rw, z0, mu, w0, a0, k_k, k_a, w2, a2, g2, ones_bd):
    nb, t_len, _ = z_rw.shape
    tm = min(t_len, 128)
    zrow = pl.BlockSpec((None, tm, RW_COLS), lambda b, i: (b, i, 0))
    zprev = pl.BlockSpec((None, 8, RW_COLS), lambda b, i: (b, jnp.maximum(i * (tm // 8) - 1, 0), 0))
    full = lambda a: pl.BlockSpec(a.shape, lambda b, i: (0,) * a.ndim)
    orow = pl.BlockSpec((None, tm, R_WIDTH), lambda b, i: (b, i, 0))
    params = (mu, w0, a0, k_k, k_a, w2, a2, g2, ones_bd)
    return pl.pallas_call(
        functools.partial(_rwkv_pre_kernel, tm=tm),
        out_shape=tuple(jax.ShapeDtypeStruct((nb, t_len, R_WIDTH), F32) for _ in range(7)),
        grid=(nb, t_len // tm),
        in_specs=[zrow, zprev, pl.BlockSpec((None, 1, RW_COLS), lambda b, i: (b, 0, 0))] + [full(p) for p in params],
        out_specs=tuple(orow for _ in range(7)),
        compiler_params=_cparams(("parallel", "arbitrary")),
        name="rwkv_pre",
    )(z_rw, z_rw, z0, *params)


def _split2(x):
    hi = x.astype(BF16)
    return hi, (x - hi.astype(F32)).astype(BF16)


def _rwkv_scan_kernel(r_ref, w_ref, k_ref, nkk_ref, b_ref, v_ref, s0_ref, ones2_ref, hot_ref, spread_ref,
                      y_ref, st_ref, s_ref, xs_ref, vc_ref, vc2_ref, *, tb):
    tblk = pl.program_id(1)
    cat = lambda xs, ax=0: jnp.concatenate(xs, axis=ax)

    @pl.when(tblk == 0)
    def _():
        s_ref[...] = s0_ref[...]

    pad = jnp.zeros((SCAN_BLOCK - tb, LANES), F32)
    v_cols = []
    for p in range(R_PAIRS):
        vp = v_ref[:, p * LANES:(p + 1) * LANES]
        v_cols.append((cat([vp, pad]) if tb < SCAN_BLOCK else vp).T)
    for sub in range(tb // SCAN_SUB):
        for g in range(SCAN_GROUPS):
            x = cat([v_cols[SCAN_GROUP_PAIRS * g + q][h * R_HEAD_DIM:(h + 1) * R_HEAD_DIM,
                                                      sub * SCAN_SUB:(sub + 1) * SCAN_SUB]
                     for q in range(SCAN_GROUP_PAIRS) for h in range(2)], 1)
            hi, mid = _split2(x)
            xs_ref[sub, g * R_HEAD_DIM:(g + 1) * R_HEAD_DIM, :] = cat([hi, mid], 1)
    ones2 = ones2_ref[...]
    spread = spread_ref[...]
    y_ref[...] = jnp.zeros(y_ref.shape, F32)
    lane_t = lax.rem(lax.broadcasted_iota(I32, (R_HEAD_DIM, LANES), 1), R_HEAD_DIM)
    rowp = lambda ref, t, p: ref[t, p:p + 1, :]
    group_pairs = R_PAIRS // SCAN_MATMULS
    half = group_pairs // 2 * R_HEAD_DIM

    groups = [range(g * group_pairs, (g + 1) * group_pairs) for g in range(SCAN_MATMULS)]

    def packed(xs):
        return cat([cat(xs[2 * j:2 * j + 2], 1) for j in range(group_pairs // 2)])

    def pair_tile(res, q):
        return res[(q // 2) * R_HEAD_DIM:(q // 2 + 1) * R_HEAD_DIM, (q % 2) * LANES:(q % 2 + 1) * LANES]

    def head_sums(g, states, t_sa, t_y=None):
        rows = [packed([(s * rowp(nkk_ref, t_sa, p)).astype(BF16) for s, p in zip(states, groups[g])])]
        if t_y is not None:
            rows.append(packed([(s * rowp(r_ref, t_y, p)).astype(BF16) for s, p in zip(states, groups[g])]))
        return jnp.dot(cat(rows), ones2, preferred_element_type=F32)

    def value_columns(sub, i, vc_out):
        xs = xs_ref[sub]
        n = xs.shape[0]
        for j0 in range(0, SCAN_BULK, SCAN_BULK // 2):
            steps = [SCAN_BULK * i + j0 + j for j in range(SCAN_BULK // 2)]
            res = jnp.dot(cat([xs * hot_ref[tt, 0:1, :] for tt in steps]), spread, preferred_element_type=F32)
            for j, tt in enumerate(steps):
                vc_out[tt] = res[j * n:(j + 1) * n]

    def step(sub, tt, vc_in, sas):
        t = sub * SCAN_SUB + tt
        t_next = jnp.minimum(t + 1, tb - 1)
        hit = lane_t == t
        new_sas = []
        for g in range(SCAN_MATMULS):
            states = []
            for q, p in enumerate(groups[g]):
                vg, vq = divmod(p, SCAN_GROUP_PAIRS)
                vc = vc_in[tt, vg * R_HEAD_DIM:(vg + 1) * R_HEAD_DIM, vq * LANES:(vq + 1) * LANES]
                s_new = (s_ref[p] * rowp(w_ref, t, p) + pair_tile(sas[g], q) * rowp(b_ref, t, p)
                         + vc * rowp(k_ref, t, p))
                s_ref[p] = s_new
                states.append(s_new)
            res = head_sums(g, states, t_next, t)
            new_sas.append(res[:half])
            for q, p in enumerate(groups[g]):
                y_ref[p] = jnp.where(hit, pair_tile(res[half:], q), y_ref[p])
        return tuple(new_sas)

    def sub_block(sub, vc_in, vc_out, sas):
        nxt = jnp.minimum(sub + 1, n_sub - 1)

        def bulk_steps(i, sas):
            sas = step(sub, SCAN_BULK * i, vc_in, sas)
            if n_sub > 1:
                value_columns(nxt, i, vc_out)
            for u in range(1, SCAN_BULK):
                sas = step(sub, SCAN_BULK * i + u, vc_in, sas)
            return sas

        return lax.fori_loop(0, SCAN_SUB // SCAN_BULK, bulk_steps, sas)

    n_sub = tb // SCAN_SUB
    for i in range(SCAN_SUB // SCAN_BULK):
        value_columns(0, i, vc_ref)
    sas = tuple(head_sums(g, [s_ref[p] for p in groups[g]], 0) for g in range(SCAN_MATMULS))
    if n_sub == 1:
        sub_block(0, vc_ref, vc2_ref, sas)
    else:
        def two_sub_blocks(i, sas):
            return sub_block(2 * i + 1, vc2_ref, vc_ref, sub_block(2 * i, vc_ref, vc2_ref, sas))

        lax.fori_loop(0, n_sub // 2, two_sub_blocks, sas)

    @pl.when(tblk == pl.num_programs(1) - 1)
    def _():
        st_ref[...] = s_ref[...]


def _rwkv_scan(r, w, k, nkk, b, v, s0):
    nb, t_len, _ = r.shape
    tb = min(t_len, SCAN_BLOCK)
    nblk = t_len // tb
    hd = R_HEAD_DIM
    nsub, sb, ng, gp = tb // SCAN_SUB, SCAN_SUB, SCAN_GROUPS, SCAN_GROUP_PAIRS
    s0p = s0.reshape(nb, R_PAIRS, 2, hd, hd).transpose(0, 1, 3, 2, 4).reshape(nb, R_PAIRS, hd, LANES)
    lane2_h = jnp.arange(2 * LANES) // hd
    ones2 = (lane2_h[:, None] == lane2_h[None, :]).astype(BF16)
    src = jnp.arange(2 * LANES) % LANES
    src_q, src_h, src_t = src // (2 * sb), (src // sb) % 2, src % sb
    dst = jnp.arange(gp * LANES)
    dst_q, dst_h = dst // LANES, (dst % LANES) // hd
    spread = ((src_q[:, None] == dst_q[None, :]) & (src_h[:, None] == dst_h[None, :])).astype(BF16)
    hot = jnp.broadcast_to((src_t[None, :] == jnp.arange(sb)[:, None])[:, None, :], (sb, 16, 2 * LANES)).astype(BF16)
    trow = pl.BlockSpec((None, tb, R_PAIRS, LANES), lambda bb, i: (bb, i, 0, 0))
    r, w, k, nkk, b = (a.reshape(nb, t_len, R_PAIRS, LANES) for a in (r, w, k, nkk, b))
    st = pl.BlockSpec((None, R_PAIRS, hd, LANES), lambda bb, i: (bb, 0, 0, 0))
    full = lambda a: pl.BlockSpec(a.shape, lambda bb, i: (0,) * a.ndim)
    ycol_spec = pl.BlockSpec((None, None, R_PAIRS, hd, LANES), lambda bb, i: (bb, i, 0, 0, 0))
    ycol, s_t = pl.pallas_call(
        functools.partial(_rwkv_scan_kernel, tb=tb),
        out_shape=(jax.ShapeDtypeStruct((nb, nblk, R_PAIRS, hd, LANES), F32),
                   jax.ShapeDtypeStruct((nb, R_PAIRS, hd, LANES), F32)),
        grid=(nb, nblk),
        in_specs=[trow, trow, trow, trow, trow, pl.BlockSpec((None, tb, R_WIDTH), lambda bb, i: (bb, i, 0)), st,
                  full(ones2), full(hot), full(spread)],
        out_specs=(ycol_spec, st),
        scratch_shapes=[pltpu.VMEM((R_PAIRS, hd, LANES), F32), pltpu.VMEM((nsub, ng * hd, 2 * LANES), BF16),
                        pltpu.VMEM((sb, ng * hd, gp * LANES), F32), pltpu.VMEM((sb, ng * hd, gp * LANES), F32)],
        compiler_params=_cparams(("parallel", "arbitrary")),
        name="rwkv_scan",
    )(r, w, k, nkk, b, v, s0p, ones2, hot, spread)
    s_t = s_t.reshape(nb, R_PAIRS, hd, 2, hd).transpose(0, 1, 3, 2, 4).reshape(nb, R_HEADS, hd, hd)
    return ycol, s_t


def _rwkv_post_kernel(y_ref, r_ref, k_ref, v_ref, g_ref, lw_ref, lb_ref, rk_ref, ones_ref, o_ref, *, tb):
    ones_bd = ones_ref[...]
    tiles = []
    for p in range(R_PAIRS):
        yt = y_ref[p].T
        tiles.append(jnp.concatenate([yt[0:tb, :], yt[R_HEAD_DIM:R_HEAD_DIM + tb, :]], axis=1))
    y = jnp.concatenate(tiles, axis=1)
    mean = _head_sums(y, ones_bd) * (1.0 / R_HEAD_DIM)
    d = y - mean
    var = _head_sums(d * d, ones_bd) * (1.0 / R_HEAD_DIM)
    yn = d * lax.rsqrt(var + GN_EPS) * lw_ref[...] + lb_ref[...]
    bonus = _head_sums(r_ref[...] * k_ref[...] * rk_ref[...], ones_bd) * v_ref[...]
    o_ref[...] = ((yn + bonus) * g_ref[...]).astype(o_ref.dtype)


def _rwkv_post(ycol, r, k, v, g, lnx_w, lnx_b, r_k, ones_bd):
    m = r.shape[0]
    nblk = ycol.shape[1]
    tb = m // (ycol.shape[0] * nblk)
    row = pl.BlockSpec((tb, R_WIDTH), lambda i: (i, 0))
    full = lambda a: pl.BlockSpec(a.shape, lambda i: (0,) * a.ndim)
    params = (lnx_w, lnx_b, r_k, ones_bd)
    return pl.pallas_call(
        functools.partial(_rwkv_post_kernel, tb=tb),
        out_shape=jax.ShapeDtypeStruct((m, R_WIDTH), BF16),
        grid=(m // tb,),
        in_specs=[pl.BlockSpec((None, None, R_PAIRS, R_HEAD_DIM, LANES), lambda i: (i // nblk, i % nblk, 0, 0, 0))]
        + [row] * 4 + [full(p) for p in params],
        out_specs=row,
        compiler_params=_cparams(("parallel",)),
        name="rwkv_post",
    )(ycol, r, k, v, g, *params)


def _ffn_up_kernel(x_ref, xp_ref, wg_ref, wu_ref, c0_ref, cw_ref, cb_ref, act_ref, tail_ref, *,
                   seq_rows, seqs_per_tile, tiles_per_seq):
    first = lax.rem(pl.program_id(0), tiles_per_seq) == 0
    wg = wg_ref[...].astype(BF16)
    gate_all = jnp.dot(x_ref[...], wg, preferred_element_type=F32)
    up_all = jnp.dot(x_ref[...], wu_ref[...].astype(BF16), preferred_element_type=F32)
    gate_prev = jnp.dot(xp_ref[...], wg, preferred_element_type=F32)
    row = lax.broadcasted_iota(I32, (seq_rows, 1), 0)
    for s in range(seqs_per_tile):
        rows = slice(s * seq_rows, (s + 1) * seq_rows)
        gate = gate_all[rows]
        prev1 = jnp.where(first, c0_ref[s, 1:2, :], gate_prev[7:8, :])
        prev2 = jnp.where(first, c0_ref[s, 0:1, :], gate_prev[6:7, :])
        g_m1 = jnp.where(row == 0, prev1, pltpu.roll(gate, 1, axis=0))
        g_m2 = jnp.where(row == 0, prev2, jnp.where(row == 1, prev1, pltpu.roll(gate, 2, axis=0)))
        conv = cb_ref[...] + g_m2 * cw_ref[0:1, :]
        conv = conv + g_m1 * cw_ref[1:2, :]
        conv = conv + gate * cw_ref[2:3, :]
        act_ref[rows, :] = (conv * jax.nn.sigmoid(conv) * up_all[rows]).astype(act_ref.dtype)
        tail_ref[s] = gate[seq_rows - 8:, :]


def _ffn_up(x, w_in, conv0, conv_w, conv_b, n_batch, t_len, tm=1024, tn=256):
    m, d = x.shape
    d_ff = w_in.shape[1] // 2
    tm = min(m, tm)
    seq_rows = min(t_len, tm)
    seqs_per_tile, tiles_per_seq = tm // seq_rows, t_len // seq_rows
    assert m % tm == 0 and tm % seq_rows == 0 and t_len % seq_rows == 0 and seq_rows % 8 == 0 and d_ff % tn == 0
    n_up = d_ff // tn
    act, tail = pl.pallas_call(
        functools.partial(_ffn_up_kernel, seq_rows=seq_rows, seqs_per_tile=seqs_per_tile, tiles_per_seq=tiles_per_seq),
        out_shape=(jax.ShapeDtypeStruct((m, d_ff), BF16),
                   jax.ShapeDtypeStruct((n_batch * tiles_per_seq, 8, d_ff), F32)),
        grid=(m // tm, n_up),
        in_specs=[pl.BlockSpec((tm, d), lambda i, j: (i, 0)),
                  pl.BlockSpec((8, d), lambda i, j: (jnp.maximum(i * (tm // 8) - 1, 0), 0)),
                  pl.BlockSpec((d, tn), lambda i, j: (0, j)),
                  pl.BlockSpec((d, tn), lambda i, j: (0, j + n_up)),
                  pl.BlockSpec((seqs_per_tile, CONV_W - 1, tn), lambda i, j: (i // tiles_per_seq, 0, j)),
                  pl.BlockSpec((CONV_W, tn), lambda i, j: (0, j)),
                  pl.BlockSpec((1, tn), lambda i, j: (0, j))],
        out_specs=(pl.BlockSpec((tm, tn), lambda i, j: (i, j)),
                   pl.BlockSpec((seqs_per_tile, 8, tn), lambda i, j: (i, 0, j))),
        compiler_params=_cparams(("parallel", "arbitrary")),
        name="ffn_up",
    )(x, x, w_in, w_in, conv0, conv_w, conv_b.reshape(1, d_ff))
    gate_tail = tail.reshape(n_batch, tiles_per_seq, 8, d_ff)[:, -1, 8 - (CONV_W - 1):, :]
    return act, gate_tail


def _pad_rw_cols(a):
    z = lambda n: jnp.zeros(a.shape[:-1] + (n,), a.dtype)
    wd0, ad0, gd0 = 3 * R_WIDTH, 3 * R_WIDTH + DECAY_LORA, 3 * R_WIDTH + DECAY_LORA + AAA_LORA
    return jnp.concatenate([a[..., :wd0], a[..., wd0:ad0], z(LORA_PAD - DECAY_LORA), a[..., ad0:gd0],
                            z(LORA_PAD - AAA_LORA), a[..., gd0:]], axis=-1)


def _unpad_rw_cols(a):
    return jnp.concatenate([a[..., :RW_WD0 + DECAY_LORA], a[..., RW_AD0:RW_AD0 + AAA_LORA], a[..., RW_GD0:]], axis=-1)


def _prep_weights(norm_mix_g, w_in, rwkv_mu, rwkv_w0, rwkv_w2, rwkv_a0, rwkv_a2, rwkv_g2, rwkv_k_k, rwkv_k_a,
                  rwkv_r_k, rwkv_lnx_w, rwkv_lnx_b, w_out, norm_ffn_g, ffn_w_in, ffn_conv_w, ffn_conv_b,
                  ffn_w_down, norm_final_g, l):
    d = w_in.shape[1]
    w_att = jnp.concatenate([w_in[l][:, :ATT_USED].astype(BF16), jnp.zeros((d, ATT_COLS - ATT_USED), BF16)], axis=1)
    w_rw = _pad_rw_cols(w_in[l][:, ATT_USED:].astype(BF16))
    row = lambda a: a.reshape(1, -1).astype(F32)
    pad_rows = lambda a, n: jnp.concatenate([a, jnp.zeros((n - a.shape[0], a.shape[1]), a.dtype)], axis=0)
    lane_h = jnp.arange(LANES) // R_HEAD_DIM
    w2_pad = pad_rows(rwkv_w2[l].astype(F32), LORA_PAD)
    w2_hi = w2_pad.astype(BF16)
    w2_mid = (w2_pad - w2_hi.astype(F32)).astype(BF16)
    return dict(
        norm_mix_g=norm_mix_g[l], w_att=w_att, w_rw=w_rw,
        mu=_pad_rw_cols(row(rwkv_mu[l])), w0=row(rwkv_w0[l]), a0=row(rwkv_a0[l]),
        k_k=row(rwkv_k_k[l]), k_a=row(rwkv_k_a[l]),
        w2=jnp.concatenate([w2_hi, w2_mid, w2_hi], axis=0), a2=pad_rows(rwkv_a2[l], LORA_PAD).astype(BF16),
        g2=rwkv_g2[l].astype(BF16),
        r_k=row(rwkv_r_k[l]), lnx_w=row(rwkv_lnx_w[l]), lnx_b=row(rwkv_lnx_b[l]),
        ones_bd=jnp.concatenate([lane_h[:, None] == lane_h[None, :]] * 2, axis=0).astype(BF16),
        w_out_a=w_out[l][:A_WIDTH].astype(BF16), w_out_r=w_out[l][A_WIDTH:].astype(BF16),
        norm_ffn_g=norm_ffn_g[l], ffn_w_in=ffn_w_in[l], conv_w=ffn_conv_w[l], conv_b=ffn_conv_b[l],
        ffn_w_down=ffn_w_down[l].astype(BF16), norm_final_g=norm_final_g,
    )


def _trunk(x, past_k, past_v, past_ik, s0, shift0, conv0, wt):
    nb, t_len, d = x.shape
    m = nb * t_len
    p_len = 0 if past_k is None else past_k.shape[1]
    x2 = x.reshape(m, d)

    h = _rmsnorm(x2, wt["norm_mix_g"], BF16)
    z_att = _matmul([h], [wt["w_att"]], tm=2048, name="proj_att")
    z_rw = _matmul([h], [wt["w_rw"]], tm=2048, name="proj_rw")

    q_bf, k_f, k_bf, v_f, v_bf, iq_bf, ikw, ik_bf = _rope_split(z_att, t_len, p_len)
    past = None if past_k is None else (past_k.reshape(nb, p_len, KV_W), past_v.reshape(nb, p_len, KV_W), past_ik)
    attn = _dsa(q_bf, iq_bf, ikw, k_bf, v_bf, ik_bf, past, n_batch=nb, t_len=t_len)

    z_rw3 = z_rw.reshape(nb, t_len, RW_COLS)
    r, w, k2, v2, nkk, b, g = _rwkv_pre(z_rw3, _pad_rw_cols(shift0.astype(F32)), wt["mu"], wt["w0"], wt["a0"],
                                       wt["k_k"], wt["k_a"], wt["w2"], wt["a2"], wt["g2"], wt["ones_bd"])
    ycol, s_t = _rwkv_scan(r, w, k2, nkk, b, v2, s0.astype(F32))
    flat = lambda a: a.reshape(m, R_WIDTH)
    rw = _rwkv_post(ycol, flat(r), flat(k2), flat(v2), flat(g), wt["lnx_w"], wt["lnx_b"], wt["r_k"], wt["ones_bd"])

    x1 = _matmul([attn, rw], [wt["w_out_a"], wt["w_out_r"]], res=x2, tm=1024, name="out_proj")
    hf = _rmsnorm(x1, wt["norm_ffn_g"], BF16)
    act, conv_t = _ffn_up(hf, wt["ffn_w_in"], conv0.astype(F32), wt["conv_w"], wt["conv_b"], nb, t_len)
    x3 = _matmul([act], [wt["ffn_w_down"]], res=x1, name="ffn_down")
    y_out = _rmsnorm(x3, wt["norm_final_g"], F32).reshape(nb, t_len, d)

    shift_t = _unpad_rw_cols(z_rw3[:, -1:])
    caches = (k_f.reshape(nb, t_len, A_KV_HEADS, A_HEAD_DIM)[None], v_f.reshape(nb, t_len, A_KV_HEADS, A_HEAD_DIM)[None],
              ikw[:, :IDX_DIM].reshape(nb, t_len, IDX_DIM)[None], s_t[None], shift_t[None], conv_t[None])
    return y_out, caches


def kernel(x_prompt, x_sample, cache_k, cache_v, cache_idx_k, state_rwkv, state_rwkv_shift, state_ffn_conv, norm_mix_g, w_in, rwkv_mu, rwkv_w0, rwkv_w2, rwkv_a0, rwkv_a2, rwkv_g2, rwkv_k_k, rwkv_k_a, rwkv_r_k, rwkv_lnx_w, rwkv_lnx_b, w_out, norm_ffn_g, ffn_w_in, ffn_conv_w, ffn_conv_b, ffn_w_down, norm_final_g):
    assert w_in.shape[0] == 1, "single-layer trunk"
    wt = _prep_weights(norm_mix_g, w_in, rwkv_mu, rwkv_w0, rwkv_w2, rwkv_a0, rwkv_a2, rwkv_g2, rwkv_k_k, rwkv_k_a,
                       rwkv_r_k, rwkv_lnx_w, rwkv_lnx_b, w_out, norm_ffn_g, ffn_w_in, ffn_conv_w, ffn_conv_b,
                       ffn_w_down, norm_final_g, 0)
    bp = x_prompt.shape[0]
    d_ff = ffn_conv_w.shape[-1]
    y_p, c_p = _trunk(x_prompt, None, None, None,
                      jnp.zeros((bp, R_HEADS, R_HEAD_DIM, R_HEAD_DIM), F32), jnp.zeros((bp, 1, RWKV_COLS), F32),
                      jnp.zeros((bp, CONV_W - 1, d_ff), F32), wt)
    y_s, c_s = _trunk(x_sample, cache_k[0], cache_v[0], cache_idx_k[0], state_rwkv[0], state_rwkv_shift[0],
                      state_ffn_conv[0], wt)
    return (y_p, y_s) + c_p + c_s
```

```python
import functools

import jax
import jax.numpy as jnp
from jax import lax
from jax.experimental import pallas as pl
from jax.experimental.pallas import tpu as pltpu

F32 = jnp.float32
BF16 = jnp.bfloat16
I32 = jnp.int32

CHUNK = 64
A_HEADS = 16
A_KV_HEADS = 4
A_HEAD_DIM = 128
A_GROUP = A_HEADS // A_KV_HEADS
A_WIDTH = A_HEADS * A_HEAD_DIM
KV_W = A_KV_HEADS * A_HEAD_DIM
IDX_HEADS = 16
IDX_DIM = 64
IQ_W = IDX_HEADS * IDX_DIM
TOPK_MAX = 256
ROPE_THETA = 500000.0
ROPE_FRAC = 4
A_SCALE = A_HEAD_DIM ** -0.5
Q_SCALE = A_SCALE * 1.4426950408889634
IDX_SCALE = (IDX_HEADS ** -0.5) * (IDX_DIM ** -0.5)
R_HEAD_DIM = 64
R_WIDTH = 2048
R_HEADS = R_WIDTH // R_HEAD_DIM
R_PAIRS = R_HEADS // 2
DECAY_LORA = 96
AAA_LORA = 96
GATE_LORA = 256
RWKV_COLS = 3 * R_WIDTH + DECAY_LORA + AAA_LORA + GATE_LORA
GN_EPS = 6.4e-4
CONV_W = 3
RMS_EPS = 1e-6

LANES = 128
VMEM_LIMIT = 56 * 1024 * 1024

ATT_Q0, ATT_K0, ATT_V0, ATT_IQ0, ATT_IK0 = 0, A_WIDTH, A_WIDTH + KV_W, A_WIDTH + 2 * KV_W, A_WIDTH + 2 * KV_W + IQ_W
ATT_USED = ATT_IK0 + IDX_DIM + IDX_HEADS
ATT_COLS = 4608
LORA_PAD = 128
RW_WD0 = 3 * R_WIDTH
RW_AD0 = RW_WD0 + LORA_PAD
RW_GD0 = RW_AD0 + LORA_PAD
RW_COLS = RW_GD0 + GATE_LORA
DSA_TQ, DSA_TK = 256, 512
COUNT_ROWS = 128
SCAN_BLOCK = 64
SCAN_SUB = 16
SCAN_GROUP_PAIRS = LANES // (2 * SCAN_SUB)
SCAN_GROUPS = R_PAIRS // SCAN_GROUP_PAIRS
SCAN_MATMULS = 4
SCAN_BULK = 16
INT_MIN = -2 ** 31
NEG_BIG = -1e30


def _cparams(sem):
    return pltpu.CompilerParams(dimension_semantics=sem, vmem_limit_bytes=VMEM_LIMIT)


def _rmsnorm_kernel(x_ref, g_ref, o_ref):
    x = x_ref[...]
    y = x * lax.rsqrt(jnp.mean(x * x, axis=-1, keepdims=True) + RMS_EPS)
    o_ref[...] = (y * g_ref[...]).astype(o_ref.dtype)


def _rmsnorm(x, g, out_dtype):
    m, d = x.shape
    tm = min(m, 256)
    return pl.pallas_call(
        _rmsnorm_kernel,
        out_shape=jax.ShapeDtypeStruct((m, d), out_dtype),
        grid=(m // tm,),
        in_specs=[pl.BlockSpec((tm, d), lambda i: (i, 0)), pl.BlockSpec((1, d), lambda i: (0, 0))],
        out_specs=pl.BlockSpec((tm, d), lambda i: (i, 0)),
        compiler_params=_cparams(("parallel",)),
        name="rmsnorm",
    )(x, g.reshape(1, d).astype(F32))


def _mm_kernel(*refs, n_pairs, has_res):
    o_ref = refs[-1]
    acc = jnp.dot(refs[0][...], refs[n_pairs][...], preferred_element_type=F32)
    for p in range(1, n_pairs):
        acc = acc + jnp.dot(refs[p][...], refs[n_pairs + p][...], preferred_element_type=F32)
    if has_res:
        acc = refs[2 * n_pairs][...] + acc
    o_ref[...] = acc.astype(o_ref.dtype)


def _matmul(a_list, b_list, res=None, tm=512, tn=512, name="matmul"):
    m = a_list[0].shape[0]
    n = b_list[0].shape[1]
    tm = min(tm, m)
    tn = min(tn, n)
    assert m % tm == 0 and n % tn == 0, (m, n, tm, tn)
    in_specs = [pl.BlockSpec((tm, a.shape[1]), lambda i, j: (i, 0)) for a in a_list]
    in_specs += [pl.BlockSpec((b.shape[0], tn), lambda i, j: (0, j)) for b in b_list]
    args = list(a_list) + list(b_list)
    if res is not None:
        in_specs.append(pl.BlockSpec((tm, tn), lambda i, j: (i, j)))
        args.append(res)
    return pl.pallas_call(
        functools.partial(_mm_kernel, n_pairs=len(a_list), has_res=res is not None),
        out_shape=jax.ShapeDtypeStruct((m, n), F32),
        grid=(m // tm, n // tn),
        in_specs=in_specs,
        out_specs=pl.BlockSpec((tm, tn), lambda i, j: (i, j)),
        compiler_params=_cparams(("parallel", "arbitrary")),
        name=name,
    )(*args)


def _rope_tile(x, cos, sin, half, d_in_head):
    lo = d_in_head < half
    hi = (d_in_head >= half) & (d_in_head < 2 * half)
    c = jnp.where(lo | hi, cos, 1.0)
    s_up = jnp.where(lo, -sin, 0.0)
    s_dn = jnp.where(hi, sin, 0.0)
    x_up = pltpu.roll(x, LANES - half, axis=1)
    x_dn = pltpu.roll(x, half, axis=1)
    return x * c + x_up * s_up + x_dn * s_dn


def _rope_kernel(z_ref, invf_ref, q_ref, kf_ref, kb_ref, vf_ref, vb_ref, iq_ref, ikw_ref, ikb_ref, *, tm, t_len, pos0):
    i = pl.program_id(0)
    row = lax.broadcasted_iota(I32, (tm, LANES), 0) + i * tm
    pos = (pos0 + lax.rem(row, t_len)).astype(F32)
    lane = lax.broadcasted_iota(I32, (tm, LANES), 1)
    ang = pos * invf_ref[0:1, :]
    cos_a, sin_a = jnp.cos(ang), jnp.sin(ang)
    half_a = A_HEAD_DIM // ROPE_FRAC // 2
    for h in range(A_HEADS):
        x = z_ref[:, ATT_Q0 + h * LANES:ATT_Q0 + (h + 1) * LANES]
        q_ref[:, h * LANES:(h + 1) * LANES] = (_rope_tile(x, cos_a, sin_a, half_a, lane) * Q_SCALE).astype(q_ref.dtype)
    for h in range(A_KV_HEADS):
        x = z_ref[:, ATT_K0 + h * LANES:ATT_K0 + (h + 1) * LANES]
        y = _rope_tile(x, cos_a, sin_a, half_a, lane)
        kf_ref[:, h * LANES:(h + 1) * LANES] = y
        kb_ref[:, h * LANES:(h + 1) * LANES] = y.astype(kb_ref.dtype)
    v = z_ref[:, ATT_V0:ATT_V0 + KV_W]
    vf_ref[...] = v
    vb_ref[...] = v.astype(vb_ref.dtype)
    ang = pos * invf_ref[1:2, :]
    cos_i, sin_i = jnp.cos(ang), jnp.sin(ang)
    half_i = IDX_DIM // ROPE_FRAC // 2
    d_i = lane & (IDX_DIM - 1)
    for h in range(IQ_W // LANES):
        x = z_ref[:, ATT_IQ0 + h * LANES:ATT_IQ0 + (h + 1) * LANES]
        iq_ref[:, h * LANES:(h + 1) * LANES] = _rope_tile(x, cos_i, sin_i, half_i, d_i).astype(iq_ref.dtype)
    x = z_ref[:, ATT_IK0:ATT_IK0 + LANES]
    d_k = jnp.where(lane < IDX_DIM, lane, IDX_DIM)
    y = _rope_tile(x, cos_i, sin_i, half_i, d_k)
    ikw_ref[...] = y
    ikb_ref[...] = y[:, :IDX_DIM].astype(ikb_ref.dtype)


def _rope_split(z_att, t_len, pos0):
    m = z_att.shape[0]
    tm = min(m, 256)
    lane = jnp.arange(LANES)
    rd_a = A_HEAD_DIM // ROPE_FRAC
    rd_i = IDX_DIM // ROPE_FRAC
    invf_a = ROPE_THETA ** (-((lane % (rd_a // 2)).astype(F32) * 2.0 / rd_a))
    invf_i = ROPE_THETA ** (-((lane % (rd_i // 2)).astype(F32) * 2.0 / rd_i))
    invf = jnp.zeros((8, LANES), F32).at[0].set(invf_a).at[1].set(invf_i)
    row_spec = lambda w: pl.BlockSpec((tm, w), lambda i: (i, 0))
    shp = lambda w, dt: jax.ShapeDtypeStruct((m, w), dt)
    return pl.pallas_call(
        functools.partial(_rope_kernel, tm=tm, t_len=t_len, pos0=pos0),
        out_shape=(shp(A_WIDTH, BF16), shp(KV_W, F32), shp(KV_W, BF16), shp(KV_W, F32), shp(KV_W, BF16),
                   shp(IQ_W, BF16), shp(LANES, F32), shp(IDX_DIM, BF16)),
        grid=(m // tm,),
        in_specs=[row_spec(ATT_COLS), pl.BlockSpec((8, LANES), lambda i: (0, 0))],
        out_specs=(row_spec(A_WIDTH), row_spec(KV_W), row_spec(KV_W), row_spec(KV_W), row_spec(KV_W),
                   row_spec(IQ_W), row_spec(LANES), row_spec(IDX_DIM)),
        compiler_params=_cparams(("parallel",)),
        name="rope_split",
    )(z_att, invf)


def _tile(x, n, axis):
    return x if n == 1 else jnp.concatenate([x] * n, axis=axis)


def _sortable(score):
    u = lax.bitcast_convert_type(score, I32)
    return jnp.where(u < 0, u ^ jnp.int32(0x7FFFFFFF), u)


def _dsa_kernel(*refs, tq, tk, pos0, n_keys, topk):
    i = pl.program_id(1)
    if pos0 > 0:
        (q_ref, iq_ref, ikw_ref, kn_ref, vn_ref, ikn_ref, pk_ref, pv_ref, pik_ref, o_ref,
         key_ref, iwb_ref, qs_ref, m_ref, l_ref, acc_ref, k_ref, v_ref, ik_ref) = refs
        t_new = kn_ref.shape[0]

        @pl.when(i == 0)
        def _():
            for new_ref, past_ref, all_ref in ((kn_ref, pk_ref, k_ref), (vn_ref, pv_ref, v_ref), (ikn_ref, pik_ref, ik_ref)):
                for r0 in range(0, pos0, DSA_TK):
                    r1 = min(r0 + DSA_TK, pos0)
                    all_ref[r0:r1, :] = past_ref[r0:r1, :].astype(all_ref.dtype)
                all_ref[pos0:pos0 + t_new, :] = new_ref[...]
                tail = all_ref.shape[0] - pos0 - t_new
                if tail:
                    all_ref[pos0 + t_new:, :] = jnp.zeros((tail, all_ref.shape[1]), all_ref.dtype)
    else:
        (q_ref, iq_ref, ikw_ref, k_ref, v_ref, ik_ref, o_ref,
         key_ref, iwb_ref, qs_ref, m_ref, l_ref, acc_ref) = refs
    q0 = pos0 + i * tq
    kmax = jnp.minimum((lax.div(q0 + tq - 1, CHUNK) + 1) * CHUNK, n_keys)
    nkb = lax.div(kmax + tk - 1, tk)
    n_rep = tk // LANES
    nt_dims = (((1,), (1,)), ((), ()))

    for h in range(IDX_HEADS):
        iwb_ref[h] = jnp.broadcast_to(ikw_ref[:, IDX_DIM + h:IDX_DIM + h + 1], (tq, LANES))
    for h in range(A_HEADS):
        qs_ref[h * tq:(h + 1) * tq, :] = q_ref[:, h * LANES:(h + 1) * LANES]

    def score_block(kb, carry):
        koff = pl.multiple_of(kb * tk, tk)
        ikb = ik_ref[pl.ds(koff, tk), :]
        acc = jnp.zeros((tq, tk), F32)
        for h in range(IDX_HEADS):
            d = lax.dot_general(iq_ref[:, h * IDX_DIM:(h + 1) * IDX_DIM], ikb, nt_dims, preferred_element_type=F32)
            acc = acc + jnp.maximum(d, 0.0) * _tile(iwb_ref[h], n_rep, 1)
        kpos = koff + lax.broadcasted_iota(I32, (tq, tk), 1)
        qpos = q0 + lax.broadcasted_iota(I32, (tq, tk), 0)
        adm = (lax.shift_right_logical(kpos, 6) <= lax.shift_right_logical(qpos, 6)) & (kpos < n_keys)
        key_ref[:, pl.ds(koff, tk)] = jnp.where(adm, _sortable(acc * IDX_SCALE), jnp.int32(INT_MIN))
        return carry

    lax.fori_loop(0, nkb, score_block, 0)

    def count(pred, *row_args):
        cr = min(tq, COUNT_ROWS)

        def chunk(r0):
            def body(kb, cnt):
                koff = pl.multiple_of(kb * tk, tk)
                keys = key_ref[r0:r0 + cr, pl.ds(koff, tk)]
                kpos = koff + lax.broadcasted_iota(I32, (cr, tk), 1)
                hit = jnp.where(pred(keys, kpos, *[a[r0:r0 + cr] for a in row_args]), 1.0, 0.0)
                for c in range(n_rep):
                    cnt = cnt + hit[:, c * LANES:(c + 1) * LANES]
                return cnt
            cnt = lax.fori_loop(0, nkb, body, jnp.zeros((cr, LANES), F32))
            return jnp.broadcast_to(jnp.sum(cnt, axis=-1, keepdims=True), (cr, LANES))

        return jnp.concatenate([chunk(r0) for r0 in range(0, tq, cr)], axis=0)

    def wide(x):
        return _tile(x, n_rep, 1)

    def bit_step(it, tu):
        cand_u = tu | lax.shift_left(jnp.int32(1), 31 - it)
        cand_s = wide(cand_u ^ jnp.int32(INT_MIN))
        cnt = count(lambda keys, kpos, c: keys >= c, cand_s)
        return jnp.where(cnt >= topk, cand_u, tu)

    tu = lax.fori_loop(0, 32, bit_step, jnp.zeros((tq, LANES), I32))
    thr = jnp.maximum(tu ^ jnp.int32(INT_MIN), jnp.int32(INT_MIN + 1))
    thr_w = wide(thr)

    n_ge = count(lambda keys, kpos, t: keys >= t, thr_w)
    n_gt = count(lambda keys, kpos, t: keys > t, thr_w)
    excess = n_ge > topk

    @pl.when(jnp.max(jnp.where(excess, 1.0, 0.0)) > 0.0)
    def _():
        need = topk - n_gt

        idx_bits = int(key_ref.shape[1]).bit_length()

        def idx_step(it, jm):
            cand = wide(jm | lax.shift_left(jnp.int32(1), idx_bits - 1 - it))
            cnt = count(lambda keys, kpos, t, c: (keys == t) & (kpos < c), thr_w, cand)
            return jnp.where(cnt < need, cand[:, :LANES], jm)

        jm = lax.fori_loop(0, idx_bits, idx_step, jnp.zeros((tq, LANES), I32))
        jm_w = wide(jnp.where(excess, jm, jnp.int32(2 ** 31 - 1)))

        def drop(kb, carry):
            koff = pl.multiple_of(kb * tk, tk)
            keys = key_ref[:, pl.ds(koff, tk)]
            kpos = koff + lax.broadcasted_iota(I32, (tq, tk), 1)
            key_ref[:, pl.ds(koff, tk)] = jnp.where((keys == thr_w) & (kpos > jm_w), jnp.int32(INT_MIN), keys)
            return carry

        lax.fori_loop(0, nkb, drop, 0)

    m_ref[...] = jnp.full(m_ref.shape, NEG_BIG, F32)
    l_ref[...] = jnp.zeros(l_ref.shape, F32)
    acc_ref[...] = jnp.zeros(acc_ref.shape, F32)
    rows = A_GROUP * tq

    def attend(kb, carry):
        koff = pl.multiple_of(kb * tk, tk)
        bias = _tile(jnp.where(key_ref[:, pl.ds(koff, tk)] >= thr_w, 0.0, NEG_BIG), A_GROUP, 0)
        for n in range(A_KV_HEADS):
            r0 = n * rows
            kn = k_ref[pl.ds(koff, tk), n * LANES:(n + 1) * LANES]
            vn = v_ref[pl.ds(koff, tk), n * LANES:(n + 1) * LANES]
            s = lax.dot_general(qs_ref[r0:r0 + rows, :], kn, nt_dims, preferred_element_type=F32) + bias
            m_prev = m_ref[r0:r0 + rows, :]
            m_new = jnp.maximum(m_prev, jnp.max(s, axis=-1, keepdims=True))
            alpha = jnp.exp2(m_prev - m_new)
            p = jnp.exp2(s - _tile(m_new, n_rep, 1))
            l_ref[r0:r0 + rows, :] = alpha * l_ref[r0:r0 + rows, :] + jnp.sum(p, axis=-1, keepdims=True)
            acc_ref[r0:r0 + rows, :] = alpha * acc_ref[r0:r0 + rows, :] + jnp.dot(
                p.astype(vn.dtype), vn, preferred_element_type=F32)
            m_ref[r0:r0 + rows, :] = m_new
        return carry

    lax.fori_loop(0, nkb, attend, 0)
    for h in range(A_HEADS):
        o_ref[:, h * LANES:(h + 1) * LANES] = (
            acc_ref[h * tq:(h + 1) * tq, :] / l_ref[h * tq:(h + 1) * tq, :]).astype(o_ref.dtype)


def _dsa(q_bf, iq_bf, ikw, k_new, v_new, ik_new, past, *, n_batch, t_len):
    pos0 = 0 if past is None else past[0].shape[1]
    n_keys = pos0 + t_len
    tq = min(t_len, DSA_TQ)
    lp = -(-n_keys // LANES) * LANES
    tk = lp if tq * lp <= DSA_TQ * DSA_TK else DSA_TK
    lp = -(-lp // tk) * tk
    assert t_len % tq == 0 and (past is None or (lp == tk and pos0 % 16 == 0 and t_len % 16 == 0)) \
        and (past is not None or lp == n_keys)
    nq = t_len // tq
    topk = min(TOPK_MAX, n_keys // 4)
    qrow = lambda w: pl.BlockSpec((tq, w), lambda b, i: (b * nq + i, 0))
    new3 = lambda a: a.reshape(n_batch, t_len, a.shape[-1])
    whole = lambda rows, w, **kw: pl.BlockSpec((None, rows, w), lambda b, i: (b, 0, 0), **kw)
    in_specs = [qrow(A_WIDTH), qrow(IQ_W), qrow(LANES)]
    args = [q_bf, iq_bf, ikw, new3(k_new), new3(v_new), new3(ik_new)]
    scratch = [
        pltpu.VMEM((tq, lp), I32),
        pltpu.VMEM((IDX_HEADS, tq, LANES), F32),
        pltpu.VMEM((A_HEADS * tq, LANES), BF16),
        pltpu.VMEM((A_HEADS * tq, LANES), F32),
        pltpu.VMEM((A_HEADS * tq, LANES), F32),
        pltpu.VMEM((A_HEADS * tq, LANES), F32),
    ]
    if past is None:
        in_specs += [whole(lp, w, pipeline_mode=pl.Buffered(1)) for w in (KV_W, KV_W, IDX_DIM)]
    else:
        in_specs += [whole(t_len, w) for w in (KV_W, KV_W, IDX_DIM)] + [whole(pos0, w) for w in (KV_W, KV_W, IDX_DIM)]
        args += list(past)
        scratch += [pltpu.VMEM((lp, w), BF16) for w in (KV_W, KV_W, IDX_DIM)]
    return pl.pallas_call(
        functools.partial(_dsa_kernel, tq=tq, tk=tk, pos0=pos0, n_keys=n_keys, topk=float(topk)),
        out_shape=jax.ShapeDtypeStruct((n_batch * t_len, A_WIDTH), BF16),
        grid=(n_batch, nq),
        in_specs=in_specs,
        out_specs=qrow(A_WIDTH),
        scratch_shapes=scratch,
        compiler_params=_cparams(("parallel", "arbitrary")),
        name="dsa",
    )(*args)


def _head_sums(x, ones2):
    n = x.shape[1] // LANES
    tm = x.shape[0]
    hi, mid = _split2(jnp.concatenate([x[:, c * LANES:(c + 1) * LANES] for c in range(n)], axis=0))
    s = jnp.dot(jnp.concatenate([hi, mid], axis=1), ones2, preferred_element_type=F32)
    return jnp.concatenate([s[c * tm:(c + 1) * tm, :] for c in range(n)], axis=1)


def _rwkv_pre_kernel(z_ref, zp_ref, z0_ref, mu_ref, w0_ref, a0_ref, kk_ref, ka_ref, w2_ref, a2_ref, g2_ref,
                     ones_ref, r_out, w_out, k_out, v_out, nkk_out, b_out, g_out, *, tm):
    i = pl.program_id(1)
    row = lax.broadcasted_iota(I32, (tm, 1), 0)

    def mixed(c0, width):
        z = z_ref[:, c0:c0 + width]
        first = jnp.where(i == 0, z0_ref[:, c0:c0 + width], zp_ref[7:8, c0:c0 + width])
        shifted = jnp.where(row == 0, first, pltpu.roll(z, 1, axis=0))
        return z + (shifted - z) * mu_ref[:, c0:c0 + width]

    r = mixed(0, R_WIDTH)
    k = mixed(R_WIDTH, R_WIDTH)
    v = mixed(2 * R_WIDTH, R_WIDTH)
    wd = mixed(RW_WD0, LORA_PAD)
    ad = mixed(RW_AD0, LORA_PAD)
    gd = mixed(RW_GD0, GATE_LORA)
    lora = lambda x, w_ref: jnp.dot(x.astype(BF16), w_ref[...], preferred_element_type=F32)
    t_hi, t_mid = _split2(jnp.tanh(wd))
    y = -(w0_ref[...] + jnp.dot(jnp.concatenate([t_hi, t_hi, t_mid], axis=1), w2_ref[...], preferred_element_type=F32))
    softplus = jnp.maximum(y, 0.0) + jnp.log(1.0 + jnp.exp(-jnp.abs(y)))
    decay = jnp.exp(-jnp.exp(-softplus - 0.5))
    a = jax.nn.sigmoid(a0_ref[...] + lora(ad, a2_ref))
    g = lora(jax.nn.sigmoid(gd), g2_ref)
    kk = k * kk_ref[...]
    kk = kk / jnp.maximum(jnp.sqrt(_head_sums(kk * kk, ones_ref[...])), 1e-12)
    r_out[...] = r
    w_out[...] = decay
    k_out[...] = k * (1.0 + (a - 1.0) * ka_ref[...])
    v_out[...] = v
    nkk_out[...] = -kk
    b_out[...] = kk * a
    g_out[...] = g


def _rwkv_pre(z_rw, z0, mu, w0, a0, k_k, k_a, w2, a2, g2, ones_bd):
    nb, t_len, _ = z_rw.shape
    tm = min(t_len, 128)
    zrow = pl.BlockSpec((None, tm, RW_COLS), lambda b, i: (b, i, 0))
    zprev = pl.BlockSpec((None, 8, RW_COLS), lambda b, i: (b, jnp.maximum(i * (tm // 8) - 1, 0), 0))
    full = lambda a: pl.BlockSpec(a.shape, lambda b, i: (0,) * a.ndim)
    orow = pl.BlockSpec((None, tm, R_WIDTH), lambda b, i: (b, i, 0))
    params = (mu, w0, a0, k_k, k_a, w2, a2, g2, ones_bd)
    return pl.pallas_call(
        functools.partial(_rwkv_pre_kernel, tm=tm),
        out_shape=tuple(jax.ShapeDtypeStruct((nb, t_len, R_WIDTH), F32) for _ in range(7)),
        grid=(nb, t_len // tm),
        in_specs=[zrow, zprev, pl.BlockSpec((None, 1, RW_COLS), lambda b, i: (b, 0, 0))] + [full(p) for p in params],
        out_specs=tuple(orow for _ in range(7)),
        compiler_params=_cparams(("parallel", "arbitrary")),
        name="rwkv_pre",
    )(z_rw, z_rw, z0, *params)


def _split2(x):
    hi = x.astype(BF16)
    return hi, (x - hi.astype(F32)).astype(BF16)


def _rwkv_scan_kernel(r_ref, w_ref, k_ref, nkk_ref, b_ref, v_ref, s0_ref, ones2_ref, hot_ref, spread_ref,
                      y_ref, st_ref, s_ref, xs_ref, vc_ref, vc2_ref, *, tb):
    tblk = pl.program_id(1)
    cat = lambda xs, ax=0: jnp.concatenate(xs, axis=ax)

    @pl.when(tblk == 0)
    def _():
        s_ref[...] = s0_ref[...]

    pad = jnp.zeros((SCAN_BLOCK - tb, LANES), F32)
    v_cols = []
    for p in range(R_PAIRS):
        vp = v_ref[:, p * LANES:(p + 1) * LANES]
        v_cols.append((cat([vp, pad]) if tb < SCAN_BLOCK else vp).T)
    for sub in range(tb // SCAN_SUB):
        for g in range(SCAN_GROUPS):
            x = cat([v_cols[SCAN_GROUP_PAIRS * g + q][h * R_HEAD_DIM:(h + 1) * R_HEAD_DIM,
                                                      sub * SCAN_SUB:(sub + 1) * SCAN_SUB]
                     for q in range(SCAN_GROUP_PAIRS) for h in range(2)], 1)
            hi, mid = _split2(x)
            xs_ref[sub, g * R_HEAD_DIM:(g + 1) * R_HEAD_DIM, :] = cat([hi, mid], 1)
    ones2 = ones2_ref[...]
    spread = spread_ref[...]
    y_ref[...] = jnp.zeros(y_ref.shape, F32)
    lane_t = lax.rem(lax.broadcasted_iota(I32, (R_HEAD_DIM, LANES), 1), R_HEAD_DIM)
    rowp = lambda ref, t, p: ref[t, p:p + 1, :]
    group_pairs = R_PAIRS // SCAN_MATMULS
    half = group_pairs // 2 * R_HEAD_DIM

    groups = [range(g * group_pairs, (g + 1) * group_pairs) for g in range(SCAN_MATMULS)]

    def packed(xs):
        return cat([cat(xs[2 * j:2 * j + 2], 1) for j in range(group_pairs // 2)])

    def pair_tile(res, q):
        return res[(q // 2) * R_HEAD_DIM:(q // 2 + 1) * R_HEAD_DIM, (q % 2) * LANES:(q % 2 + 1) * LANES]

    def head_sums(g, states, t_sa, t_y=None):
        rows = [packed([(s * rowp(nkk_ref, t_sa, p)).astype(BF16) for s, p in zip(states, groups[g])])]
        if t_y is not None:
            rows.append(packed([(s * rowp(r_ref, t_y, p)).astype(BF16) for s, p in zip(states, groups[g])]))
        return jnp.dot(cat(rows), ones2, preferred_element_type=F32)

    def value_columns(sub, i, vc_out):
        xs = xs_ref[sub]
        n = xs.shape[0]
        for j0 in range(0, SCAN_BULK, SCAN_BULK // 2):
            steps = [SCAN_BULK * i + j0 + j for j in range(SCAN_BULK // 2)]
            res = jnp.dot(cat([xs * hot_ref[tt, 0:1, :] for tt in steps]), spread, preferred_element_type=F32)
            for j, tt in enumerate(steps):
                vc_out[tt] = res[j * n:(j + 1) * n]

    def step(sub, tt, vc_in, sas):
        t = sub * SCAN_SUB + tt
        t_next = jnp.minimum(t + 1, tb - 1)
        hit = lane_t == t
        new_sas = []
        for g in range(SCAN_MATMULS):
            states = []
            for q, p in enumerate(groups[g]):
                vg, vq = divmod(p, SCAN_GROUP_PAIRS)
                vc = vc_in[tt, vg * R_HEAD_DIM:(vg + 1) * R_HEAD_DIM, vq * LANES:(vq + 1) * LANES]
                s_new = (s_ref[p] * rowp(w_ref, t, p) + pair_tile(sas[g], q) * rowp(b_ref, t, p)
                         + vc * rowp(k_ref, t, p))
                s_ref[p] = s_new
                states.append(s_new)
            res = head_sums(g, states, t_next, t)
            new_sas.append(res[:half])
            for q, p in enumerate(groups[g]):
                y_ref[p] = jnp.where(hit, pair_tile(res[half:], q), y_ref[p])
        return tuple(new_sas)

    def sub_block(sub, vc_in, vc_out, sas):
        nxt = jnp.minimum(sub + 1, n_sub - 1)

        def bulk_steps(i, sas):
            sas = step(sub, SCAN_BULK * i, vc_in, sas)
            if n_sub > 1:
                value_columns(nxt, i, vc_out)
            for u in range(1, SCAN_BULK):
                sas = step(sub, SCAN_BULK * i + u, vc_in, sas)
            return sas

        return lax.fori_loop(0, SCAN_SUB // SCAN_BULK, bulk_steps, sas)

    n_sub = tb // SCAN_SUB
    for i in range(SCAN_SUB // SCAN_BULK):
        value_columns(0, i, vc_ref)
    sas = tuple(head_sums(g, [s_ref[p] for p in groups[g]], 0) for g in range(SCAN_MATMULS))
    if n_sub == 1:
        sub_block(0, vc_ref, vc2_ref, sas)
    else:
        def two_sub_blocks(i, sas):
            return sub_block(2 * i + 1, vc2_ref, vc_ref, sub_block(2 * i, vc_ref, vc2_ref, sas))

        lax.fori_loop(0, n_sub // 2, two_sub_blocks, sas)

    @pl.when(tblk == pl.num_programs(1) - 1)
    def _():
        st_ref[...] = s_ref[...]


def _rwkv_scan(r, w, k, nkk, b, v, s0):
    nb, t_len, _ = r.shape
    tb = min(t_len, SCAN_BLOCK)
    nblk = t_len // tb
    hd = R_HEAD_DIM
    nsub, sb, ng, gp = tb // SCAN_SUB, SCAN_SUB, SCAN_GROUPS, SCAN_GROUP_PAIRS
    s0p = s0.reshape(nb, R_PAIRS, 2, hd, hd).transpose(0, 1, 3, 2, 4).reshape(nb, R_PAIRS, hd, LANES)
    lane2_h = jnp.arange(2 * LANES) // hd
    ones2 = (lane2_h[:, None] == lane2_h[None, :]).astype(BF16)
    src = jnp.arange(2 * LANES) % LANES
    src_q, src_h, src_t = src // (2 * sb), (src // sb) % 2, src % sb
    dst = jnp.arange(gp * LANES)
    dst_q, dst_h = dst // LANES, (dst % LANES) // hd
    spread = ((src_q[:, None] == dst_q[None, :]) & (src_h[:, None] == dst_h[None, :])).astype(BF16)
    hot = jnp.broadcast_to((src_t[None, :] == jnp.arange(sb)[:, None])[:, None, :], (sb, 16, 2 * LANES)).astype(BF16)
    trow = pl.BlockSpec((None, tb, R_PAIRS, LANES), lambda bb, i: (bb, i, 0, 0))
    r, w, k, nkk, b = (a.reshape(nb, t_len, R_PAIRS, LANES) for a in (r, w, k, nkk, b))
    st = pl.BlockSpec((None, R_PAIRS, hd, LANES), lambda bb, i: (bb, 0, 0, 0))
    full = lambda a: pl.BlockSpec(a.shape, lambda bb, i: (0,) * a.ndim)
    ycol_spec = pl.BlockSpec((None, None, R_PAIRS, hd, LANES), lambda bb, i: (bb, i, 0, 0, 0))
    ycol, s_t = pl.pallas_call(
        functools.partial(_rwkv_scan_kernel, tb=tb),
        out_shape=(jax.ShapeDtypeStruct((nb, nblk, R_PAIRS, hd, LANES), F32),
                   jax.ShapeDtypeStruct((nb, R_PAIRS, hd, LANES), F32)),
        grid=(nb, nblk),
        in_specs=[trow, trow, trow, trow, trow, pl.BlockSpec((None, tb, R_WIDTH), lambda bb, i: (bb, i, 0)), st,
                  full(ones2), full(hot), full(spread)],
        out_specs=(ycol_spec, st),
        scratch_shapes=[pltpu.VMEM((R_PAIRS, hd, LANES), F32), pltpu.VMEM((nsub, ng * hd, 2 * LANES), BF16),
                        pltpu.VMEM((sb, ng * hd, gp * LANES), F32), pltpu.VMEM((sb, ng * hd, gp * LANES), F32)],
        compiler_params=_cparams(("parallel", "arbitrary")),
        name="rwkv_scan",
    )(r, w, k, nkk, b, v, s0p, ones2, hot, spread)
    s_t = s_t.reshape(nb, R_PAIRS, hd, 2, hd).transpose(0, 1, 3, 2, 4).reshape(nb, R_HEADS, hd, hd)
    return ycol, s_t


def _rwkv_post_kernel(y_ref, r_ref, k_ref, v_ref, g_ref, lw_ref, lb_ref, rk_ref, ones_ref, o_ref, *, tb):
    ones_bd = ones_ref[...]
    tiles = []
    for p in range(R_PAIRS):
        yt = y_ref[p].T
        tiles.append(jnp.concatenate([yt[0:tb, :], yt[R_HEAD_DIM:R_HEAD_DIM + tb, :]], axis=1))
    y = jnp.concatenate(tiles, axis=1)
    mean = _head_sums(y, ones_bd) * (1.0 / R_HEAD_DIM)
    d = y - mean
    var = _head_sums(d * d, ones_bd) * (1.0 / R_HEAD_DIM)
    yn = d * lax.rsqrt(var + GN_EPS) * lw_ref[...] + lb_ref[...]
    bonus = _head_sums(r_ref[...] * k_ref[...] * rk_ref[...], ones_bd) * v_ref[...]
    o_ref[...] = ((yn + bonus) * g_ref[...]).astype(o_ref.dtype)


def _rwkv_post(ycol, r, k, v, g, lnx_w, lnx_b, r_k, ones_bd):
    m = r.shape[0]
    nblk = ycol.shape[1]
    tb = m // (ycol.shape[0] * nblk)
    row = pl.BlockSpec((tb, R_WIDTH), lambda i: (i, 0))
    full = lambda a: pl.BlockSpec(a.shape, lambda i: (0,) * a.ndim)
    params = (lnx_w, lnx_b, r_k, ones_bd)
    return pl.pallas_call(
        functools.partial(_rwkv_post_kernel, tb=tb),
        out_shape=jax.ShapeDtypeStruct((m, R_WIDTH), BF16),
        grid=(m // tb,),
        in_specs=[pl.BlockSpec((None, None, R_PAIRS, R_HEAD_DIM, LANES), lambda i: (i // nblk, i % nblk, 0, 0, 0))]
        + [row] * 4 + [full(p) for p in params],
        out_specs=row,
        compiler_params=_cparams(("parallel",)),
        name="rwkv_post",
    )(ycol, r, k, v, g, *params)


def _ffn_up_kernel(x_ref, xp_ref, wg_ref, wu_ref, c0_ref, cw_ref, cb_ref, act_ref, tail_ref, *,
                   seq_rows, seqs_per_tile, tiles_per_seq):
    first = lax.rem(pl.program_id(0), tiles_per_seq) == 0
    wg = wg_ref[...].astype(BF16)
    gate_all = jnp.dot(x_ref[...], wg, preferred_element_type=F32)
    up_all = jnp.dot(x_ref[...], wu_ref[...].astype(BF16), preferred_element_type=F32)
    gate_prev = jnp.dot(xp_ref[...], wg, preferred_element_type=F32)
    row = lax.broadcasted_iota(I32, (seq_rows, 1), 0)
    for s in range(seqs_per_tile):
        rows = slice(s * seq_rows, (s + 1) * seq_rows)
        gate = gate_all[rows]
        prev1 = jnp.where(first, c0_ref[s, 1:2, :], gate_prev[7:8, :])
        prev2 = jnp.where(first, c0_ref[s, 0:1, :], gate_prev[6:7, :])
        g_m1 = jnp.where(row == 0, prev1, pltpu.roll(gate, 1, axis=0))
        g_m2 = jnp.where(row == 0, prev2, jnp.where(row == 1, prev1, pltpu.roll(gate, 2, axis=0)))
        conv = cb_ref[...] + g_m2 * cw_ref[0:1, :]
        conv = conv + g_m1 * cw_ref[1:2, :]
        conv = conv + gate * cw_ref[2:3, :]
        act_ref[rows, :] = (conv * jax.nn.sigmoid(conv) * up_all[rows]).astype(act_ref.dtype)
        tail_ref[s] = gate[seq_rows - 8:, :]


def _ffn_up(x, w_in, conv0, conv_w, conv_b, n_batch, t_len, tm=1024, tn=256):
    m, d = x.shape
    d_ff = w_in.shape[1] // 2
    tm = min(m, tm)
    seq_rows = min(t_len, tm)
    seqs_per_tile, tiles_per_seq = tm // seq_rows, t_len // seq_rows
    assert m % tm == 0 and tm % seq_rows == 0 and t_len % seq_rows == 0 and seq_rows % 8 == 0 and d_ff % tn == 0
    n_up = d_ff // tn
    act, tail = pl.pallas_call(
        functools.partial(_ffn_up_kernel, seq_rows=seq_rows, seqs_per_tile=seqs_per_tile, tiles_per_seq=tiles_per_seq),
        out_shape=(jax.ShapeDtypeStruct((m, d_ff), BF16),
                   jax.ShapeDtypeStruct((n_batch * tiles_per_seq, 8, d_ff), F32)),
        grid=(m // tm, n_up),
        in_specs=[pl.BlockSpec((tm, d), lambda i, j: (i, 0)),
                  pl.BlockSpec((8, d), lambda i, j: (jnp.maximum(i * (tm // 8) - 1, 0), 0)),
                  pl.BlockSpec((d, tn), lambda i, j: (0, j)),
                  pl.BlockSpec((d, tn), lambda i, j: (0, j + n_up)),
                  pl.BlockSpec((seqs_per_tile, CONV_W - 1, tn), lambda i, j: (i // tiles_per_seq, 0, j)),
                  pl.BlockSpec((CONV_W, tn), lambda i, j: (0, j)),
                  pl.BlockSpec((1, tn), lambda i, j: (0, j))],
        out_specs=(pl.BlockSpec((tm, tn), lambda i, j: (i, j)),
                   pl.BlockSpec((seqs_per_tile, 8, tn), lambda i, j: (i, 0, j))),
        compiler_params=_cparams(("parallel", "arbitrary")),
        name="ffn_up",
    )(x, x, w_in, w_in, conv0, conv_w, conv_b.reshape(1, d_ff))
    gate_tail = tail.reshape(n_batch, tiles_per_seq, 8, d_ff)[:, -1, 8 - (CONV_W - 1):, :]
    return act, gate_tail


def _pad_rw_cols(a):
    z = lambda n: jnp.zeros(a.shape[:-1] + (n,), a.dtype)
    wd0, ad0, gd0 = 3 * R_WIDTH, 3 * R_WIDTH + DECAY_LORA, 3 * R_WIDTH + DECAY_LORA + AAA_LORA
    return jnp.concatenate([a[..., :wd0], a[..., wd0:ad0], z(LORA_PAD - DECAY_LORA), a[..., ad0:gd0],
                            z(LORA_PAD - AAA_LORA), a[..., gd0:]], axis=-1)


def _unpad_rw_cols(a):
    return jnp.concatenate([a[..., :RW_WD0 + DECAY_LORA], a[..., RW_AD0:RW_AD0 + AAA_LORA], a[..., RW_GD0:]], axis=-1)


def _prep_weights(norm_mix_g, w_in, rwkv_mu, rwkv_w0, rwkv_w2, rwkv_a0, rwkv_a2, rwkv_g2, rwkv_k_k, rwkv_k_a,
                  rwkv_r_k, rwkv_lnx_w, rwkv_lnx_b, w_out, norm_ffn_g, ffn_w_in, ffn_conv_w, ffn_conv_b,
                  ffn_w_down, norm_final_g, l):
    d = w_in.shape[1]
    w_att = jnp.concatenate([w_in[l][:, :ATT_USED].astype(BF16), jnp.zeros((d, ATT_COLS - ATT_USED), BF16)], axis=1)
    w_rw = _pad_rw_cols(w_in[l][:, ATT_USED:].astype(BF16))
    row = lambda a: a.reshape(1, -1).astype(F32)
    pad_rows = lambda a, n: jnp.concatenate([a, jnp.zeros((n - a.shape[0], a.shape[1]), a.dtype)], axis=0)
    lane_h = jnp.arange(LANES) // R_HEAD_DIM
    w2_pad = pad_rows(rwkv_w2[l].astype(F32), LORA_PAD)
    w2_hi = w2_pad.astype(BF16)
    w2_mid = (w2_pad - w2_hi.astype(F32)).astype(BF16)
    return dict(
        norm_mix_g=norm_mix_g[l], w_att=w_att, w_rw=w_rw,
        mu=_pad_rw_cols(row(rwkv_mu[l])), w0=row(rwkv_w0[l]), a0=row(rwkv_a0[l]),
        k_k=row(rwkv_k_k[l]), k_a=row(rwkv_k_a[l]),
        w2=jnp.concatenate([w2_hi, w2_mid, w2_hi], axis=0), a2=pad_rows(rwkv_a2[l], LORA_PAD).astype(BF16),
        g2=rwkv_g2[l].astype(BF16),
        r_k=row(rwkv_r_k[l]), lnx_w=row(rwkv_lnx_w[l]), lnx_b=row(rwkv_lnx_b[l]),
        ones_bd=jnp.concatenate([lane_h[:, None] == lane_h[None, :]] * 2, axis=0).astype(BF16),
        w_out_a=w_out[l][:A_WIDTH].astype(BF16), w_out_r=w_out[l][A_WIDTH:].astype(BF16),
        norm_ffn_g=norm_ffn_g[l], ffn_w_in=ffn_w_in[l], conv_w=ffn_conv_w[l], conv_b=ffn_conv_b[l],
        ffn_w_down=ffn_w_down[l].astype(BF16), norm_final_g=norm_final_g,
    )


def _trunk(x, past_k, past_v, past_ik, s0, shift0, conv0, wt):
    nb, t_len, d = x.shape
    m = nb * t_len
    p_len = 0 if past_k is None else past_k.shape[1]
    x2 = x.reshape(m, d)

    h = _rmsnorm(x2, wt["norm_mix_g"], BF16)
    z_att = _matmul([h], [wt["w_att"]], tm=2048, name="proj_att")
    z_rw = _matmul([h], [wt["w_rw"]], tm=2048, name="proj_rw")

    q_bf, k_f, k_bf, v_f, v_bf, iq_bf, ikw, ik_bf = _rope_split(z_att, t_len, p_len)
    past = None if past_k is None else (past_k.reshape(nb, p_len, KV_W), past_v.reshape(nb, p_len, KV_W), past_ik)
    attn = _dsa(q_bf, iq_bf, ikw, k_bf, v_bf, ik_bf, past, n_batch=nb, t_len=t_len)

    z_rw3 = z_rw.reshape(nb, t_len, RW_COLS)
    r, w, k2, v2, nkk, b, g = _rwkv_pre(z_rw3, _pad_rw_cols(shift0.astype(F32)), wt["mu"], wt["w0"], wt["a0"],
                                       wt["k_k"], wt["k_a"], wt["w2"], wt["a2"], wt["g2"], wt["ones_bd"])
    ycol, s_t = _rwkv_scan(r, w, k2, nkk, b, v2, s0.astype(F32))
    flat = lambda a: a.reshape(m, R_WIDTH)
    rw = _rwkv_post(ycol, flat(r), flat(k2), flat(v2), flat(g), wt["lnx_w"], wt["lnx_b"], wt["r_k"], wt["ones_bd"])

    x1 = _matmul([attn, rw], [wt["w_out_a"], wt["w_out_r"]], res=x2, tm=1024, name="out_proj")
    hf = _rmsnorm(x1, wt["norm_ffn_g"], BF16)
    act, conv_t = _ffn_up(hf, wt["ffn_w_in"], conv0.astype(F32), wt["conv_w"], wt["conv_b"], nb, t_len)
    x3 = _matmul([act], [wt["ffn_w_down"]], res=x1, name="ffn_down")
    y_out = _rmsnorm(x3, wt["norm_final_g"], F32).reshape(nb, t_len, d)

    shift_t = _unpad_rw_cols(z_rw3[:, -1:])
    caches = (k_f.reshape(nb, t_len, A_KV_HEADS, A_HEAD_DIM)[None], v_f.reshape(nb, t_len, A_KV_HEADS, A_HEAD_DIM)[None],
              ikw[:, :IDX_DIM].reshape(nb, t_len, IDX_DIM)[None], s_t[None], shift_t[None], conv_t[None])
    return y_out, caches


def kernel(x_prompt, x_sample, cache_k, cache_v, cache_idx_k, state_rwkv, state_rwkv_shift, state_ffn_conv, norm_mix_g, w_in, rwkv_mu, rwkv_w0, rwkv_w2, rwkv_a0, rwkv_a2, rwkv_g2, rwkv_k_k, rwkv_k_a, rwkv_r_k, rwkv_lnx_w, rwkv_lnx_b, w_out, norm_ffn_g, ffn_w_in, ffn_conv_w, ffn_conv_b, ffn_w_down, norm_final_g):
    assert w_in.shape[0] == 1, "single-layer trunk"
    wt = _prep_weights(norm_mix_g, w_in, rwkv_mu, rwkv_w0, rwkv_w2, rwkv_a0, rwkv_a2, rwkv_g2, rwkv_k_k, rwkv_k_a,
                       rwkv_r_k, rwkv_lnx_w, rwkv_lnx_b, w_out, norm_ffn_g, ffn_w_in, ffn_conv_w, ffn_conv_b,
                       ffn_w_down, norm_final_g, 0)
    bp = x_prompt.shape[0]
    d_ff = ffn_conv_w.shape[-1]
    y_p, c_p = _trunk(x_prompt, None, None, None,
                      jnp.zeros((bp, R_HEADS, R_HEAD_DIM, R_HEAD_DIM), F32), jnp.zeros((bp, 1, RWKV_COLS), F32),
                      jnp.zeros((bp, CONV_W - 1, d_ff), F32), wt)
    y_s, c_s = _trunk(x_sample, cache_k[0], cache_v[0], cache_idx_k[0], state_rwkv[0], state_rwkv_shift[0],
                      state_ffn_conv[0], wt)
    return (y_p, y_s) + c_p + c_s
```
